```python
import math
import jax, jax.numpy as jnp
from jax import lax
import numpy as np

D_MODEL = 1024
BATCH = 16
SEQ = 256
DEPTH = 1
DEC_BATCH = 2
DEC_SEQ = 4096
PAST_LEN = 512

GRID_W = 64
MIX_WIDTH = D_MODEL
POOL_WINDOWS = (2, 4, 8, 16)
POOL_WIDTH = MIX_WIDTH // 4
POOL_GROUP = POOL_WIDTH // len(POOL_WINDOWS)
ATTN_WIDTH = MIX_WIDTH - POOL_WIDTH
N_HEADS = 6
HEAD_DIM = ATTN_WIDTH // (2 * N_HEADS)
V_DIM = 2 * HEAD_DIM
IN_WIDTH = POOL_WIDTH + 3 * ATTN_WIDTH
N_EXPERTS = 16
CAPACITY_FACTOR = 2
D_EXPERT = D_MODEL
ROPE_BASE = 10000.0
EPS = 1e-6
Q_BLOCK = 128

kernel_name = "hybrid_pool_diffattn_ec_moe_diffusion_step"


def lambda_init(layer):
    return 0.8 - 0.6 * math.exp(-0.3 * layer)


def rms_norm(x, g):
    xf = x.astype(jnp.float32)
    y = xf * lax.rsqrt(jnp.mean(xf * xf, axis=-1, keepdims=True) + EPS)
    return (y * g.astype(jnp.float32)).astype(x.dtype)


def axial_rope_tables(T):
    t = np.arange(T)
    row, col = t // GRID_W, t % GRID_W
    half = HEAD_DIM // 2
    freqs = 1.0 / (ROPE_BASE ** (np.arange(0, half, 2) / half))
    ang_r = row[:, None] * freqs[None, :]
    ang_c = col[:, None] * freqs[None, :]
    ang = np.concatenate([ang_r, ang_r, ang_c, ang_c], axis=-1)
    return jnp.asarray(np.cos(ang), dtype=jnp.float32), jnp.asarray(np.sin(ang), dtype=jnp.float32)


def apply_axial_rope(x, cos, sin):
    xs = x.reshape(x.shape[:-1] + (2, 2, HEAD_DIM // 4))
    rot = jnp.stack([-xs[..., 1, :], xs[..., 0, :]], axis=-2).reshape(x.shape)
    c = cos[None, :, None, None, :]
    s = sin[None, :, None, None, :]
    return (x.astype(jnp.float32) * c + rot.astype(jnp.float32) * s).astype(x.dtype)


def centred_mean_minus_self(x, w):
    B, T, C = x.shape
    left = w // 2
    right = w - 1 - left
    t = np.arange(T)
    lo = np.maximum(t - left, 0)
    hi = np.minimum(t + right, T - 1) + 1
    cs = jnp.concatenate([jnp.zeros((B, 1, C), jnp.float32), jnp.cumsum(x.astype(jnp.float32), axis=1)], axis=1)
    s = cs[:, hi] - cs[:, lo]
    cnt = jnp.asarray((hi - lo), dtype=jnp.float32)
    return (s / cnt[None, :, None]).astype(x.dtype) - x


def pool_mixer(p, w_pool, pool_scale):
    B, T, _ = p.shape
    pg = p.reshape(B, T, len(POOL_WINDOWS), POOL_GROUP)
    pooled = jnp.stack([centred_mean_minus_self(pg[:, :, i], w) for i, w in enumerate(POOL_WINDOWS)], axis=2)
    out = jnp.einsum('btgc,gcd->btgd', pooled, w_pool).reshape(B, T, POOL_WIDTH)
    return out * pool_scale


def project_mixer_inputs(h, w_in, q_g, k_g):
    B, T, _ = h.shape
    z = h @ w_in
    p = z[..., :POOL_WIDTH]
    q = z[..., POOL_WIDTH:POOL_WIDTH + ATTN_WIDTH].reshape(B, T, N_HEADS, 2, HEAD_DIM)
    k = z[..., POOL_WIDTH + ATTN_WIDTH:POOL_WIDTH + 2 * ATTN_WIDTH].reshape(B, T, N_HEADS, 2, HEAD_DIM)
    v = z[..., POOL_WIDTH + 2 * ATTN_WIDTH:].reshape(B, T, N_HEADS, V_DIM)
    return p, rms_norm(q, q_g), rms_norm(k, k_g), v


def diff_attn_block(q, k, v, lam):
    s = jnp.einsum('bqhnd,bkhnd->bnhqk', q, k).astype(jnp.float32) * (1.0 / math.sqrt(HEAD_DIM))
    a = jax.nn.softmax(s, axis=-1)
    a = a[:, 0] - lam * a[:, 1]
    return jnp.einsum('bhqk,bkhv->bqhv', a.astype(v.dtype), v)


def ec_moe(h, w_router, w_gate, w_up, w_down):
    B, T, D = h.shape
    N = B * T
    xf = h.reshape(N, D)
    aff = jax.nn.softmax((xf @ w_router).astype(jnp.float32), axis=-1)
    cap = CAPACITY_FACTOR * N // N_EXPERTS
    gates, idx = lax.top_k(aff.T, cap)
    xe = xf[idx]
    hid = jax.nn.silu(jnp.einsum('ecd,edf->ecf', xe, w_gate)) * jnp.einsum('ecd,edf->ecf', xe, w_up)
    ye = jnp.einsum('ecf,efd->ecd', hid, w_down) * gates[..., None].astype(h.dtype)
    out = jnp.zeros((N, D), h.dtype).at[idx.reshape(-1)].add(ye.reshape(-1, D))
    return out.reshape(B, T, D)


def trunk_layer(x, cvec, layer, ctx_k, ctx_v, norm1_g, norm2_g, w_ada, b_ada, w_in, q_norm_g, k_norm_g,
                lambda_q1, lambda_k1, lambda_q2, lambda_k2, subln_g, w_pool, pool_scale, w_out,
                w_router, w_gate, w_up, w_down):
    B, T, _ = x.shape
    mod = jax.nn.silu(cvec) @ w_ada + b_ada
    shift1, scale1, gate1, shift2, scale2, gate2 = [m[..., None, :] for m in jnp.split(mod, 6, axis=-1)]
    h = rms_norm(x, norm1_g) * (1 + scale1) + shift1
    p, q, k, v = project_mixer_inputs(h, w_in, q_norm_g, k_norm_g)
    lam_i = lambda_init(layer)
    lam = (jnp.exp(jnp.sum(lambda_q1.astype(jnp.float32) * lambda_k1.astype(jnp.float32)))
           - jnp.exp(jnp.sum(lambda_q2.astype(jnp.float32) * lambda_k2.astype(jnp.float32))) + lam_i)
    if ctx_k is None:
        attn = diff_attn_block(q, k, v, lam)
        new_k, new_v = k, v
    else:
        cos, sin = axial_rope_tables(T)
        q = apply_axial_rope(q, cos, sin)
        k = apply_axial_rope(k, cos, sin)
        k_all = jnp.concatenate([ctx_k.astype(k.dtype), k], axis=1)
        v_all = jnp.concatenate([ctx_v.astype(v.dtype), v], axis=1)
        nb = T // Q_BLOCK
        qb = q.reshape(B, nb, Q_BLOCK, N_HEADS, 2, HEAD_DIM).transpose(1, 0, 2, 3, 4, 5)
        ob = lax.map(lambda qq: diff_attn_block(qq, k_all, v_all, lam), qb)
        attn = ob.transpose(1, 0, 2, 3, 4).reshape(B, T, N_HEADS, V_DIM)
        new_k, new_v = None, None
    attn = (rms_norm(attn, subln_g) * (1.0 - lam_i)).reshape(B, T, ATTN_WIDTH)
    pool = pool_mixer(p, w_pool, pool_scale)
    mix = jnp.concatenate([pool, attn], axis=-1) @ w_out
    x = x + gate1 * mix
    h2 = rms_norm(x, norm2_g) * (1 + scale2) + shift2
    x = x + gate2 * ec_moe(h2, w_router, w_gate, w_up, w_down)
    return x, new_k, new_v


def setup_inputs(seed: int = 0) -> dict:
    key = jax.random.key(seed)
    ks = jax.random.split(key, 32)
    f32 = jnp.float32
    nrm = lambda k, shape, s: jax.random.normal(k, shape, f32) * s
    D = D_MODEL
    return {
        "x_prompt": nrm(ks[0], (BATCH, SEQ, D), 1.0),
        "x_sample": nrm(ks[1], (DEC_BATCH, DEC_SEQ, D), 1.0),
        "cache_k": nrm(ks[2], (DEC_BATCH, DEPTH, PAST_LEN, N_HEADS, 2, HEAD_DIM), 1.0),
        "cache_v": nrm(ks[3], (DEC_BATCH, DEPTH, PAST_LEN, N_HEADS, V_DIM), 1.0),
        "c": nrm(ks[4], (DEC_BATCH, D), 1.0),
        "c_ctx": nrm(ks[5], (D,), 1.0),
        "norm1_g": 1.0 + nrm(ks[6], (DEPTH, D), 0.02),
        "norm2_g": 1.0 + nrm(ks[7], (DEPTH, D), 0.02),
        "w_ada": nrm(ks[8], (DEPTH, D, 6 * D), 0.5 * D ** -0.5),
        "b_ada": nrm(ks[9], (DEPTH, 6 * D), 0.02),
        "w_in": nrm(ks[10], (DEPTH, D, IN_WIDTH), D ** -0.5),
        "q_norm_g": 1.0 + nrm(ks[11], (DEPTH, HEAD_DIM), 0.02),
        "k_norm_g": 1.0 + nrm(ks[12], (DEPTH, HEAD_DIM), 0.02),
        "lambda_q1": nrm(ks[13], (DEPTH, HEAD_DIM), 0.1),
        "lambda_k1": nrm(ks[14], (DEPTH, HEAD_DIM), 0.1),
        "lambda_q2": nrm(ks[15], (DEPTH, HEAD_DIM), 0.1),
        "lambda_k2": nrm(ks[16], (DEPTH, HEAD_DIM), 0.1),
        "subln_g": 1.0 + nrm(ks[17], (DEPTH, V_DIM), 0.02),
        "w_pool": nrm(ks[18], (DEPTH, len(POOL_WINDOWS), POOL_GROUP, POOL_GROUP), POOL_GROUP ** -0.5),
        "pool_scale": 1.0 + nrm(ks[19], (DEPTH, POOL_WIDTH), 0.1),
        "w_out": nrm(ks[20], (DEPTH, MIX_WIDTH, D), MIX_WIDTH ** -0.5),
        "w_router": nrm(ks[21], (DEPTH, D, N_EXPERTS), D ** -0.5),
        "w_gate": nrm(ks[22], (DEPTH, N_EXPERTS, D, D_EXPERT), D ** -0.5),
        "w_up": nrm(ks[23], (DEPTH, N_EXPERTS, D, D_EXPERT), D ** -0.5),
        "w_down": nrm(ks[24], (DEPTH, N_EXPERTS, D_EXPERT, D), D_EXPERT ** -0.5),
    }


def reference(x_prompt, x_sample, cache_k, cache_v, c, c_ctx, norm1_g, norm2_g, w_ada, b_ada, w_in,
              q_norm_g, k_norm_g, lambda_q1, lambda_k1, lambda_q2, lambda_k2, subln_g, w_pool,
              pool_scale, w_out, w_router, w_gate, w_up, w_down):
    xp = x_prompt
    xs = x_sample
    ks_out, vs_out = [], []
    for l in range(DEPTH):
        layer_w = (norm1_g[l], norm2_g[l], w_ada[l], b_ada[l], w_in[l], q_norm_g[l], k_norm_g[l],
                   lambda_q1[l], lambda_k1[l], lambda_q2[l], lambda_k2[l], subln_g[l], w_pool[l],
                   pool_scale[l], w_out[l], w_router[l], w_gate[l], w_up[l], w_down[l])
        xp, k_l, v_l = trunk_layer(xp, c_ctx, l, None, None, *layer_w)
        ks_out.append(k_l)
        vs_out.append(v_l)
        xs, _, _ = trunk_layer(xs, c, l, cache_k[:, l], cache_v[:, l], *layer_w)
    ctx_k = jnp.stack(ks_out, axis=1)
    ctx_v = jnp.stack(vs_out, axis=1)
    return (xp, xs, ctx_k, ctx_v)
```

```python
import functools
import math

import numpy as np
import jax
import jax.numpy as jnp
from jax import lax
from jax.experimental import pallas as pl
from jax.experimental.pallas import tpu as pltpu

F32 = jnp.float32
BF16 = jnp.bfloat16

D_MODEL = 1024
POOL_WIDTH = 256
ATTN_WIDTH = 768
N_HEADS = 6
HEAD_DIM = 64
V_DIM = 128
IN_WIDTH = POOL_WIDTH + 3 * ATTN_WIDTH
N_EXPERTS = 16
CAPACITY_FACTOR = 2
GRID_W = 64
ROPE_BASE = 10000.0
EPS = 1e-6
LAMBDA_INIT = 0.8 - 0.6 * math.exp(-0.3 * 0)

LANES = 128
SUBLANES = 8
ROW_TILES = D_MODEL // LANES
HALO = 16
VMEM_LIMIT = 56 * 1024 * 1024


def _cparams(sem, vmem=None):
    return pltpu.CompilerParams(dimension_semantics=sem, vmem_limit_bytes=vmem)


def _ada_kernel(c_ref, w_ref, b_ref, o_ref):
    c = c_ref[...]
    s = c * jax.nn.sigmoid(c)
    o_ref[...] = jnp.dot(s.astype(BF16), w_ref[...].astype(BF16),
                         preferred_element_type=F32) + b_ref[...]


def _ada_call(cvec, w_ada, b_ada):
    rows, d = cvec.shape
    n = w_ada.shape[1]
    bn = 1536
    return pl.pallas_call(
        _ada_kernel,
        grid=(n // bn,),
        in_specs=[pl.BlockSpec((rows, d), lambda j: (0, 0)),
                  pl.BlockSpec((d, bn), lambda j: (0, j)),
                  pl.BlockSpec((1, bn), lambda j: (0, j))],
        out_specs=pl.BlockSpec((rows, bn), lambda j: (0, j)),
        out_shape=jax.ShapeDtypeStruct((rows, n), F32),
        compiler_params=_cparams(("arbitrary",)),
        name="ada",
    )(cvec, w_ada, b_ada.reshape(1, n))


def _segment_mean_square(a, seg_ref):
    sq = a * a
    hi = sq.astype(BF16)
    lo = (sq - hi.astype(F32)).astype(BF16)
    seg = seg_ref[...]
    return (jnp.dot(hi, seg, preferred_element_type=F32)
            + jnp.dot(lo, seg, preferred_element_type=F32))


def _rope(a, cos, sin_signed, first_half):
    parts = []
    for h in range(a.shape[1] // LANES):
        blk = a[:, h * LANES:(h + 1) * LANES]
        fwd = pltpu.roll(blk, LANES - HEAD_DIM // 4, 1)
        bwd = pltpu.roll(blk, HEAD_DIM // 4, 1)
        parts.append(blk * cos + jnp.where(first_half, fwd, bwd) * sin_signed)
    return jnp.concatenate(parts, axis=1)


def _pre_kernel(*refs, rope, emit_f32_kv):
    x_ref, mod_ref, g1_ref, win_ref, seg_ref, qg_ref, kg_ref = refs[:7]
    pos = 7
    if rope:
        cos_ref, sin_ref = refs[pos:pos + 2]
        pos += 2
    p_ref, q_ref, k_ref, v_ref = refs[pos:pos + 4]
    pos += 4
    if emit_f32_kv:
        kf_ref, vf_ref = refs[pos:pos + 2]

    x = x_ref[...]
    mod = mod_ref[0]
    shift1 = mod[:, :D_MODEL]
    scale1 = mod[:, D_MODEL:2 * D_MODEL]
    ms = jnp.mean(x * x, axis=1, keepdims=True)
    h = x * lax.rsqrt(ms + EPS) * g1_ref[...] * (1.0 + scale1) + shift1
    z = jnp.dot(h.astype(BF16), win_ref[...], preferred_element_type=F32)

    p_ref[...] = z[:, :POOL_WIDTH]
    qz = z[:, POOL_WIDTH:POOL_WIDTH + ATTN_WIDTH]
    kz = z[:, POOL_WIDTH + ATTN_WIDTH:POOL_WIDTH + 2 * ATTN_WIDTH]
    vz = z[:, POOL_WIDTH + 2 * ATTN_WIDTH:]

    qn = qz * lax.rsqrt(_segment_mean_square(qz, seg_ref) + EPS) * qg_ref[...]
    kn = kz * lax.rsqrt(_segment_mean_square(kz, seg_ref) + EPS) * kg_ref[...]
    if rope:
        cos = cos_ref[...]
        sin_signed = sin_ref[...]
        lane = lax.broadcasted_iota(jnp.int32, cos.shape, 1)
        first_half = (lane % (HEAD_DIM // 2)) < (HEAD_DIM // 4)
        qn = _rope(qn, cos, sin_signed, first_half)
        kn = _rope(kn, cos, sin_signed, first_half)

    q_ref[...] = (qn * (1.0 / math.sqrt(HEAD_DIM))).astype(BF16)
    k_ref[...] = kn.astype(BF16)
    v_ref[...] = vz.astype(BF16)
    if emit_f32_kv:
        kf_ref[...] = kn
        vf_ref[...] = vz


def _pre_call(x2d, mod3, g1, win_bf, seg, qg, kg, rope_tabs, *, seq, tm, mod_base, mod_stride,
              emit_f32_kv):
    n = x2d.shape[0]
    rope = rope_tabs is not None
    tiles_per_seq = seq // tm

    def mod_map(i):
        return (mod_base + mod_stride * (i // tiles_per_seq), 0, 0)

    in_specs = [
        pl.BlockSpec((tm, D_MODEL), lambda i: (i, 0)),
        pl.BlockSpec((1, 1, 2 * D_MODEL), mod_map),
        pl.BlockSpec((1, D_MODEL), lambda i: (0, 0)),
        pl.BlockSpec((D_MODEL, IN_WIDTH), lambda i: (0, 0)),
        pl.BlockSpec((ATTN_WIDTH, ATTN_WIDTH), lambda i: (0, 0)),
        pl.BlockSpec((1, ATTN_WIDTH), lambda i: (0, 0)),
        pl.BlockSpec((1, ATTN_WIDTH), lambda i: (0, 0)),
    ]
    args = [x2d, mod3, g1, win_bf, seg, qg, kg]
    if rope:
        in_specs += [pl.BlockSpec((tm, LANES), lambda i: (i % tiles_per_seq, 0))] * 2
        args += list(rope_tabs)
    out_shapes = [jax.ShapeDtypeStruct((n, POOL_WIDTH), F32)]
    out_specs = [pl.BlockSpec((tm, POOL_WIDTH), lambda i: (i, 0))]
    out_shapes += [jax.ShapeDtypeStruct((n, ATTN_WIDTH), BF16)] * 3
    out_specs += [pl.BlockSpec((tm, ATTN_WIDTH), lambda i: (i, 0))] * 3
    if emit_f32_kv:
        out_shapes += [jax.ShapeDtypeStruct((n, ATTN_WIDTH), F32)] * 2
        out_specs += [pl.BlockSpec((tm, ATTN_WIDTH), lambda i: (i, 0))] * 2
    return pl.pallas_call(
        functools.partial(_pre_kernel, rope=rope, emit_f32_kv=emit_f32_kv),
        grid=(n // tm,),
        in_specs=in_specs,
        out_specs=out_specs,
        out_shape=out_shapes,
        compiler_params=_cparams(("arbitrary",), VMEM_LIMIT),
        name="pre_rope" if rope else "pre",
    )(*args)


def _attn_kernel(*refs, has_cache, tq, tk, seq, cache_len):
    if has_cache:
        lam_ref, g_ref, q_ref, kc_ref, vc_ref, k_ref, v_ref, o_ref = refs
    else:
        lam_ref, g_ref, q_ref, k_ref, v_ref, o_ref = refs

    lv = lam_ref[...]
    lam = (jnp.exp(jnp.sum(lv[0:1] * lv[1:2], axis=1, keepdims=True))
           - jnp.exp(jnp.sum(lv[2:3] * lv[3:4], axis=1, keepdims=True)) + LAMBDA_INIT)

    q = q_ref[...]
    lane = lax.broadcasted_iota(jnp.int32, q.shape, 1)
    zero = jnp.zeros_like(q)
    qs = jnp.concatenate([jnp.where(lane < HEAD_DIM, q, zero),
                          jnp.where(lane >= HEAD_DIM, q, zero)], axis=0)

    def block(kb, vb, carry):
        m, l, acc = carry
        s = lax.dot_general(qs, kb, (((1,), (1,)), ((), ())), preferred_element_type=F32)
        m_new = jnp.maximum(m, jnp.max(s, axis=1, keepdims=True))
        alpha = jnp.exp(m - m_new)
        p = jnp.exp(s - m_new)
        l = alpha * l + jnp.sum(p, axis=1, keepdims=True)
        acc = alpha * acc + jnp.dot(p.astype(BF16), vb, preferred_element_type=F32)
        return m_new, l, acc

    carry = (jnp.full((2 * tq, 1), -jnp.inf, F32), jnp.zeros((2 * tq, 1), F32),
             jnp.zeros((2 * tq, V_DIM), F32))
    if has_cache:
        for j in range(cache_len // tk):
            carry = block(kc_ref[j * tk:(j + 1) * tk, :], vc_ref[j * tk:(j + 1) * tk, :], carry)

    def body(j, carry):
        off = pl.multiple_of(j * tk, tk)
        return block(k_ref[pl.ds(off, tk), :], v_ref[pl.ds(off, tk), :], carry)

    _, l, acc = lax.fori_loop(0, seq // tk, body, carry)
    o = acc / l
    o = o[:tq] - lam * o[tq:]
    y = o * lax.rsqrt(jnp.mean(o * o, axis=1, keepdims=True) + EPS) * g_ref[...] * (1.0 - LAMBDA_INIT)
    o_ref[...] = y.astype(BF16)


def _attn_call(lam4, subln_g, q, k, v, cache, *, batch, seq, tq, tk):
    has_cache = cache is not None
    nq = seq // tq
    in_specs = [
        pl.BlockSpec((4, HEAD_DIM), lambda b, h, i: (0, 0)),
        pl.BlockSpec((1, V_DIM), lambda b, h, i: (0, 0)),
        pl.BlockSpec((tq, LANES), lambda b, h, i: (b * nq + i, h)),
    ]
    args = [lam4, subln_g, q]
    cache_len = 0
    if has_cache:
        kc, vc = cache
        cache_len = kc.shape[0] // batch
        in_specs += [pl.BlockSpec((cache_len, LANES), lambda b, h, i: (b, h))] * 2
        args += [kc, vc]
    in_specs += [pl.BlockSpec((seq, LANES), lambda b, h, i: (b, h))] * 2
    args += [k, v]
    return pl.pallas_call(
        functools.partial(_attn_kernel, has_cache=has_cache, tq=tq, tk=tk, seq=seq,
                          cache_len=cache_len),
        grid=(batch, N_HEADS, nq),
        in_specs=in_specs,
        out_specs=pl.BlockSpec((tq, LANES), lambda b, h, i: (b * nq + i, h)),
        out_shape=jax.ShapeDtypeStruct((batch * seq, ATTN_WIDTH), BF16),
        compiler_params=_cparams(("arbitrary", "arbitrary", "arbitrary"), VMEM_LIMIT),
        name="attn_cache" if has_cache else "attn",
    )(*args)


def _post_kernel(x_ref, a_ref, p_ref, pp_ref, pn_ref, mod_ref, wout_ref, wbd_ref, ps_ref, g2_ref,
                 wr_ref, x1_ref, h2_ref, aff_ref, *, tm, seq):
    i = pl.program_id(0)
    tiles_per_seq = seq // tm
    ti = i % tiles_per_seq
    p = p_ref[...]
    prev = jnp.where(ti == 0, 0.0, pp_ref[...])
    nxt = jnp.where(ti == tiles_per_seq - 1, 0.0, pn_ref[...])
    ext = jnp.concatenate([prev, p, nxt], axis=0)
    n_ext = tm + 2 * HALO
    s2 = ext + pltpu.roll(ext, 1, 0)
    s4 = pltpu.roll(s2, 1, 0) + pltpu.roll(s2, n_ext - 1, 0)
    s8 = pltpu.roll(s4, 2, 0) + pltpu.roll(s4, n_ext - 2, 0)
    s16 = pltpu.roll(s8, 4, 0) + pltpu.roll(s8, n_ext - 4, 0)
    lane = lax.broadcasted_iota(jnp.int32, (tm, POOL_WIDTH), 1)
    grp = lane // (POOL_WIDTH // 4)
    win = jnp.where(grp == 0, s2[HALO:HALO + tm],
                    jnp.where(grp == 1, s4[HALO:HALO + tm],
                              jnp.where(grp == 2, s8[HALO:HALO + tm], s16[HALO:HALO + tm])))
    t = ti * tm + lax.broadcasted_iota(jnp.int32, (tm, POOL_WIDTH), 0)
    left = jnp.where(grp == 0, 1, jnp.where(grp == 1, 2, jnp.where(grp == 2, 4, 8)))
    lo = jnp.maximum(t - left, 0)
    hi = jnp.minimum(t + left - 1, seq - 1) + 1
    pooled = win / (hi - lo).astype(F32) - p
    pool = jnp.dot(pooled.astype(BF16), wbd_ref[...], preferred_element_type=F32) * ps_ref[...]

    cat = jnp.concatenate([pool.astype(BF16), a_ref[...]], axis=1)
    mix = jnp.dot(cat, wout_ref[...], preferred_element_type=F32)
    mod = mod_ref[0]
    gate1 = mod[:, 0:D_MODEL]
    shift2 = mod[:, D_MODEL:2 * D_MODEL]
    scale2 = mod[:, 2 * D_MODEL:3 * D_MODEL]
    x1 = x_ref[...] + gate1 * mix
    x1_ref[...] = x1
    ms = jnp.mean(x1 * x1, axis=1, keepdims=True)
    h2 = x1 * lax.rsqrt(ms + EPS) * g2_ref[...] * (1.0 + scale2) + shift2
    for s in range(ROW_TILES):
        h2_ref[pl.ds(s, tm, stride=ROW_TILES), :] = h2[:, s * LANES:(s + 1) * LANES]

    logits = lax.dot_general(wr_ref[...], h2, (((1,), (1,)), ((), ())),
                             precision=lax.Precision.HIGHEST, preferred_element_type=F32)
    e = jnp.exp(logits - jnp.max(logits, axis=0, keepdims=True))
    aff_ref[...] = e / jnp.sum(e, axis=0, keepdims=True)


def _post_call(x2d, attn, p, mod3, wout_bf, wbd_bf, pool_scale, g2, wr_t, *, seq, tm, mod_base,
               mod_stride):
    n = x2d.shape[0]
    tiles_per_seq = seq // tm
    halo_per_tile = tm // HALO
    n_halo = n // HALO

    def mod_map(i):
        return (mod_base + mod_stride * (i // tiles_per_seq), 0, 0)

    return pl.pallas_call(
        functools.partial(_post_kernel, tm=tm, seq=seq),
        grid=(n // tm,),
        in_specs=[
            pl.BlockSpec((tm, D_MODEL), lambda i: (i, 0)),
            pl.BlockSpec((tm, ATTN_WIDTH), lambda i: (i, 0)),
            pl.BlockSpec((tm, POOL_WIDTH), lambda i: (i, 0)),
            pl.BlockSpec((HALO, POOL_WIDTH), lambda i: (jnp.maximum(i * halo_per_tile - 1, 0), 0)),
            pl.BlockSpec((HALO, POOL_WIDTH),
                         lambda i: (jnp.minimum((i + 1) * halo_per_tile, n_halo - 1), 0)),
            pl.BlockSpec((1, 1, 3 * D_MODEL), lambda i: mod_map(i)[:2] + (0,)),
            pl.BlockSpec((D_MODEL, D_MODEL), lambda i: (0, 0)),
            pl.BlockSpec((POOL_WIDTH, POOL_WIDTH), lambda i: (0, 0)),
            pl.BlockSpec((1, POOL_WIDTH), lambda i: (0, 0)),
            pl.BlockSpec((1, D_MODEL), lambda i: (0, 0)),
            pl.BlockSpec((N_EXPERTS, D_MODEL), lambda i: (0, 0)),
        ],
        out_specs=[
            pl.BlockSpec((tm, D_MODEL), lambda i: (i, 0)),
            pl.BlockSpec((tm * ROW_TILES, LANES), lambda i: (i, 0)),
            pl.BlockSpec((N_EXPERTS, tm), lambda i: (0, i)),
        ],
        out_shape=[
            jax.ShapeDtypeStruct((n, D_MODEL), F32),
            jax.ShapeDtypeStruct((n * ROW_TILES, LANES), F32),
            jax.ShapeDtypeStruct((N_EXPERTS, n), F32),
        ],
        compiler_params=_cparams(("arbitrary",), VMEM_LIMIT),
        name="post",
    )(x2d, attn, p, p, p, mod3, wout_bf, wbd_bf, pool_scale, g2, wr_t)


def _cumsum_lanes(x_ref, out_ref, n):
    r = lax.broadcasted_iota(jnp.int32, (LANES, LANES), 0)
    c = lax.broadcasted_iota(jnp.int32, (LANES, LANES), 1)
    upper = jnp.where(r <= c, 1.0, 0.0).astype(BF16)
    off = jnp.zeros((x_ref.shape[0], 1), F32)
    for j in range(n // LANES):
        xc = x_ref[:, j * LANES:(j + 1) * LANES]
        out_ref[:, j * LANES:(j + 1) * LANES] = (
            jnp.dot(xc.astype(BF16), upper, preferred_element_type=F32) + off)
        off = off + jnp.sum(xc, axis=1, keepdims=True)


def _select_kernel(a_ref, idx_ref, gate_ref, mask_s, cum_s, val_s, cumb_s, valb_s, *, n, cap):
    a = a_ref[...]
    thr = jnp.zeros((N_EXPERTS, 1), jnp.int32)
    for bit in range(30, -1, -1):
        cand = thr | (1 << bit)
        cnt = jnp.sum(jnp.where(a >= pltpu.bitcast(cand, F32), 1.0, 0.0), axis=1, keepdims=True)
        thr = jnp.where(cnt >= cap, cand, thr)
    thr_f = pltpu.bitcast(thr, F32)
    above = jnp.where(a > thr_f, 1.0, 0.0)
    tied = jnp.where(a == thr_f, 1.0, 0.0)
    need = cap - jnp.sum(above, axis=1, keepdims=True)
    mask_s[...] = tied
    _cumsum_lanes(mask_s, cum_s, n)
    sel = above + tied * jnp.where(cum_s[...] <= need, 1.0, 0.0)
    mask_s[...] = sel
    _cumsum_lanes(mask_s, cum_s, n)
    cum_s[...] = jnp.where(sel > 0.0, cum_s[...], 0.0)
    val_s[...] = jnp.where(sel > 0.0, a, 0.0)

    lane_f = lax.broadcasted_iota(jnp.int32, (SUBLANES, LANES), 1).astype(F32)
    sub_f = lax.broadcasted_iota(jnp.int32, (SUBLANES, LANES), 0).astype(F32)

    def per_expert(e, _):
        cumb_s[...] = jnp.broadcast_to(cum_s[pl.ds(e, 1), :], (SUBLANES, n))
        valb_s[...] = jnp.broadcast_to(val_s[pl.ds(e, 1), :], (SUBLANES, n))

        def per_slot_tile(jt, _):
            want = sub_f + jnp.asarray(jt * SUBLANES + 1, F32)
            acc_i = jnp.zeros((SUBLANES, LANES), F32)
            acc_g = jnp.zeros((SUBLANES, LANES), F32)
            for c in range(n // LANES):
                hit = cumb_s[:, c * LANES:(c + 1) * LANES] == want
                acc_i = acc_i + jnp.where(hit, lane_f + float(c * LANES), 0.0)
                acc_g = acc_g + jnp.where(hit, valb_s[:, c * LANES:(c + 1) * LANES], 0.0)
            row0 = pl.multiple_of(jt * SUBLANES, SUBLANES)
            tok = jnp.sum(acc_i, axis=1, keepdims=True).astype(jnp.int32)
            idx_ref[e, pl.ds(row0, SUBLANES), :] = jnp.broadcast_to(tok, (SUBLANES, LANES))
            gate_ref[e, pl.ds(row0, SUBLANES), :] = jnp.broadcast_to(
                jnp.sum(acc_g, axis=1, keepdims=True), (SUBLANES, LANES))
            return 0

        lax.fori_loop(0, cap // SUBLANES, per_slot_tile, 0)
        return 0

    lax.fori_loop(0, N_EXPERTS, per_expert, 0)


def _select_call(aff_t, *, cap):
    n = aff_t.shape[1]
    return pl.pallas_call(
        functools.partial(_select_kernel, n=n, cap=cap),
        grid=(1,),
        in_specs=[pl.BlockSpec((N_EXPERTS, n), lambda i: (0, 0))],
        out_specs=[pl.BlockSpec((N_EXPERTS, cap, LANES), lambda i: (0, 0, 0))] * 2,
        out_shape=[jax.ShapeDtypeStruct((N_EXPERTS, cap, LANES), jnp.int32),
                   jax.ShapeDtypeStruct((N_EXPERTS, cap, LANES), F32)],
        scratch_shapes=[pltpu.VMEM((N_EXPERTS, n), F32), pltpu.VMEM((N_EXPERTS, n), F32),
                        pltpu.VMEM((N_EXPERTS, n), F32), pltpu.VMEM((SUBLANES, n), F32),
                        pltpu.VMEM((SUBLANES, n), F32)],
        compiler_params=_cparams(("arbitrary",), VMEM_LIMIT),
        name="select",
    )(aff_t)


def _moe_kernel(idx_ref, gate_ref, wg_ref, wu_ref, wd_ref, x_hbm, out_hbm,
                gbuf, ybuf, acc_ref, gsem, osem, *, tm, n_tiles_total):
    e = pl.program_id(0)
    t = pl.program_id(1)
    nt = pl.num_programs(1)
    step = e * nt + t
    slot = step % 2
    rows = tm * ROW_TILES

    def gather_copy(tile_step, k, dst_slot):
        tok = idx_ref[tile_step * tm + k]
        return pltpu.make_async_copy(
            x_hbm.at[pl.ds(pl.multiple_of(tok * ROW_TILES, ROW_TILES), ROW_TILES), :],
            gbuf.at[dst_slot, pl.ds(pl.multiple_of(k * ROW_TILES, ROW_TILES), ROW_TILES), :],
            gsem.at[dst_slot])

    def issue_gather(tile_step, dst_slot):
        def body(k, _):
            gather_copy(tile_step, k, dst_slot).start()
            return 0
        lax.fori_loop(0, tm, body, 0)

    @pl.when(step == 0)
    def _():
        acc_ref[...] = jnp.zeros_like(acc_ref)
        issue_gather(0, 0)

    pltpu.make_async_copy(x_hbm.at[pl.ds(0, rows), :], gbuf.at[slot], gsem.at[slot]).wait()

    @pl.when(step + 1 < n_tiles_total)
    def _():
        issue_gather(step + 1, 1 - slot)

    xe = jnp.concatenate(
        [gbuf[slot, pl.ds(s, tm, stride=ROW_TILES), :] for s in range(ROW_TILES)],
        axis=1).astype(BF16)
    g = jnp.dot(xe, wg_ref[...], preferred_element_type=F32)
    u = jnp.dot(xe, wu_ref[...], preferred_element_type=F32)
    hid = (g * jax.nn.sigmoid(g) * u).astype(BF16)
    y = jnp.dot(hid, wd_ref[...], preferred_element_type=F32)
    gate = gate_ref[...]
    for s in range(ROW_TILES):
        ybuf[pl.ds(s, tm, stride=ROW_TILES), :] = y[:, s * LANES:(s + 1) * LANES] * gate

    group = 8

    def rmw(kk, _):
        pending = []
        for r in range(group):
            k = kk * group + r
            tok = idx_ref[step * tm + k]
            off = pl.multiple_of(tok * ROW_TILES, ROW_TILES)
            src = pl.multiple_of(k * ROW_TILES, ROW_TILES)
            pending.append((off, acc_ref[pl.ds(off, ROW_TILES), :] + ybuf[pl.ds(src, ROW_TILES), :]))
        for off, val in pending:
            acc_ref[pl.ds(off, ROW_TILES), :] = val
        return 0

    lax.fori_loop(0, tm // group, rmw, 0)

    @pl.when(step == n_tiles_total - 1)
    def _():
        cp = pltpu.make_async_copy(acc_ref, out_hbm, osem)
        cp.start()
        cp.wait()


def _moe_call(idx_flat, gates3, wg_bf, wu_bf, wd_bf, h2_tiles, *, cap, tm):
    n_rows = h2_tiles.shape[0]
    nt = cap // tm
    grid_spec = pltpu.PrefetchScalarGridSpec(
        num_scalar_prefetch=1,
        grid=(N_EXPERTS, nt),
        in_specs=[
            pl.BlockSpec((None, tm, LANES), lambda e, t, idx: (e, t, 0)),
            pl.BlockSpec((None, D_MODEL, D_MODEL), lambda e, t, idx: (e, 0, 0)),
            pl.BlockSpec((None, D_MODEL, D_MODEL), lambda e, t, idx: (e, 0, 0)),
            pl.BlockSpec((None, D_MODEL, D_MODEL), lambda e, t, idx: (e, 0, 0)),
            pl.BlockSpec(memory_space=pl.ANY),
        ],
        out_specs=pl.BlockSpec(memory_space=pl.ANY),
        scratch_shapes=[
            pltpu.VMEM((2, tm * ROW_TILES, LANES), F32),
            pltpu.VMEM((tm * ROW_TILES, LANES), F32),
            pltpu.VMEM((n_rows, LANES), F32),
            pltpu.SemaphoreType.DMA((2,)),
            pltpu.SemaphoreType.DMA(()),
        ],
    )
    return pl.pallas_call(
        functools.partial(_moe_kernel, tm=tm, n_tiles_total=N_EXPERTS * nt),
        grid_spec=grid_spec,
        out_shape=jax.ShapeDtypeStruct((n_rows, LANES), F32),
        compiler_params=_cparams(("arbitrary", "arbitrary"), VMEM_LIMIT),
        name="moe",
    )(idx_flat, gates3, wg_bf, wu_bf, wd_bf, h2_tiles)


def _final_kernel(x1_ref, moe_ref, mod_ref, o_ref, *, tm):
    moe = jnp.concatenate(
        [moe_ref[pl.ds(s, tm, stride=ROW_TILES), :] for s in range(ROW_TILES)], axis=1)
    o_ref[...] = x1_ref[...] + mod_ref[0] * moe


def _final_call(x1, moe_tiles, mod3, *, seq, tm, mod_base, mod_stride):
    n = x1.shape[0]
    tiles_per_seq = seq // tm
    gate2_block = 5

    def mod_map(i):
        return (mod_base + mod_stride * (i // tiles_per_seq), 0, gate2_block)

    return pl.pallas_call(
        functools.partial(_final_kernel, tm=tm),
        grid=(n // tm,),
        in_specs=[pl.BlockSpec((tm, D_MODEL), lambda i: (i, 0)),
                  pl.BlockSpec((tm * ROW_TILES, LANES), lambda i: (i, 0)),
                  pl.BlockSpec((1, 1, D_MODEL), mod_map)],
        out_specs=pl.BlockSpec((tm, D_MODEL), lambda i: (i, 0)),
        out_shape=jax.ShapeDtypeStruct((n, D_MODEL), F32),
        compiler_params=_cparams(("arbitrary",)),
        name="final",
    )(x1, moe_tiles, mod3)


def _rope_tables(seq):
    t = np.arange(seq)
    row, col = t // GRID_W, t % GRID_W
    half = HEAD_DIM // 2
    freqs = 1.0 / (ROPE_BASE ** (np.arange(0, half, 2) / half))
    ang_r = row[:, None] * freqs[None, :]
    ang_c = col[:, None] * freqs[None, :]
    ang = np.concatenate([ang_r, ang_r, ang_c, ang_c], axis=-1)
    cos = np.tile(np.cos(ang), (1, LANES // HEAD_DIM))
    sin = np.tile(np.sin(ang), (1, LANES // HEAD_DIM))
    sign = np.where((np.arange(LANES) % (HEAD_DIM // 2)) < (HEAD_DIM // 4), -1.0, 1.0)
    return jnp.asarray(cos, F32), jnp.asarray(sin * sign[None, :], F32)


def _segment_matrix():
    seg = np.arange(ATTN_WIDTH) // HEAD_DIM
    return jnp.asarray((seg[:, None] == seg[None, :]) / HEAD_DIM, BF16)


def _trunk(x, mod3, w, cache, *, mod_base, mod_stride, tm, tq, tk, moe_tm):
    batch, seq, _ = x.shape
    n = batch * seq
    x2d = x.reshape(n, D_MODEL)
    rope_tabs = _rope_tables(seq) if cache is not None else None
    pre = _pre_call(x2d, mod3[:, :, :2 * D_MODEL], w["g1"], w["win"], w["seg"], w["qg"], w["kg"],
                    rope_tabs, seq=seq, tm=tm, mod_base=mod_base, mod_stride=mod_stride,
                    emit_f32_kv=cache is None)
    p, q, k, v = pre[:4]
    attn = _attn_call(w["lam4"], w["subln"], q, k, v, cache, batch=batch, seq=seq, tq=tq, tk=tk)
    x1, h2_tiles, aff_t = _post_call(x2d, attn, p, mod3[:, :, 2 * D_MODEL:5 * D_MODEL], w["wout"],
                                     w["wbd"], w["pool_scale"], w["g2"], w["wr_t"], seq=seq, tm=tm,
                                     mod_base=mod_base, mod_stride=mod_stride)
    cap = CAPACITY_FACTOR * n // N_EXPERTS
    idx3, gates3 = _select_call(aff_t, cap=cap)
    idx_flat = idx3[:, :, 0].reshape(N_EXPERTS * cap)
    moe_tiles = _moe_call(idx_flat, gates3, w["wg"], w["wu"], w["wd"], h2_tiles, cap=cap, tm=moe_tm)
    y = _final_call(x1, moe_tiles, mod3, seq=seq, tm=tm, mod_base=mod_base, mod_stride=mod_stride)
    y = y.reshape(batch, seq, D_MODEL)
    if cache is None:
        return y, pre[4], pre[5]
    return y, None, None


def kernel(x_prompt, x_sample, cache_k, cache_v, c, c_ctx, norm1_g, norm2_g, w_ada, b_ada, w_in,
           q_norm_g, k_norm_g, lambda_q1, lambda_k1, lambda_q2, lambda_k2, subln_g, w_pool,
           pool_scale, w_out, w_router, w_gate, w_up, w_down):
    assert w_ada.shape[0] == 1, "single-layer stack"
    batch, seq, _ = x_prompt.shape
    dec_batch, dec_seq, _ = x_sample.shape

    pad = SUBLANES - 1 - dec_batch
    cvec = jnp.concatenate([c_ctx[None, :], c, jnp.zeros((pad, D_MODEL), F32)], axis=0)
    mod = _ada_call(cvec, w_ada[0], b_ada[0])
    mod3 = mod.reshape(SUBLANES, 1, 6 * D_MODEL)

    n_groups = w_pool.shape[1]
    grp = POOL_WIDTH // n_groups
    wbd = jnp.zeros((POOL_WIDTH, POOL_WIDTH), F32)
    for g in range(n_groups):
        wbd = wbd.at[g * grp:(g + 1) * grp, g * grp:(g + 1) * grp].set(w_pool[0, g])

    w = {
        "g1": norm1_g[0].reshape(1, D_MODEL),
        "g2": norm2_g[0].reshape(1, D_MODEL),
        "win": w_in[0].astype(BF16),
        "seg": _segment_matrix(),
        "qg": jnp.tile(q_norm_g[0], ATTN_WIDTH // HEAD_DIM).reshape(1, ATTN_WIDTH),
        "kg": jnp.tile(k_norm_g[0], ATTN_WIDTH // HEAD_DIM).reshape(1, ATTN_WIDTH),
        "lam4": jnp.stack([lambda_q1[0], lambda_k1[0], lambda_q2[0], lambda_k2[0]], axis=0),
        "subln": subln_g[0].reshape(1, V_DIM),
        "wbd": wbd.astype(BF16),
        "pool_scale": pool_scale[0].reshape(1, POOL_WIDTH),
        "wout": w_out[0].astype(BF16),
        "wr_t": w_router[0].T,
        "wg": w_gate[0].astype(BF16),
        "wu": w_up[0].astype(BF16),
        "wd": w_down[0].astype(BF16),
    }

    yp, k_ctx, v_ctx = _trunk(x_prompt, mod3, w, None, mod_base=0, mod_stride=0,
                              tm=256, tq=256, tk=256, moe_tm=256)
    past = cache_k.shape[2]
    cache = (cache_k[:, 0].reshape(dec_batch * past, ATTN_WIDTH).astype(BF16),
             cache_v[:, 0].reshape(dec_batch * past, ATTN_WIDTH).astype(BF16))
    ys, _, _ = _trunk(x_sample, mod3, w, cache, mod_base=1, mod_stride=1,
                      tm=256, tq=256, tk=512, moe_tm=256)
    ctx_k = k_ctx.reshape(batch, 1, seq, N_HEADS, 2, HEAD_DIM)
    ctx_v = v_ctx.reshape(batch, 1, seq, N_HEADS, V_DIM)
    return yp, ys, ctx_k, ctx_v
```

```python
import functools
import math

import numpy as np
import jax
import jax.numpy as jnp
from jax import lax
from jax.experimental import pallas as pl
from jax.experimental.pallas import tpu as pltpu

F32 = jnp.float32
BF16 = jnp.bfloat16

D_MODEL = 1024
POOL_WIDTH = 256
ATTN_WIDTH = 768
N_HEADS = 6
HEAD_DIM = 64
V_DIM = 128
IN_WIDTH = POOL_WIDTH + 3 * ATTN_WIDTH
N_EXPERTS = 16
CAPACITY_FACTOR = 2
GRID_W = 64
ROPE_BASE = 10000.0
EPS = 1e-6
LAMBDA_INIT = 0.8 - 0.6 * math.exp(-0.3 * 0)
LOG2E = math.log2(math.e)
V_EXT = V_DIM + 16

LANES = 128
SUBLANES = 8
ROW_TILES = D_MODEL // LANES
HALO = 16
VMEM_LIMIT = 56 * 1024 * 1024


def _cparams(sem, vmem=None):
    return pltpu.CompilerParams(dimension_semantics=sem, vmem_limit_bytes=vmem)


def _ada_kernel(c_ref, w_ref, b_ref, o_ref):
    c = c_ref[...]
    s = c * jax.nn.sigmoid(c)
    o_ref[...] = jnp.dot(s.astype(BF16), w_ref[...].astype(BF16),
                         preferred_element_type=F32) + b_ref[...]


def _ada_call(cvec, w_ada, b_ada):
    rows, d = cvec.shape
    n = w_ada.shape[1]
    bn = 1536
    return pl.pallas_call(
        _ada_kernel,
        grid=(n // bn,),
        in_specs=[pl.BlockSpec((rows, d), lambda j: (0, 0)),
                  pl.BlockSpec((d, bn), lambda j: (0, j)),
                  pl.BlockSpec((1, bn), lambda j: (0, j))],
        out_specs=pl.BlockSpec((rows, bn), lambda j: (0, j)),
        out_shape=jax.ShapeDtypeStruct((rows, n), F32),
        compiler_params=_cparams(("arbitrary",)),
        name="ada",
    )(cvec, w_ada, b_ada.reshape(1, n))


def _segment_mean_square(a, seg_ref):
    sq = a * a
    hi = sq.astype(BF16)
    lo = (sq - hi.astype(F32)).astype(BF16)
    seg = seg_ref[...]
    return (jnp.dot(hi, seg, preferred_element_type=F32)
            + jnp.dot(lo, seg, preferred_element_type=F32))


def _rope(a, cos, sin_signed, first_half):
    parts = []
    for h in range(a.shape[1] // LANES):
        blk = a[:, h * LANES:(h + 1) * LANES]
        fwd = pltpu.roll(blk, LANES - HEAD_DIM // 4, 1)
        bwd = pltpu.roll(blk, HEAD_DIM // 4, 1)
        parts.append(blk * cos + jnp.where(first_half, fwd, bwd) * sin_signed)
    return jnp.concatenate(parts, axis=1)


def _pre_kernel(*refs, rope, emit_f32_kv):
    x_ref, mod_ref, g1_ref, win_ref, seg_ref, qg_ref, kg_ref = refs[:7]
    pos = 7
    if rope:
        cos_ref, sin_ref = refs[pos:pos + 2]
        pos += 2
    p_ref, q_ref, k_ref, v_ref = refs[pos:pos + 4]
    pos += 4
    if emit_f32_kv:
        kf_ref, vf_ref = refs[pos:pos + 2]

    x = x_ref[...]
    mod = mod_ref[0]
    shift1 = mod[:, :D_MODEL]
    scale1 = mod[:, D_MODEL:2 * D_MODEL]
    ms = jnp.mean(x * x, axis=1, keepdims=True)
    h = x * lax.rsqrt(ms + EPS) * g1_ref[...] * (1.0 + scale1) + shift1
    z = jnp.dot(h.astype(BF16), win_ref[...], preferred_element_type=F32)

    p_ref[...] = z[:, :POOL_WIDTH]
    qz = z[:, POOL_WIDTH:POOL_WIDTH + ATTN_WIDTH]
    kz = z[:, POOL_WIDTH + ATTN_WIDTH:POOL_WIDTH + 2 * ATTN_WIDTH]
    vz = z[:, POOL_WIDTH + 2 * ATTN_WIDTH:]

    qn = qz * lax.rsqrt(_segment_mean_square(qz, seg_ref) + EPS) * qg_ref[...]
    kn = kz * lax.rsqrt(_segment_mean_square(kz, seg_ref) + EPS) * kg_ref[...]
    if rope:
        cos = cos_ref[...]
        sin_signed = sin_ref[...]
        lane = lax.broadcasted_iota(jnp.int32, cos.shape, 1)
        first_half = (lane % (HEAD_DIM // 2)) < (HEAD_DIM // 4)
        qn = _rope(qn, cos, sin_signed, first_half)
        kn = _rope(kn, cos, sin_signed, first_half)

    q_ref[...] = (qn * (LOG2E / math.sqrt(HEAD_DIM))).T.astype(BF16)
    k_ref[...] = kn.astype(BF16)
    vt = vz.T
    ones = jnp.ones((V_EXT - V_DIM, vt.shape[1]), BF16)
    for h in range(N_HEADS):
        v_ref[h * V_EXT:h * V_EXT + V_DIM, :] = vt[h * V_DIM:(h + 1) * V_DIM, :].astype(BF16)
        v_ref[h * V_EXT + V_DIM:(h + 1) * V_EXT, :] = ones
    if emit_f32_kv:
        kf_ref[...] = kn
        vf_ref[...] = vz


def _pre_call(x2d, mod3, g1, win_bf, seg, qg, kg, rope_tabs, *, seq, tm, mod_base, mod_stride,
              emit_f32_kv):
    n = x2d.shape[0]
    rope = rope_tabs is not None
    tiles_per_seq = seq // tm

    def mod_map(i):
        return (mod_base + mod_stride * (i // tiles_per_seq), 0, 0)

    in_specs = [
        pl.BlockSpec((tm, D_MODEL), lambda i: (i, 0)),
        pl.BlockSpec((1, 1, 2 * D_MODEL), mod_map),
        pl.BlockSpec((1, D_MODEL), lambda i: (0, 0)),
        pl.BlockSpec((D_MODEL, IN_WIDTH), lambda i: (0, 0)),
        pl.BlockSpec((ATTN_WIDTH, ATTN_WIDTH), lambda i: (0, 0)),
        pl.BlockSpec((1, ATTN_WIDTH), lambda i: (0, 0)),
        pl.BlockSpec((1, ATTN_WIDTH), lambda i: (0, 0)),
    ]
    args = [x2d, mod3, g1, win_bf, seg, qg, kg]
    if rope:
        in_specs += [pl.BlockSpec((tm, LANES), lambda i: (i % tiles_per_seq, 0))] * 2
        args += list(rope_tabs)
    out_shapes = [jax.ShapeDtypeStruct((n, POOL_WIDTH), F32)]
    out_specs = [pl.BlockSpec((tm, POOL_WIDTH), lambda i: (i, 0))]
    out_shapes += [jax.ShapeDtypeStruct((ATTN_WIDTH, n), BF16),
                   jax.ShapeDtypeStruct((n, ATTN_WIDTH), BF16),
                   jax.ShapeDtypeStruct((N_HEADS * V_EXT, n), BF16)]
    out_specs += [pl.BlockSpec((ATTN_WIDTH, tm), lambda i: (0, i)),
                  pl.BlockSpec((tm, ATTN_WIDTH), lambda i: (i, 0)),
                  pl.BlockSpec((N_HEADS * V_EXT, tm), lambda i: (0, i))]
    if emit_f32_kv:
        out_shapes += [jax.ShapeDtypeStruct((n, ATTN_WIDTH), F32)] * 2
        out_specs += [pl.BlockSpec((tm, ATTN_WIDTH), lambda i: (i, 0))] * 2
    return pl.pallas_call(
        functools.partial(_pre_kernel, rope=rope, emit_f32_kv=emit_f32_kv),
        grid=(n // tm,),
        in_specs=in_specs,
        out_specs=out_specs,
        out_shape=out_shapes,
        compiler_params=_cparams(("arbitrary",), VMEM_LIMIT),
        name="pre_rope" if rope else "pre",
    )(*args)


def _attn_kernel(*refs, has_cache, tq, tk, seq, cache_len):
    if has_cache:
        lam_ref, g_ref, q_ref, kc_ref, vc_ref, k_ref, v_ref, o_ref = refs
    else:
        lam_ref, g_ref, q_ref, k_ref, v_ref, o_ref = refs

    lv = lam_ref[...]
    lam = (jnp.exp(jnp.sum(lv[0:1] * lv[1:2], axis=1, keepdims=True))
           - jnp.exp(jnp.sum(lv[2:3] * lv[3:4], axis=1, keepdims=True)) + LAMBDA_INIT)

    qt = q_ref[...]
    row = lax.broadcasted_iota(jnp.int32, qt.shape, 0)
    zero = jnp.zeros_like(qt)
    q_sub = (jnp.where(row < HEAD_DIM, qt, zero), jnp.where(row >= HEAD_DIM, qt, zero))

    def scores(kb, q_one):
        return jnp.dot(kb, q_one, preferred_element_type=F32)

    def softmax_step(s, m):
        m_new = jnp.maximum(m, jnp.max(s, axis=0, keepdims=True))
        return m_new, jnp.exp2(m - m_new), jnp.exp2(s - m_new).astype(BF16)

    def accumulate(vb, p, alpha, acc):
        return alpha * acc + jnp.dot(vb, p, preferred_element_type=F32)

    chunks = []
    if has_cache:
        chunks += [(kc_ref, vc_ref, j) for j in range(cache_len // tk)]
    chunks += [(k_ref, v_ref, j) for j in range(seq // tk)]

    def keys(c):
        kr, _, j = chunks[c]
        return kr[j * tk:(j + 1) * tk, :]

    def values(c):
        _, vr, j = chunks[c]
        return vr[:, j * tk:(j + 1) * tk]

    m1 = m2 = jnp.full((1, tq), -jnp.inf, F32)
    acc1 = acc2 = jnp.zeros((V_EXT, tq), F32)
    s1 = scores(keys(0), q_sub[0])
    for c in range(len(chunks)):
        s2 = scores(keys(c), q_sub[1])
        m1, alpha1, p1 = softmax_step(s1, m1)
        acc1 = accumulate(values(c), p1, alpha1, acc1)
        if c + 1 < len(chunks):
            s1 = scores(keys(c + 1), q_sub[0])
        m2, alpha2, p2 = softmax_step(s2, m2)
        acc2 = accumulate(values(c), p2, alpha2, acc2)
    o = (acc1[:V_DIM] / acc1[V_DIM:V_DIM + 1]
         - lam * (acc2[:V_DIM] / acc2[V_DIM:V_DIM + 1]))
    y = (o * lax.rsqrt(jnp.mean(o * o, axis=0, keepdims=True) + EPS) * g_ref[...]
         * (1.0 - LAMBDA_INIT))
    o_ref[...] = y.T.astype(BF16)


def _attn_call(lam4, subln_col, qt, k, vt, cache, *, batch, seq, tq, tk):
    has_cache = cache is not None
    nq = seq // tq
    in_specs = [
        pl.BlockSpec((4, HEAD_DIM), lambda b, h, i: (0, 0)),
        pl.BlockSpec((V_DIM, 1), lambda b, h, i: (0, 0)),
        pl.BlockSpec((LANES, tq), lambda b, h, i: (h, b * nq + i)),
    ]
    args = [lam4, subln_col, qt]
    cache_len = 0
    if has_cache:
        kc, vct = cache
        cache_len = kc.shape[0] // batch
        in_specs += [pl.BlockSpec((cache_len, LANES), lambda b, h, i: (b, h)),
                     pl.BlockSpec((V_EXT, cache_len), lambda b, h, i: (h, b))]
        args += [kc, vct]
    in_specs += [pl.BlockSpec((seq, LANES), lambda b, h, i: (b, h)),
                 pl.BlockSpec((V_EXT, seq), lambda b, h, i: (h, b))]
    args += [k, vt]
    return pl.pallas_call(
        functools.partial(_attn_kernel, has_cache=has_cache, tq=tq, tk=tk, seq=seq,
                          cache_len=cache_len),
        grid=(batch, N_HEADS, nq),
        in_specs=in_specs,
        out_specs=pl.BlockSpec((tq, LANES), lambda b, h, i: (b * nq + i, h)),
        out_shape=jax.ShapeDtypeStruct((batch * seq, ATTN_WIDTH), BF16),
        compiler_params=_cparams(("arbitrary", "arbitrary", "arbitrary"), VMEM_LIMIT),
        name="attn_cache" if has_cache else "attn",
    )(*args)


def _post_kernel(x_ref, a_ref, p_ref, pp_ref, pn_ref, mod_ref, wout_ref, wbd_ref, ps_ref, g2_ref,
                 wr_ref, x1_ref, h2_ref, aff_ref, *, tm, seq):
    i = pl.program_id(0)
    tiles_per_seq = seq // tm
    ti = i % tiles_per_seq
    p = p_ref[...]
    prev = jnp.where(ti == 0, 0.0, pp_ref[...])
    nxt = jnp.where(ti == tiles_per_seq - 1, 0.0, pn_ref[...])
    ext = jnp.concatenate([prev, p, nxt], axis=0)
    n_ext = tm + 2 * HALO
    s2 = ext + pltpu.roll(ext, 1, 0)
    s4 = pltpu.roll(s2, 1, 0) + pltpu.roll(s2, n_ext - 1, 0)
    s8 = pltpu.roll(s4, 2, 0) + pltpu.roll(s4, n_ext - 2, 0)
    s16 = pltpu.roll(s8, 4, 0) + pltpu.roll(s8, n_ext - 4, 0)
    lane = lax.broadcasted_iota(jnp.int32, (tm, POOL_WIDTH), 1)
    grp = lane // (POOL_WIDTH // 4)
    win = jnp.where(grp == 0, s2[HALO:HALO + tm],
                    jnp.where(grp == 1, s4[HALO:HALO + tm],
                              jnp.where(grp == 2, s8[HALO:HALO + tm], s16[HALO:HALO + tm])))
    t = ti * tm + lax.broadcasted_iota(jnp.int32, (tm, POOL_WIDTH), 0)
    left = jnp.where(grp == 0, 1, jnp.where(grp == 1, 2, jnp.where(grp == 2, 4, 8)))
    lo = jnp.maximum(t - left, 0)
    hi = jnp.minimum(t + left - 1, seq - 1) + 1
    pooled = win / (hi - lo).astype(F32) - p
    pool = jnp.dot(pooled.astype(BF16), wbd_ref[...], preferred_element_type=F32) * ps_ref[...]

    cat = jnp.concatenate([pool.astype(BF16), a_ref[...]], axis=1)
    mix = jnp.dot(cat, wout_ref[...], preferred_element_type=F32)
    mod = mod_ref[0]
    gate1 = mod[:, 0:D_MODEL]
    shift2 = mod[:, D_MODEL:2 * D_MODEL]
    scale2 = mod[:, 2 * D_MODEL:3 * D_MODEL]
    x1 = x_ref[...] + gate1 * mix
    x1_ref[...] = x1
    ms = jnp.mean(x1 * x1, axis=1, keepdims=True)
    h2 = x1 * lax.rsqrt(ms + EPS) * g2_ref[...] * (1.0 + scale2) + shift2
    for s in range(ROW_TILES):
        h2_ref[pl.ds(s, tm, stride=ROW_TILES), :] = h2[:, s * LANES:(s + 1) * LANES]

    logits = lax.dot_general(wr_ref[...], h2, (((1,), (1,)), ((), ())),
                             precision=lax.Precision.HIGHEST, preferred_element_type=F32)
    e = jnp.exp(logits - jnp.max(logits, axis=0, keepdims=True))
    aff_ref[...] = e / jnp.sum(e, axis=0, keepdims=True)


def _post_call(x2d, attn, p, mod3, wout_bf, wbd_bf, pool_scale, g2, wr_t, *, seq, tm, mod_base,
               mod_stride):
    n = x2d.shape[0]
    tiles_per_seq = seq // tm
    halo_per_tile = tm // HALO
    n_halo = n // HALO

    def mod_map(i):
        return (mod_base + mod_stride * (i // tiles_per_seq), 0, 0)

    return pl.pallas_call(
        functools.partial(_post_kernel, tm=tm, seq=seq),
        grid=(n // tm,),
        in_specs=[
            pl.BlockSpec((tm, D_MODEL), lambda i: (i, 0)),
            pl.BlockSpec((tm, ATTN_WIDTH), lambda i: (i, 0)),
            pl.BlockSpec((tm, POOL_WIDTH), lambda i: (i, 0)),
            pl.BlockSpec((HALO, POOL_WIDTH), lambda i: (jnp.maximum(i * halo_per_tile - 1, 0), 0)),
            pl.BlockSpec((HALO, POOL_WIDTH),
                         lambda i: (jnp.minimum((i + 1) * halo_per_tile, n_halo - 1), 0)),
            pl.BlockSpec((1, 1, 3 * D_MODEL), lambda i: mod_map(i)[:2] + (0,)),
            pl.BlockSpec((D_MODEL, D_MODEL), lambda i: (0, 0)),
            pl.BlockSpec((POOL_WIDTH, POOL_WIDTH), lambda i: (0, 0)),
            pl.BlockSpec((1, POOL_WIDTH), lambda i: (0, 0)),
            pl.BlockSpec((1, D_MODEL), lambda i: (0, 0)),
            pl.BlockSpec((N_EXPERTS, D_MODEL), lambda i: (0, 0)),
        ],
        out_specs=[
            pl.BlockSpec((tm, D_MODEL), lambda i: (i, 0)),
            pl.BlockSpec((tm * ROW_TILES, LANES), lambda i: (i, 0)),
            pl.BlockSpec((N_EXPERTS, tm), lambda i: (0, i)),
        ],
        out_shape=[
            jax.ShapeDtypeStruct((n, D_MODEL), F32),
            jax.ShapeDtypeStruct((n * ROW_TILES, LANES), F32),
            jax.ShapeDtypeStruct((N_EXPERTS, n), F32),
        ],
        compiler_params=_cparams(("arbitrary",), VMEM_LIMIT),
        name="post",
    )(x2d, attn, p, p, p, mod3, wout_bf, wbd_bf, pool_scale, g2, wr_t)


def _cumsum_lanes(x_ref, out_ref, n):
    r = lax.broadcasted_iota(jnp.int32, (LANES, LANES), 0)
    c = lax.broadcasted_iota(jnp.int32, (LANES, LANES), 1)
    upper = jnp.where(r <= c, 1.0, 0.0).astype(BF16)
    off = jnp.zeros((x_ref.shape[0], 1), F32)
    for j in range(n // LANES):
        xc = x_ref[:, j * LANES:(j + 1) * LANES]
        out_ref[:, j * LANES:(j + 1) * LANES] = (
            jnp.dot(xc.astype(BF16), upper, preferred_element_type=F32) + off)
        off = off + jnp.sum(xc, axis=1, keepdims=True)


def _select_kernel(a_ref, idx_ref, gate_ref, mask_s, cum_s, val_s, cumb_s, valb_s, *, n, cap):
    a = a_ref[...]
    thr = jnp.zeros((N_EXPERTS, 1), jnp.int32)
    for bit in range(30, -1, -1):
        cand = thr | (1 << bit)
        cnt = jnp.sum(jnp.where(a >= pltpu.bitcast(cand, F32), 1.0, 0.0), axis=1, keepdims=True)
        thr = jnp.where(cnt >= cap, cand, thr)
    thr_f = pltpu.bitcast(thr, F32)
    above = jnp.where(a > thr_f, 1.0, 0.0)
    tied = jnp.where(a == thr_f, 1.0, 0.0)
    need = cap - jnp.sum(above, axis=1, keepdims=True)
    mask_s[...] = tied
    _cumsum_lanes(mask_s, cum_s, n)
    sel = above + tied * jnp.where(cum_s[...] <= need, 1.0, 0.0)
    mask_s[...] = sel
    _cumsum_lanes(mask_s, cum_s, n)
    cum_s[...] = jnp.where(sel > 0.0, cum_s[...], 0.0)
    val_s[...] = jnp.where(sel > 0.0, a, 0.0)

    lane_f = lax.broadcasted_iota(jnp.int32, (SUBLANES, LANES), 1).astype(F32)
    sub_f = lax.broadcasted_iota(jnp.int32, (SUBLANES, LANES), 0).astype(F32)

    def per_expert(e, _):
        cumb_s[...] = jnp.broadcast_to(cum_s[pl.ds(e, 1), :], (SUBLANES, n))
        valb_s[...] = jnp.broadcast_to(val_s[pl.ds(e, 1), :], (SUBLANES, n))

        def per_slot_tile(jt, _):
            want = sub_f + jnp.asarray(jt * SUBLANES + 1, F32)
            acc_i = jnp.zeros((SUBLANES, LANES), F32)
            acc_g = jnp.zeros((SUBLANES, LANES), F32)
            for c in range(n // LANES):
                hit = cumb_s[:, c * LANES:(c + 1) * LANES] == want
                acc_i = acc_i + jnp.where(hit, lane_f + float(c * LANES), 0.0)
                acc_g = acc_g + jnp.where(hit, valb_s[:, c * LANES:(c + 1) * LANES], 0.0)
            row0 = pl.multiple_of(jt * SUBLANES, SUBLANES)
            tok = jnp.sum(acc_i, axis=1, keepdims=True).astype(jnp.int32)
            idx_ref[e, pl.ds(row0, SUBLANES), :] = jnp.broadcast_to(tok, (SUBLANES, LANES))
            gate_ref[e, pl.ds(row0, SUBLANES), :] = jnp.broadcast_to(
                jnp.sum(acc_g, axis=1, keepdims=True), (SUBLANES, LANES))
            return 0

        lax.fori_loop(0, cap // SUBLANES, per_slot_tile, 0)
        return 0

    lax.fori_loop(0, N_EXPERTS, per_expert, 0)


def _select_call(aff_t, *, cap):
    n = aff_t.shape[1]
    return pl.pallas_call(
        functools.partial(_select_kernel, n=n, cap=cap),
        grid=(1,),
        in_specs=[pl.BlockSpec((N_EXPERTS, n), lambda i: (0, 0))],
        out_specs=[pl.BlockSpec((N_EXPERTS, cap, LANES), lambda i: (0, 0, 0))] * 2,
        out_shape=[jax.ShapeDtypeStruct((N_EXPERTS, cap, LANES), jnp.int32),
                   jax.ShapeDtypeStruct((N_EXPERTS, cap, LANES), F32)],
        scratch_shapes=[pltpu.VMEM((N_EXPERTS, n), F32), pltpu.VMEM((N_EXPERTS, n), F32),
                        pltpu.VMEM((N_EXPERTS, n), F32), pltpu.VMEM((SUBLANES, n), F32),
                        pltpu.VMEM((SUBLANES, n), F32)],
        compiler_params=_cparams(("arbitrary",), VMEM_LIMIT),
        name="select",
    )(aff_t)


def _moe_kernel(idx_ref, gate_ref, wg_ref, wu_ref, wd_ref, x_hbm, out_hbm,
                gbuf, ybuf, acc_ref, gsem, osem, *, tm, n_tiles_total):
    e = pl.program_id(0)
    t = pl.program_id(1)
    nt = pl.num_programs(1)
    step = e * nt + t
    slot = step % 2
    rows = tm * ROW_TILES

    def gather_copy(tile_step, k, dst_slot):
        tok = idx_ref[tile_step * tm + k]
        return pltpu.make_async_copy(
            x_hbm.at[pl.ds(pl.multiple_of(tok * ROW_TILES, ROW_TILES), ROW_TILES), :],
            gbuf.at[dst_slot, pl.ds(pl.multiple_of(k * ROW_TILES, ROW_TILES), ROW_TILES), :],
            gsem.at[dst_slot])

    def issue_gather(tile_step, dst_slot):
        def body(k, _):
            gather_copy(tile_step, k, dst_slot).start()
            return 0
        lax.fori_loop(0, tm, body, 0)

    @pl.when(step == 0)
    def _():
        acc_ref[...] = jnp.zeros_like(acc_ref)
        issue_gather(0, 0)

    pltpu.make_async_copy(x_hbm.at[pl.ds(0, rows), :], gbuf.at[slot], gsem.at[slot]).wait()

    @pl.when(step + 1 < n_tiles_total)
    def _():
        issue_gather(step + 1, 1 - slot)

    xe = jnp.concatenate(
        [gbuf[slot, pl.ds(s, tm, stride=ROW_TILES), :] for s in range(ROW_TILES)],
        axis=1).astype(BF16)
    g = jnp.dot(xe, wg_ref[...], preferred_element_type=F32)
    u = jnp.dot(xe, wu_ref[...], preferred_element_type=F32)
    hid = (g * jax.nn.sigmoid(g) * u).astype(BF16)
    y = jnp.dot(hid, wd_ref[...], preferred_element_type=F32)
    gate = gate_ref[...]
    for s in range(ROW_TILES):
        ybuf[pl.ds(s, tm, stride=ROW_TILES), :] = y[:, s * LANES:(s + 1) * LANES] * gate

    group = 8

    def rmw(kk, _):
        pending = []
        for r in range(group):
            k = kk * group + r
            tok = idx_ref[step * tm + k]
            off = pl.multiple_of(tok * ROW_TILES, ROW_TILES)
            src = pl.multiple_of(k * ROW_TILES, ROW_TILES)
            pending.append((off, acc_ref[pl.ds(off, ROW_TILES), :] + ybuf[pl.ds(src, ROW_TILES), :]))
        for off, val in pending:
            acc_ref[pl.ds(off, ROW_TILES), :] = val
        return 0

    lax.fori_loop(0, tm // group, rmw, 0)

    @pl.when(step == n_tiles_total - 1)
    def _():
        cp = pltpu.make_async_copy(acc_ref, out_hbm, osem)
        cp.start()
        cp.wait()


def _moe_call(idx_flat, gates3, wg_bf, wu_bf, wd_bf, h2_tiles, *, cap, tm):
    n_rows = h2_tiles.shape[0]
    nt = cap // tm
    grid_spec = pltpu.PrefetchScalarGridSpec(
        num_scalar_prefetch=1,
        grid=(N_EXPERTS, nt),
        in_specs=[
            pl.BlockSpec((None, tm, LANES), lambda e, t, idx: (e, t, 0)),
            pl.BlockSpec((None, D_MODEL, D_MODEL), lambda e, t, idx: (e, 0, 0)),
            pl.BlockSpec((None, D_MODEL, D_MODEL), lambda e, t, idx: (e, 0, 0)),
            pl.BlockSpec((None, D_MODEL, D_MODEL), lambda e, t, idx: (e, 0, 0)),
            pl.BlockSpec(memory_space=pl.ANY),
        ],
        out_specs=pl.BlockSpec(memory_space=pl.ANY),
        scratch_shapes=[
            pltpu.VMEM((2, tm * ROW_TILES, LANES), F32),
            pltpu.VMEM((tm * ROW_TILES, LANES), F32),
            pltpu.VMEM((n_rows, LANES), F32),
            pltpu.SemaphoreType.DMA((2,)),
            pltpu.SemaphoreType.DMA(()),
        ],
    )
    return pl.pallas_call(
        functools.partial(_moe_kernel, tm=tm, n_tiles_total=N_EXPERTS * nt),
        grid_spec=grid_spec,
        out_shape=jax.ShapeDtypeStruct((n_rows, LANES), F32),
        compiler_params=_cparams(("arbitrary", "arbitrary"), VMEM_LIMIT),
        name="moe",
    )(idx_flat, gates3, wg_bf, wu_bf, wd_bf, h2_tiles)


def _final_kernel(x1_ref, moe_ref, mod_ref, o_ref, *, tm):
    moe = jnp.concatenate(
        [moe_ref[pl.ds(s, tm, stride=ROW_TILES), :] for s in range(ROW_TILES)], axis=1)
    o_ref[...] = x1_ref[...] + mod_ref[0] * moe


def _final_call(x1, moe_tiles, mod3, *, seq, tm, mod_base, mod_stride):
    n = x1.shape[0]
    tiles_per_seq = seq // tm
    gate2_block = 5

    def mod_map(i):
        return (mod_base + mod_stride * (i // tiles_per_seq), 0, gate2_block)

    return pl.pallas_call(
        functools.partial(_final_kernel, tm=tm),
        grid=(n // tm,),
        in_specs=[pl.BlockSpec((tm, D_MODEL), lambda i: (i, 0)),
                  pl.BlockSpec((tm * ROW_TILES, LANES), lambda i: (i, 0)),
                  pl.BlockSpec((1, 1, D_MODEL), mod_map)],
        out_specs=pl.BlockSpec((tm, D_MODEL), lambda i: (i, 0)),
        out_shape=jax.ShapeDtypeStruct((n, D_MODEL), F32),
        compiler_params=_cparams(("arbitrary",)),
        name="final",
    )(x1, moe_tiles, mod3)


def _rope_tables(seq):
    t = np.arange(seq)
    row, col = t // GRID_W, t % GRID_W
    half = HEAD_DIM // 2
    freqs = 1.0 / (ROPE_BASE ** (np.arange(0, half, 2) / half))
    ang_r = row[:, None] * freqs[None, :]
    ang_c = col[:, None] * freqs[None, :]
    ang = np.concatenate([ang_r, ang_r, ang_c, ang_c], axis=-1)
    cos = np.tile(np.cos(ang), (1, LANES // HEAD_DIM))
    sin = np.tile(np.sin(ang), (1, LANES // HEAD_DIM))
    sign = np.where((np.arange(LANES) % (HEAD_DIM // 2)) < (HEAD_DIM // 4), -1.0, 1.0)
    return jnp.asarray(cos, F32), jnp.asarray(sin * sign[None, :], F32)


def _segment_matrix():
    seg = np.arange(ATTN_WIDTH) // HEAD_DIM
    return jnp.asarray((seg[:, None] == seg[None, :]) / HEAD_DIM, BF16)


def _trunk(x, mod3, w, cache, *, mod_base, mod_stride, tm, tq, tk, moe_tm):
    batch, seq, _ = x.shape
    n = batch * seq
    x2d = x.reshape(n, D_MODEL)
    rope_tabs = _rope_tables(seq) if cache is not None else None
    pre = _pre_call(x2d, mod3[:, :, :2 * D_MODEL], w["g1"], w["win"], w["seg"], w["qg"], w["kg"],
                    rope_tabs, seq=seq, tm=tm, mod_base=mod_base, mod_stride=mod_stride,
                    emit_f32_kv=cache is None)
    p, q, k, v = pre[:4]
    attn = _attn_call(w["lam4"], w["subln"], q, k, v, cache, batch=batch, seq=seq, tq=tq, tk=tk)
    x1, h2_tiles, aff_t = _post_call(x2d, attn, p, mod3[:, :, 2 * D_MODEL:5 * D_MODEL], w["wout"],
                                     w["wbd"], w["pool_scale"], w["g2"], w["wr_t"], seq=seq, tm=tm,
                                     mod_base=mod_base, mod_stride=mod_stride)
    cap = CAPACITY_FACTOR * n // N_EXPERTS
    idx3, gates3 = _select_call(aff_t, cap=cap)
    idx_flat = idx3[:, :, 0].reshape(N_EXPERTS * cap)
    moe_tiles = _moe_call(idx_flat, gates3, w["wg"], w["wu"], w["wd"], h2_tiles, cap=cap, tm=moe_tm)
    y = _final_call(x1, moe_tiles, mod3, seq=seq, tm=tm, mod_base=mod_base, mod_stride=mod_stride)
    y = y.reshape(batch, seq, D_MODEL)
    if cache is None:
        return y, pre[4], pre[5]
    return y, None, None


def kernel(x_prompt, x_sample, cache_k, cache_v, c, c_ctx, norm1_g, norm2_g, w_ada, b_ada, w_in,
           q_norm_g, k_norm_g, lambda_q1, lambda_k1, lambda_q2, lambda_k2, subln_g, w_pool,
           pool_scale, w_out, w_router, w_gate, w_up, w_down):
    assert w_ada.shape[0] == 1, "single-layer stack"
    batch, seq, _ = x_prompt.shape
    dec_batch, dec_seq, _ = x_sample.shape

    pad = SUBLANES - 1 - dec_batch
    cvec = jnp.concatenate([c_ctx[None, :], c, jnp.zeros((pad, D_MODEL), F32)], axis=0)
    mod = _ada_call(cvec, w_ada[0], b_ada[0])
    mod3 = mod.reshape(SUBLANES, 1, 6 * D_MODEL)

    n_groups = w_pool.shape[1]
    grp = POOL_WIDTH // n_groups
    wbd = jnp.zeros((POOL_WIDTH, POOL_WIDTH), F32)
    for g in range(n_groups):
        wbd = wbd.at[g * grp:(g + 1) * grp, g * grp:(g + 1) * grp].set(w_pool[0, g])

    w = {
        "g1": norm1_g[0].reshape(1, D_MODEL),
        "g2": norm2_g[0].reshape(1, D_MODEL),
        "win": w_in[0].astype(BF16),
        "seg": _segment_matrix(),
        "qg": jnp.tile(q_norm_g[0], ATTN_WIDTH // HEAD_DIM).reshape(1, ATTN_WIDTH),
        "kg": jnp.tile(k_norm_g[0], ATTN_WIDTH // HEAD_DIM).reshape(1, ATTN_WIDTH),
        "lam4": jnp.stack([lambda_q1[0], lambda_k1[0], lambda_q2[0], lambda_k2[0]], axis=0),
        "subln": subln_g[0].reshape(V_DIM, 1),
        "wbd": wbd.astype(BF16),
        "pool_scale": pool_scale[0].reshape(1, POOL_WIDTH),
        "wout": w_out[0].astype(BF16),
        "wr_t": w_router[0].T,
        "wg": w_gate[0].astype(BF16),
        "wu": w_up[0].astype(BF16),
        "wd": w_down[0].astype(BF16),
    }

    yp, k_ctx, v_ctx = _trunk(x_prompt, mod3, w, None, mod_base=0, mod_stride=0,
                              tm=256, tq=256, tk=256, moe_tm=256)
    past = cache_k.shape[2]
    cv = cache_v[:, 0].reshape(dec_batch * past, N_HEADS, V_DIM).transpose(1, 2, 0).astype(BF16)
    cv = jnp.concatenate([cv, jnp.ones((N_HEADS, V_EXT - V_DIM, dec_batch * past), BF16)], axis=1)
    cache = (cache_k[:, 0].reshape(dec_batch * past, ATTN_WIDTH).astype(BF16),
             cv.reshape(N_HEADS * V_EXT, dec_batch * past))
    ys, _, _ = _trunk(x_sample, mod3, w, cache, mod_base=1, mod_stride=1,
                      tm=256, tq=256, tk=512, moe_tm=256)
    ctx_k = k_ctx.reshape(batch, 1, seq, N_HEADS, 2, HEAD_DIM)
    ctx_v = v_ctx.reshape(batch, 1, seq, N_HEADS, V_DIM)
    return yp, ys, ctx_k, ctx_v
```

```python
import functools
import math

import numpy as np
import jax
import jax.numpy as jnp
from jax import lax
from jax.experimental import pallas as pl
from jax.experimental.pallas import tpu as pltpu

F32 = jnp.float32
BF16 = jnp.bfloat16

D_MODEL = 1024
POOL_WIDTH = 256
ATTN_WIDTH = 768
N_HEADS = 6
HEAD_DIM = 64
V_DIM = 128
IN_WIDTH = POOL_WIDTH + 3 * ATTN_WIDTH
N_EXPERTS = 16
CAPACITY_FACTOR = 2
GRID_W = 64
ROPE_BASE = 10000.0
EPS = 1e-6
LAMBDA_INIT = 0.8 - 0.6 * math.exp(-0.3 * 0)
LOG2E = math.log2(math.e)
V_EXT = V_DIM + 16

LANES = 128
SUBLANES = 8
ROW_TILES = D_MODEL // LANES
HALO = 16
VMEM_LIMIT = 56 * 1024 * 1024


def _cparams(sem, vmem=None):
    return pltpu.CompilerParams(dimension_semantics=sem, vmem_limit_bytes=vmem)


def _ada_kernel(c_ref, w_ref, b_ref, o_ref):
    c = c_ref[...]
    s = c * jax.nn.sigmoid(c)
    o_ref[...] = jnp.dot(s.astype(BF16), w_ref[...].astype(BF16),
                         preferred_element_type=F32) + b_ref[...]


def _ada_call(cvec, w_ada, b_ada):
    rows, d = cvec.shape
    n = w_ada.shape[1]
    bn = 1536
    return pl.pallas_call(
        _ada_kernel,
        grid=(n // bn,),
        in_specs=[pl.BlockSpec((rows, d), lambda j: (0, 0)),
                  pl.BlockSpec((d, bn), lambda j: (0, j)),
                  pl.BlockSpec((1, bn), lambda j: (0, j))],
        out_specs=pl.BlockSpec((rows, bn), lambda j: (0, j)),
        out_shape=jax.ShapeDtypeStruct((rows, n), F32),
        compiler_params=_cparams(("arbitrary",)),
        name="ada",
    )(cvec, w_ada, b_ada.reshape(1, n))


def _segment_mean_square(a, seg_ref):
    return jnp.dot((a * a).astype(BF16), seg_ref[...], preferred_element_type=F32)


def _rope(a, cos, sin_signed, first_half):
    parts = []
    for h in range(a.shape[1] // LANES):
        blk = a[:, h * LANES:(h + 1) * LANES]
        fwd = pltpu.roll(blk, LANES - HEAD_DIM // 4, 1)
        bwd = pltpu.roll(blk, HEAD_DIM // 4, 1)
        parts.append(blk * cos + jnp.where(first_half, fwd, bwd) * sin_signed)
    return jnp.concatenate(parts, axis=1)


def _pre_kernel(*refs, rope, emit_f32_kv):
    x_ref, mod_ref, g1_ref, win_ref, seg_ref, qg_ref, kg_ref = refs[:7]
    pos = 7
    if rope:
        cos_ref, sin_ref = refs[pos:pos + 2]
        pos += 2
    p_ref, q_ref, k_ref, v_ref = refs[pos:pos + 4]
    pos += 4
    if emit_f32_kv:
        kf_ref, vf_ref = refs[pos:pos + 2]

    x = x_ref[...]
    mod = mod_ref[0]
    shift1 = mod[:, :D_MODEL]
    scale1 = mod[:, D_MODEL:2 * D_MODEL]
    ms = jnp.mean(x * x, axis=1, keepdims=True)
    h = x * lax.rsqrt(ms + EPS) * g1_ref[...] * (1.0 + scale1) + shift1
    z = jnp.dot(h.astype(BF16), win_ref[...], preferred_element_type=F32)

    p_ref[...] = z[:, :POOL_WIDTH]
    qz = z[:, POOL_WIDTH:POOL_WIDTH + ATTN_WIDTH]
    kz = z[:, POOL_WIDTH + ATTN_WIDTH:POOL_WIDTH + 2 * ATTN_WIDTH]
    vz = z[:, POOL_WIDTH + 2 * ATTN_WIDTH:]

    qn = qz * lax.rsqrt(_segment_mean_square(qz, seg_ref) + EPS) * qg_ref[...]
    kn = kz * lax.rsqrt(_segment_mean_square(kz, seg_ref) + EPS) * kg_ref[...]
    if rope:
        cos = cos_ref[...]
        sin_signed = sin_ref[...]
        lane = lax.broadcasted_iota(jnp.int32, cos.shape, 1)
        first_half = (lane % (HEAD_DIM // 2)) < (HEAD_DIM // 4)
        qn = _rope(qn, cos, sin_signed, first_half)
        kn = _rope(kn, cos, sin_signed, first_half)

    q_ref[...] = (qn * (LOG2E / math.sqrt(HEAD_DIM))).T.astype(BF16)
    k_ref[...] = kn.astype(BF16)
    vt = vz.T
    ones = jnp.ones((V_EXT - V_DIM, vt.shape[1]), BF16)
    for h in range(N_HEADS):
        v_ref[h * V_EXT:h * V_EXT + V_DIM, :] = vt[h * V_DIM:(h + 1) * V_DIM, :].astype(BF16)
        v_ref[h * V_EXT + V_DIM:(h + 1) * V_EXT, :] = ones
    if emit_f32_kv:
        kf_ref[...] = kn
        vf_ref[...] = vz


def _pre_call(x2d, mod3, g1, win_bf, seg, qg, kg, rope_tabs, *, seq, tm, mod_base, mod_stride,
              emit_f32_kv):
    n = x2d.shape[0]
    rope = rope_tabs is not None
    tiles_per_seq = seq // tm

    def mod_map(i):
        return (mod_base + mod_stride * (i // tiles_per_seq), 0, 0)

    in_specs = [
        pl.BlockSpec((tm, D_MODEL), lambda i: (i, 0)),
        pl.BlockSpec((1, 1, 2 * D_MODEL), mod_map),
        pl.BlockSpec((1, D_MODEL), lambda i: (0, 0)),
        pl.BlockSpec((D_MODEL, IN_WIDTH), lambda i: (0, 0)),
        pl.BlockSpec((ATTN_WIDTH, ATTN_WIDTH), lambda i: (0, 0)),
        pl.BlockSpec((1, ATTN_WIDTH), lambda i: (0, 0)),
        pl.BlockSpec((1, ATTN_WIDTH), lambda i: (0, 0)),
    ]
    args = [x2d, mod3, g1, win_bf, seg, qg, kg]
    if rope:
        in_specs += [pl.BlockSpec((tm, LANES), lambda i: (i % tiles_per_seq, 0))] * 2
        args += list(rope_tabs)
    out_shapes = [jax.ShapeDtypeStruct((n, POOL_WIDTH), F32)]
    out_specs = [pl.BlockSpec((tm, POOL_WIDTH), lambda i: (i, 0))]
    out_shapes += [jax.ShapeDtypeStruct((ATTN_WIDTH, n), BF16),
                   jax.ShapeDtypeStruct((n, ATTN_WIDTH), BF16),
                   jax.ShapeDtypeStruct((N_HEADS * V_EXT, n), BF16)]
    out_specs += [pl.BlockSpec((ATTN_WIDTH, tm), lambda i: (0, i)),
                  pl.BlockSpec((tm, ATTN_WIDTH), lambda i: (i, 0)),
                  pl.BlockSpec((N_HEADS * V_EXT, tm), lambda i: (0, i))]
    if emit_f32_kv:
        out_shapes += [jax.ShapeDtypeStruct((n, ATTN_WIDTH), F32)] * 2
        out_specs += [pl.BlockSpec((tm, ATTN_WIDTH), lambda i: (i, 0))] * 2
    return pl.pallas_call(
        functools.partial(_pre_kernel, rope=rope, emit_f32_kv=emit_f32_kv),
        grid=(n // tm,),
        in_specs=in_specs,
        out_specs=out_specs,
        out_shape=out_shapes,
        compiler_params=_cparams(("arbitrary",), VMEM_LIMIT),
        name="pre_rope" if rope else "pre",
    )(*args)


def _attn_kernel(*refs, has_cache, tq, tk, seq, cache_len):
    if has_cache:
        lam_ref, g_ref, q_ref, kc_ref, vc_ref, k_ref, v_ref, o_ref = refs
    else:
        lam_ref, g_ref, q_ref, k_ref, v_ref, o_ref = refs

    lv = lam_ref[...]
    lam = (jnp.exp(jnp.sum(lv[0:1] * lv[1:2], axis=1, keepdims=True))
           - jnp.exp(jnp.sum(lv[2:3] * lv[3:4], axis=1, keepdims=True)) + LAMBDA_INIT)

    qt = q_ref[...]
    row = lax.broadcasted_iota(jnp.int32, qt.shape, 0)
    zero = jnp.zeros_like(qt)
    q_sub = (jnp.where(row < HEAD_DIM, qt, zero), jnp.where(row >= HEAD_DIM, qt, zero))

    def scores(kb, q_one):
        return jnp.dot(kb, q_one, preferred_element_type=F32)

    def softmax_step(s, m):
        m_new = jnp.maximum(m, jnp.max(s, axis=0, keepdims=True))
        return m_new, jnp.exp2(m - m_new), jnp.exp2(s - m_new).astype(BF16)

    def accumulate(vb, p, alpha, acc):
        return alpha * acc + jnp.dot(vb, p, preferred_element_type=F32)

    chunks = []
    if has_cache:
        chunks += [(kc_ref, vc_ref, j) for j in range(cache_len // tk)]
    chunks += [(k_ref, v_ref, j) for j in range(seq // tk)]

    def keys(c):
        kr, _, j = chunks[c]
        return kr[j * tk:(j + 1) * tk, :]

    def values(c):
        _, vr, j = chunks[c]
        return vr[:, j * tk:(j + 1) * tk]

    m1 = m2 = jnp.full((1, tq), -jnp.inf, F32)
    acc1 = acc2 = jnp.zeros((V_EXT, tq), F32)
    s1 = scores(keys(0), q_sub[0])
    for c in range(len(chunks)):
        s2 = scores(keys(c), q_sub[1])
        m1, alpha1, p1 = softmax_step(s1, m1)
        acc1 = accumulate(values(c), p1, alpha1, acc1)
        if c + 1 < len(chunks):
            s1 = scores(keys(c + 1), q_sub[0])
        m2, alpha2, p2 = softmax_step(s2, m2)
        acc2 = accumulate(values(c), p2, alpha2, acc2)
    o = (acc1[:V_DIM] / acc1[V_DIM:V_DIM + 1]
         - lam * (acc2[:V_DIM] / acc2[V_DIM:V_DIM + 1]))
    y = (o * lax.rsqrt(jnp.mean(o * o, axis=0, keepdims=True) + EPS) * g_ref[...]
         * (1.0 - LAMBDA_INIT))
    o_ref[...] = y.T.astype(BF16)


def _attn_call(lam4, subln_col, qt, k, vt, cache, *, batch, seq, tq, tk):
    has_cache = cache is not None
    nq = seq // tq
    in_specs = [
        pl.BlockSpec((4, HEAD_DIM), lambda b, h, i: (0, 0)),
        pl.BlockSpec((V_DIM, 1), lambda b, h, i: (0, 0)),
        pl.BlockSpec((LANES, tq), lambda b, h, i: (h, b * nq + i)),
    ]
    args = [lam4, subln_col, qt]
    cache_len = 0
    if has_cache:
        kc, vct = cache
        cache_len = kc.shape[0] // batch
        in_specs += [pl.BlockSpec((cache_len, LANES), lambda b, h, i: (b, h)),
                     pl.BlockSpec((V_EXT, cache_len), lambda b, h, i: (h, b))]
        args += [kc, vct]
    in_specs += [pl.BlockSpec((seq, LANES), lambda b, h, i: (b, h)),
                 pl.BlockSpec((V_EXT, seq), lambda b, h, i: (h, b))]
    args += [k, vt]
    return pl.pallas_call(
        functools.partial(_attn_kernel, has_cache=has_cache, tq=tq, tk=tk, seq=seq,
                          cache_len=cache_len),
        grid=(batch, N_HEADS, nq),
        in_specs=in_specs,
        out_specs=pl.BlockSpec((tq, LANES), lambda b, h, i: (b * nq + i, h)),
        out_shape=jax.ShapeDtypeStruct((batch * seq, ATTN_WIDTH), BF16),
        compiler_params=_cparams(("arbitrary", "arbitrary", "arbitrary"), VMEM_LIMIT),
        name="attn_cache" if has_cache else "attn",
    )(*args)


def _post_kernel(x_ref, a_ref, p_ref, pp_ref, pn_ref, mod_ref, wout_ref, wbd_ref, ps_ref, g2_ref,
                 wr_ref, x1_ref, h2_ref, aff_ref, affc_ref, *, tm, seq):
    i = pl.program_id(0)
    tiles_per_seq = seq // tm
    ti = i % tiles_per_seq
    p = p_ref[...]
    prev = jnp.where(ti == 0, 0.0, pp_ref[...])
    nxt = jnp.where(ti == tiles_per_seq - 1, 0.0, pn_ref[...])
    ext = jnp.concatenate([prev, p, nxt], axis=0)
    n_ext = tm + 2 * HALO
    s2 = ext + pltpu.roll(ext, 1, 0)
    s4 = pltpu.roll(s2, 1, 0) + pltpu.roll(s2, n_ext - 1, 0)
    s8 = pltpu.roll(s4, 2, 0) + pltpu.roll(s4, n_ext - 2, 0)
    s16 = pltpu.roll(s8, 4, 0) + pltpu.roll(s8, n_ext - 4, 0)
    lane = lax.broadcasted_iota(jnp.int32, (tm, POOL_WIDTH), 1)
    grp = lane // (POOL_WIDTH // 4)
    win = jnp.where(grp == 0, s2[HALO:HALO + tm],
                    jnp.where(grp == 1, s4[HALO:HALO + tm],
                              jnp.where(grp == 2, s8[HALO:HALO + tm], s16[HALO:HALO + tm])))
    t = ti * tm + lax.broadcasted_iota(jnp.int32, (tm, POOL_WIDTH), 0)
    left = jnp.where(grp == 0, 1, jnp.where(grp == 1, 2, jnp.where(grp == 2, 4, 8)))
    lo = jnp.maximum(t - left, 0)
    hi = jnp.minimum(t + left - 1, seq - 1) + 1
    pooled = win / (hi - lo).astype(F32) - p
    pool = jnp.dot(pooled.astype(BF16), wbd_ref[...], preferred_element_type=F32) * ps_ref[...]

    cat = jnp.concatenate([pool.astype(BF16), a_ref[...]], axis=1)
    mix = jnp.dot(cat, wout_ref[...], preferred_element_type=F32)
    mod = mod_ref[0]
    gate1 = mod[:, 0:D_MODEL]
    shift2 = mod[:, D_MODEL:2 * D_MODEL]
    scale2 = mod[:, 2 * D_MODEL:3 * D_MODEL]
    x1 = x_ref[...] + gate1 * mix
    x1_ref[...] = x1
    ms = jnp.mean(x1 * x1, axis=1, keepdims=True)
    h2 = x1 * lax.rsqrt(ms + EPS) * g2_ref[...] * (1.0 + scale2) + shift2
    h2_ref[...] = h2

    logits = lax.dot_general(wr_ref[...], h2, (((1,), (1,)), ((), ())),
                             precision=lax.Precision.HIGHEST, preferred_element_type=F32)
    e = jnp.exp(logits - jnp.max(logits, axis=0, keepdims=True))
    aff = e / jnp.sum(e, axis=0, keepdims=True)
    aff_ref[...] = aff
    for c in range(tm // LANES):
        affc_ref[c * N_EXPERTS:(c + 1) * N_EXPERTS, :] = aff[:, c * LANES:(c + 1) * LANES]


def _post_call(x2d, attn, p, mod3, wout_bf, wbd_bf, pool_scale, g2, wr_t, *, seq, tm, mod_base,
               mod_stride):
    n = x2d.shape[0]
    tiles_per_seq = seq // tm
    halo_per_tile = tm // HALO
    n_halo = n // HALO

    def mod_map(i):
        return (mod_base + mod_stride * (i // tiles_per_seq), 0, 0)

    return pl.pallas_call(
        functools.partial(_post_kernel, tm=tm, seq=seq),
        grid=(n // tm,),
        in_specs=[
            pl.BlockSpec((tm, D_MODEL), lambda i: (i, 0)),
            pl.BlockSpec((tm, ATTN_WIDTH), lambda i: (i, 0)),
            pl.BlockSpec((tm, POOL_WIDTH), lambda i: (i, 0)),
            pl.BlockSpec((HALO, POOL_WIDTH), lambda i: (jnp.maximum(i * halo_per_tile - 1, 0), 0)),
            pl.BlockSpec((HALO, POOL_WIDTH),
                         lambda i: (jnp.minimum((i + 1) * halo_per_tile, n_halo - 1), 0)),
            pl.BlockSpec((1, 1, 3 * D_MODEL), lambda i: mod_map(i)[:2] + (0,)),
            pl.BlockSpec((D_MODEL, D_MODEL), lambda i: (0, 0)),
            pl.BlockSpec((POOL_WIDTH, POOL_WIDTH), lambda i: (0, 0)),
            pl.BlockSpec((1, POOL_WIDTH), lambda i: (0, 0)),
            pl.BlockSpec((1, D_MODEL), lambda i: (0, 0)),
            pl.BlockSpec((N_EXPERTS, D_MODEL), lambda i: (0, 0)),
        ],
        out_specs=[
            pl.BlockSpec((tm, D_MODEL), lambda i: (i, 0)),
            pl.BlockSpec((tm, D_MODEL), lambda i: (i, 0)),
            pl.BlockSpec((N_EXPERTS, tm), lambda i: (0, i)),
            pl.BlockSpec((tm // LANES * N_EXPERTS, LANES), lambda i: (i, 0)),
        ],
        out_shape=[
            jax.ShapeDtypeStruct((n, D_MODEL), F32),
            jax.ShapeDtypeStruct((n, D_MODEL), F32),
            jax.ShapeDtypeStruct((N_EXPERTS, n), F32),
            jax.ShapeDtypeStruct((n // LANES * N_EXPERTS, LANES), F32),
        ],
        compiler_params=_cparams(("arbitrary",), VMEM_LIMIT),
        name="post",
    )(x2d, attn, p, p, p, mod3, wout_bf, wbd_bf, pool_scale, g2, wr_t)


def _select_kernel(a_ref, ac_ref, idx_ref, gate_ref, *, n, cap):
    nc = n // LANES
    a = a_ref[...]
    thr = jnp.zeros((N_EXPERTS, 1), jnp.int32)
    for bit in range(30, -1, -1):
        cand = thr | (1 << bit)
        cnt = jnp.sum(jnp.where(a >= pltpu.bitcast(cand, F32), 1.0, 0.0), axis=1, keepdims=True)
        thr = jnp.where(cnt >= cap, cand, thr)
    thr_all = pltpu.bitcast(thr, F32)
    need_all = cap - jnp.sum(jnp.where(a > thr_all, 1.0, 0.0), axis=1, keepdims=True)

    r = lax.broadcasted_iota(jnp.int32, (LANES, LANES), 0)
    c = lax.broadcasted_iota(jnp.int32, (LANES, LANES), 1)
    upper = jnp.where(r <= c, 1.0, 0.0).astype(BF16)
    lower = jnp.where(c < r, 1.0, 0.0).astype(BF16)
    row_valid = r < nc
    chunk_col = lax.broadcasted_iota(jnp.int32, (LANES, 1), 0).astype(F32)
    slot = lax.broadcasted_iota(jnp.int32, (1, cap), 1).astype(F32)

    def lane_counts(mask):
        local = jnp.dot(mask.astype(BF16), upper, preferred_element_type=F32)
        total = jnp.broadcast_to(local[:, LANES - 1:LANES], (LANES, LANES))
        before = jnp.dot(lower, total.astype(BF16), preferred_element_type=F32)
        return local, total, before

    for e in range(N_EXPERTS):
        av = ac_ref[pl.ds(e, nc, stride=N_EXPERTS), :]
        if nc < LANES:
            av = jnp.concatenate([av, jnp.zeros((LANES - nc, LANES), F32)], axis=0)
        thr_e = thr_all[e:e + 1, :]
        above = jnp.where(row_valid & (av > thr_e), 1.0, 0.0)
        tied = jnp.where(row_valid & (av == thr_e), 1.0, 0.0)
        t_local, _, t_before = lane_counts(tied)
        sel = above + tied * jnp.where(t_local + t_before <= need_all[e:e + 1, :], 1.0, 0.0)
        s_local, s_total, s_before = lane_counts(sel)
        rank = jnp.where(sel > 0.0, s_local, 0.0)

        start = s_before[:, 0:1]
        stop = start + s_total[:, 0:1]
        onehot = jnp.where((slot >= start) & (slot < stop), 1.0, 0.0)
        chunk_of_slot = jnp.sum(onehot * chunk_col, axis=0, keepdims=True)
        start_of_slot = jnp.sum(onehot * start, axis=0, keepdims=True)

        at = av.T
        hi = at.astype(BF16)
        rest = at - hi.astype(F32)
        mid = rest.astype(BF16)
        lo = (rest - mid.astype(F32)).astype(BF16)
        lhs = jnp.concatenate([rank.T.astype(BF16), hi, mid, lo], axis=0)
        picked = jnp.dot(lhs, onehot.astype(BF16), preferred_element_type=F32)
        rank_p = picked[0:LANES]
        aff_p = picked[LANES:2 * LANES] + picked[2 * LANES:3 * LANES] + picked[3 * LANES:]
        hit = rank_p == (slot - start_of_slot + 1.0)
        lane_of_slot = jnp.sum(jnp.where(hit, chunk_col, 0.0), axis=0, keepdims=True)
        idx_ref[e:e + 1, :] = (chunk_of_slot * LANES + lane_of_slot).astype(jnp.int32)
        gate_ref[e:e + 1, :] = jnp.sum(jnp.where(hit, aff_p, 0.0), axis=0, keepdims=True)


def _select_call(aff_t, aff_c, *, cap):
    n = aff_t.shape[1]
    assert n % LANES == 0 and n // LANES <= LANES and cap % LANES == 0
    return pl.pallas_call(
        functools.partial(_select_kernel, n=n, cap=cap),
        grid=(1,),
        in_specs=[pl.BlockSpec((N_EXPERTS, n), lambda i: (0, 0)),
                  pl.BlockSpec(aff_c.shape, lambda i: (0, 0))],
        out_specs=[pl.BlockSpec((N_EXPERTS, cap), lambda i: (0, 0))] * 2,
        out_shape=[jax.ShapeDtypeStruct((N_EXPERTS, cap), jnp.int32),
                   jax.ShapeDtypeStruct((N_EXPERTS, cap), F32)],
        compiler_params=_cparams(("arbitrary",), VMEM_LIMIT),
        name="select",
    )(aff_t, aff_c)


def _moe_kernel(idx_ref, gate_ref, wg_ref, wu_ref, wd_ref, x_hbm, out_hbm,
                gbuf, ybuf, acc_ref, gsem, osem, *, tm, n_tiles_total):
    e = pl.program_id(0)
    t = pl.program_id(1)
    nt = pl.num_programs(1)
    step = e * nt + t
    slot = step % 2
    group = 8

    def gather_copy(tile_step, k, dst_slot):
        tok = idx_ref[tile_step * tm + k]
        return pltpu.make_async_copy(x_hbm.at[pl.ds(tok, 1), :],
                                     gbuf.at[dst_slot, pl.ds(k, 1), :], gsem.at[dst_slot])

    def issue_gather(tile_step, dst_slot):
        def body(kk, _):
            for r in range(group):
                gather_copy(tile_step, kk * group + r, dst_slot).start()
            return 0
        lax.fori_loop(0, tm // group, body, 0)

    @pl.when(step == 0)
    def _():
        acc_ref[...] = jnp.zeros_like(acc_ref)
        issue_gather(0, 0)

    pltpu.make_async_copy(x_hbm.at[pl.ds(0, tm), :], gbuf.at[slot], gsem.at[slot]).wait()

    @pl.when(step + 1 < n_tiles_total)
    def _():
        issue_gather(step + 1, 1 - slot)

    xe = gbuf[slot].astype(BF16)
    g = jnp.dot(xe, wg_ref[...], preferred_element_type=F32)
    u = jnp.dot(xe, wu_ref[...], preferred_element_type=F32)
    hid = (g * jax.nn.sigmoid(g) * u).astype(BF16)
    y = jnp.dot(hid, wd_ref[...], preferred_element_type=F32)
    gate = jnp.broadcast_to(gate_ref[...], (LANES, tm)).T
    for s in range(ROW_TILES):
        ybuf[pl.ds(s, tm, stride=ROW_TILES), :] = y[:, s * LANES:(s + 1) * LANES] * gate

    def rmw(kk, _):
        pending = []
        for r in range(group):
            k = kk * group + r
            tok = idx_ref[step * tm + k]
            off = pl.multiple_of(tok * ROW_TILES, ROW_TILES)
            src = pl.multiple_of(k * ROW_TILES, ROW_TILES)
            pending.append((off, acc_ref[pl.ds(off, ROW_TILES), :] + ybuf[pl.ds(src, ROW_TILES), :]))
        for off, val in pending:
            acc_ref[pl.ds(off, ROW_TILES), :] = val
        return 0

    lax.fori_loop(0, tm // group, rmw, 0)

    @pl.when(step == n_tiles_total - 1)
    def _():
        cp = pltpu.make_async_copy(acc_ref, out_hbm, osem)
        cp.start()
        cp.wait()


def _moe_call(idx_flat, gates3, wg_bf, wu_bf, wd_bf, h2, *, cap, tm):
    n_rows = h2.shape[0] * ROW_TILES
    nt = cap // tm
    grid_spec = pltpu.PrefetchScalarGridSpec(
        num_scalar_prefetch=1,
        grid=(N_EXPERTS, nt),
        in_specs=[
            pl.BlockSpec((None, 1, tm), lambda e, t, idx: (e * nt + t, 0, 0)),
            pl.BlockSpec((None, D_MODEL, D_MODEL), lambda e, t, idx: (e, 0, 0)),
            pl.BlockSpec((None, D_MODEL, D_MODEL), lambda e, t, idx: (e, 0, 0)),
            pl.BlockSpec((None, D_MODEL, D_MODEL), lambda e, t, idx: (e, 0, 0)),
            pl.BlockSpec(memory_space=pl.ANY),
        ],
        out_specs=pl.BlockSpec(memory_space=pl.ANY),
        scratch_shapes=[
            pltpu.VMEM((2, tm, D_MODEL), F32),
            pltpu.VMEM((tm * ROW_TILES, LANES), F32),
            pltpu.VMEM((n_rows, LANES), F32),
            pltpu.SemaphoreType.DMA((2,)),
            pltpu.SemaphoreType.DMA(()),
        ],
    )
    return pl.pallas_call(
        functools.partial(_moe_kernel, tm=tm, n_tiles_total=N_EXPERTS * nt),
        grid_spec=grid_spec,
        out_shape=jax.ShapeDtypeStruct((n_rows, LANES), F32),
        compiler_params=_cparams(("arbitrary", "arbitrary"), VMEM_LIMIT),
        name="moe",
    )(idx_flat, gates3, wg_bf, wu_bf, wd_bf, h2)


def _final_kernel(x1_ref, moe_ref, mod_ref, o_ref, *, tm):
    moe = jnp.concatenate(
        [moe_ref[pl.ds(s, tm, stride=ROW_TILES), :] for s in range(ROW_TILES)], axis=1)
    o_ref[...] = x1_ref[...] + mod_ref[0] * moe


def _final_call(x1, moe_tiles, mod3, *, seq, tm, mod_base, mod_stride):
    n = x1.shape[0]
    tiles_per_seq = seq // tm
    gate2_block = 5

    def mod_map(i):
        return (mod_base + mod_stride * (i // tiles_per_seq), 0, gate2_block)

    return pl.pallas_call(
        functools.partial(_final_kernel, tm=tm),
        grid=(n // tm,),
        in_specs=[pl.BlockSpec((tm, D_MODEL), lambda i: (i, 0)),
                  pl.BlockSpec((tm * ROW_TILES, LANES), lambda i: (i, 0)),
                  pl.BlockSpec((1, 1, D_MODEL), mod_map)],
        out_specs=pl.BlockSpec((tm, D_MODEL), lambda i: (i, 0)),
        out_shape=jax.ShapeDtypeStruct((n, D_MODEL), F32),
        compiler_params=_cparams(("arbitrary",)),
        name="final",
    )(x1, moe_tiles, mod3)


def _rope_tables(seq):
    t = np.arange(seq)
    row, col = t // GRID_W, t % GRID_W
    half = HEAD_DIM // 2
    freqs = 1.0 / (ROPE_BASE ** (np.arange(0, half, 2) / half))
    ang_r = row[:, None] * freqs[None, :]
    ang_c = col[:, None] * freqs[None, :]
    ang = np.concatenate([ang_r, ang_r, ang_c, ang_c], axis=-1)
    cos = np.tile(np.cos(ang), (1, LANES // HEAD_DIM))
    sin = np.tile(np.sin(ang), (1, LANES // HEAD_DIM))
    sign = np.where((np.arange(LANES) % (HEAD_DIM // 2)) < (HEAD_DIM // 4), -1.0, 1.0)
    return jnp.asarray(cos, F32), jnp.asarray(sin * sign[None, :], F32)


def _segment_matrix():
    seg = np.arange(ATTN_WIDTH) // HEAD_DIM
    return jnp.asarray((seg[:, None] == seg[None, :]) / HEAD_DIM, BF16)


def _trunk(x, mod3, w, cache, *, mod_base, mod_stride, tm, tq, tk, moe_tm):
    batch, seq, _ = x.shape
    n = batch * seq
    x2d = x.reshape(n, D_MODEL)
    rope_tabs = _rope_tables(seq) if cache is not None else None
    pre = _pre_call(x2d, mod3[:, :, :2 * D_MODEL], w["g1"], w["win"], w["seg"], w["qg"], w["kg"],
                    rope_tabs, seq=seq, tm=tm, mod_base=mod_base, mod_stride=mod_stride,
                    emit_f32_kv=cache is None)
    p, q, k, v = pre[:4]
    attn = _attn_call(w["lam4"], w["subln"], q, k, v, cache, batch=batch, seq=seq, tq=tq, tk=tk)
    x1, h2, aff_t, aff_c = _post_call(x2d, attn, p, mod3[:, :, 2 * D_MODEL:5 * D_MODEL], w["wout"],
                                     w["wbd"], w["pool_scale"], w["g2"], w["wr_t"], seq=seq, tm=tm,
                                     mod_base=mod_base, mod_stride=mod_stride)
    cap = CAPACITY_FACTOR * n // N_EXPERTS
    idx, gates = _select_call(aff_t, aff_c, cap=cap)
    moe_tiles = _moe_call(idx.reshape(N_EXPERTS * cap), gates.reshape(-1, 1, moe_tm),
                          w["wg"], w["wu"], w["wd"], h2, cap=cap, tm=moe_tm)
    y = _final_call(x1, moe_tiles, mod3, seq=seq, tm=tm, mod_base=mod_base, mod_stride=mod_stride)
    y = y.reshape(batch, seq, D_MODEL)
    if cache is None:
        return y, pre[4], pre[5]
    return y, None, None


def kernel(x_prompt, x_sample, cache_k, cache_v, c, c_ctx, norm1_g, norm2_g, w_ada, b_ada, w_in,
           q_norm_g, k_norm_g, lambda_q1, lambda_k1, lambda_q2, lambda_k2, subln_g, w_pool,
           pool_scale, w_out, w_router, w_gate, w_up, w_down):
    assert w_ada.shape[0] == 1, "single-layer stack"
    batch, seq, _ = x_prompt.shape
    dec_batch, dec_seq, _ = x_sample.shape

    pad = SUBLANES - 1 - dec_batch
    cvec = jnp.concatenate([c_ctx[None, :], c, jnp.zeros((pad, D_MODEL), F32)], axis=0)
    mod = _ada_call(cvec, w_ada[0], b_ada[0])
    mod3 = mod.reshape(SUBLANES, 1, 6 * D_MODEL)

    n_groups = w_pool.shape[1]
    grp = POOL_WIDTH // n_groups
    wbd = jnp.zeros((POOL_WIDTH, POOL_WIDTH), F32)
    for g in range(n_groups):
        wbd = wbd.at[g * grp:(g + 1) * grp, g * grp:(g + 1) * grp].set(w_pool[0, g])

    w = {
        "g1": norm1_g[0].reshape(1, D_MODEL),
        "g2": norm2_g[0].reshape(1, D_MODEL),
        "win": w_in[0].astype(BF16),
        "seg": _segment_matrix(),
        "qg": jnp.tile(q_norm_g[0], ATTN_WIDTH // HEAD_DIM).reshape(1, ATTN_WIDTH),
        "kg": jnp.tile(k_norm_g[0], ATTN_WIDTH // HEAD_DIM).reshape(1, ATTN_WIDTH),
        "lam4": jnp.stack([lambda_q1[0], lambda_k1[0], lambda_q2[0], lambda_k2[0]], axis=0),
        "subln": subln_g[0].reshape(V_DIM, 1),
        "wbd": wbd.astype(BF16),
        "pool_scale": pool_scale[0].reshape(1, POOL_WIDTH),
        "wout": w_out[0].astype(BF16),
        "wr_t": w_router[0].T,
        "wg": w_gate[0].astype(BF16),
        "wu": w_up[0].astype(BF16),
        "wd": w_down[0].astype(BF16),
    }

    yp, k_ctx, v_ctx = _trunk(x_prompt, mod3, w, None, mod_base=0, mod_stride=0,
                              tm=256, tq=256, tk=256, moe_tm=256)
    past = cache_k.shape[2]
    cv = cache_v[:, 0].reshape(dec_batch * past, N_HEADS, V_DIM).transpose(1, 2, 0).astype(BF16)
    cv = jnp.concatenate([cv, jnp.ones((N_HEADS, V_EXT - V_DIM, dec_batch * past), BF16)], axis=1)
    cache = (cache_k[:, 0].reshape(dec_batch * past, ATTN_WIDTH).astype(BF16),
             cv.reshape(N_HEADS * V_EXT, dec_batch * past))
    ys, _, _ = _trunk(x_sample, mod3, w, cache, mod_base=1, mod_stride=1,
                      tm=256, tq=256, tk=512, moe_tm=256)
    ctx_k = k_ctx.reshape(batch, 1, seq, N_HEADS, 2, HEAD_DIM)
    ctx_v = v_ctx.reshape(batch, 1, seq, N_HEADS, V_DIM)
    return yp, ys, ctx_k, ctx_v
```

```python
import functools
import math

import numpy as np
import jax
import jax.numpy as jnp
from jax import lax
from jax.experimental import pallas as pl
from jax.experimental.pallas import tpu as pltpu

F32 = jnp.float32
BF16 = jnp.bfloat16

D_MODEL = 1024
POOL_WIDTH = 256
ATTN_WIDTH = 768
N_HEADS = 6
HEAD_DIM = 64
V_DIM = 128
IN_WIDTH = POOL_WIDTH + 3 * ATTN_WIDTH
N_EXPERTS = 16
CAPACITY_FACTOR = 2
GRID_W = 64
ROPE_BASE = 10000.0
EPS = 1e-6
LAMBDA_INIT = 0.8 - 0.6 * math.exp(-0.3 * 0)
LOG2E = math.log2(math.e)
V_EXT = V_DIM + 16
SCORES_AHEAD = 3

LANES = 128
SUBLANES = 8
ROW_TILES = D_MODEL // LANES
HALO = 16
VMEM_LIMIT = 56 * 1024 * 1024


def _cparams(sem, vmem=None):
    return pltpu.CompilerParams(dimension_semantics=sem, vmem_limit_bytes=vmem)


def _ada_kernel(c_ref, w_ref, b_ref, o_ref):
    c = c_ref[...]
    s = c * jax.nn.sigmoid(c)
    o_ref[...] = jnp.dot(s.astype(BF16), w_ref[...].astype(BF16),
                         preferred_element_type=F32) + b_ref[...]


def _ada_call(cvec, w_ada, b_ada):
    rows, d = cvec.shape
    n = w_ada.shape[1]
    bn = 1536
    return pl.pallas_call(
        _ada_kernel,
        grid=(n // bn,),
        in_specs=[pl.BlockSpec((rows, d), lambda j: (0, 0)),
                  pl.BlockSpec((d, bn), lambda j: (0, j)),
                  pl.BlockSpec((1, bn), lambda j: (0, j))],
        out_specs=pl.BlockSpec((rows, bn), lambda j: (0, j)),
        out_shape=jax.ShapeDtypeStruct((rows, n), F32),
        compiler_params=_cparams(("arbitrary",)),
        name="ada",
    )(cvec, w_ada, b_ada.reshape(1, n))


def _segment_mean_square(a, seg_ref):
    return jnp.dot((a * a).astype(BF16), seg_ref[...], preferred_element_type=F32)


def _rope(a, cos, sin_signed, first_half):
    parts = []
    for h in range(a.shape[1] // LANES):
        blk = a[:, h * LANES:(h + 1) * LANES]
        fwd = pltpu.roll(blk, LANES - HEAD_DIM // 4, 1)
        bwd = pltpu.roll(blk, HEAD_DIM // 4, 1)
        parts.append(blk * cos + jnp.where(first_half, fwd, bwd) * sin_signed)
    return jnp.concatenate(parts, axis=1)


def _pre_kernel(*refs, rope, emit_f32_kv):
    x_ref, mod_ref, g1_ref, win_ref, seg_ref, qg_ref, kg_ref = refs[:7]
    pos = 7
    if rope:
        cos_ref, sin_ref = refs[pos:pos + 2]
        pos += 2
    p_ref, q_ref, k_ref, v_ref = refs[pos:pos + 4]
    pos += 4
    if emit_f32_kv:
        kf_ref, vf_ref = refs[pos:pos + 2]

    x = x_ref[...]
    mod = mod_ref[0]
    shift1 = mod[:, :D_MODEL]
    scale1 = mod[:, D_MODEL:2 * D_MODEL]
    ms = jnp.mean(x * x, axis=1, keepdims=True)
    h = x * lax.rsqrt(ms + EPS) * g1_ref[...] * (1.0 + scale1) + shift1
    z = jnp.dot(h.astype(BF16), win_ref[...], preferred_element_type=F32)

    p_ref[...] = z[:, :POOL_WIDTH]
    qz = z[:, POOL_WIDTH:POOL_WIDTH + ATTN_WIDTH]
    kz = z[:, POOL_WIDTH + ATTN_WIDTH:POOL_WIDTH + 2 * ATTN_WIDTH]
    vz = z[:, POOL_WIDTH + 2 * ATTN_WIDTH:]

    qn = qz * lax.rsqrt(_segment_mean_square(qz, seg_ref) + EPS) * qg_ref[...]
    kn = kz * lax.rsqrt(_segment_mean_square(kz, seg_ref) + EPS) * kg_ref[...]
    if rope:
        cos = cos_ref[...]
        sin_signed = sin_ref[...]
        lane = lax.broadcasted_iota(jnp.int32, cos.shape, 1)
        first_half = (lane % (HEAD_DIM // 2)) < (HEAD_DIM // 4)
        qn = _rope(qn, cos, sin_signed, first_half)
        kn = _rope(kn, cos, sin_signed, first_half)

    q_ref[...] = (qn * (LOG2E / math.sqrt(HEAD_DIM))).T.astype(BF16)
    k_ref[...] = kn.astype(BF16)
    vt = vz.T
    ones = jnp.ones((V_EXT - V_DIM, vt.shape[1]), BF16)
    for h in range(N_HEADS):
        v_ref[h * V_EXT:h * V_EXT + V_DIM, :] = vt[h * V_DIM:(h + 1) * V_DIM, :].astype(BF16)
        v_ref[h * V_EXT + V_DIM:(h + 1) * V_EXT, :] = ones
    if emit_f32_kv:
        kf_ref[...] = kn
        vf_ref[...] = vz


def _pre_call(x2d, mod3, g1, win_bf, seg, qg, kg, rope_tabs, *, seq, tm, mod_base, mod_stride,
              emit_f32_kv):
    n = x2d.shape[0]
    rope = rope_tabs is not None
    tiles_per_seq = seq // tm

    def mod_map(i):
        return (mod_base + mod_stride * (i // tiles_per_seq), 0, 0)

    in_specs = [
        pl.BlockSpec((tm, D_MODEL), lambda i: (i, 0)),
        pl.BlockSpec((1, 1, 2 * D_MODEL), mod_map),
        pl.BlockSpec((1, D_MODEL), lambda i: (0, 0)),
        pl.BlockSpec((D_MODEL, IN_WIDTH), lambda i: (0, 0)),
        pl.BlockSpec((ATTN_WIDTH, ATTN_WIDTH), lambda i: (0, 0)),
        pl.BlockSpec((1, ATTN_WIDTH), lambda i: (0, 0)),
        pl.BlockSpec((1, ATTN_WIDTH), lambda i: (0, 0)),
    ]
    args = [x2d, mod3, g1, win_bf, seg, qg, kg]
    if rope:
        in_specs += [pl.BlockSpec((tm, LANES), lambda i: (i % tiles_per_seq, 0))] * 2
        args += list(rope_tabs)
    out_shapes = [jax.ShapeDtypeStruct((n, POOL_WIDTH), F32)]
    out_specs = [pl.BlockSpec((tm, POOL_WIDTH), lambda i: (i, 0))]
    out_shapes += [jax.ShapeDtypeStruct((ATTN_WIDTH, n), BF16),
                   jax.ShapeDtypeStruct((n, ATTN_WIDTH), BF16),
                   jax.ShapeDtypeStruct((N_HEADS * V_EXT, n), BF16)]
    out_specs += [pl.BlockSpec((ATTN_WIDTH, tm), lambda i: (0, i)),
                  pl.BlockSpec((tm, ATTN_WIDTH), lambda i: (i, 0)),
                  pl.BlockSpec((N_HEADS * V_EXT, tm), lambda i: (0, i))]
    if emit_f32_kv:
        out_shapes += [jax.ShapeDtypeStruct((n, ATTN_WIDTH), F32)] * 2
        out_specs += [pl.BlockSpec((tm, ATTN_WIDTH), lambda i: (i, 0))] * 2
    return pl.pallas_call(
        functools.partial(_pre_kernel, rope=rope, emit_f32_kv=emit_f32_kv),
        grid=(n // tm,),
        in_specs=in_specs,
        out_specs=out_specs,
        out_shape=out_shapes,
        compiler_params=_cparams(("arbitrary",), VMEM_LIMIT),
        name="pre_rope" if rope else "pre",
    )(*args)


def _attn_kernel(*refs, has_cache, heads, tq, tw, tk, seq, cache_len):
    if has_cache:
        lam_ref, g_ref, q_ref, kc_ref, vc_ref, k_ref, v_ref, o_ref = refs
    else:
        lam_ref, g_ref, q_ref, k_ref, v_ref, o_ref = refs

    lv = lam_ref[...]
    lam = (jnp.exp(jnp.sum(lv[0:1] * lv[1:2], axis=1, keepdims=True))
           - jnp.exp(jnp.sum(lv[2:3] * lv[3:4], axis=1, keepdims=True)) + LAMBDA_INIT)

    row = lax.broadcasted_iota(jnp.int32, (LANES, tq), 0)
    zero = jnp.zeros((LANES, tq), BF16)

    def sub_queries(h):
        qt = q_ref[h * LANES:(h + 1) * LANES, :]
        return (jnp.where(row < HEAD_DIM, qt, zero), jnp.where(row >= HEAD_DIM, qt, zero))

    def scores(kb, q_one):
        return jnp.dot(kb, q_one, preferred_element_type=F32)

    def softmax_step(s, m):
        m_new = jnp.maximum(m, jnp.max(s, axis=0, keepdims=True))
        return m_new, jnp.exp2(m - m_new), jnp.exp2(s - m_new).astype(BF16)

    def accumulate(vb, p, alpha, acc):
        return alpha * acc + jnp.dot(vb, p, preferred_element_type=F32)

    chunks = []
    if has_cache:
        chunks += [(kc_ref, vc_ref, j) for j in range(cache_len // tk)]
    chunks += [(k_ref, v_ref, j) for j in range(seq // tk)]

    def keys(c, h):
        kr, _, j = chunks[c]
        return kr[j * tk:(j + 1) * tk, h * LANES:(h + 1) * LANES]

    def values(c, h):
        _, vr, j = chunks[c]
        return vr[h * V_EXT:(h + 1) * V_EXT, j * tk:(j + 1) * tk]

    chains = []
    for h in range(heads):
        q_sub = sub_queries(h)
        chains += [(h, q_sub[sub][:, w * tw:(w + 1) * tw])
                   for w in range(tq // tw) for sub in range(2)]
    per_head = len(chains) // heads
    items = [(c, ch) for h in range(heads) for c in range(len(chunks))
             for ch in range(h * per_head, (h + 1) * per_head)]
    m = [jnp.full((1, tw), -jnp.inf, F32)] * len(chains)
    acc = [jnp.zeros((V_EXT, tw), F32)] * len(chains)
    queue = [scores(keys(c, chains[ch][0]), chains[ch][1]) for c, ch in items[:SCORES_AHEAD]]
    for i, (c, ch) in enumerate(items):
        s_cur = queue.pop(0)
        if i + SCORES_AHEAD < len(items):
            nc, nch = items[i + SCORES_AHEAD]
            queue.append(scores(keys(nc, chains[nch][0]), chains[nch][1]))
        m[ch], alpha, p = softmax_step(s_cur, m[ch])
        acc[ch] = accumulate(values(c, chains[ch][0]), p, alpha, acc[ch])
    for h in range(heads):
        outs = []
        for w in range(tq // tw):
            a1, a2 = acc[h * per_head + 2 * w], acc[h * per_head + 2 * w + 1]
            outs.append(a1[:V_DIM] / a1[V_DIM:V_DIM + 1]
                        - lam * (a2[:V_DIM] / a2[V_DIM:V_DIM + 1]))
        o = outs[0] if len(outs) == 1 else jnp.concatenate(outs, axis=1)
        y = (o * lax.rsqrt(jnp.mean(o * o, axis=0, keepdims=True) + EPS) * g_ref[...]
             * (1.0 - LAMBDA_INIT))
        o_ref[:, h * LANES:(h + 1) * LANES] = y.T.astype(BF16)


def _attn_call(lam4, subln_col, qt, k, vt, cache, *, batch, seq, heads, tq, tw, tk):
    has_cache = cache is not None
    nq = seq // tq
    in_specs = [
        pl.BlockSpec((4, HEAD_DIM), lambda b, h, i: (0, 0)),
        pl.BlockSpec((V_DIM, 1), lambda b, h, i: (0, 0)),
        pl.BlockSpec((heads * LANES, tq), lambda b, h, i: (h, b * nq + i)),
    ]
    args = [lam4, subln_col, qt]
    cache_len = 0
    if has_cache:
        kc, vct = cache
        cache_len = kc.shape[0] // batch
        in_specs += [pl.BlockSpec((cache_len, heads * LANES), lambda b, h, i: (b, h)),
                     pl.BlockSpec((heads * V_EXT, cache_len), lambda b, h, i: (h, b))]
        args += [kc, vct]
    in_specs += [pl.BlockSpec((seq, heads * LANES), lambda b, h, i: (b, h)),
                 pl.BlockSpec((heads * V_EXT, seq), lambda b, h, i: (h, b))]
    args += [k, vt]
    return pl.pallas_call(
        functools.partial(_attn_kernel, has_cache=has_cache, heads=heads, tq=tq, tw=tw, tk=tk,
                          seq=seq, cache_len=cache_len),
        grid=(batch, N_HEADS // heads, nq),
        in_specs=in_specs,
        out_specs=pl.BlockSpec((tq, heads * LANES), lambda b, h, i: (b * nq + i, h)),
        out_shape=jax.ShapeDtypeStruct((batch * seq, ATTN_WIDTH), BF16),
        compiler_params=_cparams(("arbitrary", "arbitrary", "arbitrary"), VMEM_LIMIT),
        name="attn_cache" if has_cache else "attn",
    )(*args)


def _post_kernel(x_ref, a_ref, p_ref, pp_ref, pn_ref, mod_ref, wout_ref, wbd_ref, ps_ref, g2_ref,
                 wr_ref, x1_ref, h2_ref, aff_ref, affc_ref, *, tm, seq):
    i = pl.program_id(0)
    tiles_per_seq = seq // tm
    ti = i % tiles_per_seq
    p = p_ref[...]
    prev = jnp.where(ti == 0, 0.0, pp_ref[...])
    nxt = jnp.where(ti == tiles_per_seq - 1, 0.0, pn_ref[...])
    ext = jnp.concatenate([prev, p, nxt], axis=0)
    n_ext = tm + 2 * HALO
    s2 = ext + pltpu.roll(ext, 1, 0)
    s4 = pltpu.roll(s2, 1, 0) + pltpu.roll(s2, n_ext - 1, 0)
    s8 = pltpu.roll(s4, 2, 0) + pltpu.roll(s4, n_ext - 2, 0)
    s16 = pltpu.roll(s8, 4, 0) + pltpu.roll(s8, n_ext - 4, 0)
    lane = lax.broadcasted_iota(jnp.int32, (tm, POOL_WIDTH), 1)
    grp = lane // (POOL_WIDTH // 4)
    win = jnp.where(grp == 0, s2[HALO:HALO + tm],
                    jnp.where(grp == 1, s4[HALO:HALO + tm],
                              jnp.where(grp == 2, s8[HALO:HALO + tm], s16[HALO:HALO + tm])))
    t = ti * tm + lax.broadcasted_iota(jnp.int32, (tm, POOL_WIDTH), 0)
    left = jnp.where(grp == 0, 1, jnp.where(grp == 1, 2, jnp.where(grp == 2, 4, 8)))
    lo = jnp.maximum(t - left, 0)
    hi = jnp.minimum(t + left - 1, seq - 1) + 1
    pooled = win / (hi - lo).astype(F32) - p
    pool = jnp.dot(pooled.astype(BF16), wbd_ref[...], preferred_element_type=F32) * ps_ref[...]

    cat = jnp.concatenate([pool.astype(BF16), a_ref[...]], axis=1)
    mix = jnp.dot(cat, wout_ref[...], preferred_element_type=F32)
    mod = mod_ref[0]
    gate1 = mod[:, 0:D_MODEL]
    shift2 = mod[:, D_MODEL:2 * D_MODEL]
    scale2 = mod[:, 2 * D_MODEL:3 * D_MODEL]
    x1 = x_ref[...] + gate1 * mix
    x1_ref[...] = x1
    ms = jnp.mean(x1 * x1, axis=1, keepdims=True)
    h2 = x1 * lax.rsqrt(ms + EPS) * g2_ref[...] * (1.0 + scale2) + shift2
    h2_ref[...] = h2

    logits = lax.dot_general(wr_ref[...], h2, (((1,), (1,)), ((), ())),
                             precision=lax.Precision.HIGHEST, preferred_element_type=F32)
    e = jnp.exp(logits - jnp.max(logits, axis=0, keepdims=True))
    aff = e / jnp.sum(e, axis=0, keepdims=True)
    aff_ref[...] = aff
    for c in range(tm // LANES):
        affc_ref[c * N_EXPERTS:(c + 1) * N_EXPERTS, :] = aff[:, c * LANES:(c + 1) * LANES]


def _post_call(x2d, attn, p, mod3, wout_bf, wbd_bf, pool_scale, g2, wr_t, *, seq, tm, mod_base,
               mod_stride):
    n = x2d.shape[0]
    tiles_per_seq = seq // tm
    halo_per_tile = tm // HALO
    n_halo = n // HALO

    def mod_map(i):
        return (mod_base + mod_stride * (i // tiles_per_seq), 0, 0)

    return pl.pallas_call(
        functools.partial(_post_kernel, tm=tm, seq=seq),
        grid=(n // tm,),
        in_specs=[
            pl.BlockSpec((tm, D_MODEL), lambda i: (i, 0)),
            pl.BlockSpec((tm, ATTN_WIDTH), lambda i: (i, 0)),
            pl.BlockSpec((tm, POOL_WIDTH), lambda i: (i, 0)),
            pl.BlockSpec((HALO, POOL_WIDTH), lambda i: (jnp.maximum(i * halo_per_tile - 1, 0), 0)),
            pl.BlockSpec((HALO, POOL_WIDTH),
                         lambda i: (jnp.minimum((i + 1) * halo_per_tile, n_halo - 1), 0)),
            pl.BlockSpec((1, 1, 3 * D_MODEL), lambda i: mod_map(i)[:2] + (0,)),
            pl.BlockSpec((D_MODEL, D_MODEL), lambda i: (0, 0)),
            pl.BlockSpec((POOL_WIDTH, POOL_WIDTH), lambda i: (0, 0)),
            pl.BlockSpec((1, POOL_WIDTH), lambda i: (0, 0)),
            pl.BlockSpec((1, D_MODEL), lambda i: (0, 0)),
            pl.BlockSpec((N_EXPERTS, D_MODEL), lambda i: (0, 0)),
        ],
        out_specs=[
            pl.BlockSpec((tm, D_MODEL), lambda i: (i, 0)),
            pl.BlockSpec((tm, D_MODEL), lambda i: (i, 0)),
            pl.BlockSpec((N_EXPERTS, tm), lambda i: (0, i)),
            pl.BlockSpec((tm // LANES * N_EXPERTS, LANES), lambda i: (i, 0)),
        ],
        out_shape=[
            jax.ShapeDtypeStruct((n, D_MODEL), F32),
            jax.ShapeDtypeStruct((n, D_MODEL), F32),
            jax.ShapeDtypeStruct((N_EXPERTS, n), F32),
            jax.ShapeDtypeStruct((n // LANES * N_EXPERTS, LANES), F32),
        ],
        compiler_params=_cparams(("arbitrary",), VMEM_LIMIT),
        name="post",
    )(x2d, attn, p, p, p, mod3, wout_bf, wbd_bf, pool_scale, g2, wr_t)


def _select_kernel(a_ref, ac_ref, idx_ref, gate_ref, *, n, cap):
    nc = n // LANES
    a = a_ref[...]
    thr = jnp.zeros((N_EXPERTS, 1), jnp.int32)
    for bit in range(30, -1, -1):
        cand = thr | (1 << bit)
        cnt = jnp.sum(jnp.where(a >= pltpu.bitcast(cand, F32), 1.0, 0.0), axis=1, keepdims=True)
        thr = jnp.where(cnt >= cap, cand, thr)
    thr_all = pltpu.bitcast(thr, F32)
    need_all = cap - jnp.sum(jnp.where(a > thr_all, 1.0, 0.0), axis=1, keepdims=True)

    r = lax.broadcasted_iota(jnp.int32, (LANES, LANES), 0)
    c = lax.broadcasted_iota(jnp.int32, (LANES, LANES), 1)
    upper = jnp.where(r <= c, 1.0, 0.0).astype(BF16)
    lower = jnp.where(c < r, 1.0, 0.0).astype(BF16)
    row_valid = r < nc
    chunk_col = lax.broadcasted_iota(jnp.int32, (LANES, 1), 0).astype(F32)
    slot = lax.broadcasted_iota(jnp.int32, (1, cap), 1).astype(F32)

    def lane_counts(mask):
        local = jnp.dot(mask.astype(BF16), upper, preferred_element_type=F32)
        total = jnp.broadcast_to(local[:, LANES - 1:LANES], (LANES, LANES))
        before = jnp.dot(lower, total.astype(BF16), preferred_element_type=F32)
        return local, total, before

    for e in range(N_EXPERTS):
        av = ac_ref[pl.ds(e, nc, stride=N_EXPERTS), :]
        if nc < LANES:
            av = jnp.concatenate([av, jnp.zeros((LANES - nc, LANES), F32)], axis=0)
        thr_e = thr_all[e:e + 1, :]
        above = jnp.where(row_valid & (av > thr_e), 1.0, 0.0)
        tied = jnp.where(row_valid & (av == thr_e), 1.0, 0.0)
        t_local, _, t_before = lane_counts(tied)
        sel = above + tied * jnp.where(t_local + t_before <= need_all[e:e + 1, :], 1.0, 0.0)
        s_local, s_total, s_before = lane_counts(sel)
        rank = jnp.where(sel > 0.0, s_local, 0.0)

        start = s_before[:, 0:1]
        stop = start + s_total[:, 0:1]
        onehot = jnp.where((slot >= start) & (slot < stop), 1.0, 0.0)
        chunk_of_slot = jnp.sum(onehot * chunk_col, axis=0, keepdims=True)
        start_of_slot = jnp.sum(onehot * start, axis=0, keepdims=True)

        at = av.T
        hi = at.astype(BF16)
        rest = at - hi.astype(F32)
        mid = rest.astype(BF16)
        lo = (rest - mid.astype(F32)).astype(BF16)
        lhs = jnp.concatenate([rank.T.astype(BF16), hi, mid, lo], axis=0)
        picked = jnp.dot(lhs, onehot.astype(BF16), preferred_element_type=F32)
        rank_p = picked[0:LANES]
        aff_p = picked[LANES:2 * LANES] + picked[2 * LANES:3 * LANES] + picked[3 * LANES:]
        hit = rank_p == (slot - start_of_slot + 1.0)
        lane_of_slot = jnp.sum(jnp.where(hit, chunk_col, 0.0), axis=0, keepdims=True)
        idx_ref[e:e + 1, :] = (chunk_of_slot * LANES + lane_of_slot).astype(jnp.int32)
        gate_ref[e:e + 1, :] = jnp.sum(jnp.where(hit, aff_p, 0.0), axis=0, keepdims=True)


def _select_call(aff_t, aff_c, *, cap):
    n = aff_t.shape[1]
    assert n % LANES == 0 and n // LANES <= LANES and cap % LANES == 0
    return pl.pallas_call(
        functools.partial(_select_kernel, n=n, cap=cap),
        grid=(1,),
        in_specs=[pl.BlockSpec((N_EXPERTS, n), lambda i: (0, 0)),
                  pl.BlockSpec(aff_c.shape, lambda i: (0, 0))],
        out_specs=[pl.BlockSpec((N_EXPERTS, cap), lambda i: (0, 0))] * 2,
        out_shape=[jax.ShapeDtypeStruct((N_EXPERTS, cap), jnp.int32),
                   jax.ShapeDtypeStruct((N_EXPERTS, cap), F32)],
        compiler_params=_cparams(("arbitrary",), VMEM_LIMIT),
        name="select",
    )(aff_t, aff_c)


def _moe_kernel(idx_ref, gate_ref, wg_ref, wu_ref, wd_ref, x_hbm, out_hbm,
                gbuf, ybuf, acc_ref, gsem, osem, *, tm, n_tiles_total):
    e = pl.program_id(0)
    t = pl.program_id(1)
    nt = pl.num_programs(1)
    step = e * nt + t
    slot = step % 2
    group = 8
    last = n_tiles_total - 1

    def gather_start(tile_step, k, dst_slot):
        tok = idx_ref[tile_step * tm + k]
        pltpu.make_async_copy(x_hbm.at[pl.ds(tok, 1), :],
                              gbuf.at[dst_slot, pl.ds(k, 1), :], gsem.at[dst_slot]).start()

    def gather_wait(dst_slot):
        pltpu.make_async_copy(x_hbm.at[pl.ds(0, tm), :], gbuf.at[dst_slot], gsem.at[dst_slot]).wait()

    def scatter_add(tile_step, src_slot, k0):
        pending = []
        for r in range(group):
            k = k0 + r
            tok = idx_ref[tile_step * tm + k]
            off = pl.multiple_of(tok * ROW_TILES, ROW_TILES)
            src = pl.multiple_of(k * ROW_TILES, ROW_TILES)
            pending.append((off, acc_ref[pl.ds(off, ROW_TILES), :]
                            + ybuf[src_slot, pl.ds(src, ROW_TILES), :]))
        for off, val in pending:
            acc_ref[pl.ds(off, ROW_TILES), :] = val

    @pl.when(step == 0)
    def _():
        acc_ref[...] = jnp.zeros_like(acc_ref)
        ybuf[...] = jnp.zeros_like(ybuf)

        def body(k, _):
            gather_start(0, k, 0)
            return 0
        lax.fori_loop(0, tm, body, 0)

    gather_wait(slot)
    xe = gbuf[slot].astype(BF16)

    nxt = jnp.minimum(step + 1, last)
    prev = jnp.maximum(step - 1, 0)
    g = jnp.dot(xe, wg_ref[...], preferred_element_type=F32)
    for k in range(tm // 2):
        gather_start(nxt, k, 1 - slot)
    u = jnp.dot(xe, wu_ref[...], preferred_element_type=F32)
    for k in range(tm // 2, tm):
        gather_start(nxt, k, 1 - slot)
    hid = (g * jax.nn.sigmoid(g) * u).astype(BF16)
    y = jnp.dot(hid, wd_ref[...], preferred_element_type=F32)
    for k0 in range(0, tm, group):
        scatter_add(prev, 1 - slot, k0)
    gate = jnp.broadcast_to(gate_ref[...], (LANES, tm)).T
    for s in range(ROW_TILES):
        ybuf[slot, pl.ds(s, tm, stride=ROW_TILES), :] = y[:, s * LANES:(s + 1) * LANES] * gate

    @pl.when(step == last)
    def _():
        gather_wait(1 - slot)

        def body(kk, _):
            scatter_add(step, slot, kk * group)
            return 0
        lax.fori_loop(0, tm // group, body, 0)
        cp = pltpu.make_async_copy(acc_ref, out_hbm, osem)
        cp.start()
        cp.wait()


def _moe_call(idx_flat, gates3, wg_bf, wu_bf, wd_bf, h2, *, cap, tm):
    n_rows = h2.shape[0] * ROW_TILES
    nt = cap // tm
    grid_spec = pltpu.PrefetchScalarGridSpec(
        num_scalar_prefetch=1,
        grid=(N_EXPERTS, nt),
        in_specs=[
            pl.BlockSpec((None, 1, tm), lambda e, t, idx: (e * nt + t, 0, 0)),
            pl.BlockSpec((None, D_MODEL, D_MODEL), lambda e, t, idx: (e, 0, 0)),
            pl.BlockSpec((None, D_MODEL, D_MODEL), lambda e, t, idx: (e, 0, 0)),
            pl.BlockSpec((None, D_MODEL, D_MODEL), lambda e, t, idx: (e, 0, 0)),
            pl.BlockSpec(memory_space=pl.ANY),
        ],
        out_specs=pl.BlockSpec(memory_space=pl.ANY),
        scratch_shapes=[
            pltpu.VMEM((2, tm, D_MODEL), F32),
            pltpu.VMEM((2, tm * ROW_TILES, LANES), F32),
            pltpu.VMEM((n_rows, LANES), F32),
            pltpu.SemaphoreType.DMA((2,)),
            pltpu.SemaphoreType.DMA(()),
        ],
    )
    return pl.pallas_call(
        functools.partial(_moe_kernel, tm=tm, n_tiles_total=N_EXPERTS * nt),
        grid_spec=grid_spec,
        out_shape=jax.ShapeDtypeStruct((n_rows, LANES), F32),
        compiler_params=_cparams(("arbitrary", "arbitrary"), VMEM_LIMIT),
        name="moe",
    )(idx_flat, gates3, wg_bf, wu_bf, wd_bf, h2)


def _final_kernel(x1_ref, moe_ref, mod_ref, o_ref, *, tm):
    moe = jnp.concatenate(
        [moe_ref[pl.ds(s, tm, stride=ROW_TILES), :] for s in range(ROW_TILES)], axis=1)
    o_ref[...] = x1_ref[...] + mod_ref[0] * moe


def _final_call(x1, moe_tiles, mod3, *, seq, tm, mod_base, mod_stride):
    n = x1.shape[0]
    tiles_per_seq = seq // tm
    gate2_block = 5

    def mod_map(i):
        return (mod_base + mod_stride * (i // tiles_per_seq), 0, gate2_block)

    return pl.pallas_call(
        functools.partial(_final_kernel, tm=tm),
        grid=(n // tm,),
        in_specs=[pl.BlockSpec((tm, D_MODEL), lambda i: (i, 0)),
                  pl.BlockSpec((tm * ROW_TILES, LANES), lambda i: (i, 0)),
                  pl.BlockSpec((1, 1, D_MODEL), mod_map)],
        out_specs=pl.BlockSpec((tm, D_MODEL), lambda i: (i, 0)),
        out_shape=jax.ShapeDtypeStruct((n, D_MODEL), F32),
        compiler_params=_cparams(("arbitrary",)),
        name="final",
    )(x1, moe_tiles, mod3)


def _rope_tables(seq):
    t = np.arange(seq)
    row, col = t // GRID_W, t % GRID_W
    half = HEAD_DIM // 2
    freqs = 1.0 / (ROPE_BASE ** (np.arange(0, half, 2) / half))
    ang_r = row[:, None] * freqs[None, :]
    ang_c = col[:, None] * freqs[None, :]
    ang = np.concatenate([ang_r, ang_r, ang_c, ang_c], axis=-1)
    cos = np.tile(np.cos(ang), (1, LANES // HEAD_DIM))
    sin = np.tile(np.sin(ang), (1, LANES // HEAD_DIM))
    sign = np.where((np.arange(LANES) % (HEAD_DIM // 2)) < (HEAD_DIM // 4), -1.0, 1.0)
    return jnp.asarray(cos, F32), jnp.asarray(sin * sign[None, :], F32)


def _segment_matrix():
    seg = np.arange(ATTN_WIDTH) // HEAD_DIM
    return jnp.asarray((seg[:, None] == seg[None, :]) / HEAD_DIM, BF16)


def _trunk(x, mod3, w, cache, *, mod_base, mod_stride, tm, heads, tq, tk, moe_tm):
    batch, seq, _ = x.shape
    n = batch * seq
    x2d = x.reshape(n, D_MODEL)
    rope_tabs = _rope_tables(seq) if cache is not None else None
    pre = _pre_call(x2d, mod3[:, :, :2 * D_MODEL], w["g1"], w["win"], w["seg"], w["qg"], w["kg"],
                    rope_tabs, seq=seq, tm=tm, mod_base=mod_base, mod_stride=mod_stride,
                    emit_f32_kv=cache is None)
    p, q, k, v = pre[:4]
    attn = _attn_call(w["lam4"], w["subln"], q, k, v, cache, batch=batch, seq=seq,
                      heads=heads, tq=tq, tw=256, tk=tk)
    x1, h2, aff_t, aff_c = _post_call(x2d, attn, p, mod3[:, :, 2 * D_MODEL:5 * D_MODEL], w["wout"],
                                     w["wbd"], w["pool_scale"], w["g2"], w["wr_t"], seq=seq, tm=tm,
                                     mod_base=mod_base, mod_stride=mod_stride)
    cap = CAPACITY_FACTOR * n // N_EXPERTS
    idx, gates = _select_call(aff_t, aff_c, cap=cap)
    moe_tiles = _moe_call(idx.reshape(N_EXPERTS * cap), gates.reshape(-1, 1, moe_tm),
                          w["wg"], w["wu"], w["wd"], h2, cap=cap, tm=moe_tm)
    y = _final_call(x1, moe_tiles, mod3, seq=seq, tm=tm, mod_base=mod_base, mod_stride=mod_stride)
    y = y.reshape(batch, seq, D_MODEL)
    if cache is None:
        return y, pre[4], pre[5]
    return y, None, None


def kernel(x_prompt, x_sample, cache_k, cache_v, c, c_ctx, norm1_g, norm2_g, w_ada, b_ada, w_in,
           q_norm_g, k_norm_g, lambda_q1, lambda_k1, lambda_q2, lambda_k2, subln_g, w_pool,
           pool_scale, w_out, w_router, w_gate, w_up, w_down):
    assert w_ada.shape[0] == 1, "single-layer stack"
    batch, seq, _ = x_prompt.shape
    dec_batch, dec_seq, _ = x_sample.shape

    pad = SUBLANES - 1 - dec_batch
    cvec = jnp.concatenate([c_ctx[None, :], c, jnp.zeros((pad, D_MODEL), F32)], axis=0)
    mod = _ada_call(cvec, w_ada[0], b_ada[0])
    mod3 = mod.reshape(SUBLANES, 1, 6 * D_MODEL)

    n_groups = w_pool.shape[1]
    grp = POOL_WIDTH // n_groups
    eye = jnp.eye(n_groups, dtype=F32)
    wbd = (w_pool[0][:, :, None, :] * eye[:, None, :, None]).reshape(POOL_WIDTH, POOL_WIDTH)

    w = {
        "g1": norm1_g[0].reshape(1, D_MODEL),
        "g2": norm2_g[0].reshape(1, D_MODEL),
        "win": w_in[0].astype(BF16),
        "seg": _segment_matrix(),
        "qg": jnp.tile(q_norm_g[0], ATTN_WIDTH // HEAD_DIM).reshape(1, ATTN_WIDTH),
        "kg": jnp.tile(k_norm_g[0], ATTN_WIDTH // HEAD_DIM).reshape(1, ATTN_WIDTH),
        "lam4": jnp.stack([lambda_q1[0], lambda_k1[0], lambda_q2[0], lambda_k2[0]], axis=0),
        "subln": subln_g[0].reshape(V_DIM, 1),
        "wbd": wbd.astype(BF16),
        "pool_scale": pool_scale[0].reshape(1, POOL_WIDTH),
        "wout": w_out[0].astype(BF16),
        "wr_t": w_router[0].T,
        "wg": w_gate[0].astype(BF16),
        "wu": w_up[0].astype(BF16),
        "wd": w_down[0].astype(BF16),
    }

    yp, k_ctx, v_ctx = _trunk(x_prompt, mod3, w, None, mod_base=0, mod_stride=0,
                              tm=256, heads=N_HEADS, tq=256, tk=256, moe_tm=256)
    past = cache_k.shape[2]
    cv = cache_v[:, 0].reshape(dec_batch * past, N_HEADS, V_DIM).transpose(1, 2, 0).astype(BF16)
    cv = jnp.concatenate([cv, jnp.ones((N_HEADS, V_EXT - V_DIM, dec_batch * past), BF16)], axis=1)
    cache = (cache_k[:, 0].reshape(dec_batch * past, ATTN_WIDTH).astype(BF16),
             cv.reshape(N_HEADS * V_EXT, dec_batch * past))
    ys, _, _ = _trunk(x_sample, mod3, w, cache, mod_base=1, mod_stride=1,
                      tm=256, heads=1, tq=512, tk=512, moe_tm=256)
    ctx_k = k_ctx.reshape(batch, 1, seq, N_HEADS, 2, HEAD_DIM)
    ctx_v = v_ctx.reshape(batch, 1, seq, N_HEADS, V_DIM)
    return yp, ys, ctx_k, ctx_v
```

```python
import functools
import math

import numpy as np
import jax
import jax.numpy as jnp
from jax import lax
from jax.experimental import pallas as pl
from jax.experimental.pallas import tpu as pltpu

F32 = jnp.float32
BF16 = jnp.bfloat16

D_MODEL = 1024
POOL_WIDTH = 256
ATTN_WIDTH = 768
N_HEADS = 6
HEAD_DIM = 64
V_DIM = 128
IN_WIDTH = POOL_WIDTH + 3 * ATTN_WIDTH
N_EXPERTS = 16
CAPACITY_FACTOR = 2
GRID_W = 64
ROPE_BASE = 10000.0
EPS = 1e-6
LAMBDA_INIT = 0.8 - 0.6 * math.exp(-0.3 * 0)
LOG2E = math.log2(math.e)
V_EXT = V_DIM + 16
GATHER_DMA_PRIORITY = 1
SCORES_AHEAD = 3

LANES = 128
SUBLANES = 8
ROW_TILES = D_MODEL // LANES
HALO = 16
VMEM_LIMIT = 56 * 1024 * 1024


def _cparams(sem, vmem=None):
    return pltpu.CompilerParams(dimension_semantics=sem, vmem_limit_bytes=vmem)


def _ada_kernel(c_ref, w_ref, b_ref, o_ref):
    c = c_ref[...]
    s = c * jax.nn.sigmoid(c)
    o_ref[...] = jnp.dot(s.astype(BF16), w_ref[...].astype(BF16),
                         preferred_element_type=F32) + b_ref[...]


def _ada_call(cvec, w_ada, b_ada):
    rows, d = cvec.shape
    n = w_ada.shape[1]
    bn = 1536
    return pl.pallas_call(
        _ada_kernel,
        grid=(n // bn,),
        in_specs=[pl.BlockSpec((rows, d), lambda j: (0, 0)),
                  pl.BlockSpec((d, bn), lambda j: (0, j)),
                  pl.BlockSpec((1, bn), lambda j: (0, j))],
        out_specs=pl.BlockSpec((rows, bn), lambda j: (0, j)),
        out_shape=jax.ShapeDtypeStruct((rows, n), F32),
        compiler_params=_cparams(("arbitrary",)),
        name="ada",
    )(cvec, w_ada, b_ada.reshape(1, n))


def _segment_mean_square(a, seg_ref):
    return jnp.dot((a * a).astype(BF16), seg_ref[...], preferred_element_type=F32)


def _rope(a, cos, sin_signed, first_half):
    parts = []
    for h in range(a.shape[1] // LANES):
        blk = a[:, h * LANES:(h + 1) * LANES]
        fwd = pltpu.roll(blk, LANES - HEAD_DIM // 4, 1)
        bwd = pltpu.roll(blk, HEAD_DIM // 4, 1)
        parts.append(blk * cos + jnp.where(first_half, fwd, bwd) * sin_signed)
    return jnp.concatenate(parts, axis=1)


def _pre_kernel(*refs, rope, emit_f32_kv):
    x_ref, mod_ref, g1_ref, win_ref, seg_ref, qg_ref, kg_ref = refs[:7]
    pos = 7
    if rope:
        cos_ref, sin_ref = refs[pos:pos + 2]
        pos += 2
    p_ref, q_ref, k_ref, v_ref = refs[pos:pos + 4]
    pos += 4
    if emit_f32_kv:
        kf_ref, vf_ref = refs[pos:pos + 2]

    x = x_ref[...]
    mod = mod_ref[0]
    shift1 = mod[:, :D_MODEL]
    scale1 = mod[:, D_MODEL:2 * D_MODEL]
    ms = jnp.mean(x * x, axis=1, keepdims=True)
    h = x * lax.rsqrt(ms + EPS) * g1_ref[...] * (1.0 + scale1) + shift1
    z = jnp.dot(h.astype(BF16), win_ref[...], preferred_element_type=F32)

    p_ref[...] = z[:, :POOL_WIDTH]
    qz = z[:, POOL_WIDTH:POOL_WIDTH + ATTN_WIDTH]
    kz = z[:, POOL_WIDTH + ATTN_WIDTH:POOL_WIDTH + 2 * ATTN_WIDTH]
    vz = z[:, POOL_WIDTH + 2 * ATTN_WIDTH:]

    qn = qz * lax.rsqrt(_segment_mean_square(qz, seg_ref) + EPS) * qg_ref[...]
    kn = kz * lax.rsqrt(_segment_mean_square(kz, seg_ref) + EPS) * kg_ref[...]
    if rope:
        cos = cos_ref[...]
        sin_signed = sin_ref[...]
        lane = lax.broadcasted_iota(jnp.int32, cos.shape, 1)
        first_half = (lane % (HEAD_DIM // 2)) < (HEAD_DIM // 4)
        qn = _rope(qn, cos, sin_signed, first_half)
        kn = _rope(kn, cos, sin_signed, first_half)

    q_ref[...] = (qn * (LOG2E / math.sqrt(HEAD_DIM))).T.astype(BF16)
    k_ref[...] = kn.astype(BF16)
    vt = vz.T
    ones = jnp.ones((V_EXT - V_DIM, vt.shape[1]), BF16)
    for h in range(N_HEADS):
        v_ref[h * V_EXT:h * V_EXT + V_DIM, :] = vt[h * V_DIM:(h + 1) * V_DIM, :].astype(BF16)
        v_ref[h * V_EXT + V_DIM:(h + 1) * V_EXT, :] = ones
    if emit_f32_kv:
        kf_ref[0] = kn.T
        for h in range(N_HEADS):
            vf_ref[0, h] = vz[:, h * V_DIM:(h + 1) * V_DIM]


def _pre_call(x2d, mod3, g1, win_bf, seg, qg, kg, rope_tabs, *, seq, tm, mod_base, mod_stride,
              emit_f32_kv):
    n = x2d.shape[0]
    rope = rope_tabs is not None
    tiles_per_seq = seq // tm

    def mod_map(i):
        return (mod_base + mod_stride * (i // tiles_per_seq), 0, 0)

    in_specs = [
        pl.BlockSpec((tm, D_MODEL), lambda i: (i, 0)),
        pl.BlockSpec((1, 1, 2 * D_MODEL), mod_map),
        pl.BlockSpec((1, D_MODEL), lambda i: (0, 0)),
        pl.BlockSpec((D_MODEL, IN_WIDTH), lambda i: (0, 0)),
        pl.BlockSpec((ATTN_WIDTH, ATTN_WIDTH), lambda i: (0, 0)),
        pl.BlockSpec((1, ATTN_WIDTH), lambda i: (0, 0)),
        pl.BlockSpec((1, ATTN_WIDTH), lambda i: (0, 0)),
    ]
    args = [x2d, mod3, g1, win_bf, seg, qg, kg]
    if rope:
        in_specs += [pl.BlockSpec((tm, LANES), lambda i: (i % tiles_per_seq, 0))] * 2
        args += list(rope_tabs)
    out_shapes = [jax.ShapeDtypeStruct((n, POOL_WIDTH), F32)]
    out_specs = [pl.BlockSpec((tm, POOL_WIDTH), lambda i: (i, 0))]
    out_shapes += [jax.ShapeDtypeStruct((ATTN_WIDTH, n), BF16),
                   jax.ShapeDtypeStruct((n, ATTN_WIDTH), BF16),
                   jax.ShapeDtypeStruct((N_HEADS * V_EXT, n), BF16)]
    out_specs += [pl.BlockSpec((ATTN_WIDTH, tm), lambda i: (0, i)),
                  pl.BlockSpec((tm, ATTN_WIDTH), lambda i: (i, 0)),
                  pl.BlockSpec((N_HEADS * V_EXT, tm), lambda i: (0, i))]
    if emit_f32_kv:
        def seq_map(i):
            return (i // tiles_per_seq, 0, i % tiles_per_seq)

        out_shapes += [jax.ShapeDtypeStruct((n // seq, ATTN_WIDTH, seq), F32),
                       jax.ShapeDtypeStruct((n // seq, N_HEADS, seq, V_DIM), F32)]
        out_specs += [pl.BlockSpec((1, ATTN_WIDTH, tm), seq_map),
                      pl.BlockSpec((1, N_HEADS, tm, V_DIM),
                                   lambda i: (i // tiles_per_seq, 0, i % tiles_per_seq, 0))]
    return pl.pallas_call(
        functools.partial(_pre_kernel, rope=rope, emit_f32_kv=emit_f32_kv),
        grid=(n // tm,),
        in_specs=in_specs,
        out_specs=out_specs,
        out_shape=out_shapes,
        compiler_params=_cparams(("arbitrary",), VMEM_LIMIT),
        name="pre_rope" if rope else "pre",
    )(*args)


def _attn_kernel(*refs, has_cache, heads, tq, tw, tk, seq, cache_len):
    if has_cache:
        lam_ref, g_ref, q_ref, kc_ref, vc_ref, k_ref, v_ref, o_ref = refs
    else:
        lam_ref, g_ref, q_ref, k_ref, v_ref, o_ref = refs

    lv = lam_ref[...]
    lam = (jnp.exp(jnp.sum(lv[0:1] * lv[1:2], axis=1, keepdims=True))
           - jnp.exp(jnp.sum(lv[2:3] * lv[3:4], axis=1, keepdims=True)) + LAMBDA_INIT)

    row = lax.broadcasted_iota(jnp.int32, (LANES, tq), 0)
    zero = jnp.zeros((LANES, tq), BF16)

    def sub_queries(h):
        qt = q_ref[h * LANES:(h + 1) * LANES, :]
        return (jnp.where(row < HEAD_DIM, qt, zero), jnp.where(row >= HEAD_DIM, qt, zero))

    def scores(kb, q_one):
        return jnp.dot(kb, q_one, preferred_element_type=F32)

    def softmax_step(s, m):
        m_new = jnp.maximum(m, jnp.max(s, axis=0, keepdims=True))
        return m_new, jnp.exp2(m - m_new), jnp.exp2(s - m_new).astype(BF16)

    def accumulate(vb, p, alpha, acc):
        return alpha * acc + jnp.dot(vb, p, preferred_element_type=F32)

    chunks = []
    if has_cache:
        chunks += [(kc_ref, vc_ref, j) for j in range(cache_len // tk)]
    chunks += [(k_ref, v_ref, j) for j in range(seq // tk)]

    def keys(c, h):
        kr, _, j = chunks[c]
        return kr[j * tk:(j + 1) * tk, h * LANES:(h + 1) * LANES]

    def values(c, h):
        _, vr, j = chunks[c]
        return vr[h * V_EXT:(h + 1) * V_EXT, j * tk:(j + 1) * tk]

    chains = []
    for h in range(heads):
        q_sub = sub_queries(h)
        chains += [(h, q_sub[sub][:, w * tw:(w + 1) * tw])
                   for w in range(tq // tw) for sub in range(2)]
    per_head = len(chains) // heads
    items = [(c, ch) for h in range(heads) for c in range(len(chunks))
             for ch in range(h * per_head, (h + 1) * per_head)]
    m = [jnp.full((1, tw), -jnp.inf, F32)] * len(chains)
    acc = [jnp.zeros((V_EXT, tw), F32)] * len(chains)
    queue = [scores(keys(c, chains[ch][0]), chains[ch][1]) for c, ch in items[:SCORES_AHEAD]]
    for i, (c, ch) in enumerate(items):
        s_cur = queue.pop(0)
        if i + SCORES_AHEAD < len(items):
            nc, nch = items[i + SCORES_AHEAD]
            queue.append(scores(keys(nc, chains[nch][0]), chains[nch][1]))
        m[ch], alpha, p = softmax_step(s_cur, m[ch])
        acc[ch] = accumulate(values(c, chains[ch][0]), p, alpha, acc[ch])
    for h in range(heads):
        outs = []
        for w in range(tq // tw):
            a1, a2 = acc[h * per_head + 2 * w], acc[h * per_head + 2 * w + 1]
            outs.append(a1[:V_DIM] / a1[V_DIM:V_DIM + 1]
                        - lam * (a2[:V_DIM] / a2[V_DIM:V_DIM + 1]))
        o = outs[0] if len(outs) == 1 else jnp.concatenate(outs, axis=1)
        y = (o * lax.rsqrt(jnp.mean(o * o, axis=0, keepdims=True) + EPS) * g_ref[...]
             * (1.0 - LAMBDA_INIT))
        o_ref[:, h * LANES:(h + 1) * LANES] = y.T.astype(BF16)


def _attn_call(lam4, subln_col, qt, k, vt, cache, *, batch, seq, heads, tq, tw, tk):
    has_cache = cache is not None
    nq = seq // tq
    in_specs = [
        pl.BlockSpec((4, HEAD_DIM), lambda b, h, i: (0, 0)),
        pl.BlockSpec((V_DIM, 1), lambda b, h, i: (0, 0)),
        pl.BlockSpec((heads * LANES, tq), lambda b, h, i: (h, b * nq + i)),
    ]
    args = [lam4, subln_col, qt]
    cache_len = 0
    if has_cache:
        kc, vct = cache
        cache_len = kc.shape[0] // batch
        in_specs += [pl.BlockSpec((cache_len, heads * LANES), lambda b, h, i: (b, h)),
                     pl.BlockSpec((heads * V_EXT, cache_len), lambda b, h, i: (h, b))]
        args += [kc, vct]
    in_specs += [pl.BlockSpec((seq, heads * LANES), lambda b, h, i: (b, h)),
                 pl.BlockSpec((heads * V_EXT, seq), lambda b, h, i: (h, b))]
    args += [k, vt]
    return pl.pallas_call(
        functools.partial(_attn_kernel, has_cache=has_cache, heads=heads, tq=tq, tw=tw, tk=tk,
                          seq=seq, cache_len=cache_len),
        grid=(batch, N_HEADS // heads, nq),
        in_specs=in_specs,
        out_specs=pl.BlockSpec((tq, heads * LANES), lambda b, h, i: (b * nq + i, h)),
        out_shape=jax.ShapeDtypeStruct((batch * seq, ATTN_WIDTH), BF16),
        compiler_params=_cparams(("arbitrary", "arbitrary", "arbitrary"), VMEM_LIMIT),
        name="attn_cache" if has_cache else "attn",
    )(*args)


def _post_kernel(x_ref, a_ref, p_ref, pp_ref, pn_ref, mod_ref, wout_ref, wbd_ref, ps_ref, g2_ref,
                 wr_ref, x1_ref, h2_ref, aff_ref, affc_ref, *, tm, seq):
    i = pl.program_id(0)
    tiles_per_seq = seq // tm
    ti = i % tiles_per_seq
    p = p_ref[...]
    prev = jnp.where(ti == 0, 0.0, pp_ref[...])
    nxt = jnp.where(ti == tiles_per_seq - 1, 0.0, pn_ref[...])
    ext = jnp.concatenate([prev, p, nxt], axis=0)
    n_ext = tm + 2 * HALO
    s2 = ext + pltpu.roll(ext, 1, 0)
    s4 = pltpu.roll(s2, 1, 0) + pltpu.roll(s2, n_ext - 1, 0)
    s8 = pltpu.roll(s4, 2, 0) + pltpu.roll(s4, n_ext - 2, 0)
    s16 = pltpu.roll(s8, 4, 0) + pltpu.roll(s8, n_ext - 4, 0)
    lane = lax.broadcasted_iota(jnp.int32, (tm, POOL_WIDTH), 1)
    grp = lane // (POOL_WIDTH // 4)
    win = jnp.where(grp == 0, s2[HALO:HALO + tm],
                    jnp.where(grp == 1, s4[HALO:HALO + tm],
                              jnp.where(grp == 2, s8[HALO:HALO + tm], s16[HALO:HALO + tm])))
    t = ti * tm + lax.broadcasted_iota(jnp.int32, (tm, POOL_WIDTH), 0)
    left = jnp.where(grp == 0, 1, jnp.where(grp == 1, 2, jnp.where(grp == 2, 4, 8)))
    lo = jnp.maximum(t - left, 0)
    hi = jnp.minimum(t + left - 1, seq - 1) + 1
    pooled = win / (hi - lo).astype(F32) - p
    pool = jnp.dot(pooled.astype(BF16), wbd_ref[...], preferred_element_type=F32) * ps_ref[...]

    cat = jnp.concatenate([pool.astype(BF16), a_ref[...]], axis=1)
    mix = jnp.dot(cat, wout_ref[...], preferred_element_type=F32)
    mod = mod_ref[0]
    gate1 = mod[:, 0:D_MODEL]
    shift2 = mod[:, D_MODEL:2 * D_MODEL]
    scale2 = mod[:, 2 * D_MODEL:3 * D_MODEL]
    x1 = x_ref[...] + gate1 * mix
    x1_ref[...] = x1
    ms = jnp.mean(x1 * x1, axis=1, keepdims=True)
    h2 = x1 * lax.rsqrt(ms + EPS) * g2_ref[...] * (1.0 + scale2) + shift2
    h2_ref[...] = h2

    logits = lax.dot_general(wr_ref[...], h2, (((1,), (1,)), ((), ())),
                             precision=lax.Precision.HIGHEST, preferred_element_type=F32)
    e = jnp.exp(logits - jnp.max(logits, axis=0, keepdims=True))
    aff = e / jnp.sum(e, axis=0, keepdims=True)
    aff_ref[...] = aff
    for c in range(tm // LANES):
        affc_ref[c * N_EXPERTS:(c + 1) * N_EXPERTS, :] = aff[:, c * LANES:(c + 1) * LANES]


def _post_call(x2d, attn, p, mod3, wout_bf, wbd_bf, pool_scale, g2, wr_t, *, seq, tm, mod_base,
               mod_stride):
    n = x2d.shape[0]
    tiles_per_seq = seq // tm
    halo_per_tile = tm // HALO
    n_halo = n // HALO

    def mod_map(i):
        return (mod_base + mod_stride * (i // tiles_per_seq), 0, 0)

    return pl.pallas_call(
        functools.partial(_post_kernel, tm=tm, seq=seq),
        grid=(n // tm,),
        in_specs=[
            pl.BlockSpec((tm, D_MODEL), lambda i: (i, 0)),
            pl.BlockSpec((tm, ATTN_WIDTH), lambda i: (i, 0)),
            pl.BlockSpec((tm, POOL_WIDTH), lambda i: (i, 0)),
            pl.BlockSpec((HALO, POOL_WIDTH), lambda i: (jnp.maximum(i * halo_per_tile - 1, 0), 0)),
            pl.BlockSpec((HALO, POOL_WIDTH),
                         lambda i: (jnp.minimum((i + 1) * halo_per_tile, n_halo - 1), 0)),
            pl.BlockSpec((1, 1, 3 * D_MODEL), lambda i: mod_map(i)[:2] + (0,)),
            pl.BlockSpec((D_MODEL, D_MODEL), lambda i: (0, 0)),
            pl.BlockSpec((POOL_WIDTH, POOL_WIDTH), lambda i: (0, 0)),
            pl.BlockSpec((1, POOL_WIDTH), lambda i: (0, 0)),
            pl.BlockSpec((1, D_MODEL), lambda i: (0, 0)),
            pl.BlockSpec((N_EXPERTS, D_MODEL), lambda i: (0, 0)),
        ],
        out_specs=[
            pl.BlockSpec((tm, D_MODEL), lambda i: (i, 0)),
            pl.BlockSpec((tm, D_MODEL), lambda i: (i, 0)),
            pl.BlockSpec((N_EXPERTS, tm), lambda i: (0, i)),
            pl.BlockSpec((tm // LANES * N_EXPERTS, LANES), lambda i: (i, 0)),
        ],
        out_shape=[
            jax.ShapeDtypeStruct((n, D_MODEL), F32),
            jax.ShapeDtypeStruct((n, D_MODEL), F32),
            jax.ShapeDtypeStruct((N_EXPERTS, n), F32),
            jax.ShapeDtypeStruct((n // LANES * N_EXPERTS, LANES), F32),
        ],
        compiler_params=_cparams(("arbitrary",), VMEM_LIMIT),
        name="post",
    )(x2d, attn, p, p, p, mod3, wout_bf, wbd_bf, pool_scale, g2, wr_t)


def _select_kernel(a_ref, ac_ref, idx_ref, gate_ref, *, n, cap):
    nc = n // LANES
    a = a_ref[...]
    thr = jnp.zeros((N_EXPERTS, 1), jnp.int32)
    for bit in range(30, -1, -1):
        cand = thr | (1 << bit)
        cnt = jnp.sum(jnp.where(a >= pltpu.bitcast(cand, F32), 1.0, 0.0), axis=1, keepdims=True)
        thr = jnp.where(cnt >= cap, cand, thr)
    thr_all = pltpu.bitcast(thr, F32)
    need_all = cap - jnp.sum(jnp.where(a > thr_all, 1.0, 0.0), axis=1, keepdims=True)

    r = lax.broadcasted_iota(jnp.int32, (LANES, LANES), 0)
    c = lax.broadcasted_iota(jnp.int32, (LANES, LANES), 1)
    upper = jnp.where(r <= c, 1.0, 0.0).astype(BF16)
    lower = jnp.where(c < r, 1.0, 0.0).astype(BF16)
    row_valid = r < nc
    chunk_col = lax.broadcasted_iota(jnp.int32, (LANES, 1), 0).astype(F32)
    slot = lax.broadcasted_iota(jnp.int32, (1, cap), 1).astype(F32)

    def lane_counts(mask):
        local = jnp.dot(mask.astype(BF16), upper, preferred_element_type=F32)
        total = jnp.broadcast_to(local[:, LANES - 1:LANES], (LANES, LANES))
        before = jnp.dot(lower, total.astype(BF16), preferred_element_type=F32)
        return local, total, before

    for e in range(N_EXPERTS):
        av = ac_ref[pl.ds(e, nc, stride=N_EXPERTS), :]
        if nc < LANES:
            av = jnp.concatenate([av, jnp.zeros((LANES - nc, LANES), F32)], axis=0)
        thr_e = thr_all[e:e + 1, :]
        above = jnp.where(row_valid & (av > thr_e), 1.0, 0.0)
        tied = jnp.where(row_valid & (av == thr_e), 1.0, 0.0)
        t_local, _, t_before = lane_counts(tied)
        sel = above + tied * jnp.where(t_local + t_before <= need_all[e:e + 1, :], 1.0, 0.0)
        s_local, s_total, s_before = lane_counts(sel)
        rank = jnp.where(sel > 0.0, s_local, 0.0)

        start = s_before[:, 0:1]
        stop = start + s_total[:, 0:1]
        onehot = jnp.where((slot >= start) & (slot < stop), 1.0, 0.0)
        chunk_of_slot = jnp.sum(onehot * chunk_col, axis=0, keepdims=True)
        start_of_slot = jnp.sum(onehot * start, axis=0, keepdims=True)

        at = av.T
        hi = at.astype(BF16)
        rest = at - hi.astype(F32)
        mid = rest.astype(BF16)
        lo = (rest - mid.astype(F32)).astype(BF16)
        lhs = jnp.concatenate([rank.T.astype(BF16), hi, mid, lo], axis=0)
        picked = jnp.dot(lhs, onehot.astype(BF16), preferred_element_type=F32)
        rank_p = picked[0:LANES]
        aff_p = picked[LANES:2 * LANES] + picked[2 * LANES:3 * LANES] + picked[3 * LANES:]
        hit = rank_p == (slot - start_of_slot + 1.0)
        lane_of_slot = jnp.sum(jnp.where(hit, chunk_col, 0.0), axis=0, keepdims=True)
        idx_ref[e:e + 1, :] = (chunk_of_slot * LANES + lane_of_slot).astype(jnp.int32)
        gate_ref[e:e + 1, :] = jnp.sum(jnp.where(hit, aff_p, 0.0), axis=0, keepdims=True)


def _select_call(aff_t, aff_c, *, cap):
    n = aff_t.shape[1]
    assert n % LANES == 0 and n // LANES <= LANES and cap % LANES == 0
    return pl.pallas_call(
        functools.partial(_select_kernel, n=n, cap=cap),
        grid=(1,),
        in_specs=[pl.BlockSpec((N_EXPERTS, n), lambda i: (0, 0)),
                  pl.BlockSpec(aff_c.shape, lambda i: (0, 0))],
        out_specs=[pl.BlockSpec((N_EXPERTS, cap), lambda i: (0, 0))] * 2,
        out_shape=[jax.ShapeDtypeStruct((N_EXPERTS, cap), jnp.int32),
                   jax.ShapeDtypeStruct((N_EXPERTS, cap), F32)],
        compiler_params=_cparams(("arbitrary",), VMEM_LIMIT),
        name="select",
    )(aff_t, aff_c)


def _moe_kernel(idx_ref, gate_ref, wg_ref, wu_ref, wd_ref, x_hbm, out_hbm,
                gbuf, ybuf, acc_ref, gsem, osem, *, tm, n_tiles_total):
    e = pl.program_id(0)
    t = pl.program_id(1)
    nt = pl.num_programs(1)
    step = e * nt + t
    slot = step % 2
    group = 8
    last = n_tiles_total - 1

    def gather_start(tile_step, k, dst_slot):
        tok = idx_ref[tile_step * tm + k]
        pltpu.make_async_copy(x_hbm.at[pl.ds(tok, 1), :],
                              gbuf.at[dst_slot, pl.ds(k, 1), :], gsem.at[dst_slot]
                              ).start(priority=GATHER_DMA_PRIORITY)

    def gather_wait(dst_slot):
        pltpu.make_async_copy(x_hbm.at[pl.ds(0, tm), :], gbuf.at[dst_slot], gsem.at[dst_slot]).wait()

    def scatter_add(tile_step, src_slot, k0):
        pending = []
        for r in range(group):
            k = k0 + r
            tok = idx_ref[tile_step * tm + k]
            off = pl.multiple_of(tok * ROW_TILES, ROW_TILES)
            src = pl.multiple_of(k * ROW_TILES, ROW_TILES)
            pending.append((off, acc_ref[pl.ds(off, ROW_TILES), :]
                            + ybuf[src_slot, pl.ds(src, ROW_TILES), :]))
        for off, val in pending:
            acc_ref[pl.ds(off, ROW_TILES), :] = val

    @pl.when(step == 0)
    def _():
        acc_ref[...] = jnp.zeros_like(acc_ref)
        ybuf[...] = jnp.zeros_like(ybuf)

        def body(k, _):
            gather_start(0, k, 0)
            return 0
        lax.fori_loop(0, tm, body, 0)

    gather_wait(slot)
    xe = gbuf[slot].astype(BF16)

    nxt = jnp.minimum(step + 1, last)
    prev = jnp.maximum(step - 1, 0)
    g = jnp.dot(xe, wg_ref[...], preferred_element_type=F32)
    for k in range(tm // 2):
        gather_start(nxt, k, 1 - slot)
    u = jnp.dot(xe, wu_ref[...], preferred_element_type=F32)
    for k in range(tm // 2, tm):
        gather_start(nxt, k, 1 - slot)
    hid = (g * jax.nn.sigmoid(g) * u).astype(BF16)
    y = jnp.dot(hid, wd_ref[...], preferred_element_type=F32)
    for k0 in range(0, tm, group):
        scatter_add(prev, 1 - slot, k0)
    gate = jnp.broadcast_to(gate_ref[...], (LANES, tm)).T
    for s in range(ROW_TILES):
        ybuf[slot, pl.ds(s, tm, stride=ROW_TILES), :] = y[:, s * LANES:(s + 1) * LANES] * gate

    @pl.when(step == last)
    def _():
        gather_wait(1 - slot)

        def body(kk, _):
            scatter_add(step, slot, kk * group)
            return 0
        lax.fori_loop(0, tm // group, body, 0)
        cp = pltpu.make_async_copy(acc_ref, out_hbm, osem)
        cp.start()
        cp.wait()


def _moe_call(idx_flat, gates3, wg_bf, wu_bf, wd_bf, h2, *, cap, tm):
    n_rows = h2.shape[0] * ROW_TILES
    nt = cap // tm
    grid_spec = pltpu.PrefetchScalarGridSpec(
        num_scalar_prefetch=1,
        grid=(N_EXPERTS, nt),
        in_specs=[
            pl.BlockSpec((None, 1, tm), lambda e, t, idx: (e * nt + t, 0, 0)),
            pl.BlockSpec((None, D_MODEL, D_MODEL), lambda e, t, idx: (e, 0, 0)),
            pl.BlockSpec((None, D_MODEL, D_MODEL), lambda e, t, idx: (e, 0, 0)),
            pl.BlockSpec((None, D_MODEL, D_MODEL), lambda e, t, idx: (e, 0, 0)),
            pl.BlockSpec(memory_space=pl.ANY),
        ],
        out_specs=pl.BlockSpec(memory_space=pl.ANY),
        scratch_shapes=[
            pltpu.VMEM((2, tm, D_MODEL), F32),
            pltpu.VMEM((2, tm * ROW_TILES, LANES), F32),
            pltpu.VMEM((n_rows, LANES), F32),
            pltpu.SemaphoreType.DMA((2,)),
            pltpu.SemaphoreType.DMA(()),
        ],
    )
    return pl.pallas_call(
        functools.partial(_moe_kernel, tm=tm, n_tiles_total=N_EXPERTS * nt),
        grid_spec=grid_spec,
        out_shape=jax.ShapeDtypeStruct((n_rows, LANES), F32),
        compiler_params=_cparams(("arbitrary", "arbitrary"), VMEM_LIMIT),
        name="moe",
    )(idx_flat, gates3, wg_bf, wu_bf, wd_bf, h2)


def _final_kernel(x1_ref, moe_ref, mod_ref, o_ref, *, tm):
    moe = jnp.concatenate(
        [moe_ref[pl.ds(s, tm, stride=ROW_TILES), :] for s in range(ROW_TILES)], axis=1)
    o_ref[...] = x1_ref[...] + mod_ref[0] * moe


def _final_call(x1, moe_tiles, mod3, *, seq, tm, mod_base, mod_stride):
    n = x1.shape[0]
    tiles_per_seq = seq // tm
    gate2_block = 5

    def mod_map(i):
        return (mod_base + mod_stride * (i // tiles_per_seq), 0, gate2_block)

    return pl.pallas_call(
        functools.partial(_final_kernel, tm=tm),
        grid=(n // tm,),
        in_specs=[pl.BlockSpec((tm, D_MODEL), lambda i: (i, 0)),
                  pl.BlockSpec((tm * ROW_TILES, LANES), lambda i: (i, 0)),
                  pl.BlockSpec((1, 1, D_MODEL), mod_map)],
        out_specs=pl.BlockSpec((tm, D_MODEL), lambda i: (i, 0)),
        out_shape=jax.ShapeDtypeStruct((n, D_MODEL), F32),
        compiler_params=_cparams(("arbitrary",)),
        name="final",
    )(x1, moe_tiles, mod3)


def _rope_tables(seq):
    t = np.arange(seq)
    row, col = t // GRID_W, t % GRID_W
    half = HEAD_DIM // 2
    freqs = 1.0 / (ROPE_BASE ** (np.arange(0, half, 2) / half))
    ang_r = row[:, None] * freqs[None, :]
    ang_c = col[:, None] * freqs[None, :]
    ang = np.concatenate([ang_r, ang_r, ang_c, ang_c], axis=-1)
    cos = np.tile(np.cos(ang), (1, LANES // HEAD_DIM))
    sin = np.tile(np.sin(ang), (1, LANES // HEAD_DIM))
    sign = np.where((np.arange(LANES) % (HEAD_DIM // 2)) < (HEAD_DIM // 4), -1.0, 1.0)
    return jnp.asarray(cos, F32), jnp.asarray(sin * sign[None, :], F32)


def _segment_matrix():
    seg = np.arange(ATTN_WIDTH) // HEAD_DIM
    return jnp.asarray((seg[:, None] == seg[None, :]) / HEAD_DIM, BF16)


def _trunk(x, mod3, w, cache, *, mod_base, mod_stride, tm, heads, tq, tk, moe_tm):
    batch, seq, _ = x.shape
    n = batch * seq
    x2d = x.reshape(n, D_MODEL)
    rope_tabs = _rope_tables(seq) if cache is not None else None
    pre = _pre_call(x2d, mod3[:, :, :2 * D_MODEL], w["g1"], w["win"], w["seg"], w["qg"], w["kg"],
                    rope_tabs, seq=seq, tm=tm, mod_base=mod_base, mod_stride=mod_stride,
                    emit_f32_kv=cache is None)
    p, q, k, v = pre[:4]
    attn = _attn_call(w["lam4"], w["subln"], q, k, v, cache, batch=batch, seq=seq,
                      heads=heads, tq=tq, tw=256, tk=tk)
    x1, h2, aff_t, aff_c = _post_call(x2d, attn, p, mod3[:, :, 2 * D_MODEL:5 * D_MODEL], w["wout"],
                                     w["wbd"], w["pool_scale"], w["g2"], w["wr_t"], seq=seq, tm=tm,
                                     mod_base=mod_base, mod_stride=mod_stride)
    cap = CAPACITY_FACTOR * n // N_EXPERTS
    idx, gates = _select_call(aff_t, aff_c, cap=cap)
    moe_tiles = _moe_call(idx.reshape(N_EXPERTS * cap), gates.reshape(-1, 1, moe_tm),
                          w["wg"], w["wu"], w["wd"], h2, cap=cap, tm=moe_tm)
    y = _final_call(x1, moe_tiles, mod3, seq=seq, tm=tm, mod_base=mod_base, mod_stride=mod_stride)
    y = y.reshape(batch, seq, D_MODEL)
    if cache is None:
        return y, pre[4], pre[5]
    return y, None, None


def kernel(x_prompt, x_sample, cache_k, cache_v, c, c_ctx, norm1_g, norm2_g, w_ada, b_ada, w_in,
           q_norm_g, k_norm_g, lambda_q1, lambda_k1, lambda_q2, lambda_k2, subln_g, w_pool,
           pool_scale, w_out, w_router, w_gate, w_up, w_down):
    assert w_ada.shape[0] == 1, "single-layer stack"
    batch, seq, _ = x_prompt.shape
    dec_batch, dec_seq, _ = x_sample.shape

    pad = SUBLANES - 1 - dec_batch
    cvec = jnp.concatenate([c_ctx[None, :], c, jnp.zeros((pad, D_MODEL), F32)], axis=0)
    mod = _ada_call(cvec, w_ada[0], b_ada[0])
    mod3 = mod.reshape(SUBLANES, 1, 6 * D_MODEL)

    n_groups = w_pool.shape[1]
    grp = POOL_WIDTH // n_groups
    eye = jnp.eye(n_groups, dtype=F32)
    wbd = (w_pool[0][:, :, None, :] * eye[:, None, :, None]).reshape(POOL_WIDTH, POOL_WIDTH)

    w = {
        "g1": norm1_g[0].reshape(1, D_MODEL),
        "g2": norm2_g[0].reshape(1, D_MODEL),
        "win": w_in[0].astype(BF16),
        "seg": _segment_matrix(),
        "qg": jnp.tile(q_norm_g[0], ATTN_WIDTH // HEAD_DIM).reshape(1, ATTN_WIDTH),
        "kg": jnp.tile(k_norm_g[0], ATTN_WIDTH // HEAD_DIM).reshape(1, ATTN_WIDTH),
        "lam4": jnp.stack([lambda_q1[0], lambda_k1[0], lambda_q2[0], lambda_k2[0]], axis=0),
        "subln": subln_g[0].reshape(V_DIM, 1),
        "wbd": wbd.astype(BF16),
        "pool_scale": pool_scale[0].reshape(1, POOL_WIDTH),
        "wout": w_out[0].astype(BF16),
        "wr_t": w_router[0].T,
        "wg": w_gate[0].astype(BF16),
        "wu": w_up[0].astype(BF16),
        "wd": w_down[0].astype(BF16),
    }

    yp, k_ctx, v_ctx = _trunk(x_prompt, mod3, w, None, mod_base=0, mod_stride=0,
                              tm=256, heads=N_HEADS, tq=256, tk=256, moe_tm=256)
    past = cache_k.shape[2]
    cv = cache_v[:, 0].reshape(dec_batch * past, N_HEADS, V_DIM).transpose(1, 2, 0).astype(BF16)
    cv = jnp.concatenate([cv, jnp.ones((N_HEADS, V_EXT - V_DIM, dec_batch * past), BF16)], axis=1)
    cache = (cache_k[:, 0].reshape(dec_batch * past, ATTN_WIDTH).astype(BF16),
             cv.reshape(N_HEADS * V_EXT, dec_batch * past))
    ys, _, _ = _trunk(x_sample, mod3, w, cache, mod_base=1, mod_stride=1,
                      tm=256, heads=1, tq=512, tk=512, moe_tm=256)
    ctx_k = (k_ctx.reshape(batch, N_HEADS, 2, HEAD_DIM, seq).transpose(0, 4, 1, 2, 3)
             .reshape(batch, 1, seq, N_HEADS, 2, HEAD_DIM))
    ctx_v = v_ctx.transpose(0, 2, 1, 3).reshape(batch, 1, seq, N_HEADS, V_DIM)
    return yp, ys, ctx_k, ctx_v
```

```python
import functools
import math

import numpy as np
import jax
import jax.numpy as jnp
from jax import lax
from jax.experimental import pallas as pl
from jax.experimental.pallas import tpu as pltpu

F32 = jnp.float32
BF16 = jnp.bfloat16

D_MODEL = 1024
POOL_WIDTH = 256
ATTN_WIDTH = 768
N_HEADS = 6
HEAD_DIM = 64
V_DIM = 128
IN_WIDTH = POOL_WIDTH + 3 * ATTN_WIDTH
N_EXPERTS = 16
CAPACITY_FACTOR = 2
GRID_W = 64
ROPE_BASE = 10000.0
EPS = 1e-6
LAMBDA_INIT = 0.8 - 0.6 * math.exp(-0.3 * 0)
LOG2E = math.log2(math.e)
V_EXT = V_DIM + 16
SCORES_AHEAD = 3

LANES = 128
SUBLANES = 8
ROW_TILES = D_MODEL // LANES
HALO = 16
VMEM_LIMIT = 56 * 1024 * 1024


def _cparams(sem, vmem=None):
    return pltpu.CompilerParams(dimension_semantics=sem, vmem_limit_bytes=vmem)


def _ada_kernel(c_ref, w_ref, b_ref, o_ref):
    c = c_ref[...]
    s = c * jax.nn.sigmoid(c)
    o_ref[...] = jnp.dot(s.astype(BF16), w_ref[...].astype(BF16),
                         preferred_element_type=F32) + b_ref[...]


def _ada_call(cvec, w_ada, b_ada):
    rows, d = cvec.shape
    n = w_ada.shape[1]
    bn = 1536
    return pl.pallas_call(
        _ada_kernel,
        grid=(n // bn,),
        in_specs=[pl.BlockSpec((rows, d), lambda j: (0, 0)),
                  pl.BlockSpec((d, bn), lambda j: (0, j)),
                  pl.BlockSpec((1, bn), lambda j: (0, j))],
        out_specs=pl.BlockSpec((rows, bn), lambda j: (0, j)),
        out_shape=jax.ShapeDtypeStruct((rows, n), F32),
        compiler_params=_cparams(("arbitrary",)),
        name="ada",
    )(cvec, w_ada, b_ada.reshape(1, n))


def _segment_mean_square(a, seg_ref):
    return jnp.dot((a * a).astype(BF16), seg_ref[...], preferred_element_type=F32)


def _rope(a, cos, sin_signed, first_half):
    parts = []
    for h in range(a.shape[1] // LANES):
        blk = a[:, h * LANES:(h + 1) * LANES]
        fwd = pltpu.roll(blk, LANES - HEAD_DIM // 4, 1)
        bwd = pltpu.roll(blk, HEAD_DIM // 4, 1)
        parts.append(blk * cos + jnp.where(first_half, fwd, bwd) * sin_signed)
    return jnp.concatenate(parts, axis=1)


def _pre_kernel(*refs, rope, emit_f32_kv):
    x_ref, mod_ref, g1_ref, win_ref, seg_ref, qg_ref, kg_ref = refs[:7]
    pos = 7
    if rope:
        cos_ref, sin_ref = refs[pos:pos + 2]
        pos += 2
    p_ref, q_ref, k_ref, v_ref = refs[pos:pos + 4]
    pos += 4
    if emit_f32_kv:
        kf_ref, vf_ref = refs[pos:pos + 2]

    x = x_ref[...]
    mod = mod_ref[0]
    shift1 = mod[:, :D_MODEL]
    scale1 = mod[:, D_MODEL:2 * D_MODEL]
    ms = jnp.mean(x * x, axis=1, keepdims=True)
    h = x * lax.rsqrt(ms + EPS) * g1_ref[...] * (1.0 + scale1) + shift1
    z = jnp.dot(h.astype(BF16), win_ref[...], preferred_element_type=F32)

    p_ref[...] = z[:, :POOL_WIDTH]
    qz = z[:, POOL_WIDTH:POOL_WIDTH + ATTN_WIDTH]
    kz = z[:, POOL_WIDTH + ATTN_WIDTH:POOL_WIDTH + 2 * ATTN_WIDTH]
    vz = z[:, POOL_WIDTH + 2 * ATTN_WIDTH:]

    qn = qz * lax.rsqrt(_segment_mean_square(qz, seg_ref) + EPS) * qg_ref[...]
    kn = kz * lax.rsqrt(_segment_mean_square(kz, seg_ref) + EPS) * kg_ref[...]
    if rope:
        cos = cos_ref[...]
        sin_signed = sin_ref[...]
        lane = lax.broadcasted_iota(jnp.int32, cos.shape, 1)
        first_half = (lane % (HEAD_DIM // 2)) < (HEAD_DIM // 4)
        qn = _rope(qn, cos, sin_signed, first_half)
        kn = _rope(kn, cos, sin_signed, first_half)

    q_ref[...] = (qn * (LOG2E / math.sqrt(HEAD_DIM))).T.astype(BF16)
    k_ref[...] = kn.astype(BF16)
    vt = vz.T
    ones = jnp.ones((V_EXT - V_DIM, vt.shape[1]), BF16)
    for h in range(N_HEADS):
        v_ref[h * V_EXT:h * V_EXT + V_DIM, :] = vt[h * V_DIM:(h + 1) * V_DIM, :].astype(BF16)
        v_ref[h * V_EXT + V_DIM:(h + 1) * V_EXT, :] = ones
    if emit_f32_kv:
        kf_ref[0] = kn.T
        for h in range(N_HEADS):
            vf_ref[0, h] = vz[:, h * V_DIM:(h + 1) * V_DIM]


def _pre_call(x2d, mod3, g1, win_bf, seg, qg, kg, rope_tabs, *, seq, tm, mod_base, mod_stride,
              emit_f32_kv):
    n = x2d.shape[0]
    rope = rope_tabs is not None
    tiles_per_seq = seq // tm

    def mod_map(i):
        return (mod_base + mod_stride * (i // tiles_per_seq), 0, 0)

    in_specs = [
        pl.BlockSpec((tm, D_MODEL), lambda i: (i, 0)),
        pl.BlockSpec((1, 1, 2 * D_MODEL), mod_map),
        pl.BlockSpec((1, D_MODEL), lambda i: (0, 0)),
        pl.BlockSpec((D_MODEL, IN_WIDTH), lambda i: (0, 0)),
        pl.BlockSpec((ATTN_WIDTH, ATTN_WIDTH), lambda i: (0, 0)),
        pl.BlockSpec((1, ATTN_WIDTH), lambda i: (0, 0)),
        pl.BlockSpec((1, ATTN_WIDTH), lambda i: (0, 0)),
    ]
    args = [x2d, mod3, g1, win_bf, seg, qg, kg]
    if rope:
        in_specs += [pl.BlockSpec((tm, LANES), lambda i: (i % tiles_per_seq, 0))] * 2
        args += list(rope_tabs)
    out_shapes = [jax.ShapeDtypeStruct((n, POOL_WIDTH), F32)]
    out_specs = [pl.BlockSpec((tm, POOL_WIDTH), lambda i: (i, 0))]
    out_shapes += [jax.ShapeDtypeStruct((ATTN_WIDTH, n), BF16),
                   jax.ShapeDtypeStruct((n, ATTN_WIDTH), BF16),
                   jax.ShapeDtypeStruct((N_HEADS * V_EXT, n), BF16)]
    out_specs += [pl.BlockSpec((ATTN_WIDTH, tm), lambda i: (0, i)),
                  pl.BlockSpec((tm, ATTN_WIDTH), lambda i: (i, 0)),
                  pl.BlockSpec((N_HEADS * V_EXT, tm), lambda i: (0, i))]
    if emit_f32_kv:
        def seq_map(i):
            return (i // tiles_per_seq, 0, i % tiles_per_seq)

        out_shapes += [jax.ShapeDtypeStruct((n // seq, ATTN_WIDTH, seq), F32),
                       jax.ShapeDtypeStruct((n // seq, N_HEADS, seq, V_DIM), F32)]
        out_specs += [pl.BlockSpec((1, ATTN_WIDTH, tm), seq_map),
                      pl.BlockSpec((1, N_HEADS, tm, V_DIM),
                                   lambda i: (i // tiles_per_seq, 0, i % tiles_per_seq, 0))]
    return pl.pallas_call(
        functools.partial(_pre_kernel, rope=rope, emit_f32_kv=emit_f32_kv),
        grid=(n // tm,),
        in_specs=in_specs,
        out_specs=out_specs,
        out_shape=out_shapes,
        compiler_params=_cparams(("arbitrary",), VMEM_LIMIT),
        name="pre_rope" if rope else "pre",
    )(*args)


def _attn_kernel(*refs, has_cache, heads, tq, tw, tk, seq, cache_len):
    if has_cache:
        lam_ref, g_ref, q_ref, kc_ref, vc_ref, k_ref, v_ref, o_ref = refs
    else:
        lam_ref, g_ref, q_ref, k_ref, v_ref, o_ref = refs

    lv = lam_ref[...]
    lam = (jnp.exp(jnp.sum(lv[0:1] * lv[1:2], axis=1, keepdims=True))
           - jnp.exp(jnp.sum(lv[2:3] * lv[3:4], axis=1, keepdims=True)) + LAMBDA_INIT)

    row = lax.broadcasted_iota(jnp.int32, (LANES, tq), 0)
    zero = jnp.zeros((LANES, tq), BF16)

    def sub_queries(h):
        qt = q_ref[h * LANES:(h + 1) * LANES, :]
        return (jnp.where(row < HEAD_DIM, qt, zero), jnp.where(row >= HEAD_DIM, qt, zero))

    def scores(kb, q_one):
        return jnp.dot(kb, q_one, preferred_element_type=F32)

    def softmax_step(s, m):
        m_new = jnp.maximum(m, jnp.max(s, axis=0, keepdims=True))
        return m_new, jnp.exp2(m - m_new), jnp.exp2(s - m_new).astype(BF16)

    def accumulate(vb, p, alpha, acc):
        return alpha * acc + jnp.dot(vb, p, preferred_element_type=F32)

    chunks = []
    if has_cache:
        chunks += [(kc_ref, vc_ref, j) for j in range(cache_len // tk)]
    chunks += [(k_ref, v_ref, j) for j in range(seq // tk)]

    def keys(c, h):
        kr, _, j = chunks[c]
        return kr[j * tk:(j + 1) * tk, h * LANES:(h + 1) * LANES]

    def values(c, h):
        _, vr, j = chunks[c]
        return vr[h * V_EXT:(h + 1) * V_EXT, j * tk:(j + 1) * tk]

    chains = []
    for h in range(heads):
        q_sub = sub_queries(h)
        chains += [(h, q_sub[sub][:, w * tw:(w + 1) * tw])
                   for w in range(tq // tw) for sub in range(2)]
    per_head = len(chains) // heads
    items = [(c, ch) for h in range(heads) for c in range(len(chunks))
             for ch in range(h * per_head, (h + 1) * per_head)]
    m = [jnp.full((1, tw), -jnp.inf, F32)] * len(chains)
    acc = [jnp.zeros((V_EXT, tw), F32)] * len(chains)
    queue = [scores(keys(c, chains[ch][0]), chains[ch][1]) for c, ch in items[:SCORES_AHEAD]]
    for i, (c, ch) in enumerate(items):
        s_cur = queue.pop(0)
        if i + SCORES_AHEAD < len(items):
            nc, nch = items[i + SCORES_AHEAD]
            queue.append(scores(keys(nc, chains[nch][0]), chains[nch][1]))
        m[ch], alpha, p = softmax_step(s_cur, m[ch])
        acc[ch] = accumulate(values(c, chains[ch][0]), p, alpha, acc[ch])
    for h in range(heads):
        outs = []
        for w in range(tq // tw):
            a1, a2 = acc[h * per_head + 2 * w], acc[h * per_head + 2 * w + 1]
            outs.append(a1[:V_DIM] / a1[V_DIM:V_DIM + 1]
                        - lam * (a2[:V_DIM] / a2[V_DIM:V_DIM + 1]))
        o = outs[0] if len(outs) == 1 else jnp.concatenate(outs, axis=1)
        y = (o * lax.rsqrt(jnp.mean(o * o, axis=0, keepdims=True) + EPS) * g_ref[...]
             * (1.0 - LAMBDA_INIT))
        o_ref[:, h * LANES:(h + 1) * LANES] = y.T.astype(BF16)


def _attn_call(lam4, subln_col, qt, k, vt, cache, *, batch, seq, heads, tq, tw, tk):
    has_cache = cache is not None
    nq = seq // tq
    in_specs = [
        pl.BlockSpec((4, HEAD_DIM), lambda b, h, i: (0, 0)),
        pl.BlockSpec((V_DIM, 1), lambda b, h, i: (0, 0)),
        pl.BlockSpec((heads * LANES, tq), lambda b, h, i: (h, b * nq + i)),
    ]
    args = [lam4, subln_col, qt]
    cache_len = 0
    if has_cache:
        kc, vct = cache
        cache_len = kc.shape[0] // batch
        in_specs += [pl.BlockSpec((cache_len, heads * LANES), lambda b, h, i: (b, h)),
                     pl.BlockSpec((heads * V_EXT, cache_len), lambda b, h, i: (h, b))]
        args += [kc, vct]
    in_specs += [pl.BlockSpec((seq, heads * LANES), lambda b, h, i: (b, h)),
                 pl.BlockSpec((heads * V_EXT, seq), lambda b, h, i: (h, b))]
    args += [k, vt]
    return pl.pallas_call(
        functools.partial(_attn_kernel, has_cache=has_cache, heads=heads, tq=tq, tw=tw, tk=tk,
                          seq=seq, cache_len=cache_len),
        grid=(batch, N_HEADS // heads, nq),
        in_specs=in_specs,
        out_specs=pl.BlockSpec((tq, heads * LANES), lambda b, h, i: (b * nq + i, h)),
        out_shape=jax.ShapeDtypeStruct((batch * seq, ATTN_WIDTH), BF16),
        compiler_params=_cparams(("arbitrary", "arbitrary", "arbitrary"), VMEM_LIMIT),
        name="attn_cache" if has_cache else "attn",
    )(*args)


def _post_kernel(x_ref, a_ref, p_ref, pp_ref, pn_ref, mod_ref, wout_ref, wbd_ref, ps_ref, g2_ref,
                 wr_ref, x1_ref, h2_ref, aff_ref, affc_ref, *, tm, seq):
    i = pl.program_id(0)
    tiles_per_seq = seq // tm
    ti = i % tiles_per_seq
    p = p_ref[...]
    prev = jnp.where(ti == 0, 0.0, pp_ref[...])
    nxt = jnp.where(ti == tiles_per_seq - 1, 0.0, pn_ref[...])
    ext = jnp.concatenate([prev, p, nxt], axis=0)
    n_ext = tm + 2 * HALO
    s2 = ext + pltpu.roll(ext, 1, 0)
    s4 = pltpu.roll(s2, 1, 0) + pltpu.roll(s2, n_ext - 1, 0)
    s8 = pltpu.roll(s4, 2, 0) + pltpu.roll(s4, n_ext - 2, 0)
    s16 = pltpu.roll(s8, 4, 0) + pltpu.roll(s8, n_ext - 4, 0)
    lane = lax.broadcasted_iota(jnp.int32, (tm, POOL_WIDTH), 1)
    grp = lane // (POOL_WIDTH // 4)
    win = jnp.where(grp == 0, s2[HALO:HALO + tm],
                    jnp.where(grp == 1, s4[HALO:HALO + tm],
                              jnp.where(grp == 2, s8[HALO:HALO + tm], s16[HALO:HALO + tm])))
    t = ti * tm + lax.broadcasted_iota(jnp.int32, (tm, POOL_WIDTH), 0)
    left = jnp.where(grp == 0, 1, jnp.where(grp == 1, 2, jnp.where(grp == 2, 4, 8)))
    lo = jnp.maximum(t - left, 0)
    hi = jnp.minimum(t + left - 1, seq - 1) + 1
    pooled = win / (hi - lo).astype(F32) - p
    pool = jnp.dot(pooled.astype(BF16), wbd_ref[...], preferred_element_type=F32) * ps_ref[...]

    cat = jnp.concatenate([pool.astype(BF16), a_ref[...]], axis=1)
    mix = jnp.dot(cat, wout_ref[...], preferred_element_type=F32)
    mod = mod_ref[0]
    gate1 = mod[:, 0:D_MODEL]
    shift2 = mod[:, D_MODEL:2 * D_MODEL]
    scale2 = mod[:, 2 * D_MODEL:3 * D_MODEL]
    x1 = x_ref[...] + gate1 * mix
    x1_ref[...] = x1
    ms = jnp.mean(x1 * x1, axis=1, keepdims=True)
    h2 = x1 * lax.rsqrt(ms + EPS) * g2_ref[...] * (1.0 + scale2) + shift2
    for s in range(ROW_TILES):
        h2_ref[pl.ds(s, tm, stride=ROW_TILES), :] = h2[:, s * LANES:(s + 1) * LANES]

    logits = lax.dot_general(wr_ref[...], h2, (((1,), (1,)), ((), ())),
                             precision=lax.Precision.HIGHEST, preferred_element_type=F32)
    e = jnp.exp(logits - jnp.max(logits, axis=0, keepdims=True))
    aff = e / jnp.sum(e, axis=0, keepdims=True)
    aff_ref[...] = aff
    for c in range(tm // LANES):
        affc_ref[c * N_EXPERTS:(c + 1) * N_EXPERTS, :] = aff[:, c * LANES:(c + 1) * LANES]


def _post_call(x2d, attn, p, mod3, wout_bf, wbd_bf, pool_scale, g2, wr_t, *, seq, tm, mod_base,
               mod_stride):
    n = x2d.shape[0]
    tiles_per_seq = seq // tm
    halo_per_tile = tm // HALO
    n_halo = n // HALO

    def mod_map(i):
        return (mod_base + mod_stride * (i // tiles_per_seq), 0, 0)

    return pl.pallas_call(
        functools.partial(_post_kernel, tm=tm, seq=seq),
        grid=(n // tm,),
        in_specs=[
            pl.BlockSpec((tm, D_MODEL), lambda i: (i, 0)),
            pl.BlockSpec((tm, ATTN_WIDTH), lambda i: (i, 0)),
            pl.BlockSpec((tm, POOL_WIDTH), lambda i: (i, 0)),
            pl.BlockSpec((HALO, POOL_WIDTH), lambda i: (jnp.maximum(i * halo_per_tile - 1, 0), 0)),
            pl.BlockSpec((HALO, POOL_WIDTH),
                         lambda i: (jnp.minimum((i + 1) * halo_per_tile, n_halo - 1), 0)),
            pl.BlockSpec((1, 1, 3 * D_MODEL), lambda i: mod_map(i)[:2] + (0,)),
            pl.BlockSpec((D_MODEL, D_MODEL), lambda i: (0, 0)),
            pl.BlockSpec((POOL_WIDTH, POOL_WIDTH), lambda i: (0, 0)),
            pl.BlockSpec((1, POOL_WIDTH), lambda i: (0, 0)),
            pl.BlockSpec((1, D_MODEL), lambda i: (0, 0)),
            pl.BlockSpec((N_EXPERTS, D_MODEL), lambda i: (0, 0)),
        ],
        out_specs=[
            pl.BlockSpec((tm, D_MODEL), lambda i: (i, 0)),
            pl.BlockSpec((tm * ROW_TILES, LANES), lambda i: (i, 0)),
            pl.BlockSpec((N_EXPERTS, tm), lambda i: (0, i)),
            pl.BlockSpec((tm // LANES * N_EXPERTS, LANES), lambda i: (i, 0)),
        ],
        out_shape=[
            jax.ShapeDtypeStruct((n, D_MODEL), F32),
            jax.ShapeDtypeStruct((n * ROW_TILES, LANES), F32),
            jax.ShapeDtypeStruct((N_EXPERTS, n), F32),
            jax.ShapeDtypeStruct((n // LANES * N_EXPERTS, LANES), F32),
        ],
        compiler_params=_cparams(("arbitrary",), VMEM_LIMIT),
        name="post",
    )(x2d, attn, p, p, p, mod3, wout_bf, wbd_bf, pool_scale, g2, wr_t)


def _select_kernel(a_ref, ac_ref, idx_ref, gate_ref, *, n, cap):
    nc = n // LANES
    a = a_ref[...]
    thr = jnp.zeros((N_EXPERTS, 1), jnp.int32)
    for bit in range(30, -1, -1):
        cand = thr | (1 << bit)
        cnt = jnp.sum(jnp.where(a >= pltpu.bitcast(cand, F32), 1.0, 0.0), axis=1, keepdims=True)
        thr = jnp.where(cnt >= cap, cand, thr)
    thr_all = pltpu.bitcast(thr, F32)
    need_all = cap - jnp.sum(jnp.where(a > thr_all, 1.0, 0.0), axis=1, keepdims=True)

    r = lax.broadcasted_iota(jnp.int32, (LANES, LANES), 0)
    c = lax.broadcasted_iota(jnp.int32, (LANES, LANES), 1)
    upper = jnp.where(r <= c, 1.0, 0.0).astype(BF16)
    lower = jnp.where(c < r, 1.0, 0.0).astype(BF16)
    row_valid = r < nc
    chunk_col = lax.broadcasted_iota(jnp.int32, (LANES, 1), 0).astype(F32)
    slot = lax.broadcasted_iota(jnp.int32, (1, cap), 1).astype(F32)

    def lane_counts(mask):
        local = jnp.dot(mask.astype(BF16), upper, preferred_element_type=F32)
        total = jnp.broadcast_to(local[:, LANES - 1:LANES], (LANES, LANES))
        before = jnp.dot(lower, total.astype(BF16), preferred_element_type=F32)
        return local, total, before

    for e in range(N_EXPERTS):
        av = ac_ref[pl.ds(e, nc, stride=N_EXPERTS), :]
        if nc < LANES:
            av = jnp.concatenate([av, jnp.zeros((LANES - nc, LANES), F32)], axis=0)
        thr_e = thr_all[e:e + 1, :]
        above = jnp.where(row_valid & (av > thr_e), 1.0, 0.0)
        tied = jnp.where(row_valid & (av == thr_e), 1.0, 0.0)
        t_local, _, t_before = lane_counts(tied)
        sel = above + tied * jnp.where(t_local + t_before <= need_all[e:e + 1, :], 1.0, 0.0)
        s_local, s_total, s_before = lane_counts(sel)
        rank = jnp.where(sel > 0.0, s_local, 0.0)

        start = s_before[:, 0:1]
        stop = start + s_total[:, 0:1]
        onehot = jnp.where((slot >= start) & (slot < stop), 1.0, 0.0)
        chunk_of_slot = jnp.sum(onehot * chunk_col, axis=0, keepdims=True)
        start_of_slot = jnp.sum(onehot * start, axis=0, keepdims=True)

        at = av.T
        hi = at.astype(BF16)
        rest = at - hi.astype(F32)
        mid = rest.astype(BF16)
        lo = (rest - mid.astype(F32)).astype(BF16)
        lhs = jnp.concatenate([rank.T.astype(BF16), hi, mid, lo], axis=0)
        picked = jnp.dot(lhs, onehot.astype(BF16), preferred_element_type=F32)
        rank_p = picked[0:LANES]
        aff_p = picked[LANES:2 * LANES] + picked[2 * LANES:3 * LANES] + picked[3 * LANES:]
        hit = rank_p == (slot - start_of_slot + 1.0)
        lane_of_slot = jnp.sum(jnp.where(hit, chunk_col, 0.0), axis=0, keepdims=True)
        idx_ref[e:e + 1, :] = (chunk_of_slot * LANES + lane_of_slot).astype(jnp.int32)
        gate_ref[e:e + 1, :] = jnp.sum(jnp.where(hit, aff_p, 0.0), axis=0, keepdims=True)


def _select_call(aff_t, aff_c, *, cap):
    n = aff_t.shape[1]
    assert n % LANES == 0 and n // LANES <= LANES and cap % LANES == 0
    return pl.pallas_call(
        functools.partial(_select_kernel, n=n, cap=cap),
        grid=(1,),
        in_specs=[pl.BlockSpec((N_EXPERTS, n), lambda i: (0, 0)),
                  pl.BlockSpec(aff_c.shape, lambda i: (0, 0))],
        out_specs=[pl.BlockSpec((N_EXPERTS, cap), lambda i: (0, 0))] * 2,
        out_shape=[jax.ShapeDtypeStruct((N_EXPERTS, cap), jnp.int32),
                   jax.ShapeDtypeStruct((N_EXPERTS, cap), F32)],
        compiler_params=_cparams(("arbitrary",), VMEM_LIMIT),
        name="select",
    )(aff_t, aff_c)


def _moe_kernel(idx_ref, gate_ref, wg_ref, wu_ref, wd_ref, x_hbm, out_hbm,
                gbuf, ybuf, acc_ref, gsem, osem, *, tm, n_tiles_total):
    e = pl.program_id(0)
    t = pl.program_id(1)
    nt = pl.num_programs(1)
    step = e * nt + t
    group = 8
    last = n_tiles_total - 1
    rows = tm * ROW_TILES

    def gather_start(tile_step, k, dst_slot):
        tok = idx_ref[tile_step * tm + k]
        pltpu.make_async_copy(
            x_hbm.at[pl.ds(pl.multiple_of(tok * ROW_TILES, ROW_TILES), ROW_TILES), :],
            gbuf.at[dst_slot, pl.ds(k * ROW_TILES, ROW_TILES), :], gsem.at[dst_slot]).start()

    def gather_wait(dst_slot):
        pltpu.make_async_copy(x_hbm.at[pl.ds(0, rows), :], gbuf.at[dst_slot],
                              gsem.at[dst_slot]).wait()

    def scatter_add(tile_step, src_slot, k0):
        pending = []
        for r in range(group):
            k = k0 + r
            tok = idx_ref[tile_step * tm + k]
            off = pl.multiple_of(tok * ROW_TILES, ROW_TILES)
            src = pl.multiple_of(k * ROW_TILES, ROW_TILES)
            pending.append((off, acc_ref[pl.ds(off, ROW_TILES), :]
                            + ybuf[src_slot, pl.ds(src, ROW_TILES), :]))
        for off, val in pending:
            acc_ref[pl.ds(off, ROW_TILES), :] = val

    @pl.when(step == 0)
    def _():
        acc_ref[...] = jnp.zeros_like(acc_ref)
        ybuf[...] = jnp.zeros_like(ybuf)

        def body(k, _):
            gather_start(0, k, 0)
            return 0
        lax.fori_loop(0, tm, body, 0)

    def tile_body(slot):
        gather_wait(slot)
        nxt = jnp.minimum(step + 1, last)
        prev = jnp.maximum(step - 1, 0)
        xe = jnp.concatenate(
            [gbuf[slot, pl.ds(s, tm, stride=ROW_TILES), :] for s in range(ROW_TILES)],
            axis=1).astype(BF16)
        for k in range(tm):
            gather_start(nxt, k, 1 - slot)
        g = jnp.dot(xe, wg_ref[...], preferred_element_type=F32)
        u = jnp.dot(xe, wu_ref[...], preferred_element_type=F32)
        hid = (g * jax.nn.sigmoid(g) * u).astype(BF16)
        y = jnp.dot(hid, wd_ref[...], preferred_element_type=F32)
        for k0 in range(0, tm, group):
            scatter_add(prev, 1 - slot, k0)
        gate = jnp.broadcast_to(gate_ref[...], (LANES, tm)).T
        for s in range(ROW_TILES):
            ybuf[slot, pl.ds(s, tm, stride=ROW_TILES), :] = y[:, s * LANES:(s + 1) * LANES] * gate

        @pl.when(step == last)
        def _():
            gather_wait(1 - slot)

            def body(kk, _):
                scatter_add(step, slot, kk * group)
                return 0
            lax.fori_loop(0, tm // group, body, 0)
            cp = pltpu.make_async_copy(acc_ref, out_hbm, osem)
            cp.start()
            cp.wait()

    for parity in range(2):
        pl.when(step % 2 == parity)(functools.partial(tile_body, parity))


def _moe_call(idx_flat, gates3, wg_bf, wu_bf, wd_bf, h2, *, cap, tm):
    n_rows = h2.shape[0]
    nt = cap // tm
    grid_spec = pltpu.PrefetchScalarGridSpec(
        num_scalar_prefetch=1,
        grid=(N_EXPERTS, nt),
        in_specs=[
            pl.BlockSpec((None, 1, tm), lambda e, t, idx: (e * nt + t, 0, 0)),
            pl.BlockSpec((None, D_MODEL, D_MODEL), lambda e, t, idx: (e, 0, 0)),
            pl.BlockSpec((None, D_MODEL, D_MODEL), lambda e, t, idx: (e, 0, 0)),
            pl.BlockSpec((None, D_MODEL, D_MODEL), lambda e, t, idx: (e, 0, 0)),
            pl.BlockSpec(memory_space=pl.ANY),
        ],
        out_specs=pl.BlockSpec(memory_space=pl.ANY),
        scratch_shapes=[
            pltpu.VMEM((2, tm * ROW_TILES, LANES), F32),
            pltpu.VMEM((2, tm * ROW_TILES, LANES), F32),
            pltpu.VMEM((n_rows, LANES), F32),
            pltpu.SemaphoreType.DMA((2,)),
            pltpu.SemaphoreType.DMA(()),
        ],
    )
    return pl.pallas_call(
        functools.partial(_moe_kernel, tm=tm, n_tiles_total=N_EXPERTS * nt),
        grid_spec=grid_spec,
        out_shape=jax.ShapeDtypeStruct((n_rows, LANES), F32),
        compiler_params=_cparams(("arbitrary", "arbitrary"), VMEM_LIMIT),
        name="moe",
    )(idx_flat, gates3, wg_bf, wu_bf, wd_bf, h2)


def _final_kernel(x1_ref, moe_ref, mod_ref, o_ref, *, tm):
    moe = jnp.concatenate(
        [moe_ref[pl.ds(s, tm, stride=ROW_TILES), :] for s in range(ROW_TILES)], axis=1)
    o_ref[...] = x1_ref[...] + mod_ref[0] * moe


def _final_call(x1, moe_tiles, mod3, *, seq, tm, mod_base, mod_stride):
    n = x1.shape[0]
    tiles_per_seq = seq // tm
    gate2_block = 5

    def mod_map(i):
        return (mod_base + mod_stride * (i // tiles_per_seq), 0, gate2_block)

    return pl.pallas_call(
        functools.partial(_final_kernel, tm=tm),
        grid=(n // tm,),
        in_specs=[pl.BlockSpec((tm, D_MODEL), lambda i: (i, 0)),
                  pl.BlockSpec((tm * ROW_TILES, LANES), lambda i: (i, 0)),
                  pl.BlockSpec((1, 1, D_MODEL), mod_map)],
        out_specs=pl.BlockSpec((tm, D_MODEL), lambda i: (i, 0)),
        out_shape=jax.ShapeDtypeStruct((n, D_MODEL), F32),
        compiler_params=_cparams(("arbitrary",)),
        name="final",
    )(x1, moe_tiles, mod3)


def _rope_tables(seq):
    t = np.arange(seq)
    row, col = t // GRID_W, t % GRID_W
    half = HEAD_DIM // 2
    freqs = 1.0 / (ROPE_BASE ** (np.arange(0, half, 2) / half))
    ang_r = row[:, None] * freqs[None, :]
    ang_c = col[:, None] * freqs[None, :]
    ang = np.concatenate([ang_r, ang_r, ang_c, ang_c], axis=-1)
    cos = np.tile(np.cos(ang), (1, LANES // HEAD_DIM))
    sin = np.tile(np.sin(ang), (1, LANES // HEAD_DIM))
    sign = np.where((np.arange(LANES) % (HEAD_DIM // 2)) < (HEAD_DIM // 4), -1.0, 1.0)
    return jnp.asarray(cos, F32), jnp.asarray(sin * sign[None, :], F32)


def _segment_matrix():
    seg = np.arange(ATTN_WIDTH) // HEAD_DIM
    return jnp.asarray((seg[:, None] == seg[None, :]) / HEAD_DIM, BF16)


def _trunk(x, mod3, w, cache, *, mod_base, mod_stride, tm, heads, tq, tk, moe_tm):
    batch, seq, _ = x.shape
    n = batch * seq
    x2d = x.reshape(n, D_MODEL)
    rope_tabs = _rope_tables(seq) if cache is not None else None
    pre = _pre_call(x2d, mod3[:, :, :2 * D_MODEL], w["g1"], w["win"], w["seg"], w["qg"], w["kg"],
                    rope_tabs, seq=seq, tm=tm, mod_base=mod_base, mod_stride=mod_stride,
                    emit_f32_kv=cache is None)
    p, q, k, v = pre[:4]
    attn = _attn_call(w["lam4"], w["subln"], q, k, v, cache, batch=batch, seq=seq,
                      heads=heads, tq=tq, tw=256, tk=tk)
    x1, h2, aff_t, aff_c = _post_call(x2d, attn, p, mod3[:, :, 2 * D_MODEL:5 * D_MODEL], w["wout"],
                                     w["wbd"], w["pool_scale"], w["g2"], w["wr_t"], seq=seq, tm=tm,
                                     mod_base=mod_base, mod_stride=mod_stride)
    cap = CAPACITY_FACTOR * n // N_EXPERTS
    idx, gates = _select_call(aff_t, aff_c, cap=cap)
    moe_tiles = _moe_call(idx.reshape(N_EXPERTS * cap), gates.reshape(-1, 1, moe_tm),
                          w["wg"], w["wu"], w["wd"], h2, cap=cap, tm=moe_tm)
    y = _final_call(x1, moe_tiles, mod3, seq=seq, tm=tm, mod_base=mod_base, mod_stride=mod_stride)
    y = y.reshape(batch, seq, D_MODEL)
    if cache is None:
        return y, pre[4], pre[5]
    return y, None, None


def kernel(x_prompt, x_sample, cache_k, cache_v, c, c_ctx, norm1_g, norm2_g, w_ada, b_ada, w_in,
           q_norm_g, k_norm_g, lambda_q1, lambda_k1, lambda_q2, lambda_k2, subln_g, w_pool,
           pool_scale, w_out, w_router, w_gate, w_up, w_down):
    assert w_ada.shape[0] == 1, "single-layer stack"
    batch, seq, _ = x_prompt.shape
    dec_batch, dec_seq, _ = x_sample.shape

    pad = SUBLANES - 1 - dec_batch
    cvec = jnp.concatenate([c_ctx[None, :], c, jnp.zeros((pad, D_MODEL), F32)], axis=0)
    mod = _ada_call(cvec, w_ada[0], b_ada[0])
    mod3 = mod.reshape(SUBLANES, 1, 6 * D_MODEL)

    n_groups = w_pool.shape[1]
    grp = POOL_WIDTH // n_groups
    eye = jnp.eye(n_groups, dtype=F32)
    wbd = (w_pool[0][:, :, None, :] * eye[:, None, :, None]).reshape(POOL_WIDTH, POOL_WIDTH)

    w = {
        "g1": norm1_g[0].reshape(1, D_MODEL),
        "g2": norm2_g[0].reshape(1, D_MODEL),
        "win": w_in[0].astype(BF16),
        "seg": _segment_matrix(),
        "qg": jnp.tile(q_norm_g[0], ATTN_WIDTH // HEAD_DIM).reshape(1, ATTN_WIDTH),
        "kg": jnp.tile(k_norm_g[0], ATTN_WIDTH // HEAD_DIM).reshape(1, ATTN_WIDTH),
        "lam4": jnp.stack([lambda_q1[0], lambda_k1[0], lambda_q2[0], lambda_k2[0]], axis=0),
        "subln": subln_g[0].reshape(V_DIM, 1),
        "wbd": wbd.astype(BF16),
        "pool_scale": pool_scale[0].reshape(1, POOL_WIDTH),
        "wout": w_out[0].astype(BF16),
        "wr_t": w_router[0].T,
        "wg": w_gate[0].astype(BF16),
        "wu": w_up[0].astype(BF16),
        "wd": w_down[0].astype(BF16),
    }

    yp, k_ctx, v_ctx = _trunk(x_prompt, mod3, w, None, mod_base=0, mod_stride=0,
                              tm=256, heads=N_HEADS, tq=256, tk=256, moe_tm=256)
    past = cache_k.shape[2]
    cv = cache_v[:, 0].reshape(dec_batch * past, N_HEADS, V_DIM).transpose(1, 2, 0).astype(BF16)
    cv = jnp.concatenate([cv, jnp.ones((N_HEADS, V_EXT - V_DIM, dec_batch * past), BF16)], axis=1)
    cache = (cache_k[:, 0].reshape(dec_batch * past, ATTN_WIDTH).astype(BF16),
             cv.reshape(N_HEADS * V_EXT, dec_batch * past))
    ys, _, _ = _trunk(x_sample, mod3, w, cache, mod_base=1, mod_stride=1,
                      tm=256, heads=1, tq=512, tk=512, moe_tm=256)
    ctx_k = (k_ctx.reshape(batch, N_HEADS, 2, HEAD_DIM, seq).transpose(0, 4, 1, 2, 3)
             .reshape(batch, 1, seq, N_HEADS, 2, HEAD_DIM))
    ctx_v = v_ctx.transpose(0, 2, 1, 3).reshape(batch, 1, seq, N_HEADS, V_DIM)
    return yp, ys, ctx_k, ctx_v
```

```python
import functools
import math

import numpy as np
import jax
import jax.numpy as jnp
from jax import lax
from jax.experimental import pallas as pl
from jax.experimental.pallas import tpu as pltpu

F32 = jnp.float32
BF16 = jnp.bfloat16

D_MODEL = 1024
POOL_WIDTH = 256
ATTN_WIDTH = 768
N_HEADS = 6
HEAD_DIM = 64
V_DIM = 128
IN_WIDTH = POOL_WIDTH + 3 * ATTN_WIDTH
N_EXPERTS = 16
CAPACITY_FACTOR = 2
GRID_W = 64
ROPE_BASE = 10000.0
EPS = 1e-6
LAMBDA_INIT = 0.8 - 0.6 * math.exp(-0.3 * 0)
LOG2E = math.log2(math.e)
V_EXT = V_DIM + 16
SCORES_AHEAD = 3

LANES = 128
SUBLANES = 8
ROW_TILES = D_MODEL // LANES
HALO = 16
VMEM_LIMIT = 56 * 1024 * 1024


def _cparams(sem, vmem=None):
    return pltpu.CompilerParams(dimension_semantics=sem, vmem_limit_bytes=vmem)


def _ada_kernel(c_ref, w_ref, b_ref, o_ref):
    c = c_ref[...]
    s = c * jax.nn.sigmoid(c)
    o_ref[...] = jnp.dot(s.astype(BF16), w_ref[...].astype(BF16),
                         preferred_element_type=F32) + b_ref[...]


def _ada_call(cvec, w_ada, b_ada):
    rows, d = cvec.shape
    n = w_ada.shape[1]
    bn = 1536
    return pl.pallas_call(
        _ada_kernel,
        grid=(n // bn,),
        in_specs=[pl.BlockSpec((rows, d), lambda j: (0, 0)),
                  pl.BlockSpec((d, bn), lambda j: (0, j)),
                  pl.BlockSpec((1, bn), lambda j: (0, j))],
        out_specs=pl.BlockSpec((rows, bn), lambda j: (0, j)),
        out_shape=jax.ShapeDtypeStruct((rows, n), F32),
        compiler_params=_cparams(("arbitrary",)),
        name="ada",
    )(cvec, w_ada, b_ada.reshape(1, n))


def _segment_mean_square(a, seg_ref):
    return jnp.dot((a * a).astype(BF16), seg_ref[...], preferred_element_type=F32)


def _rope(a, cos, sin_signed, first_half):
    parts = []
    for h in range(a.shape[1] // LANES):
        blk = a[:, h * LANES:(h + 1) * LANES]
        fwd = pltpu.roll(blk, LANES - HEAD_DIM // 4, 1)
        bwd = pltpu.roll(blk, HEAD_DIM // 4, 1)
        parts.append(blk * cos + jnp.where(first_half, fwd, bwd) * sin_signed)
    return jnp.concatenate(parts, axis=1)


def _pre_kernel(*refs, rope, emit_f32_kv):
    x_ref, mod_ref, g1_ref, win_ref, seg_ref, qg_ref, kg_ref = refs[:7]
    pos = 7
    if rope:
        cos_ref, sin_ref = refs[pos:pos + 2]
        pos += 2
    p_ref, q_ref, k_ref, v_ref = refs[pos:pos + 4]
    pos += 4
    if emit_f32_kv:
        kf_ref, vf_ref = refs[pos:pos + 2]

    x = x_ref[...]
    mod = mod_ref[0]
    shift1 = mod[:, :D_MODEL]
    scale1 = mod[:, D_MODEL:2 * D_MODEL]
    ms = jnp.mean(x * x, axis=1, keepdims=True)
    h = x * lax.rsqrt(ms + EPS) * g1_ref[...] * (1.0 + scale1) + shift1
    z = jnp.dot(h.astype(BF16), win_ref[...], preferred_element_type=F32)

    p_ref[...] = z[:, :POOL_WIDTH]
    qz = z[:, POOL_WIDTH:POOL_WIDTH + ATTN_WIDTH]
    kz = z[:, POOL_WIDTH + ATTN_WIDTH:POOL_WIDTH + 2 * ATTN_WIDTH]
    vz = z[:, POOL_WIDTH + 2 * ATTN_WIDTH:]

    qn = qz * lax.rsqrt(_segment_mean_square(qz, seg_ref) + EPS) * qg_ref[...]
    kn = kz * lax.rsqrt(_segment_mean_square(kz, seg_ref) + EPS) * kg_ref[...]
    if rope:
        cos = cos_ref[...]
        sin_signed = sin_ref[...]
        lane = lax.broadcasted_iota(jnp.int32, cos.shape, 1)
        first_half = (lane % (HEAD_DIM // 2)) < (HEAD_DIM // 4)
        qn = _rope(qn, cos, sin_signed, first_half)
        kn = _rope(kn, cos, sin_signed, first_half)

    q_ref[...] = (qn * (LOG2E / math.sqrt(HEAD_DIM))).T.astype(BF16)
    k_ref[...] = kn.astype(BF16)
    vt = vz.T
    ones = jnp.ones((V_EXT - V_DIM, vt.shape[1]), BF16)
    for h in range(N_HEADS):
        v_ref[h * V_EXT:h * V_EXT + V_DIM, :] = vt[h * V_DIM:(h + 1) * V_DIM, :].astype(BF16)
        v_ref[h * V_EXT + V_DIM:(h + 1) * V_EXT, :] = ones
    if emit_f32_kv:
        kf_ref[0] = kn.T
        for h in range(N_HEADS):
            vf_ref[0, h] = vz[:, h * V_DIM:(h + 1) * V_DIM]


def _pre_call(x2d, mod3, g1, win_bf, seg, qg, kg, rope_tabs, *, seq, tm, mod_base, mod_stride,
              emit_f32_kv):
    n = x2d.shape[0]
    rope = rope_tabs is not None
    tiles_per_seq = seq // tm

    def mod_map(i):
        return (mod_base + mod_stride * (i // tiles_per_seq), 0, 0)

    in_specs = [
        pl.BlockSpec((tm, D_MODEL), lambda i: (i, 0)),
        pl.BlockSpec((1, 1, 2 * D_MODEL), mod_map),
        pl.BlockSpec((1, D_MODEL), lambda i: (0, 0)),
        pl.BlockSpec((D_MODEL, IN_WIDTH), lambda i: (0, 0)),
        pl.BlockSpec((ATTN_WIDTH, ATTN_WIDTH), lambda i: (0, 0)),
        pl.BlockSpec((1, ATTN_WIDTH), lambda i: (0, 0)),
        pl.BlockSpec((1, ATTN_WIDTH), lambda i: (0, 0)),
    ]
    args = [x2d, mod3, g1, win_bf, seg, qg, kg]
    if rope:
        in_specs += [pl.BlockSpec((tm, LANES), lambda i: (i % tiles_per_seq, 0))] * 2
        args += list(rope_tabs)
    out_shapes = [jax.ShapeDtypeStruct((n, POOL_WIDTH), F32)]
    out_specs = [pl.BlockSpec((tm, POOL_WIDTH), lambda i: (i, 0))]
    out_shapes += [jax.ShapeDtypeStruct((ATTN_WIDTH, n), BF16),
                   jax.ShapeDtypeStruct((n, ATTN_WIDTH), BF16),
                   jax.ShapeDtypeStruct((N_HEADS * V_EXT, n), BF16)]
    out_specs += [pl.BlockSpec((ATTN_WIDTH, tm), lambda i: (0, i)),
                  pl.BlockSpec((tm, ATTN_WIDTH), lambda i: (i, 0)),
                  pl.BlockSpec((N_HEADS * V_EXT, tm), lambda i: (0, i))]
    if emit_f32_kv:
        def seq_map(i):
            return (i // tiles_per_seq, 0, i % tiles_per_seq)

        out_shapes += [jax.ShapeDtypeStruct((n // seq, ATTN_WIDTH, seq), F32),
                       jax.ShapeDtypeStruct((n // seq, N_HEADS, seq, V_DIM), F32)]
        out_specs += [pl.BlockSpec((1, ATTN_WIDTH, tm), seq_map),
                      pl.BlockSpec((1, N_HEADS, tm, V_DIM),
                                   lambda i: (i // tiles_per_seq, 0, i % tiles_per_seq, 0))]
    return pl.pallas_call(
        functools.partial(_pre_kernel, rope=rope, emit_f32_kv=emit_f32_kv),
        grid=(n // tm,),
        in_specs=in_specs,
        out_specs=out_specs,
        out_shape=out_shapes,
        compiler_params=_cparams(("arbitrary",), VMEM_LIMIT),
        name="pre_rope" if rope else "pre",
    )(*args)


def _attn_kernel(*refs, has_cache, heads, tq, tw, tk, seq, cache_len):
    if has_cache:
        lam_ref, g_ref, q_ref, kc_ref, vc_ref, k_ref, v_ref, o_ref = refs
    else:
        lam_ref, g_ref, q_ref, k_ref, v_ref, o_ref = refs

    lv = lam_ref[...]
    lam = (jnp.exp(jnp.sum(lv[0:1] * lv[1:2], axis=1, keepdims=True))
           - jnp.exp(jnp.sum(lv[2:3] * lv[3:4], axis=1, keepdims=True)) + LAMBDA_INIT)

    row = lax.broadcasted_iota(jnp.int32, (LANES, tq), 0)
    zero = jnp.zeros((LANES, tq), BF16)

    def sub_queries(h):
        qt = q_ref[h * LANES:(h + 1) * LANES, :]
        return (jnp.where(row < HEAD_DIM, qt, zero), jnp.where(row >= HEAD_DIM, qt, zero))

    def scores(kb, q_one):
        return jnp.dot(kb, q_one, preferred_element_type=F32)

    def softmax_step(s, m):
        m_new = jnp.maximum(m, jnp.max(s, axis=0, keepdims=True))
        return m_new, jnp.exp2(m - m_new), jnp.exp2(s - m_new).astype(BF16)

    def accumulate(vb, p, alpha, acc):
        return alpha * acc + jnp.dot(vb, p, preferred_element_type=F32)

    chunks = []
    if has_cache:
        chunks += [(kc_ref, vc_ref, j) for j in range(cache_len // tk)]
    chunks += [(k_ref, v_ref, j) for j in range(seq // tk)]

    def keys(c, h):
        kr, _, j = chunks[c]
        return kr[j * tk:(j + 1) * tk, h * LANES:(h + 1) * LANES]

    def values(c, h):
        _, vr, j = chunks[c]
        return vr[h * V_EXT:(h + 1) * V_EXT, j * tk:(j + 1) * tk]

    chains = []
    for h in range(heads):
        q_sub = sub_queries(h)
        chains += [(h, q_sub[sub][:, w * tw:(w + 1) * tw])
                   for w in range(tq // tw) for sub in range(2)]
    per_head = len(chains) // heads
    items = [(c, ch) for h in range(heads) for c in range(len(chunks))
             for ch in range(h * per_head, (h + 1) * per_head)]
    m = [jnp.full((1, tw), -jnp.inf, F32)] * len(chains)
    acc = [jnp.zeros((V_EXT, tw), F32)] * len(chains)
    queue = [scores(keys(c, chains[ch][0]), chains[ch][1]) for c, ch in items[:SCORES_AHEAD]]
    for i, (c, ch) in enumerate(items):
        s_cur = queue.pop(0)
        if i + SCORES_AHEAD < len(items):
            nc, nch = items[i + SCORES_AHEAD]
            queue.append(scores(keys(nc, chains[nch][0]), chains[nch][1]))
        m[ch], alpha, p = softmax_step(s_cur, m[ch])
        acc[ch] = accumulate(values(c, chains[ch][0]), p, alpha, acc[ch])
    for h in range(heads):
        outs = []
        for w in range(tq // tw):
            a1, a2 = acc[h * per_head + 2 * w], acc[h * per_head + 2 * w + 1]
            outs.append(a1[:V_DIM] / a1[V_DIM:V_DIM + 1]
                        - lam * (a2[:V_DIM] / a2[V_DIM:V_DIM + 1]))
        o = outs[0] if len(outs) == 1 else jnp.concatenate(outs, axis=1)
        y = (o * lax.rsqrt(jnp.mean(o * o, axis=0, keepdims=True) + EPS) * g_ref[...]
             * (1.0 - LAMBDA_INIT))
        o_ref[:, h * LANES:(h + 1) * LANES] = y.T.astype(BF16)


def _attn_call(lam4, subln_col, qt, k, vt, cache, *, batch, seq, heads, tq, tw, tk):
    has_cache = cache is not None
    nq = seq // tq
    in_specs = [
        pl.BlockSpec((4, HEAD_DIM), lambda b, h, i: (0, 0)),
        pl.BlockSpec((V_DIM, 1), lambda b, h, i: (0, 0)),
        pl.BlockSpec((heads * LANES, tq), lambda b, h, i: (h, b * nq + i)),
    ]
    args = [lam4, subln_col, qt]
    cache_len = 0
    if has_cache:
        kc, vct = cache
        cache_len = kc.shape[0] // batch
        in_specs += [pl.BlockSpec((cache_len, heads * LANES), lambda b, h, i: (b, h)),
                     pl.BlockSpec((heads * V_EXT, cache_len), lambda b, h, i: (h, b))]
        args += [kc, vct]
    in_specs += [pl.BlockSpec((seq, heads * LANES), lambda b, h, i: (b, h)),
                 pl.BlockSpec((heads * V_EXT, seq), lambda b, h, i: (h, b))]
    args += [k, vt]
    return pl.pallas_call(
        functools.partial(_attn_kernel, has_cache=has_cache, heads=heads, tq=tq, tw=tw, tk=tk,
                          seq=seq, cache_len=cache_len),
        grid=(batch, N_HEADS // heads, nq),
        in_specs=in_specs,
        out_specs=pl.BlockSpec((tq, heads * LANES), lambda b, h, i: (b * nq + i, h)),
        out_shape=jax.ShapeDtypeStruct((batch * seq, ATTN_WIDTH), BF16),
        compiler_params=_cparams(("arbitrary", "arbitrary", "arbitrary"), VMEM_LIMIT),
        name="attn_cache" if has_cache else "attn",
    )(*args)


def _post_kernel(x_ref, a_ref, p_ref, pp_ref, pn_ref, mod_ref, wout_ref, wbd_ref, ps_ref, g2_ref,
                 wr_ref, x1_ref, h2_ref, aff_ref, affc_ref, *, tm, seq):
    i = pl.program_id(0)
    tiles_per_seq = seq // tm
    ti = i % tiles_per_seq
    p = p_ref[...]
    prev = jnp.where(ti == 0, 0.0, pp_ref[...])
    nxt = jnp.where(ti == tiles_per_seq - 1, 0.0, pn_ref[...])
    ext = jnp.concatenate([prev, p, nxt], axis=0)
    n_ext = tm + 2 * HALO
    s2 = ext + pltpu.roll(ext, 1, 0)
    s4 = pltpu.roll(s2, 1, 0) + pltpu.roll(s2, n_ext - 1, 0)
    s8 = pltpu.roll(s4, 2, 0) + pltpu.roll(s4, n_ext - 2, 0)
    s16 = pltpu.roll(s8, 4, 0) + pltpu.roll(s8, n_ext - 4, 0)
    lane = lax.broadcasted_iota(jnp.int32, (tm, POOL_WIDTH), 1)
    grp = lane // (POOL_WIDTH // 4)
    win = jnp.where(grp == 0, s2[HALO:HALO + tm],
                    jnp.where(grp == 1, s4[HALO:HALO + tm],
                              jnp.where(grp == 2, s8[HALO:HALO + tm], s16[HALO:HALO + tm])))
    t = ti * tm + lax.broadcasted_iota(jnp.int32, (tm, POOL_WIDTH), 0)
    left = jnp.where(grp == 0, 1, jnp.where(grp == 1, 2, jnp.where(grp == 2, 4, 8)))
    lo = jnp.maximum(t - left, 0)
    hi = jnp.minimum(t + left - 1, seq - 1) + 1
    pooled = win / (hi - lo).astype(F32) - p
    pool = jnp.dot(pooled.astype(BF16), wbd_ref[...], preferred_element_type=F32) * ps_ref[...]

    cat = jnp.concatenate([pool.astype(BF16), a_ref[...]], axis=1)
    mix = jnp.dot(cat, wout_ref[...], preferred_element_type=F32)
    mod = mod_ref[0]
    gate1 = mod[:, 0:D_MODEL]
    shift2 = mod[:, D_MODEL:2 * D_MODEL]
    scale2 = mod[:, 2 * D_MODEL:3 * D_MODEL]
    x1 = x_ref[...] + gate1 * mix
    x1_ref[...] = x1
    ms = jnp.mean(x1 * x1, axis=1, keepdims=True)
    h2 = x1 * lax.rsqrt(ms + EPS) * g2_ref[...] * (1.0 + scale2) + shift2
    for s in range(ROW_TILES):
        h2_ref[pl.ds(s, tm, stride=ROW_TILES), :] = h2[:, s * LANES:(s + 1) * LANES]

    logits = lax.dot_general(wr_ref[...], h2, (((1,), (1,)), ((), ())),
                             precision=lax.Precision.HIGHEST, preferred_element_type=F32)
    e = jnp.exp(logits - jnp.max(logits, axis=0, keepdims=True))
    aff = e / jnp.sum(e, axis=0, keepdims=True)
    aff_ref[...] = aff
    for c in range(tm // LANES):
        affc_ref[c * N_EXPERTS:(c + 1) * N_EXPERTS, :] = aff[:, c * LANES:(c + 1) * LANES]


def _post_call(x2d, attn, p, mod3, wout_bf, wbd_bf, pool_scale, g2, wr_t, *, seq, tm, mod_base,
               mod_stride):
    n = x2d.shape[0]
    tiles_per_seq = seq // tm
    halo_per_tile = tm // HALO
    n_halo = n // HALO

    def mod_map(i):
        return (mod_base + mod_stride * (i // tiles_per_seq), 0, 0)

    return pl.pallas_call(
        functools.partial(_post_kernel, tm=tm, seq=seq),
        grid=(n // tm,),
        in_specs=[
            pl.BlockSpec((tm, D_MODEL), lambda i: (i, 0)),
            pl.BlockSpec((tm, ATTN_WIDTH), lambda i: (i, 0)),
            pl.BlockSpec((tm, POOL_WIDTH), lambda i: (i, 0)),
            pl.BlockSpec((HALO, POOL_WIDTH), lambda i: (jnp.maximum(i * halo_per_tile - 1, 0), 0)),
            pl.BlockSpec((HALO, POOL_WIDTH),
                         lambda i: (jnp.minimum((i + 1) * halo_per_tile, n_halo - 1), 0)),
            pl.BlockSpec((1, 1, 3 * D_MODEL), lambda i: mod_map(i)[:2] + (0,)),
            pl.BlockSpec((D_MODEL, D_MODEL), lambda i: (0, 0)),
            pl.BlockSpec((POOL_WIDTH, POOL_WIDTH), lambda i: (0, 0)),
            pl.BlockSpec((1, POOL_WIDTH), lambda i: (0, 0)),
            pl.BlockSpec((1, D_MODEL), lambda i: (0, 0)),
            pl.BlockSpec((N_EXPERTS, D_MODEL), lambda i: (0, 0)),
        ],
        out_specs=[
            pl.BlockSpec((tm, D_MODEL), lambda i: (i, 0)),
            pl.BlockSpec((tm * ROW_TILES, LANES), lambda i: (i, 0)),
            pl.BlockSpec((N_EXPERTS, tm), lambda i: (0, i)),
            pl.BlockSpec((tm // LANES * N_EXPERTS, LANES), lambda i: (i, 0)),
        ],
        out_shape=[
            jax.ShapeDtypeStruct((n, D_MODEL), F32),
            jax.ShapeDtypeStruct((n * ROW_TILES, LANES), F32),
            jax.ShapeDtypeStruct((N_EXPERTS, n), F32),
            jax.ShapeDtypeStruct((n // LANES * N_EXPERTS, LANES), F32),
        ],
        compiler_params=_cparams(("arbitrary",), VMEM_LIMIT),
        name="post",
    )(x2d, attn, p, p, p, mod3, wout_bf, wbd_bf, pool_scale, g2, wr_t)


def _select_kernel(a_ref, ac_ref, idx_ref, gate_ref, *, n, cap):
    nc = n // LANES
    a = a_ref[...]
    thr = jnp.zeros((N_EXPERTS, 1), jnp.int32)
    for bit in range(30, -1, -1):
        cand = thr | (1 << bit)
        cnt = jnp.sum(jnp.where(a >= pltpu.bitcast(cand, F32), 1.0, 0.0), axis=1, keepdims=True)
        thr = jnp.where(cnt >= cap, cand, thr)
    thr_all = pltpu.bitcast(thr, F32)
    need_all = cap - jnp.sum(jnp.where(a > thr_all, 1.0, 0.0), axis=1, keepdims=True)

    r = lax.broadcasted_iota(jnp.int32, (LANES, LANES), 0)
    c = lax.broadcasted_iota(jnp.int32, (LANES, LANES), 1)
    upper = jnp.where(r <= c, 1.0, 0.0).astype(BF16)
    lower = jnp.where(c < r, 1.0, 0.0).astype(BF16)
    row_valid = r < nc
    chunk_col = lax.broadcasted_iota(jnp.int32, (LANES, 1), 0).astype(F32)
    slot = lax.broadcasted_iota(jnp.int32, (1, cap), 1).astype(F32)

    def lane_counts(mask):
        local = jnp.dot(mask.astype(BF16), upper, preferred_element_type=F32)
        total = jnp.broadcast_to(local[:, LANES - 1:LANES], (LANES, LANES))
        before = jnp.dot(lower, total.astype(BF16), preferred_element_type=F32)
        return local, total, before

    for e in range(N_EXPERTS):
        av = ac_ref[pl.ds(e, nc, stride=N_EXPERTS), :]
        if nc < LANES:
            av = jnp.concatenate([av, jnp.zeros((LANES - nc, LANES), F32)], axis=0)
        thr_e = thr_all[e:e + 1, :]
        above = jnp.where(row_valid & (av > thr_e), 1.0, 0.0)
        tied = jnp.where(row_valid & (av == thr_e), 1.0, 0.0)
        t_local, _, t_before = lane_counts(tied)
        sel = above + tied * jnp.where(t_local + t_before <= need_all[e:e + 1, :], 1.0, 0.0)
        s_local, s_total, s_before = lane_counts(sel)
        rank = jnp.where(sel > 0.0, s_local, 0.0)

        start = s_before[:, 0:1]
        stop = start + s_total[:, 0:1]
        onehot = jnp.where((slot >= start) & (slot < stop), 1.0, 0.0)
        chunk_of_slot = jnp.sum(onehot * chunk_col, axis=0, keepdims=True)
        start_of_slot = jnp.sum(onehot * start, axis=0, keepdims=True)

        at = av.T
        hi = at.astype(BF16)
        rest = at - hi.astype(F32)
        mid = rest.astype(BF16)
        lo = (rest - mid.astype(F32)).astype(BF16)
        lhs = jnp.concatenate([rank.T.astype(BF16), hi, mid, lo], axis=0)
        picked = jnp.dot(lhs, onehot.astype(BF16), preferred_element_type=F32)
        rank_p = picked[0:LANES]
        aff_p = picked[LANES:2 * LANES] + picked[2 * LANES:3 * LANES] + picked[3 * LANES:]
        hit = rank_p == (slot - start_of_slot + 1.0)
        lane_of_slot = jnp.sum(jnp.where(hit, chunk_col, 0.0), axis=0, keepdims=True)
        idx_ref[e:e + 1, :] = (chunk_of_slot * LANES + lane_of_slot).astype(jnp.int32)
        gate_ref[e:e + 1, :] = jnp.sum(jnp.where(hit, aff_p, 0.0), axis=0, keepdims=True)


def _select_call(aff_t, aff_c, *, cap):
    n = aff_t.shape[1]
    assert n % LANES == 0 and n // LANES <= LANES and cap % LANES == 0
    return pl.pallas_call(
        functools.partial(_select_kernel, n=n, cap=cap),
        grid=(1,),
        in_specs=[pl.BlockSpec((N_EXPERTS, n), lambda i: (0, 0)),
                  pl.BlockSpec(aff_c.shape, lambda i: (0, 0))],
        out_specs=[pl.BlockSpec((N_EXPERTS, cap), lambda i: (0, 0))] * 2,
        out_shape=[jax.ShapeDtypeStruct((N_EXPERTS, cap), jnp.int32),
                   jax.ShapeDtypeStruct((N_EXPERTS, cap), F32)],
        compiler_params=_cparams(("arbitrary",), VMEM_LIMIT),
        name="select",
    )(aff_t, aff_c)


def _moe_kernel(idx_ref, gate_ref, wg_ref, wu_ref, wd_ref, x_hbm, out_hbm,
                gbuf, ybuf, acc_ref, gsem, osem, *, tm, n_tiles_total):
    e = pl.program_id(0)
    t = pl.program_id(1)
    nt = pl.num_programs(1)
    step = e * nt + t
    group = 8
    last = n_tiles_total - 1
    rows = tm * ROW_TILES

    def gather_start(tile_step, k, dst_slot):
        tok = idx_ref[tile_step * tm + k]
        pltpu.make_async_copy(
            x_hbm.at[pl.ds(pl.multiple_of(tok * ROW_TILES, ROW_TILES), ROW_TILES), :],
            gbuf.at[dst_slot, pl.ds(k * ROW_TILES, ROW_TILES), :], gsem.at[dst_slot]).start()

    def gather_wait(dst_slot):
        pltpu.make_async_copy(x_hbm.at[pl.ds(0, rows), :], gbuf.at[dst_slot],
                              gsem.at[dst_slot]).wait()

    def scatter_add(tile_step, src_slot, k0):
        pending = []
        for r in range(group):
            k = k0 + r
            tok = idx_ref[tile_step * tm + k]
            off = pl.multiple_of(tok * ROW_TILES, ROW_TILES)
            src = pl.multiple_of(k * ROW_TILES, ROW_TILES)
            pending.append((off, acc_ref[pl.ds(off, ROW_TILES), :]
                            + ybuf[src_slot, pl.ds(src, ROW_TILES), :]))
        for off, val in pending:
            acc_ref[pl.ds(off, ROW_TILES), :] = val

    @pl.when(step == 0)
    def _():
        acc_ref[...] = jnp.zeros_like(acc_ref)
        ybuf[...] = jnp.zeros_like(ybuf)

        def body(k, _):
            gather_start(0, k, 0)
            return 0
        lax.fori_loop(0, tm, body, 0)

    def tile_body(slot):
        nxt = jnp.minimum(step + 1, last)
        prev = jnp.maximum(step - 1, 0)
        for k in range(tm):
            gather_start(nxt, k, 1 - slot)
        for k0 in range(0, tm, group):
            scatter_add(prev, 1 - slot, k0)
        gather_wait(slot)
        xe = jnp.concatenate(
            [gbuf[slot, pl.ds(s, tm, stride=ROW_TILES), :] for s in range(ROW_TILES)],
            axis=1).astype(BF16)
        g = jnp.dot(xe, wg_ref[...], preferred_element_type=F32)
        u = jnp.dot(xe, wu_ref[...], preferred_element_type=F32)
        hid = (g * jax.nn.sigmoid(g) * u).astype(BF16)
        y = jnp.dot(hid, wd_ref[...], preferred_element_type=F32)
        gate =jnp.broadcast_to(gate_ref[...], (LANES, tm)).T
        for s in range(ROW_TILES):
            ybuf[slot, pl.ds(s, tm, stride=ROW_TILES), :] = y[:, s * LANES:(s + 1) * LANES] * gate

        @pl.when(step == last)
        def _():
            gather_wait(1 - slot)

            def body(kk, _):
                scatter_add(step, slot, kk * group)
                return 0
            lax.fori_loop(0, tm // group, body, 0)
            cp = pltpu.make_async_copy(acc_ref, out_hbm, osem)
            cp.start()
            cp.wait()

    for parity in range(2):
        pl.when(step % 2 == parity)(functools.partial(tile_body, parity))


def _moe_call(idx_flat, gates3, wg_bf, wu_bf, wd_bf, h2, *, cap, tm):
    n_rows = h2.shape[0]
    nt = cap // tm
    grid_spec = pltpu.PrefetchScalarGridSpec(
        num_scalar_prefetch=1,
        grid=(N_EXPERTS, nt),
        in_specs=[
            pl.BlockSpec((None, 1, tm), lambda e, t, idx: (e * nt + t, 0, 0)),
            pl.BlockSpec((None, D_MODEL, D_MODEL), lambda e, t, idx: (e, 0, 0)),
            pl.BlockSpec((None, D_MODEL, D_MODEL), lambda e, t, idx: (e, 0, 0)),
            pl.BlockSpec((None, D_MODEL, D_MODEL), lambda e, t, idx: (e, 0, 0)),
            pl.BlockSpec(memory_space=pl.ANY),
        ],
        out_specs=pl.BlockSpec(memory_space=pl.ANY),
        scratch_shapes=[
            pltpu.VMEM((2, tm * ROW_TILES, LANES), F32),
            pltpu.VMEM((2, tm * ROW_TILES, LANES), F32),
            pltpu.VMEM((n_rows, LANES), F32),
            pltpu.SemaphoreType.DMA((2,)),
            pltpu.SemaphoreType.DMA(()),
        ],
    )
    return pl.pallas_call(
        functools.partial(_moe_kernel, tm=tm, n_tiles_total=N_EXPERTS * nt),
        grid_spec=grid_spec,
        out_shape=jax.ShapeDtypeStruct((n_rows, LANES), F32),
        compiler_params=_cparams(("arbitrary", "arbitrary"), VMEM_LIMIT),
        name="moe",
    )(idx_flat, gates3, wg_bf, wu_bf, wd_bf, h2)


def _final_kernel(x1_ref, moe_ref, mod_ref, o_ref, *, tm):
    moe = jnp.concatenate(
        [moe_ref[pl.ds(s, tm, stride=ROW_TILES), :] for s in range(ROW_TILES)], axis=1)
    o_ref[...] = x1_ref[...] + mod_ref[0] * moe


def _final_call(x1, moe_tiles, mod3, *, seq, tm, mod_base, mod_stride):
    n = x1.shape[0]
    tiles_per_seq = seq // tm
    gate2_block = 5

    def mod_map(i):
        return (mod_base + mod_stride * (i // tiles_per_seq), 0, gate2_block)

    return pl.pallas_call(
        functools.partial(_final_kernel, tm=tm),
        grid=(n // tm,),
        in_specs=[pl.BlockSpec((tm, D_MODEL), lambda i: (i, 0)),
                  pl.BlockSpec((tm * ROW_TILES, LANES), lambda i: (i, 0)),
                  pl.BlockSpec((1, 1, D_MODEL), mod_map)],
        out_specs=pl.BlockSpec((tm, D_MODEL), lambda i: (i, 0)),
        out_shape=jax.ShapeDtypeStruct((n, D_MODEL), F32),
        compiler_params=_cparams(("arbitrary",)),
        name="final",
    )(x1, moe_tiles, mod3)


def _rope_tables(seq):
    t = np.arange(seq)
    row, col = t // GRID_W, t % GRID_W
    half = HEAD_DIM // 2
    freqs = 1.0 / (ROPE_BASE ** (np.arange(0, half, 2) / half))
    ang_r = row[:, None] * freqs[None, :]
    ang_c = col[:, None] * freqs[None, :]
    ang = np.concatenate([ang_r, ang_r, ang_c, ang_c], axis=-1)
    cos = np.tile(np.cos(ang), (1, LANES // HEAD_DIM))
    sin = np.tile(np.sin(ang), (1, LANES // HEAD_DIM))
    sign = np.where((np.arange(LANES) % (HEAD_DIM // 2)) < (HEAD_DIM // 4), -1.0, 1.0)
    return jnp.asarray(cos, F32), jnp.asarray(sin * sign[None, :], F32)


def _segment_matrix():
    seg = np.arange(ATTN_WIDTH) // HEAD_DIM
    return jnp.asarray((seg[:, None] == seg[None, :]) / HEAD_DIM, BF16)


def _trunk(x, mod3, w, cache, *, mod_base, mod_stride, tm, heads, tq, tk, moe_tm):
    batch, seq, _ = x.shape
    n = batch * seq
    x2d = x.reshape(n, D_MODEL)
    rope_tabs = _rope_tables(seq) if cache is not None else None
    pre = _pre_call(x2d, mod3[:, :, :2 * D_MODEL], w["g1"], w["win"], w["seg"], w["qg"], w["kg"],
                    rope_tabs, seq=seq, tm=tm, mod_base=mod_base, mod_stride=mod_stride,
                    emit_f32_kv=cache is None)
    p, q, k, v = pre[:4]
    attn = _attn_call(w["lam4"], w["subln"], q, k, v, cache, batch=batch, seq=seq,
                      heads=heads, tq=tq, tw=256, tk=tk)
    x1, h2, aff_t, aff_c = _post_call(x2d, attn, p, mod3[:, :, 2 * D_MODEL:5 * D_MODEL], w["wout"],
                                     w["wbd"], w["pool_scale"], w["g2"], w["wr_t"], seq=seq, tm=tm,
                                     mod_base=mod_base, mod_stride=mod_stride)
    cap = CAPACITY_FACTOR * n // N_EXPERTS
    idx, gates = _select_call(aff_t, aff_c, cap=cap)
    moe_tiles = _moe_call(idx.reshape(N_EXPERTS * cap), gates.reshape(-1, 1, moe_tm),
                          w["wg"], w["wu"], w["wd"], h2, cap=cap, tm=moe_tm)
    y = _final_call(x1, moe_tiles, mod3, seq=seq, tm=tm, mod_base=mod_base, mod_stride=mod_stride)
    y = y.reshape(batch, seq, D_MODEL)
    if cache is None:
        return y, pre[4], pre[5]
    return y, None, None


def kernel(x_prompt, x_sample, cache_k, cache_v, c, c_ctx, norm1_g, norm2_g, w_ada, b_ada, w_in,
           q_norm_g, k_norm_g, lambda_q1, lambda_k1, lambda_q2, lambda_k2, subln_g, w_pool,
           pool_scale, w_out, w_router, w_gate, w_up, w_down):
    assert w_ada.shape[0] == 1, "single-layer stack"
    batch, seq, _ = x_prompt.shape
    dec_batch, dec_seq, _ = x_sample.shape

    pad = SUBLANES - 1 - dec_batch
    cvec = jnp.concatenate([c_ctx[None, :], c, jnp.zeros((pad, D_MODEL), F32)], axis=0)
    mod = _ada_call(cvec, w_ada[0], b_ada[0])
    mod3 = mod.reshape(SUBLANES, 1, 6 * D_MODEL)

    n_groups = w_pool.shape[1]
    grp = POOL_WIDTH // n_groups
    eye = jnp.eye(n_groups, dtype=F32)
    wbd = (w_pool[0][:, :, None, :] * eye[:, None, :, None]).reshape(POOL_WIDTH, POOL_WIDTH)

    w = {
        "g1": norm1_g[0].reshape(1, D_MODEL),
        "g2": norm2_g[0].reshape(1, D_MODEL),
        "win": w_in[0].astype(BF16),
        "seg": _segment_matrix(),
        "qg": jnp.tile(q_norm_g[0], ATTN_WIDTH // HEAD_DIM).reshape(1, ATTN_WIDTH),
        "kg": jnp.tile(k_norm_g[0], ATTN_WIDTH // HEAD_DIM).reshape(1, ATTN_WIDTH),
        "lam4": jnp.stack([lambda_q1[0], lambda_k1[0], lambda_q2[0], lambda_k2[0]], axis=0),
        "subln": subln_g[0].reshape(V_DIM, 1),
        "wbd": wbd.astype(BF16),
        "pool_scale": pool_scale[0].reshape(1, POOL_WIDTH),
        "wout": w_out[0].astype(BF16),
        "wr_t": w_router[0].T,
        "wg": w_gate[0].astype(BF16),
        "wu": w_up[0].astype(BF16),
        "wd": w_down[0].astype(BF16),
    }

    yp, k_ctx, v_ctx = _trunk(x_prompt, mod3, w, None, mod_base=0, mod_stride=0,
                              tm=256, heads=N_HEADS, tq=256, tk=256, moe_tm=256)
    past = cache_k.shape[2]
    cv = cache_v[:, 0].reshape(dec_batch * past, N_HEADS, V_DIM).transpose(1, 2, 0).astype(BF16)
    cv = jnp.concatenate([cv, jnp.ones((N_HEADS, V_EXT - V_DIM, dec_batch * past), BF16)], axis=1)
    cache = (cache_k[:, 0].reshape(dec_batch * past, ATTN_WIDTH).astype(BF16),
             cv.reshape(N_HEADS * V_EXT, dec_batch * past))
    ys, _, _ = _trunk(x_sample, mod3, w, cache, mod_base=1, mod_stride=1,
                      tm=256, heads=1, tq=512, tk=512, moe_tm=256)
    ctx_k = (k_ctx.reshape(batch, N_HEADS, 2, HEAD_DIM, seq).transpose(0, 4, 1, 2, 3)
             .reshape(batch, 1, seq, N_HEADS, 2, HEAD_DIM))
    ctx_v = v_ctx.transpose(0, 2, 1, 3).reshape(batch, 1, seq, N_HEADS, V_DIM)
    return yp, ys, ctx_k, ctx_v
```

```python
import functools
import math

import numpy as np
import jax
import jax.numpy as jnp
from jax import lax
from jax.experimental import pallas as pl
from jax.experimental.pallas import tpu as pltpu

F32 = jnp.float32
BF16 = jnp.bfloat16

D_MODEL = 1024
POOL_WIDTH = 256
ATTN_WIDTH = 768
N_HEADS = 6
HEAD_DIM = 64
V_DIM = 128
IN_WIDTH = POOL_WIDTH + 3 * ATTN_WIDTH
N_EXPERTS = 16
CAPACITY_FACTOR = 2
GRID_W = 64
ROPE_BASE = 10000.0
EPS = 1e-6
LAMBDA_INIT = 0.8 - 0.6 * math.exp(-0.3 * 0)
LOG2E = math.log2(math.e)
V_EXT = V_DIM + 16
SCORES_AHEAD = 3

LANES = 128
SUBLANES = 8
ROW_TILES = D_MODEL // LANES
HALO = 16
VMEM_LIMIT = 56 * 1024 * 1024


def _cparams(sem, vmem=None):
    return pltpu.CompilerParams(dimension_semantics=sem, vmem_limit_bytes=vmem)


def _ada_kernel(c_ref, w_ref, b_ref, o_ref):
    c = c_ref[...]
    s = c * jax.nn.sigmoid(c)
    o_ref[...] = jnp.dot(s.astype(BF16), w_ref[...].astype(BF16),
                         preferred_element_type=F32) + b_ref[...]


def _ada_call(cvec, w_ada, b_ada):
    rows, d = cvec.shape
    n = w_ada.shape[1]
    bn = 1536
    return pl.pallas_call(
        _ada_kernel,
        grid=(n // bn,),
        in_specs=[pl.BlockSpec((rows, d), lambda j: (0, 0)),
                  pl.BlockSpec((d, bn), lambda j: (0, j)),
                  pl.BlockSpec((1, bn), lambda j: (0, j))],
        out_specs=pl.BlockSpec((rows, bn), lambda j: (0, j)),
        out_shape=jax.ShapeDtypeStruct((rows, n), F32),
        compiler_params=_cparams(("arbitrary",)),
        name="ada",
    )(cvec, w_ada, b_ada.reshape(1, n))


def _segment_mean_square(a, seg_ref):
    return jnp.dot((a * a).astype(BF16), seg_ref[...], preferred_element_type=F32)


def _rope(a, cos, sin_signed, first_half):
    parts = []
    for h in range(a.shape[1] // LANES):
        blk = a[:, h * LANES:(h + 1) * LANES]
        fwd = pltpu.roll(blk, LANES - HEAD_DIM // 4, 1)
        bwd = pltpu.roll(blk, HEAD_DIM // 4, 1)
        parts.append(blk * cos + jnp.where(first_half, fwd, bwd) * sin_signed)
    return jnp.concatenate(parts, axis=1)


def _pre_kernel(*refs, rope, emit_f32_kv):
    x_ref, mod_ref, g1_ref, win_ref, seg_ref, qg_ref, kg_ref = refs[:7]
    pos = 7
    if rope:
        cos_ref, sin_ref = refs[pos:pos + 2]
        pos += 2
    p_ref, q_ref, k_ref, v_ref = refs[pos:pos + 4]
    pos += 4
    if emit_f32_kv:
        kf_ref, vf_ref = refs[pos:pos + 2]

    x = x_ref[...]
    mod = mod_ref[0]
    shift1 = mod[:, :D_MODEL]
    scale1 = mod[:, D_MODEL:2 * D_MODEL]
    ms = jnp.mean(x * x, axis=1, keepdims=True)
    h = x * lax.rsqrt(ms + EPS) * g1_ref[...] * (1.0 + scale1) + shift1
    z = jnp.dot(h.astype(BF16), win_ref[...], preferred_element_type=F32)

    p_ref[...] = z[:, :POOL_WIDTH]
    qz = z[:, POOL_WIDTH:POOL_WIDTH + ATTN_WIDTH]
    kz = z[:, POOL_WIDTH + ATTN_WIDTH:POOL_WIDTH + 2 * ATTN_WIDTH]
    vz = z[:, POOL_WIDTH + 2 * ATTN_WIDTH:]

    qn = qz * lax.rsqrt(_segment_mean_square(qz, seg_ref) + EPS) * qg_ref[...]
    kn = kz * lax.rsqrt(_segment_mean_square(kz, seg_ref) + EPS) * kg_ref[...]
    if rope:
        cos = cos_ref[...]
        sin_signed = sin_ref[...]
        lane = lax.broadcasted_iota(jnp.int32, cos.shape, 1)
        first_half = (lane % (HEAD_DIM // 2)) < (HEAD_DIM // 4)
        qn = _rope(qn, cos, sin_signed, first_half)
        kn = _rope(kn, cos, sin_signed, first_half)

    q_ref[...] = (qn * (LOG2E / math.sqrt(HEAD_DIM))).T.astype(BF16)
    k_ref[...] = kn.astype(BF16)
    vt = vz.T
    ones = jnp.ones((V_EXT - V_DIM, vt.shape[1]), BF16)
    for h in range(N_HEADS):
        v_ref[h * V_EXT:h * V_EXT + V_DIM, :] = vt[h * V_DIM:(h + 1) * V_DIM, :].astype(BF16)
        v_ref[h * V_EXT + V_DIM:(h + 1) * V_EXT, :] = ones
    if emit_f32_kv:
        kf_ref[0] = kn.T
        for h in range(N_HEADS):
            vf_ref[0, h] = vz[:, h * V_DIM:(h + 1) * V_DIM]


def _pre_call(x2d, mod3, g1, win_bf, seg, qg, kg, rope_tabs, *, seq, tm, mod_base, mod_stride,
              emit_f32_kv):
    n = x2d.shape[0]
    rope = rope_tabs is not None
    tiles_per_seq = seq // tm

    def mod_map(i):
        return (mod_base + mod_stride * (i // tiles_per_seq), 0, 0)

    in_specs = [
        pl.BlockSpec((tm, D_MODEL), lambda i: (i, 0)),
        pl.BlockSpec((1, 1, 2 * D_MODEL), mod_map),
        pl.BlockSpec((1, D_MODEL), lambda i: (0, 0)),
        pl.BlockSpec((D_MODEL, IN_WIDTH), lambda i: (0, 0)),
        pl.BlockSpec((ATTN_WIDTH, ATTN_WIDTH), lambda i: (0, 0)),
        pl.BlockSpec((1, ATTN_WIDTH), lambda i: (0, 0)),
        pl.BlockSpec((1, ATTN_WIDTH), lambda i: (0, 0)),
    ]
    args = [x2d, mod3, g1, win_bf, seg, qg, kg]
    if rope:
        in_specs += [pl.BlockSpec((tm, LANES), lambda i: (i % tiles_per_seq, 0))] * 2
        args += list(rope_tabs)
    out_shapes = [jax.ShapeDtypeStruct((n, POOL_WIDTH), F32)]
    out_specs = [pl.BlockSpec((tm, POOL_WIDTH), lambda i: (i, 0))]
    out_shapes += [jax.ShapeDtypeStruct((ATTN_WIDTH, n), BF16),
                   jax.ShapeDtypeStruct((n, ATTN_WIDTH), BF16),
                   jax.ShapeDtypeStruct((N_HEADS * V_EXT, n), BF16)]
    out_specs += [pl.BlockSpec((ATTN_WIDTH, tm), lambda i: (0, i)),
                  pl.BlockSpec((tm, ATTN_WIDTH), lambda i: (i, 0)),
                  pl.BlockSpec((N_HEADS * V_EXT, tm), lambda i: (0, i))]
    if emit_f32_kv:
        def seq_map(i):
            return (i // tiles_per_seq, 0, i % tiles_per_seq)

        out_shapes += [jax.ShapeDtypeStruct((n // seq, ATTN_WIDTH, seq), F32),
                       jax.ShapeDtypeStruct((n // seq, N_HEADS, seq, V_DIM), F32)]
        out_specs += [pl.BlockSpec((1, ATTN_WIDTH, tm), seq_map),
                      pl.BlockSpec((1, N_HEADS, tm, V_DIM),
                                   lambda i: (i // tiles_per_seq, 0, i % tiles_per_seq, 0))]
    return pl.pallas_call(
        functools.partial(_pre_kernel, rope=rope, emit_f32_kv=emit_f32_kv),
        grid=(n // tm,),
        in_specs=in_specs,
        out_specs=out_specs,
        out_shape=out_shapes,
        compiler_params=_cparams(("arbitrary",), VMEM_LIMIT),
        name="pre_rope" if rope else "pre",
    )(*args)


def _attn_kernel(*refs, has_cache, heads, tq, tw, tk, seq, cache_len):
    if has_cache:
        lam_ref, g_ref, q_ref, kc_ref, vc_ref, k_ref, v_ref, o_ref = refs
    else:
        lam_ref, g_ref, q_ref, k_ref, v_ref, o_ref = refs

    lv = lam_ref[...]
    lam = (jnp.exp(jnp.sum(lv[0:1] * lv[1:2], axis=1, keepdims=True))
           - jnp.exp(jnp.sum(lv[2:3] * lv[3:4], axis=1, keepdims=True)) + LAMBDA_INIT)

    row = lax.broadcasted_iota(jnp.int32, (LANES, tq), 0)
    zero = jnp.zeros((LANES, tq), BF16)

    def sub_queries(h):
        qt = q_ref[h * LANES:(h + 1) * LANES, :]
        return (jnp.where(row < HEAD_DIM, qt, zero), jnp.where(row >= HEAD_DIM, qt, zero))

    def scores(kb, q_one):
        return jnp.dot(kb, q_one, preferred_element_type=F32)

    def softmax_step(s, m):
        m_new = jnp.maximum(m, jnp.max(s, axis=0, keepdims=True))
        return m_new, jnp.exp2(m - m_new), jnp.exp2(s - m_new).astype(BF16)

    def accumulate(vb, p, alpha, acc):
        return alpha * acc + jnp.dot(vb, p, preferred_element_type=F32)

    chunks = []
    if has_cache:
        chunks += [(kc_ref, vc_ref, j) for j in range(cache_len // tk)]
    chunks += [(k_ref, v_ref, j) for j in range(seq // tk)]

    def keys(c, h):
        kr, _, j = chunks[c]
        return kr[j * tk:(j + 1) * tk, h * LANES:(h + 1) * LANES]

    def values(c, h):
        _, vr, j = chunks[c]
        return vr[h * V_EXT:(h + 1) * V_EXT, j * tk:(j + 1) * tk]

    chains = []
    for h in range(heads):
        q_sub = sub_queries(h)
        chains += [(h, q_sub[sub][:, w * tw:(w + 1) * tw])
                   for w in range(tq // tw) for sub in range(2)]
    per_head = len(chains) // heads
    items = [(c, ch) for h in range(heads) for c in range(len(chunks))
             for ch in range(h * per_head, (h + 1) * per_head)]
    m = [jnp.full((1, tw), -jnp.inf, F32)] * len(chains)
    acc = [jnp.zeros((V_EXT, tw), F32)] * len(chains)
    queue = [scores(keys(c, chains[ch][0]), chains[ch][1]) for c, ch in items[:SCORES_AHEAD]]
    for i, (c, ch) in enumerate(items):
        s_cur = queue.pop(0)
        if i + SCORES_AHEAD < len(items):
            nc, nch = items[i + SCORES_AHEAD]
            queue.append(scores(keys(nc, chains[nch][0]), chains[nch][1]))
        m[ch], alpha, p = softmax_step(s_cur, m[ch])
        acc[ch] = accumulate(values(c, chains[ch][0]), p, alpha, acc[ch])
    for h in range(heads):
        outs = []
        for w in range(tq // tw):
            a1, a2 = acc[h * per_head + 2 * w], acc[h * per_head + 2 * w + 1]
            outs.append(a1[:V_DIM] / a1[V_DIM:V_DIM + 1]
                        - lam * (a2[:V_DIM] / a2[V_DIM:V_DIM + 1]))
        o = outs[0] if len(outs) == 1 else jnp.concatenate(outs, axis=1)
        y = (o * lax.rsqrt(jnp.mean(o * o, axis=0, keepdims=True) + EPS) * g_ref[...]
             * (1.0 - LAMBDA_INIT))
        o_ref[:, h * LANES:(h + 1) * LANES] = y.T.astype(BF16)


def _attn_call(lam4, subln_col, qt, k, vt, cache, *, batch, seq, heads, tq, tw, tk):
    has_cache = cache is not None
    nq = seq // tq
    in_specs = [
        pl.BlockSpec((4, HEAD_DIM), lambda b, h, i: (0, 0)),
        pl.BlockSpec((V_DIM, 1), lambda b, h, i: (0, 0)),
        pl.BlockSpec((heads * LANES, tq), lambda b, h, i: (h, b * nq + i)),
    ]
    args = [lam4, subln_col, qt]
    cache_len = 0
    if has_cache:
        kc, vct = cache
        cache_len = kc.shape[0] // batch
        in_specs += [pl.BlockSpec((cache_len, heads * LANES), lambda b, h, i: (b, h)),
                     pl.BlockSpec((heads * V_EXT, cache_len), lambda b, h, i: (h, b))]
        args += [kc, vct]
    in_specs += [pl.BlockSpec((seq, heads * LANES), lambda b, h, i: (b, h)),
                 pl.BlockSpec((heads * V_EXT, seq), lambda b, h, i: (h, b))]
    args += [k, vt]
    return pl.pallas_call(
        functools.partial(_attn_kernel, has_cache=has_cache, heads=heads, tq=tq, tw=tw, tk=tk,
                          seq=seq, cache_len=cache_len),
        grid=(batch, N_HEADS // heads, nq),
        in_specs=in_specs,
        out_specs=pl.BlockSpec((tq, heads * LANES), lambda b, h, i: (b * nq + i, h)),
        out_shape=jax.ShapeDtypeStruct((batch * seq, ATTN_WIDTH), BF16),
        compiler_params=_cparams(("arbitrary", "arbitrary", "arbitrary"), VMEM_LIMIT),
        name="attn_cache" if has_cache else "attn",
    )(*args)


def _post_kernel(x_ref, a_ref, p_ref, pp_ref, pn_ref, mod_ref, wout_ref, wbd_ref, ps_ref, g2_ref,
                 wr_ref, x1_ref, h2_ref, aff_ref, affc_ref, *, tm, seq):
    i = pl.program_id(0)

    def centred_mean_minus_self(rows, prev, nxt, t0):
        r = rows.shape[0]
        ext = jnp.concatenate([prev, rows, nxt], axis=0)
        n_ext = r + 2 * HALO
        s2 = ext + pltpu.roll(ext, 1, 0)
        s4 = pltpu.roll(s2, 1, 0) + pltpu.roll(s2, n_ext - 1, 0)
        s8 = pltpu.roll(s4, 2, 0) + pltpu.roll(s4, n_ext - 2, 0)
        s16 = pltpu.roll(s8, 4, 0) + pltpu.roll(s8, n_ext - 4, 0)
        lane = lax.broadcasted_iota(jnp.int32, (r, POOL_WIDTH), 1)
        grp = lane // (POOL_WIDTH // 4)
        win = jnp.where(grp == 0, s2[HALO:HALO + r],
                        jnp.where(grp == 1, s4[HALO:HALO + r],
                                  jnp.where(grp == 2, s8[HALO:HALO + r], s16[HALO:HALO + r])))
        t = t0 + lax.broadcasted_iota(jnp.int32, (r, POOL_WIDTH), 0)
        left = jnp.where(grp == 0, 1, jnp.where(grp == 1, 2, jnp.where(grp == 2, 4, 8)))
        lo = jnp.maximum(t - left, 0)
        hi = jnp.minimum(t + left - 1, seq - 1) + 1
        return win / (hi - lo).astype(F32) - rows

    if tm <= seq:
        tiles_per_seq = seq // tm
        ti = i % tiles_per_seq
        pooled = centred_mean_minus_self(
            p_ref[...], jnp.where(ti == 0, 0.0, pp_ref[...]),
            jnp.where(ti == tiles_per_seq - 1, 0.0, pn_ref[...]), ti * tm)
    else:
        halo = jnp.zeros((HALO, POOL_WIDTH), F32)
        pooled = jnp.concatenate(
            [centred_mean_minus_self(p_ref[j * seq:(j + 1) * seq, :], halo, halo, 0)
             for j in range(tm // seq)], axis=0)
    pool = jnp.dot(pooled.astype(BF16), wbd_ref[...], preferred_element_type=F32) * ps_ref[...]

    cat = jnp.concatenate([pool.astype(BF16), a_ref[...]], axis=1)
    mix = jnp.dot(cat, wout_ref[...], preferred_element_type=F32)
    mod = mod_ref[0]
    gate1 = mod[:, 0:D_MODEL]
    shift2 = mod[:, D_MODEL:2 * D_MODEL]
    scale2 = mod[:, 2 * D_MODEL:3 * D_MODEL]
    x1 = x_ref[...] + gate1 * mix
    x1_ref[...] = x1
    ms = jnp.mean(x1 * x1, axis=1, keepdims=True)
    h2 = x1 * lax.rsqrt(ms + EPS) * g2_ref[...] * (1.0 + scale2) + shift2
    for s in range(ROW_TILES):
        h2_ref[pl.ds(s, tm, stride=ROW_TILES), :] = h2[:, s * LANES:(s + 1) * LANES]

    logits = lax.dot_general(wr_ref[...], h2, (((1,), (1,)), ((), ())),
                             precision=lax.Precision.HIGHEST, preferred_element_type=F32)
    e = jnp.exp(logits - jnp.max(logits, axis=0, keepdims=True))
    aff = e / jnp.sum(e, axis=0, keepdims=True)
    aff_ref[...] = aff
    for c in range(tm // LANES):
        affc_ref[c * N_EXPERTS:(c + 1) * N_EXPERTS, :] = aff[:, c * LANES:(c + 1) * LANES]


def _post_call(x2d, attn, p, mod3, wout_bf, wbd_bf, pool_scale, g2, wr_t, *, seq, tm, mod_base,
               mod_stride):
    n = x2d.shape[0]
    assert tm <= seq or mod_stride == 0, "a tile spanning sequences needs one modulation row"
    halo_per_tile = tm // HALO
    n_halo = n // HALO

    def mod_map(i):
        return (mod_base + mod_stride * (i * tm // seq), 0, 0)

    return pl.pallas_call(
        functools.partial(_post_kernel, tm=tm, seq=seq),
        grid=(n // tm,),
        in_specs=[
            pl.BlockSpec((tm, D_MODEL), lambda i: (i, 0)),
            pl.BlockSpec((tm, ATTN_WIDTH), lambda i: (i, 0)),
            pl.BlockSpec((tm, POOL_WIDTH), lambda i: (i, 0)),
            pl.BlockSpec((HALO, POOL_WIDTH), lambda i: (jnp.maximum(i * halo_per_tile - 1, 0), 0)),
            pl.BlockSpec((HALO, POOL_WIDTH),
                         lambda i: (jnp.minimum((i + 1) * halo_per_tile, n_halo - 1), 0)),
            pl.BlockSpec((1, 1, 3 * D_MODEL), lambda i: mod_map(i)[:2] + (0,)),
            pl.BlockSpec((D_MODEL, D_MODEL), lambda i: (0, 0)),
            pl.BlockSpec((POOL_WIDTH, POOL_WIDTH), lambda i: (0, 0)),
            pl.BlockSpec((1, POOL_WIDTH), lambda i: (0, 0)),
            pl.BlockSpec((1, D_MODEL), lambda i: (0, 0)),
            pl.BlockSpec((N_EXPERTS, D_MODEL), lambda i: (0, 0)),
        ],
        out_specs=[
            pl.BlockSpec((tm, D_MODEL), lambda i: (i, 0)),
            pl.BlockSpec((tm * ROW_TILES, LANES), lambda i: (i, 0)),
            pl.BlockSpec((N_EXPERTS, tm), lambda i: (0, i)),
            pl.BlockSpec((tm // LANES * N_EXPERTS, LANES), lambda i: (i, 0)),
        ],
        out_shape=[
            jax.ShapeDtypeStruct((n, D_MODEL), F32),
            jax.ShapeDtypeStruct((n * ROW_TILES, LANES), F32),
            jax.ShapeDtypeStruct((N_EXPERTS, n), F32),
            jax.ShapeDtypeStruct((n // LANES * N_EXPERTS, LANES), F32),
        ],
        compiler_params=_cparams(("arbitrary",), VMEM_LIMIT),
        name="post",
    )(x2d, attn, p, p, p, mod3, wout_bf, wbd_bf, pool_scale, g2, wr_t)


def _select_kernel(a_ref, ac_ref, idx_ref, gate_ref, *, n, cap):
    nc = n // LANES
    a = a_ref[...]
    thr = jnp.zeros((N_EXPERTS, 1), jnp.int32)
    for bit in range(30, -1, -1):
        cand = thr | (1 << bit)
        cnt = jnp.sum(jnp.where(a >= pltpu.bitcast(cand, F32), 1.0, 0.0), axis=1, keepdims=True)
        thr = jnp.where(cnt >= cap, cand, thr)
    thr_all = pltpu.bitcast(thr, F32)
    need_all = cap - jnp.sum(jnp.where(a > thr_all, 1.0, 0.0), axis=1, keepdims=True)

    r = lax.broadcasted_iota(jnp.int32, (LANES, LANES), 0)
    c = lax.broadcasted_iota(jnp.int32, (LANES, LANES), 1)
    upper = jnp.where(r <= c, 1.0, 0.0).astype(BF16)
    lower = jnp.where(c < r, 1.0, 0.0).astype(BF16)
    row_valid = r < nc
    chunk_col = lax.broadcasted_iota(jnp.int32, (LANES, 1), 0).astype(F32)
    slot = lax.broadcasted_iota(jnp.int32, (1, cap), 1).astype(F32)

    def lane_counts(mask):
        local = jnp.dot(mask.astype(BF16), upper, preferred_element_type=F32)
        total = jnp.broadcast_to(local[:, LANES - 1:LANES], (LANES, LANES))
        before = jnp.dot(lower, total.astype(BF16), preferred_element_type=F32)
        return local, total, before

    for e in range(N_EXPERTS):
        av = ac_ref[pl.ds(e, nc, stride=N_EXPERTS), :]
        if nc < LANES:
            av = jnp.concatenate([av, jnp.zeros((LANES - nc, LANES), F32)], axis=0)
        thr_e = thr_all[e:e + 1, :]
        above = jnp.where(row_valid & (av > thr_e), 1.0, 0.0)
        tied = jnp.where(row_valid & (av == thr_e), 1.0, 0.0)
        t_local, _, t_before = lane_counts(tied)
        sel = above + tied * jnp.where(t_local + t_before <= need_all[e:e + 1, :], 1.0, 0.0)
        s_local, s_total, s_before = lane_counts(sel)
        rank = jnp.where(sel > 0.0, s_local, 0.0)

        start = s_before[:, 0:1]
        stop = start + s_total[:, 0:1]
        onehot = jnp.where((slot >= start) & (slot < stop), 1.0, 0.0)
        chunk_of_slot = jnp.sum(onehot * chunk_col, axis=0, keepdims=True)
        start_of_slot = jnp.sum(onehot * start, axis=0, keepdims=True)

        at = av.T
        hi = at.astype(BF16)
        rest = at - hi.astype(F32)
        mid = rest.astype(BF16)
        lo = (rest - mid.astype(F32)).astype(BF16)
        lhs = jnp.concatenate([rank.T.astype(BF16), hi, mid, lo], axis=0)
        picked = jnp.dot(lhs, onehot.astype(BF16), preferred_element_type=F32)
        rank_p = picked[0:LANES]
        aff_p = picked[LANES:2 * LANES] + picked[2 * LANES:3 * LANES] + picked[3 * LANES:]
        hit = rank_p == (slot - start_of_slot + 1.0)
        lane_of_slot = jnp.sum(jnp.where(hit, chunk_col, 0.0), axis=0, keepdims=True)
        idx_ref[e:e + 1, :] = (chunk_of_slot * LANES + lane_of_slot).astype(jnp.int32)
        gate_ref[e:e + 1, :] = jnp.sum(jnp.where(hit, aff_p, 0.0), axis=0, keepdims=True)


def _select_call(aff_t, aff_c, *, cap):
    n = aff_t.shape[1]
    assert n % LANES == 0 and n // LANES <= LANES and cap % LANES == 0
    return pl.pallas_call(
        functools.partial(_select_kernel, n=n, cap=cap),
        grid=(1,),
        in_specs=[pl.BlockSpec((N_EXPERTS, n), lambda i: (0, 0)),
                  pl.BlockSpec(aff_c.shape, lambda i: (0, 0))],
        out_specs=[pl.BlockSpec((N_EXPERTS, cap), lambda i: (0, 0))] * 2,
        out_shape=[jax.ShapeDtypeStruct((N_EXPERTS, cap), jnp.int32),
                   jax.ShapeDtypeStruct((N_EXPERTS, cap), F32)],
        compiler_params=_cparams(("arbitrary",), VMEM_LIMIT),
        name="select",
    )(aff_t, aff_c)


def _moe_kernel(idx_ref, gate_ref, wg_ref, wu_ref, wd_ref, x_hbm, out_hbm,
                gbuf, ybuf, acc_ref, gsem, osem, *, tm, n_tiles_total):
    e = pl.program_id(0)
    t = pl.program_id(1)
    nt = pl.num_programs(1)
    step = e * nt + t
    group = 8
    last = n_tiles_total - 1
    rows = tm * ROW_TILES

    def gather_start(tile_step, k, dst_slot):
        tok = idx_ref[tile_step * tm + k]
        pltpu.make_async_copy(
            x_hbm.at[pl.ds(pl.multiple_of(tok * ROW_TILES, ROW_TILES), ROW_TILES), :],
            gbuf.at[dst_slot, pl.ds(k * ROW_TILES, ROW_TILES), :], gsem.at[dst_slot]).start()

    def gather_wait(dst_slot):
        pltpu.make_async_copy(x_hbm.at[pl.ds(0, rows), :], gbuf.at[dst_slot],
                              gsem.at[dst_slot]).wait()

    def scatter_add(tile_step, src_slot, k0):
        pending = []
        for r in range(group):
            k = k0 + r
            tok = idx_ref[tile_step * tm + k]
            off = pl.multiple_of(tok * ROW_TILES, ROW_TILES)
            src = pl.multiple_of(k * ROW_TILES, ROW_TILES)
            pending.append((off, acc_ref[pl.ds(off, ROW_TILES), :]
                            + ybuf[src_slot, pl.ds(src, ROW_TILES), :]))
        for off, val in pending:
            acc_ref[pl.ds(off, ROW_TILES), :] = val

    @pl.when(step == 0)
    def _():
        acc_ref[...] = jnp.zeros_like(acc_ref)
        ybuf[...] = jnp.zeros_like(ybuf)

        def body(k, _):
            gather_start(0, k, 0)
            return 0
        lax.fori_loop(0, tm, body, 0)

    def tile_body(slot):
        nxt = jnp.minimum(step + 1, last)
        prev = jnp.maximum(step - 1, 0)
        for k in range(tm):
            gather_start(nxt, k, 1 - slot)
        for k0 in range(0, tm, group):
            scatter_add(prev, 1 - slot, k0)
        gather_wait(slot)
        xe = jnp.concatenate(
            [gbuf[slot, pl.ds(s, tm, stride=ROW_TILES), :] for s in range(ROW_TILES)],
            axis=1).astype(BF16)
        g = jnp.dot(xe, wg_ref[...], preferred_element_type=F32)
        u = jnp.dot(xe, wu_ref[...], preferred_element_type=F32)
        hid = (g * jax.nn.sigmoid(g) * u).astype(BF16)
        y = jnp.dot(hid, wd_ref[...], preferred_element_type=F32)
        gate =jnp.broadcast_to(gate_ref[...], (LANES, tm)).T
        for s in range(ROW_TILES):
            ybuf[slot, pl.ds(s, tm, stride=ROW_TILES), :] = y[:, s * LANES:(s + 1) * LANES] * gate

        @pl.when(step == last)
        def _():
            gather_wait(1 - slot)

            def body(kk, _):
                scatter_add(step, slot, kk * group)
                return 0
            lax.fori_loop(0, tm // group, body, 0)
            cp = pltpu.make_async_copy(acc_ref, out_hbm, osem)
            cp.start()
            cp.wait()

    for parity in range(2):
        pl.when(step % 2 == parity)(functools.partial(tile_body, parity))


def _moe_call(idx_flat, gates3, wg_bf, wu_bf, wd_bf, h2, *, cap, tm):
    n_rows = h2.shape[0]
    nt = cap // tm
    grid_spec = pltpu.PrefetchScalarGridSpec(
        num_scalar_prefetch=1,
        grid=(N_EXPERTS, nt),
        in_specs=[
            pl.BlockSpec((None, 1, tm), lambda e, t, idx: (e * nt + t, 0, 0)),
            pl.BlockSpec((None, D_MODEL, D_MODEL), lambda e, t, idx: (e, 0, 0)),
            pl.BlockSpec((None, D_MODEL, D_MODEL), lambda e, t, idx: (e, 0, 0)),
            pl.BlockSpec((None, D_MODEL, D_MODEL), lambda e, t, idx: (e, 0, 0)),
            pl.BlockSpec(memory_space=pl.ANY),
        ],
        out_specs=pl.BlockSpec(memory_space=pl.ANY),
        scratch_shapes=[
            pltpu.VMEM((2, tm * ROW_TILES, LANES), F32),
            pltpu.VMEM((2, tm * ROW_TILES, LANES), F32),
            pltpu.VMEM((n_rows, LANES), F32),
            pltpu.SemaphoreType.DMA((2,)),
            pltpu.SemaphoreType.DMA(()),
        ],
    )
    return pl.pallas_call(
        functools.partial(_moe_kernel, tm=tm, n_tiles_total=N_EXPERTS * nt),
        grid_spec=grid_spec,
        out_shape=jax.ShapeDtypeStruct((n_rows, LANES), F32),
        compiler_params=_cparams(("arbitrary", "arbitrary"), VMEM_LIMIT),
        name="moe",
    )(idx_flat, gates3, wg_bf, wu_bf, wd_bf, h2)


def _final_kernel(x1_ref, moe_ref, mod_ref, o_ref, *, tm):
    moe = jnp.concatenate(
        [moe_ref[pl.ds(s, tm, stride=ROW_TILES), :] for s in range(ROW_TILES)], axis=1)
    o_ref[...] = x1_ref[...] + mod_ref[0] * moe


def _final_call(x1, moe_tiles, mod3, *, seq, tm, mod_base, mod_stride):
    n = x1.shape[0]
    assert tm <= seq or mod_stride == 0, "a tile spanning sequences needs one modulation row"
    gate2_block = 5

    def mod_map(i):
        return (mod_base + mod_stride * (i * tm // seq), 0, gate2_block)

    return pl.pallas_call(
        functools.partial(_final_kernel, tm=tm),
        grid=(n // tm,),
        in_specs=[pl.BlockSpec((tm, D_MODEL), lambda i: (i, 0)),
                  pl.BlockSpec((tm * ROW_TILES, LANES), lambda i: (i, 0)),
                  pl.BlockSpec((1, 1, D_MODEL), mod_map)],
        out_specs=pl.BlockSpec((tm, D_MODEL), lambda i: (i, 0)),
        out_shape=jax.ShapeDtypeStruct((n, D_MODEL), F32),
        compiler_params=_cparams(("arbitrary",)),
        name="final",
    )(x1, moe_tiles, mod3)


def _rope_tables(seq):
    t = np.arange(seq)
    row, col = t // GRID_W, t % GRID_W
    half = HEAD_DIM // 2
    freqs = 1.0 / (ROPE_BASE ** (np.arange(0, half, 2) / half))
    ang_r = row[:, None] * freqs[None, :]
    ang_c = col[:, None] * freqs[None, :]
    ang = np.concatenate([ang_r, ang_r, ang_c, ang_c], axis=-1)
    cos = np.tile(np.cos(ang), (1, LANES // HEAD_DIM))
    sin = np.tile(np.sin(ang), (1, LANES // HEAD_DIM))
    sign = np.where((np.arange(LANES) % (HEAD_DIM // 2)) < (HEAD_DIM // 4), -1.0, 1.0)
    return jnp.asarray(cos, F32), jnp.asarray(sin * sign[None, :], F32)


def _segment_matrix():
    seg = np.arange(ATTN_WIDTH) // HEAD_DIM
    return jnp.asarray((seg[:, None] == seg[None, :]) / HEAD_DIM, BF16)


def _trunk(x, mod3, w, cache, *, mod_base, mod_stride, tm, tm_post, heads, tq, tk, moe_tm):
    batch, seq, _ = x.shape
    n = batch * seq
    x2d = x.reshape(n, D_MODEL)
    rope_tabs = _rope_tables(seq) if cache is not None else None
    pre = _pre_call(x2d, mod3[:, :, :2 * D_MODEL], w["g1"], w["win"], w["seg"], w["qg"], w["kg"],
                    rope_tabs, seq=seq, tm=tm, mod_base=mod_base, mod_stride=mod_stride,
                    emit_f32_kv=cache is None)
    p, q, k, v = pre[:4]
    attn = _attn_call(w["lam4"], w["subln"], q, k, v, cache, batch=batch, seq=seq,
                      heads=heads, tq=tq, tw=256, tk=tk)
    x1, h2, aff_t, aff_c = _post_call(x2d, attn, p, mod3[:, :, 2 * D_MODEL:5 * D_MODEL], w["wout"],
                                     w["wbd"], w["pool_scale"], w["g2"], w["wr_t"], seq=seq, tm=tm_post,
                                     mod_base=mod_base, mod_stride=mod_stride)
    cap = CAPACITY_FACTOR * n // N_EXPERTS
    idx, gates = _select_call(aff_t, aff_c, cap=cap)
    moe_tiles = _moe_call(idx.reshape(N_EXPERTS * cap), gates.reshape(-1, 1, moe_tm),
                          w["wg"], w["wu"], w["wd"], h2, cap=cap, tm=moe_tm)
    y = _final_call(x1, moe_tiles, mod3, seq=seq, tm=tm_post, mod_base=mod_base, mod_stride=mod_stride)
    y = y.reshape(batch, seq, D_MODEL)
    if cache is None:
        return y, pre[4], pre[5]
    return y, None, None


def kernel(x_prompt, x_sample, cache_k, cache_v, c, c_ctx, norm1_g, norm2_g, w_ada, b_ada, w_in,
           q_norm_g, k_norm_g, lambda_q1, lambda_k1, lambda_q2, lambda_k2, subln_g, w_pool,
           pool_scale, w_out, w_router, w_gate, w_up, w_down):
    assert w_ada.shape[0] == 1, "single-layer stack"
    batch, seq, _ = x_prompt.shape
    dec_batch, dec_seq, _ = x_sample.shape

    pad = SUBLANES - 1 - dec_batch
    cvec = jnp.concatenate([c_ctx[None, :], c, jnp.zeros((pad, D_MODEL), F32)], axis=0)
    mod = _ada_call(cvec, w_ada[0], b_ada[0])
    mod3 = mod.reshape(SUBLANES, 1, 6 * D_MODEL)

    n_groups = w_pool.shape[1]
    grp = POOL_WIDTH // n_groups
    eye = jnp.eye(n_groups, dtype=F32)
    wbd = (w_pool[0][:, :, None, :] * eye[:, None, :, None]).reshape(POOL_WIDTH, POOL_WIDTH)

    w = {
        "g1": norm1_g[0].reshape(1, D_MODEL),
        "g2": norm2_g[0].reshape(1, D_MODEL),
        "win": w_in[0].astype(BF16),
        "seg": _segment_matrix(),
        "qg": jnp.tile(q_norm_g[0], ATTN_WIDTH // HEAD_DIM).reshape(1, ATTN_WIDTH),
        "kg": jnp.tile(k_norm_g[0], ATTN_WIDTH // HEAD_DIM).reshape(1, ATTN_WIDTH),
        "lam4": jnp.stack([lambda_q1[0], lambda_k1[0], lambda_q2[0], lambda_k2[0]], axis=0),
        "subln": subln_g[0].reshape(V_DIM, 1),
        "wbd": wbd.astype(BF16),
        "pool_scale": pool_scale[0].reshape(1, POOL_WIDTH),
        "wout": w_out[0].astype(BF16),
        "wr_t": w_router[0].T,
        "wg": w_gate[0].astype(BF16),
        "wu": w_up[0].astype(BF16),
        "wd": w_down[0].astype(BF16),
    }

    yp, k_ctx, v_ctx = _trunk(x_prompt, mod3, w, None, mod_base=0, mod_stride=0,
                              tm=256, tm_post=512, heads=N_HEADS, tq=256, tk=256, moe_tm=256)
    past = cache_k.shape[2]
    cv = cache_v[:, 0].reshape(dec_batch * past, N_HEADS, V_DIM).transpose(1, 2, 0).astype(BF16)
    cv = jnp.concatenate([cv, jnp.ones((N_HEADS, V_EXT - V_DIM, dec_batch * past), BF16)], axis=1)
    cache = (cache_k[:, 0].reshape(dec_batch * past, ATTN_WIDTH).astype(BF16),
             cv.reshape(N_HEADS * V_EXT, dec_batch * past))
    ys, _, _ = _trunk(x_sample, mod3, w, cache, mod_base=1, mod_stride=1,
                      tm=512, tm_post=512, heads=1, tq=512, tk=512, moe_tm=256)
    ctx_k = (k_ctx.reshape(batch, N_HEADS, 2, HEAD_DIM, seq).transpose(0, 4, 1, 2, 3)
             .reshape(batch, 1, seq, N_HEADS, 2, HEAD_DIM))
    ctx_v = v_ctx.transpose(0, 2, 1, 3).reshape(batch, 1, seq, N_HEADS, V_DIM)
    return yp, ys, ctx_k, ctx_v
```

```python
import functools
import math

import numpy as np
import jax
import jax.numpy as jnp
from jax import lax
from jax.experimental import pallas as pl
from jax.experimental.pallas import tpu as pltpu

F32 = jnp.float32
BF16 = jnp.bfloat16

D_MODEL = 1024
POOL_WIDTH = 256
ATTN_WIDTH = 768
N_HEADS = 6
HEAD_DIM = 64
V_DIM = 128
IN_WIDTH = POOL_WIDTH + 3 * ATTN_WIDTH
N_EXPERTS = 16
CAPACITY_FACTOR = 2
GRID_W = 64
ROPE_BASE = 10000.0
EPS = 1e-6
LAMBDA_INIT = 0.8 - 0.6 * math.exp(-0.3 * 0)
LOG2E = math.log2(math.e)
V_EXT = V_DIM + 16
SEG_BLOCK = 256
SCORES_AHEAD = 3

LANES = 128
SUBLANES = 8
ROW_TILES = D_MODEL // LANES
HALO = 16
VMEM_LIMIT = 56 * 1024 * 1024


def _cparams(sem, vmem=None):
    return pltpu.CompilerParams(dimension_semantics=sem, vmem_limit_bytes=vmem)


def _ada_kernel(c_ref, w_ref, b_ref, o_ref):
    c = c_ref[...]
    s = c * jax.nn.sigmoid(c)
    o_ref[...] = jnp.dot(s.astype(BF16), w_ref[...].astype(BF16),
                         preferred_element_type=F32) + b_ref[...]


def _ada_call(cvec, w_ada, b_ada):
    rows, d = cvec.shape
    n = w_ada.shape[1]
    bn = 1536
    return pl.pallas_call(
        _ada_kernel,
        grid=(n // bn,),
        in_specs=[pl.BlockSpec((rows, d), lambda j: (0, 0)),
                  pl.BlockSpec((d, bn), lambda j: (0, j)),
                  pl.BlockSpec((1, bn), lambda j: (0, j))],
        out_specs=pl.BlockSpec((rows, bn), lambda j: (0, j)),
        out_shape=jax.ShapeDtypeStruct((rows, n), F32),
        compiler_params=_cparams(("arbitrary",)),
        name="ada",
    )(cvec, w_ada, b_ada.reshape(1, n))


def _segment_mean_square(a, seg_ref):
    sq = (a * a).astype(BF16)
    seg = seg_ref[...]
    return jnp.concatenate(
        [jnp.dot(sq[:, j:j + SEG_BLOCK], seg, preferred_element_type=F32)
         for j in range(0, a.shape[1], SEG_BLOCK)], axis=1)


def _rope(a, cos, sin_signed, first_half):
    parts = []
    for h in range(a.shape[1] // LANES):
        blk = a[:, h * LANES:(h + 1) * LANES]
        fwd = pltpu.roll(blk, LANES - HEAD_DIM // 4, 1)
        bwd = pltpu.roll(blk, HEAD_DIM // 4, 1)
        parts.append(blk * cos + jnp.where(first_half, fwd, bwd) * sin_signed)
    return jnp.concatenate(parts, axis=1)


def _side_cast(refs, n_inputs, enabled):
    if not enabled:
        return refs
    src_ref, dst_ref = refs[n_inputs - 1], refs[-1]
    dst_ref[...] = src_ref[...].astype(BF16)
    return refs[:n_inputs - 1] + refs[n_inputs:-1]


def _with_side_cast(w, index_map, in_specs, args, out_specs, out_shapes):
    if w is None:
        return
    _, rows, cols = w.shape
    in_specs.append(pl.BlockSpec((None, rows, cols), index_map))
    args.append(w)
    out_specs.append(pl.BlockSpec((None, rows, cols), index_map))
    out_shapes.append(jax.ShapeDtypeStruct(w.shape, BF16))


def _pre_kernel(*refs, rope, emit_f32_kv, side_cast):
    refs = _side_cast(refs, 7 + (2 if rope else 0) + 1, side_cast)
    x_ref, mod_ref, g1_ref, win_ref, seg_ref, qg_ref, kg_ref = refs[:7]
    pos = 7
    if rope:
        cos_ref, sin_ref = refs[pos:pos + 2]
        pos += 2
    p_ref, q_ref, k_ref, v_ref = refs[pos:pos + 4]
    pos += 4
    if emit_f32_kv:
        kf_ref, vf_ref = refs[pos:pos + 2]

    x = x_ref[...]
    mod = mod_ref[0]
    shift1 = mod[:, :D_MODEL]
    scale1 = mod[:, D_MODEL:2 * D_MODEL]
    ms = jnp.mean(x * x, axis=1, keepdims=True)
    h = x * lax.rsqrt(ms + EPS) * g1_ref[...] * (1.0 + scale1) + shift1
    z = jnp.dot(h.astype(BF16), win_ref[...], preferred_element_type=F32)

    p_ref[...] = z[:, :POOL_WIDTH]
    qz = z[:, POOL_WIDTH:POOL_WIDTH + ATTN_WIDTH]
    kz = z[:, POOL_WIDTH + ATTN_WIDTH:POOL_WIDTH + 2 * ATTN_WIDTH]
    vz = z[:, POOL_WIDTH + 2 * ATTN_WIDTH:]

    qn = qz * lax.rsqrt(_segment_mean_square(qz, seg_ref) + EPS) * qg_ref[...]
    kn = kz * lax.rsqrt(_segment_mean_square(kz, seg_ref) + EPS) * kg_ref[...]
    if rope:
        cos = cos_ref[...]
        sin_signed = sin_ref[...]
        lane = lax.broadcasted_iota(jnp.int32, cos.shape, 1)
        first_half = (lane % (HEAD_DIM // 2)) < (HEAD_DIM // 4)
        qn = _rope(qn, cos, sin_signed, first_half)
        kn = _rope(kn, cos, sin_signed, first_half)

    q_ref[...] = (qn * (LOG2E / math.sqrt(HEAD_DIM))).T.astype(BF16)
    k_ref[...] = kn.astype(BF16)
    vt = vz.T
    ones = jnp.ones((V_EXT - V_DIM, vt.shape[1]), BF16)
    for h in range(N_HEADS):
        v_ref[h * V_EXT:h * V_EXT + V_DIM, :] = vt[h * V_DIM:(h + 1) * V_DIM, :].astype(BF16)
        v_ref[h * V_EXT + V_DIM:(h + 1) * V_EXT, :] = ones
    if emit_f32_kv:
        kf_ref[0] = kn.T
        for h in range(N_HEADS):
            vf_ref[0, h] = vz[:, h * V_DIM:(h + 1) * V_DIM]


def _pre_call(x2d, mod3, g1, win_bf, seg, qg, kg, rope_tabs, *, seq, tm, mod_base, mod_stride,
              emit_f32_kv, cast_w=None):
    n = x2d.shape[0]
    rope = rope_tabs is not None
    tiles_per_seq = seq // tm

    def mod_map(i):
        return (mod_base + mod_stride * (i // tiles_per_seq), 0, 0)

    in_specs = [
        pl.BlockSpec((tm, D_MODEL), lambda i: (i, 0)),
        pl.BlockSpec((1, 1, 2 * D_MODEL), mod_map),
        pl.BlockSpec((1, D_MODEL), lambda i: (0, 0)),
        pl.BlockSpec((D_MODEL, IN_WIDTH), lambda i: (0, 0)),
        pl.BlockSpec((SEG_BLOCK, SEG_BLOCK), lambda i: (0, 0)),
        pl.BlockSpec((1, ATTN_WIDTH), lambda i: (0, 0)),
        pl.BlockSpec((1, ATTN_WIDTH), lambda i: (0, 0)),
    ]
    args = [x2d, mod3, g1, win_bf, seg, qg, kg]
    if rope:
        in_specs += [pl.BlockSpec((tm, LANES), lambda i: (i % tiles_per_seq, 0))] * 2
        args += list(rope_tabs)
    out_shapes = [jax.ShapeDtypeStruct((n, POOL_WIDTH), F32)]
    out_specs = [pl.BlockSpec((tm, POOL_WIDTH), lambda i: (i, 0))]
    out_shapes += [jax.ShapeDtypeStruct((ATTN_WIDTH, n), BF16),
                   jax.ShapeDtypeStruct((n, ATTN_WIDTH), BF16),
                   jax.ShapeDtypeStruct((N_HEADS * V_EXT, n), BF16)]
    out_specs += [pl.BlockSpec((ATTN_WIDTH, tm), lambda i: (0, i)),
                  pl.BlockSpec((tm, ATTN_WIDTH), lambda i: (i, 0)),
                  pl.BlockSpec((N_HEADS * V_EXT, tm), lambda i: (0, i))]
    if emit_f32_kv:
        def seq_map(i):
            return (i // tiles_per_seq, 0, i % tiles_per_seq)

        out_shapes += [jax.ShapeDtypeStruct((n // seq, ATTN_WIDTH, seq), F32),
                       jax.ShapeDtypeStruct((n // seq, N_HEADS, seq, V_DIM), F32)]
        out_specs += [pl.BlockSpec((1, ATTN_WIDTH, tm), seq_map),
                      pl.BlockSpec((1, N_HEADS, tm, V_DIM),
                                   lambda i: (i // tiles_per_seq, 0, i % tiles_per_seq, 0))]
    if cast_w is not None:
        assert cast_w.shape[0] == n // tm, "one expert matrix per grid step"
    _with_side_cast(cast_w, lambda i: (i, 0, 0), in_specs, args, out_specs, out_shapes)
    return pl.pallas_call(
        functools.partial(_pre_kernel, rope=rope, emit_f32_kv=emit_f32_kv,
                          side_cast=cast_w is not None),
        grid=(n // tm,),
        in_specs=in_specs,
        out_specs=out_specs,
        out_shape=out_shapes,
        compiler_params=_cparams(("arbitrary",), VMEM_LIMIT),
        name="pre_rope" if rope else "pre",
    )(*args)


def _attn_kernel(*refs, has_cache, heads, tq, tw, tk, seq, cache_len, side_cast):
    refs = _side_cast(refs, (7 if has_cache else 5) + 1, side_cast)
    if has_cache:
        lam_ref, g_ref, q_ref, kc_ref, vc_ref, k_ref, v_ref, o_ref = refs
    else:
        lam_ref, g_ref, q_ref, k_ref, v_ref, o_ref = refs

    lv = lam_ref[...]
    lam = (jnp.exp(jnp.sum(lv[0:1] * lv[1:2], axis=1, keepdims=True))
           - jnp.exp(jnp.sum(lv[2:3] * lv[3:4], axis=1, keepdims=True)) + LAMBDA_INIT)

    row = lax.broadcasted_iota(jnp.int32, (LANES, tq), 0)
    zero = jnp.zeros((LANES, tq), BF16)

    def sub_queries(h):
        qt = q_ref[h * LANES:(h + 1) * LANES, :]
        return (jnp.where(row < HEAD_DIM, qt, zero), jnp.where(row >= HEAD_DIM, qt, zero))

    def scores(kb, q_one):
        return jnp.dot(kb, q_one, preferred_element_type=F32)

    def softmax_step(s, m):
        m_new = jnp.maximum(m, jnp.max(s, axis=0, keepdims=True))
        return m_new, jnp.exp2(m - m_new), jnp.exp2(s - m_new).astype(BF16)

    def accumulate(vb, p, alpha, acc):
        return alpha * acc + jnp.dot(vb, p, preferred_element_type=F32)

    chunks = []
    if has_cache:
        chunks += [(kc_ref, vc_ref, j) for j in range(cache_len // tk)]
    chunks += [(k_ref, v_ref, j) for j in range(seq // tk)]

    def keys(c, h):
        kr, _, j = chunks[c]
        return kr[j * tk:(j + 1) * tk, h * LANES:(h + 1) * LANES]

    def values(c, h):
        _, vr, j = chunks[c]
        return vr[h * V_EXT:(h + 1) * V_EXT, j * tk:(j + 1) * tk]

    chains = []
    for h in range(heads):
        q_sub = sub_queries(h)
        chains += [(h, q_sub[sub][:, w * tw:(w + 1) * tw])
                   for w in range(tq // tw) for sub in range(2)]
    per_head = len(chains) // heads
    items = [(c, ch) for h in range(heads) for c in range(len(chunks))
             for ch in range(h * per_head, (h + 1) * per_head)]
    m = [jnp.full((1, tw), -jnp.inf, F32)] * len(chains)
    acc = [jnp.zeros((V_EXT, tw), F32)] * len(chains)
    queue = [scores(keys(c, chains[ch][0]), chains[ch][1]) for c, ch in items[:SCORES_AHEAD]]
    for i, (c, ch) in enumerate(items):
        s_cur = queue.pop(0)
        if i + SCORES_AHEAD < len(items):
            nc, nch = items[i + SCORES_AHEAD]
            queue.append(scores(keys(nc, chains[nch][0]), chains[nch][1]))
        m[ch], alpha, p = softmax_step(s_cur, m[ch])
        acc[ch] = accumulate(values(c, chains[ch][0]), p, alpha, acc[ch])
    for h in range(heads):
        outs = []
        for w in range(tq // tw):
            a1, a2 = acc[h * per_head + 2 * w], acc[h * per_head + 2 * w + 1]
            outs.append(a1[:V_DIM] / a1[V_DIM:V_DIM + 1]
                        - lam * (a2[:V_DIM] / a2[V_DIM:V_DIM + 1]))
        o = outs[0] if len(outs) == 1 else jnp.concatenate(outs, axis=1)
        y = (o * lax.rsqrt(jnp.mean(o * o, axis=0, keepdims=True) + EPS) * g_ref[...]
             * (1.0 - LAMBDA_INIT))
        o_ref[:, h * LANES:(h + 1) * LANES] = y.T.astype(BF16)


def _attn_call(lam4, subln_col, qt, k, vt, cache, *, batch, seq, heads, tq, tw, tk, cast_w=None):
    has_cache = cache is not None
    nq = seq // tq
    in_specs = [
        pl.BlockSpec((4, HEAD_DIM), lambda b, h, i: (0, 0)),
        pl.BlockSpec((V_DIM, 1), lambda b, h, i: (0, 0)),
        pl.BlockSpec((heads * LANES, tq), lambda b, h, i: (h, b * nq + i)),
    ]
    args = [lam4, subln_col, qt]
    cache_len = 0
    if has_cache:
        kc, vct = cache
        cache_len = kc.shape[0] // batch
        in_specs += [pl.BlockSpec((cache_len, heads * LANES), lambda b, h, i: (b, h)),
                     pl.BlockSpec((heads * V_EXT, cache_len), lambda b, h, i: (h, b))]
        args += [kc, vct]
    in_specs += [pl.BlockSpec((seq, heads * LANES), lambda b, h, i: (b, h)),
                 pl.BlockSpec((heads * V_EXT, seq), lambda b, h, i: (h, b))]
    args += [k, vt]
    out_specs = [pl.BlockSpec((tq, heads * LANES), lambda b, h, i: (b * nq + i, h))]
    out_shapes = [jax.ShapeDtypeStruct((batch * seq, ATTN_WIDTH), BF16)]
    grid = (batch, N_HEADS // heads, nq)
    if cast_w is not None:
        assert cast_w.shape[0] == grid[0] * grid[1] * grid[2], "one expert matrix per grid step"
    _with_side_cast(cast_w, lambda b, h, i: ((b * grid[1] + h) * grid[2] + i, 0, 0),
                    in_specs, args, out_specs, out_shapes)
    return pl.pallas_call(
        functools.partial(_attn_kernel, has_cache=has_cache, heads=heads, tq=tq, tw=tw, tk=tk,
                          seq=seq, cache_len=cache_len, side_cast=cast_w is not None),
        grid=grid,
        in_specs=in_specs,
        out_specs=out_specs,
        out_shape=out_shapes,
        compiler_params=_cparams(("arbitrary", "arbitrary", "arbitrary"), VMEM_LIMIT),
        name="attn_cache" if has_cache else "attn",
    )(*args)


def _post_kernel(*refs, tm, seq, side_cast):
    (x_ref, a_ref, p_ref, pp_ref, pn_ref, mod_ref, wout_ref, wbd_ref, ps_ref, g2_ref, wr_ref,
     x1_ref, h2_ref, aff_ref, affc_ref) = _side_cast(refs, 12, side_cast)
    i = pl.program_id(0)

    def centred_mean_minus_self(rows, prev, nxt, t0):
        r = rows.shape[0]
        ext = jnp.concatenate([prev, rows, nxt], axis=0)
        n_ext = r + 2 * HALO
        s2 = ext + pltpu.roll(ext, 1, 0)
        s4 = pltpu.roll(s2, 1, 0) + pltpu.roll(s2, n_ext - 1, 0)
        s8 = pltpu.roll(s4, 2, 0) + pltpu.roll(s4, n_ext - 2, 0)
        s16 = pltpu.roll(s8, 4, 0) + pltpu.roll(s8, n_ext - 4, 0)
        lane = lax.broadcasted_iota(jnp.int32, (r, POOL_WIDTH), 1)
        grp = lane // (POOL_WIDTH // 4)
        win = jnp.where(grp == 0, s2[HALO:HALO + r],
                        jnp.where(grp == 1, s4[HALO:HALO + r],
                                  jnp.where(grp == 2, s8[HALO:HALO + r], s16[HALO:HALO + r])))
        t = t0 + lax.broadcasted_iota(jnp.int32, (r, POOL_WIDTH), 0)
        left = jnp.where(grp == 0, 1, jnp.where(grp == 1, 2, jnp.where(grp == 2, 4, 8)))
        lo = jnp.maximum(t - left, 0)
        hi = jnp.minimum(t + left - 1, seq - 1) + 1
        return win / (hi - lo).astype(F32) - rows

    if tm <= seq:
        tiles_per_seq = seq // tm
        ti = i % tiles_per_seq
        pooled = centred_mean_minus_self(
            p_ref[...], jnp.where(ti == 0, 0.0, pp_ref[...]),
            jnp.where(ti == tiles_per_seq - 1, 0.0, pn_ref[...]), ti * tm)
    else:
        halo = jnp.zeros((HALO, POOL_WIDTH), F32)
        pooled = jnp.concatenate(
            [centred_mean_minus_self(p_ref[j * seq:(j + 1) * seq, :], halo, halo, 0)
             for j in range(tm // seq)], axis=0)
    pool = jnp.dot(pooled.astype(BF16), wbd_ref[...], preferred_element_type=F32) * ps_ref[...]

    cat = jnp.concatenate([pool.astype(BF16), a_ref[...]], axis=1)
    mix = jnp.dot(cat, wout_ref[...], preferred_element_type=F32)
    mod = mod_ref[0]
    gate1 = mod[:, 0:D_MODEL]
    shift2 = mod[:, D_MODEL:2 * D_MODEL]
    scale2 = mod[:, 2 * D_MODEL:3 * D_MODEL]
    x1 = x_ref[...] + gate1 * mix
    x1_ref[...] = x1
    ms = jnp.mean(x1 * x1, axis=1, keepdims=True)
    h2 = x1 * lax.rsqrt(ms + EPS) * g2_ref[...] * (1.0 + scale2) + shift2
    for s in range(ROW_TILES):
        h2_ref[pl.ds(s, tm, stride=ROW_TILES), :] = h2[:, s * LANES:(s + 1) * LANES]

    logits = lax.dot_general(wr_ref[...], h2, (((1,), (1,)), ((), ())),
                             precision=lax.Precision.HIGHEST, preferred_element_type=F32)
    e = jnp.exp(logits - jnp.max(logits, axis=0, keepdims=True))
    aff = e / jnp.sum(e, axis=0, keepdims=True)
    aff_ref[...] = aff
    for c in range(tm // LANES):
        affc_ref[c * N_EXPERTS:(c + 1) * N_EXPERTS, :] = aff[:, c * LANES:(c + 1) * LANES]


def _post_call(x2d, attn, p, mod3, wout_bf, wbd_bf, pool_scale, g2, wr_t, *, seq, tm, mod_base,
               mod_stride, cast_w=None):
    n = x2d.shape[0]
    assert cast_w is None or cast_w.shape[0] == n // tm, "one expert matrix per grid step"
    assert tm <= seq or mod_stride == 0, "a tile spanning sequences needs one modulation row"
    halo_per_tile = tm // HALO
    n_halo = n // HALO

    def mod_map(i):
        return (mod_base + mod_stride * (i * tm // seq), 0, 0)

    in_specs = [
            pl.BlockSpec((tm, D_MODEL), lambda i: (i, 0)),
            pl.BlockSpec((tm, ATTN_WIDTH), lambda i: (i, 0)),
            pl.BlockSpec((tm, POOL_WIDTH), lambda i: (i, 0)),
            pl.BlockSpec((HALO, POOL_WIDTH), lambda i: (jnp.maximum(i * halo_per_tile - 1, 0), 0)),
            pl.BlockSpec((HALO, POOL_WIDTH),
                         lambda i: (jnp.minimum((i + 1) * halo_per_tile, n_halo - 1), 0)),
            pl.BlockSpec((1, 1, 3 * D_MODEL), lambda i: mod_map(i)[:2] + (0,)),
            pl.BlockSpec((D_MODEL, D_MODEL), lambda i: (0, 0)),
            pl.BlockSpec((POOL_WIDTH, POOL_WIDTH), lambda i: (0, 0)),
            pl.BlockSpec((1, POOL_WIDTH), lambda i: (0, 0)),
            pl.BlockSpec((1, D_MODEL), lambda i: (0, 0)),
            pl.BlockSpec((N_EXPERTS, D_MODEL), lambda i: (0, 0)),
    ]
    out_specs = [
            pl.BlockSpec((tm, D_MODEL), lambda i: (i, 0)),
            pl.BlockSpec((tm * ROW_TILES, LANES), lambda i: (i, 0)),
            pl.BlockSpec((N_EXPERTS, tm), lambda i: (0, i)),
            pl.BlockSpec((tm // LANES * N_EXPERTS, LANES), lambda i: (i, 0)),
    ]
    out_shapes = [
            jax.ShapeDtypeStruct((n, D_MODEL), F32),
            jax.ShapeDtypeStruct((n * ROW_TILES, LANES), F32),
            jax.ShapeDtypeStruct((N_EXPERTS, n), F32),
            jax.ShapeDtypeStruct((n // LANES * N_EXPERTS, LANES), F32),
    ]
    args = [x2d, attn, p, p, p, mod3, wout_bf, wbd_bf, pool_scale, g2, wr_t]
    _with_side_cast(cast_w, lambda i: (i, 0, 0), in_specs, args, out_specs, out_shapes)
    return pl.pallas_call(
        functools.partial(_post_kernel, tm=tm, seq=seq, side_cast=cast_w is not None),
        grid=(n // tm,),
        in_specs=in_specs,
        out_specs=out_specs,
        out_shape=out_shapes,
        compiler_params=_cparams(("arbitrary",), VMEM_LIMIT),
        name="post",
    )(*args)


def _select_kernel(a_ref, ac_ref, idx_ref, gate_ref, *, n, cap):
    nc = n // LANES
    a = a_ref[...]
    thr = jnp.zeros((N_EXPERTS, 1), jnp.int32)
    for bit in range(30, -1, -1):
        cand = thr | (1 << bit)
        cnt = jnp.sum(jnp.where(a >= pltpu.bitcast(cand, F32), 1.0, 0.0), axis=1, keepdims=True)
        thr = jnp.where(cnt >= cap, cand, thr)
    thr_all = pltpu.bitcast(thr, F32)
    need_all = cap - jnp.sum(jnp.where(a > thr_all, 1.0, 0.0), axis=1, keepdims=True)

    r = lax.broadcasted_iota(jnp.int32, (LANES, LANES), 0)
    c = lax.broadcasted_iota(jnp.int32, (LANES, LANES), 1)
    upper = jnp.where(r <= c, 1.0, 0.0).astype(BF16)
    lower = jnp.where(c < r, 1.0, 0.0).astype(BF16)
    row_valid = r < nc
    chunk_col = lax.broadcasted_iota(jnp.int32, (LANES, 1), 0).astype(F32)
    slot = lax.broadcasted_iota(jnp.int32, (1, cap), 1).astype(F32)

    def lane_counts(mask):
        local = jnp.dot(mask.astype(BF16), upper, preferred_element_type=F32)
        total = jnp.broadcast_to(local[:, LANES - 1:LANES], (LANES, LANES))
        before = jnp.dot(lower, total.astype(BF16), preferred_element_type=F32)
        return local, total, before

    for e in range(N_EXPERTS):
        av = ac_ref[pl.ds(e, nc, stride=N_EXPERTS), :]
        if nc < LANES:
            av = jnp.concatenate([av, jnp.zeros((LANES - nc, LANES), F32)], axis=0)
        thr_e = thr_all[e:e + 1, :]
        above = jnp.where(row_valid & (av > thr_e), 1.0, 0.0)
        tied = jnp.where(row_valid & (av == thr_e), 1.0, 0.0)
        t_local, _, t_before = lane_counts(tied)
        sel = above + tied * jnp.where(t_local + t_before <= need_all[e:e + 1, :], 1.0, 0.0)
        s_local, s_total, s_before = lane_counts(sel)
        rank = jnp.where(sel > 0.0, s_local, 0.0)

        start = s_before[:, 0:1]
        stop = start + s_total[:, 0:1]
        onehot = jnp.where((slot >= start) & (slot < stop), 1.0, 0.0)
        chunk_of_slot = jnp.sum(onehot * chunk_col, axis=0, keepdims=True)
        start_of_slot = jnp.sum(onehot * start, axis=0, keepdims=True)

        at = av.T
        hi = at.astype(BF16)
        rest = at - hi.astype(F32)
        mid = rest.astype(BF16)
        lo = (rest - mid.astype(F32)).astype(BF16)
        lhs = jnp.concatenate([rank.T.astype(BF16), hi, mid, lo], axis=0)
        picked = jnp.dot(lhs, onehot.astype(BF16), preferred_element_type=F32)
        rank_p = picked[0:LANES]
        aff_p = picked[LANES:2 * LANES] + picked[2 * LANES:3 * LANES] + picked[3 * LANES:]
        hit = rank_p == (slot - start_of_slot + 1.0)
        lane_of_slot = jnp.sum(jnp.where(hit, chunk_col, 0.0), axis=0, keepdims=True)
        idx_ref[e:e + 1, :] = (chunk_of_slot * LANES + lane_of_slot).astype(jnp.int32)
        gate_ref[e:e + 1, :] = jnp.sum(jnp.where(hit, aff_p, 0.0), axis=0, keepdims=True)


def _select_call(aff_t, aff_c, *, cap):
    n = aff_t.shape[1]
    assert n % LANES == 0 and n // LANES <= LANES and cap % LANES == 0
    return pl.pallas_call(
        functools.partial(_select_kernel, n=n, cap=cap),
        grid=(1,),
        in_specs=[pl.BlockSpec((N_EXPERTS, n), lambda i: (0, 0)),
                  pl.BlockSpec(aff_c.shape, lambda i: (0, 0))],
        out_specs=[pl.BlockSpec((N_EXPERTS, cap), lambda i: (0, 0))] * 2,
        out_shape=[jax.ShapeDtypeStruct((N_EXPERTS, cap), jnp.int32),
                   jax.ShapeDtypeStruct((N_EXPERTS, cap), F32)],
        compiler_params=_cparams(("arbitrary",), VMEM_LIMIT),
        name="select",
    )(aff_t, aff_c)


def _moe_kernel(idx_ref, gate_ref, wg_ref, wu_ref, wd_ref, x_hbm, out_hbm,
                gbuf, ybuf, acc_ref, gsem, osem, *, tm, n_tiles_total):
    e = pl.program_id(0)
    t = pl.program_id(1)
    nt = pl.num_programs(1)
    step = e * nt + t
    group = 8
    last = n_tiles_total - 1
    rows = tm * ROW_TILES

    def gather_start(tile_step, k, dst_slot):
        tok = idx_ref[tile_step * tm + k]
        pltpu.make_async_copy(
            x_hbm.at[pl.ds(pl.multiple_of(tok * ROW_TILES, ROW_TILES), ROW_TILES), :],
            gbuf.at[dst_slot, pl.ds(k * ROW_TILES, ROW_TILES), :], gsem.at[dst_slot]).start()

    def gather_wait(dst_slot):
        pltpu.make_async_copy(x_hbm.at[pl.ds(0, rows), :], gbuf.at[dst_slot],
                              gsem.at[dst_slot]).wait()

    def scatter_add(tile_step, src_slot, k0):
        pending = []
        for r in range(group):
            k = k0 + r
            tok = idx_ref[tile_step * tm + k]
            off = pl.multiple_of(tok * ROW_TILES, ROW_TILES)
            src = pl.multiple_of(k * ROW_TILES, ROW_TILES)
            pending.append((off, acc_ref[pl.ds(off, ROW_TILES), :]
                            + ybuf[src_slot, pl.ds(src, ROW_TILES), :]))
        for off, val in pending:
            acc_ref[pl.ds(off, ROW_TILES), :] = val

    @pl.when(step == 0)
    def _():
        acc_ref[...] = jnp.zeros_like(acc_ref)
        ybuf[...] = jnp.zeros_like(ybuf)

        def body(k, _):
            gather_start(0, k, 0)
            return 0
        lax.fori_loop(0, tm, body, 0)

    def tile_body(slot):
        nxt = jnp.minimum(step + 1, last)
        prev = jnp.maximum(step - 1, 0)
        for k in range(tm):
            gather_start(nxt, k, 1 - slot)
        gather_wait(slot)
        xe = jnp.concatenate(
            [gbuf[slot, pl.ds(s, tm, stride=ROW_TILES), :] for s in range(ROW_TILES)],
            axis=1).astype(BF16)
        g = jnp.dot(xe, wg_ref[...], preferred_element_type=F32)
        for k0 in range(0, tm // 2, group):
            scatter_add(prev, 1 - slot, k0)
        u = jnp.dot(xe, wu_ref[...], preferred_element_type=F32)
        for k0 in range(tm // 2, tm, group):
            scatter_add(prev, 1 - slot, k0)
        hid = (g * jax.nn.sigmoid(g) * u).astype(BF16)
        y = jnp.dot(hid, wd_ref[...], preferred_element_type=F32)
        gate =jnp.broadcast_to(gate_ref[...], (LANES, tm)).T
        for s in range(ROW_TILES):
            ybuf[slot, pl.ds(s, tm, stride=ROW_TILES), :] = y[:, s * LANES:(s + 1) * LANES] * gate

        @pl.when(step == last)
        def _():
            gather_wait(1 - slot)

            def body(kk, _):
                scatter_add(step, slot, kk * group)
                return 0
            lax.fori_loop(0, tm // group, body, 0)
            cp = pltpu.make_async_copy(acc_ref, out_hbm, osem)
            cp.start()
            cp.wait()

    for parity in range(2):
        pl.when(step % 2 == parity)(functools.partial(tile_body, parity))


def _moe_call(idx_flat, gates3, wg_bf, wu_bf, wd_bf, h2, *, cap, tm):
    n_rows = h2.shape[0]
    nt = cap // tm
    grid_spec = pltpu.PrefetchScalarGridSpec(
        num_scalar_prefetch=1,
        grid=(N_EXPERTS, nt),
        in_specs=[
            pl.BlockSpec((None, 1, tm), lambda e, t, idx: (e * nt + t, 0, 0)),
            pl.BlockSpec((None, D_MODEL, D_MODEL), lambda e, t, idx: (e, 0, 0)),
            pl.BlockSpec((None, D_MODEL, D_MODEL), lambda e, t, idx: (e, 0, 0)),
            pl.BlockSpec((None, D_MODEL, D_MODEL), lambda e, t, idx: (e, 0, 0)),
            pl.BlockSpec(memory_space=pl.ANY),
        ],
        out_specs=pl.BlockSpec(memory_space=pl.ANY),
        scratch_shapes=[
            pltpu.VMEM((2, tm * ROW_TILES, LANES), F32),
            pltpu.VMEM((2, tm * ROW_TILES, LANES), F32),
            pltpu.VMEM((n_rows, LANES), F32),
            pltpu.SemaphoreType.DMA((2,)),
            pltpu.SemaphoreType.DMA(()),
        ],
    )
    return pl.pallas_call(
        functools.partial(_moe_kernel, tm=tm, n_tiles_total=N_EXPERTS * nt),
        grid_spec=grid_spec,
        out_shape=jax.ShapeDtypeStruct((n_rows, LANES), F32),
        compiler_params=_cparams(("arbitrary", "arbitrary"), VMEM_LIMIT),
        name="moe",
    )(idx_flat, gates3, wg_bf, wu_bf, wd_bf, h2)


def _final_kernel(x1_ref, moe_ref, mod_ref, o_ref, *, tm):
    moe = jnp.concatenate(
        [moe_ref[pl.ds(s, tm, stride=ROW_TILES), :] for s in range(ROW_TILES)], axis=1)
    o_ref[...] = x1_ref[...] + mod_ref[0] * moe


def _final_call(x1, moe_tiles, mod3, *, seq, tm, mod_base, mod_stride):
    n = x1.shape[0]
    assert tm <= seq or mod_stride == 0, "a tile spanning sequences needs one modulation row"
    gate2_block = 5

    def mod_map(i):
        return (mod_base + mod_stride * (i * tm // seq), 0, gate2_block)

    return pl.pallas_call(
        functools.partial(_final_kernel, tm=tm),
        grid=(n // tm,),
        in_specs=[pl.BlockSpec((tm, D_MODEL), lambda i: (i, 0)),
                  pl.BlockSpec((tm * ROW_TILES, LANES), lambda i: (i, 0)),
                  pl.BlockSpec((1, 1, D_MODEL), mod_map)],
        out_specs=pl.BlockSpec((tm, D_MODEL), lambda i: (i, 0)),
        out_shape=jax.ShapeDtypeStruct((n, D_MODEL), F32),
        compiler_params=_cparams(("arbitrary",)),
        name="final",
    )(x1, moe_tiles, mod3)


def _rope_tables(seq):
    t = np.arange(seq)
    row, col = t // GRID_W, t % GRID_W
    half = HEAD_DIM // 2
    freqs = 1.0 / (ROPE_BASE ** (np.arange(0, half, 2) / half))
    ang_r = row[:, None] * freqs[None, :]
    ang_c = col[:, None] * freqs[None, :]
    ang = np.concatenate([ang_r, ang_r, ang_c, ang_c], axis=-1)
    cos = np.tile(np.cos(ang), (1, LANES // HEAD_DIM))
    sin = np.tile(np.sin(ang), (1, LANES // HEAD_DIM))
    sign = np.where((np.arange(LANES) % (HEAD_DIM // 2)) < (HEAD_DIM // 4), -1.0, 1.0)
    return jnp.asarray(cos, F32), jnp.asarray(sin * sign[None, :], F32)


def _segment_matrix():
    seg = np.arange(SEG_BLOCK) // HEAD_DIM
    return jnp.asarray((seg[:, None] == seg[None, :]) / HEAD_DIM, BF16)


def _token_mixing(x, mod3, w, cache, *, mod_base, mod_stride, tm, tm_post, heads, tq, tk,
                  cast_pre=None, cast_attn=None, cast_post=None):
    batch, seq, _ = x.shape
    n = batch * seq
    x2d = x.reshape(n, D_MODEL)
    rope_tabs = _rope_tables(seq) if cache is not None else None
    casts = {}
    pre = list(_pre_call(x2d, mod3[:, :, :2 * D_MODEL], w["g1"], w["win"], w["seg"], w["qg"],
                         w["kg"], rope_tabs, seq=seq, tm=tm, mod_base=mod_base,
                         mod_stride=mod_stride, emit_f32_kv=cache is None, cast_w=cast_pre))
    if cast_pre is not None:
        casts["pre"] = pre.pop()
    p, q, k, v = pre[:4]
    attn = list(_attn_call(w["lam4"], w["subln"], q, k, v, cache, batch=batch, seq=seq,
                           heads=heads, tq=tq, tw=256, tk=tk, cast_w=cast_attn))
    if cast_attn is not None:
        casts["attn"] = attn.pop()
    post = list(_post_call(x2d, attn[0], p, mod3[:, :, 2 * D_MODEL:5 * D_MODEL], w["wout"],
                           w["wbd"], w["pool_scale"], w["g2"], w["wr_t"], seq=seq, tm=tm_post,
                           mod_base=mod_base, mod_stride=mod_stride, cast_w=cast_post))
    if cast_post is not None:
        casts["post"] = post.pop()
    x1, h2, aff_t, aff_c = post
    cap = CAPACITY_FACTOR * n // N_EXPERTS
    idx, gates = _select_call(aff_t, aff_c, cap=cap)
    return {"x1": x1, "h2": h2, "idx": idx, "gates": gates, "cap": cap, "kv": pre[4:],
            "casts": casts, "shape": (batch, seq), "tm_post": tm_post,
            "mod": (mod_base, mod_stride)}


def _channel_mixing(mixed, mod3, wg, wu, wd, *, moe_tm):
    batch, seq = mixed["shape"]
    cap = mixed["cap"]
    mod_base, mod_stride = mixed["mod"]
    moe_tiles = _moe_call(mixed["idx"].reshape(N_EXPERTS * cap),
                          mixed["gates"].reshape(-1, 1, moe_tm), wg, wu, wd, mixed["h2"],
                          cap=cap, tm=moe_tm)
    y = _final_call(mixed["x1"], moe_tiles, mod3, seq=seq, tm=mixed["tm_post"],
                    mod_base=mod_base, mod_stride=mod_stride)
    return y.reshape(batch, seq, D_MODEL)


def kernel(x_prompt, x_sample, cache_k, cache_v, c, c_ctx, norm1_g, norm2_g, w_ada, b_ada, w_in,
           q_norm_g, k_norm_g, lambda_q1, lambda_k1, lambda_q2, lambda_k2, subln_g, w_pool,
           pool_scale, w_out, w_router, w_gate, w_up, w_down):
    assert w_ada.shape[0] == 1, "single-layer stack"
    batch, seq, _ = x_prompt.shape
    dec_batch, dec_seq, _ = x_sample.shape

    pad = SUBLANES - 1 - dec_batch
    cvec = jnp.concatenate([c_ctx[None, :], c, jnp.zeros((pad, D_MODEL), F32)], axis=0)
    mod = _ada_call(cvec, w_ada[0], b_ada[0])
    mod3 = mod.reshape(SUBLANES, 1, 6 * D_MODEL)

    n_groups = w_pool.shape[1]
    grp = POOL_WIDTH // n_groups
    eye = jnp.eye(n_groups, dtype=F32)
    wbd = (w_pool[0][:, :, None, :] * eye[:, None, :, None]).reshape(POOL_WIDTH, POOL_WIDTH)

    w = {
        "g1": norm1_g[0].reshape(1, D_MODEL),
        "g2": norm2_g[0].reshape(1, D_MODEL),
        "win": w_in[0].astype(BF16),
        "seg": _segment_matrix(),
        "qg": jnp.tile(q_norm_g[0], ATTN_WIDTH // HEAD_DIM).reshape(1, ATTN_WIDTH),
        "kg": jnp.tile(k_norm_g[0], ATTN_WIDTH // HEAD_DIM).reshape(1, ATTN_WIDTH),
        "lam4": jnp.stack([lambda_q1[0], lambda_k1[0], lambda_q2[0], lambda_k2[0]], axis=0),
        "subln": subln_g[0].reshape(V_DIM, 1),
        "wbd": wbd.astype(BF16),
        "pool_scale": pool_scale[0].reshape(1, POOL_WIDTH),
        "wout": w_out[0].astype(BF16),
        "wr_t": w_router[0].T,
    }

    ctx = _token_mixing(x_prompt, mod3, w, None, mod_base=0, mod_stride=0, tm=256, tm_post=512,
                        heads=N_HEADS, tq=256, tk=256, cast_attn=w_down[0])
    past = cache_k.shape[2]
    cv = cache_v[:, 0].reshape(dec_batch * past, N_HEADS, V_DIM).transpose(1, 2, 0).astype(BF16)
    cv = jnp.concatenate([cv, jnp.ones((N_HEADS, V_EXT - V_DIM, dec_batch * past), BF16)], axis=1)
    cache = (cache_k[:, 0].reshape(dec_batch * past, ATTN_WIDTH).astype(BF16),
             cv.reshape(N_HEADS * V_EXT, dec_batch * past))
    lat = _token_mixing(x_sample, mod3, w, cache, mod_base=1, mod_stride=1, tm=512, tm_post=512,
                        heads=1, tq=512, tk=512, cast_pre=w_gate[0], cast_post=w_up[0])
    wg, wu, wd = lat["casts"]["pre"], lat["casts"]["post"], ctx["casts"]["attn"]
    yp = _channel_mixing(ctx, mod3, wg, wu, wd, moe_tm=256)
    ys = _channel_mixing(lat, mod3, wg, wu, wd, moe_tm=256)
    k_ctx, v_ctx = ctx["kv"]
    ctx_k =(k_ctx.reshape(batch, N_HEADS, 2, HEAD_DIM, seq).transpose(0, 4, 1, 2, 3)
             .reshape(batch, 1, seq, N_HEADS, 2, HEAD_DIM))
    ctx_v = v_ctx.transpose(0, 2, 1, 3).reshape(batch, 1, seq, N_HEADS, V_DIM)
    return yp, ys, ctx_k, ctx_v
```

```python
import functools
import math

import numpy as np
import jax
import jax.numpy as jnp
from jax import lax
from jax.experimental import pallas as pl
from jax.experimental.pallas import tpu as pltpu

F32 = jnp.float32
BF16 = jnp.bfloat16

D_MODEL = 1024
POOL_WIDTH = 256
ATTN_WIDTH = 768
N_HEADS = 6
HEAD_DIM = 64
V_DIM = 128
IN_WIDTH = POOL_WIDTH + 3 * ATTN_WIDTH
N_EXPERTS = 16
CAPACITY_FACTOR = 2
GRID_W = 64
ROPE_BASE = 10000.0
EPS = 1e-6
LAMBDA_INIT = 0.8 - 0.6 * math.exp(-0.3 * 0)
LOG2E = math.log2(math.e)
V_EXT = V_DIM + 16
SEG_BLOCK = 256
SCORES_AHEAD = 3

LANES = 128
SUBLANES = 8
ROW_TILES = D_MODEL // LANES
HALO = 16
VMEM_LIMIT = 56 * 1024 * 1024


def _cparams(sem, vmem=None):
    return pltpu.CompilerParams(dimension_semantics=sem, vmem_limit_bytes=vmem)


def _ada_kernel(c_ref, w_ref, b_ref, o_ref):
    c = c_ref[...]
    s = c * jax.nn.sigmoid(c)
    o_ref[...] = jnp.dot(s.astype(BF16), w_ref[...].astype(BF16),
                         preferred_element_type=F32) + b_ref[...]


def _ada_call(cvec, w_ada, b_ada):
    rows, d = cvec.shape
    n = w_ada.shape[1]
    bn = 1536
    return pl.pallas_call(
        _ada_kernel,
        grid=(n // bn,),
        in_specs=[pl.BlockSpec((rows, d), lambda j: (0, 0)),
                  pl.BlockSpec((d, bn), lambda j: (0, j)),
                  pl.BlockSpec((1, bn), lambda j: (0, j))],
        out_specs=pl.BlockSpec((rows, bn), lambda j: (0, j)),
        out_shape=jax.ShapeDtypeStruct((rows, n), F32),
        compiler_params=_cparams(("arbitrary",)),
        name="ada",
    )(cvec, w_ada, b_ada.reshape(1, n))


def _segment_mean_square(a, seg_ref):
    sq = (a * a).astype(BF16)
    seg = seg_ref[...]
    return jnp.concatenate(
        [jnp.dot(sq[:, j:j + SEG_BLOCK], seg, preferred_element_type=F32)
         for j in range(0, a.shape[1], SEG_BLOCK)], axis=1)


def _rope(a, cos, sin_signed, first_half):
    parts = []
    for h in range(a.shape[1] // LANES):
        blk = a[:, h * LANES:(h + 1) * LANES]
        fwd = pltpu.roll(blk, LANES - HEAD_DIM // 4, 1)
        bwd = pltpu.roll(blk, HEAD_DIM // 4, 1)
        parts.append(blk * cos + jnp.where(first_half, fwd, bwd) * sin_signed)
    return jnp.concatenate(parts, axis=1)


def _side_cast(refs, n_inputs, enabled, step=None, steps_per_block=1):
    if not enabled:
        return refs
    src_ref, dst_ref = refs[n_inputs - 1], refs[-1]

    def convert():
        dst_ref[...] = src_ref[...].astype(BF16)

    if steps_per_block == 1:
        convert()
    else:
        pl.when(step % steps_per_block == 0)(convert)
    return refs[:n_inputs - 1] + refs[n_inputs:-1]


def _with_side_cast(w, index_map, in_specs, args, out_specs, out_shapes):
    if w is None:
        return
    _, rows, cols = w.shape
    in_specs.append(pl.BlockSpec((None, rows, cols), index_map))
    args.append(w)
    out_specs.append(pl.BlockSpec((None, rows, cols), index_map))
    out_shapes.append(jax.ShapeDtypeStruct(w.shape, BF16))


def _pre_kernel(*refs, rope, emit_f32_kv, side_cast):
    refs = _side_cast(refs, 7 + (2 if rope else 0) + 1, side_cast)
    x_ref, mod_ref, g1_ref, win_ref, seg_ref, qg_ref, kg_ref = refs[:7]
    pos = 7
    if rope:
        cos_ref, sin_ref = refs[pos:pos + 2]
        pos += 2
    p_ref, q_ref, k_ref, v_ref = refs[pos:pos + 4]
    pos += 4
    if emit_f32_kv:
        kf_ref, vf_ref = refs[pos:pos + 2]

    x = x_ref[...]
    mod = mod_ref[0]
    shift1 = mod[:, :D_MODEL]
    scale1 = mod[:, D_MODEL:2 * D_MODEL]
    ms = jnp.mean(x * x, axis=1, keepdims=True)
    h = x * lax.rsqrt(ms + EPS) * g1_ref[...] * (1.0 + scale1) + shift1
    z = jnp.dot(h.astype(BF16), win_ref[...], preferred_element_type=F32)

    p_ref[...] = z[:, :POOL_WIDTH]
    qz = z[:, POOL_WIDTH:POOL_WIDTH + ATTN_WIDTH]
    kz = z[:, POOL_WIDTH + ATTN_WIDTH:POOL_WIDTH + 2 * ATTN_WIDTH]
    vz = z[:, POOL_WIDTH + 2 * ATTN_WIDTH:]

    qn = qz * lax.rsqrt(_segment_mean_square(qz, seg_ref) + EPS) * qg_ref[...]
    kn = kz * lax.rsqrt(_segment_mean_square(kz, seg_ref) + EPS) * kg_ref[...]
    if rope:
        cos = cos_ref[...]
        sin_signed = sin_ref[...]
        lane = lax.broadcasted_iota(jnp.int32, cos.shape, 1)
        first_half = (lane % (HEAD_DIM // 2)) < (HEAD_DIM // 4)
        qn = _rope(qn, cos, sin_signed, first_half)
        kn = _rope(kn, cos, sin_signed, first_half)

    q_ref[...] = (qn * (LOG2E / math.sqrt(HEAD_DIM))).T.astype(BF16)
    k_ref[...] = kn.astype(BF16)
    vt = vz.T
    ones = jnp.ones((V_EXT - V_DIM, vt.shape[1]), BF16)
    for h in range(N_HEADS):
        v_ref[h * V_EXT:h * V_EXT + V_DIM, :] = vt[h * V_DIM:(h + 1) * V_DIM, :].astype(BF16)
        v_ref[h * V_EXT + V_DIM:(h + 1) * V_EXT, :] = ones
    if emit_f32_kv:
        kf_ref[0] = kn.T
        for h in range(N_HEADS):
            vf_ref[0, h] = vz[:, h * V_DIM:(h + 1) * V_DIM]


def _pre_call(x2d, mod3, g1, win_bf, seg, qg, kg, rope_tabs, *, seq, tm, mod_base, mod_stride,
              emit_f32_kv, cast_w=None):
    n = x2d.shape[0]
    rope = rope_tabs is not None
    tiles_per_seq = seq // tm

    def mod_map(i):
        return (mod_base + mod_stride * (i // tiles_per_seq), 0, 0)

    in_specs = [
        pl.BlockSpec((tm, D_MODEL), lambda i: (i, 0)),
        pl.BlockSpec((1, 1, 2 * D_MODEL), mod_map),
        pl.BlockSpec((1, D_MODEL), lambda i: (0, 0)),
        pl.BlockSpec((D_MODEL, IN_WIDTH), lambda i: (0, 0)),
        pl.BlockSpec((SEG_BLOCK, SEG_BLOCK), lambda i: (0, 0)),
        pl.BlockSpec((1, ATTN_WIDTH), lambda i: (0, 0)),
        pl.BlockSpec((1, ATTN_WIDTH), lambda i: (0, 0)),
    ]
    args = [x2d, mod3, g1, win_bf, seg, qg, kg]
    if rope:
        in_specs += [pl.BlockSpec((tm, LANES), lambda i: (i % tiles_per_seq, 0))] * 2
        args += list(rope_tabs)
    out_shapes = [jax.ShapeDtypeStruct((n, POOL_WIDTH), F32)]
    out_specs = [pl.BlockSpec((tm, POOL_WIDTH), lambda i: (i, 0))]
    out_shapes += [jax.ShapeDtypeStruct((ATTN_WIDTH, n), BF16),
                   jax.ShapeDtypeStruct((n, ATTN_WIDTH), BF16),
                   jax.ShapeDtypeStruct((N_HEADS * V_EXT, n), BF16)]
    out_specs += [pl.BlockSpec((ATTN_WIDTH, tm), lambda i: (0, i)),
                  pl.BlockSpec((tm, ATTN_WIDTH), lambda i: (i, 0)),
                  pl.BlockSpec((N_HEADS * V_EXT, tm), lambda i: (0, i))]
    if emit_f32_kv:
        def seq_map(i):
            return (i // tiles_per_seq, 0, i % tiles_per_seq)

        out_shapes += [jax.ShapeDtypeStruct((n // seq, ATTN_WIDTH, seq), F32),
                       jax.ShapeDtypeStruct((n // seq, N_HEADS, seq, V_DIM), F32)]
        out_specs += [pl.BlockSpec((1, ATTN_WIDTH, tm), seq_map),
                      pl.BlockSpec((1, N_HEADS, tm, V_DIM),
                                   lambda i: (i // tiles_per_seq, 0, i % tiles_per_seq, 0))]
    if cast_w is not None:
        assert cast_w.shape[0] == n // tm, "one expert matrix per grid step"
    _with_side_cast(cast_w, lambda i: (i, 0, 0), in_specs, args, out_specs, out_shapes)
    return pl.pallas_call(
        functools.partial(_pre_kernel, rope=rope, emit_f32_kv=emit_f32_kv,
                          side_cast=cast_w is not None),
        grid=(n // tm,),
        in_specs=in_specs,
        out_specs=out_specs,
        out_shape=out_shapes,
        compiler_params=_cparams(("arbitrary",), VMEM_LIMIT),
        name="pre_rope" if rope else "pre",
    )(*args)


def _attn_kernel(*refs, has_cache, heads, tq, tw, tk, seq, cache_len, side_cast):
    step = ((pl.program_id(0) * pl.num_programs(1) + pl.program_id(1)) * pl.num_programs(2)
            + pl.program_id(2))
    refs = _side_cast(refs, (7 if has_cache else 5) + 1, side_cast > 0, step, side_cast)
    if has_cache:
        lam_ref, g_ref, q_ref, kc_ref, vc_ref, k_ref, v_ref, o_ref = refs
    else:
        lam_ref, g_ref, q_ref, k_ref, v_ref, o_ref = refs

    lv = lam_ref[...]
    lam = (jnp.exp(jnp.sum(lv[0:1] * lv[1:2], axis=1, keepdims=True))
           - jnp.exp(jnp.sum(lv[2:3] * lv[3:4], axis=1, keepdims=True)) + LAMBDA_INIT)

    row = lax.broadcasted_iota(jnp.int32, (LANES, tq), 0)
    zero = jnp.zeros((LANES, tq), BF16)

    def sub_queries(h):
        qt = q_ref[h * LANES:(h + 1) * LANES, :]
        return (jnp.where(row < HEAD_DIM, qt, zero), jnp.where(row >= HEAD_DIM, qt, zero))

    def scores(kb, q_one):
        return jnp.dot(kb, q_one, preferred_element_type=F32)

    def softmax_step(s, m):
        m_new = jnp.maximum(m, jnp.max(s, axis=0, keepdims=True))
        return m_new, jnp.exp2(m - m_new), jnp.exp2(s - m_new).astype(BF16)

    def accumulate(vb, p, alpha, acc):
        return alpha * acc + jnp.dot(vb, p, preferred_element_type=F32)

    chunks = []
    if has_cache:
        chunks += [(kc_ref, vc_ref, j) for j in range(cache_len // tk)]
    chunks += [(k_ref, v_ref, j) for j in range(seq // tk)]

    def keys(c, h):
        kr, _, j = chunks[c]
        return kr[j * tk:(j + 1) * tk, h * LANES:(h + 1) * LANES]

    def values(c, h):
        _, vr, j = chunks[c]
        return vr[h * V_EXT:(h + 1) * V_EXT, j * tk:(j + 1) * tk]

    chains = []
    for h in range(heads):
        q_sub = sub_queries(h)
        chains += [(h, q_sub[sub][:, w * tw:(w + 1) * tw])
                   for w in range(tq // tw) for sub in range(2)]
    per_head = len(chains) // heads
    items = [(c, ch) for h in range(heads) for c in range(len(chunks))
             for ch in range(h * per_head, (h + 1) * per_head)]
    m = [jnp.full((1, tw), -jnp.inf, F32)] * len(chains)
    acc = [jnp.zeros((V_EXT, tw), F32)] * len(chains)
    queue = [scores(keys(c, chains[ch][0]), chains[ch][1]) for c, ch in items[:SCORES_AHEAD]]
    for i, (c, ch) in enumerate(items):
        s_cur = queue.pop(0)
        if i + SCORES_AHEAD < len(items):
            nc, nch = items[i + SCORES_AHEAD]
            queue.append(scores(keys(nc, chains[nch][0]), chains[nch][1]))
        m[ch], alpha, p = softmax_step(s_cur, m[ch])
        acc[ch] = accumulate(values(c, chains[ch][0]), p, alpha, acc[ch])
    for h in range(heads):
        outs = []
        for w in range(tq // tw):
            a1, a2 = acc[h * per_head + 2 * w], acc[h * per_head + 2 * w + 1]
            outs.append(a1[:V_DIM] / a1[V_DIM:V_DIM + 1]
                        - lam * (a2[:V_DIM] / a2[V_DIM:V_DIM + 1]))
        o = outs[0] if len(outs) == 1 else jnp.concatenate(outs, axis=1)
        y = (o * lax.rsqrt(jnp.mean(o * o, axis=0, keepdims=True) + EPS) * g_ref[...]
             * (1.0 - LAMBDA_INIT))
        o_ref[:, h * LANES:(h + 1) * LANES] = y.T.astype(BF16)


def _attn_call(lam4, subln_col, qt, k, vt, cache, *, batch, seq, heads, tq, tw, tk, cast_w=None):
    has_cache = cache is not None
    nq = seq // tq
    in_specs = [
        pl.BlockSpec((4, HEAD_DIM), lambda b, h, i: (0, 0)),
        pl.BlockSpec((V_DIM, 1), lambda b, h, i: (0, 0)),
        pl.BlockSpec((heads * LANES, tq), lambda b, h, i: (h, b * nq + i)),
    ]
    args = [lam4, subln_col, qt]
    cache_len = 0
    if has_cache:
        kc, vct = cache
        cache_len = kc.shape[0] // batch
        in_specs += [pl.BlockSpec((cache_len, heads * LANES), lambda b, h, i: (b, h)),
                     pl.BlockSpec((heads * V_EXT, cache_len), lambda b, h, i: (h, b))]
        args += [kc, vct]
    in_specs += [pl.BlockSpec((seq, heads * LANES), lambda b, h, i: (b, h)),
                 pl.BlockSpec((heads * V_EXT, seq), lambda b, h, i: (h, b))]
    args += [k, vt]
    out_specs = [pl.BlockSpec((tq, heads * LANES), lambda b, h, i: (b * nq + i, h))]
    out_shapes = [jax.ShapeDtypeStruct((batch * seq, ATTN_WIDTH), BF16)]
    grid = (batch, N_HEADS // heads, nq)
    steps_per_block = 0
    if cast_w is not None:
        steps_per_block, rem = divmod(grid[0] * grid[1] * grid[2], cast_w.shape[0])
        assert rem == 0 and steps_per_block > 0, "whole number of grid steps per expert matrix"
    _with_side_cast(
        cast_w, lambda b, h, i: (((b * grid[1] + h) * grid[2] + i) // steps_per_block, 0, 0),
        in_specs, args, out_specs, out_shapes)
    return pl.pallas_call(
        functools.partial(_attn_kernel, has_cache=has_cache, heads=heads, tq=tq, tw=tw, tk=tk,
                          seq=seq, cache_len=cache_len, side_cast=steps_per_block),
        grid=grid,
        in_specs=in_specs,
        out_specs=out_specs,
        out_shape=out_shapes,
        compiler_params=_cparams(("arbitrary", "arbitrary", "arbitrary"), VMEM_LIMIT),
        name="attn_cache" if has_cache else "attn",
    )(*args)


def _post_kernel(*refs, tm, seq, side_cast):
    (x_ref, a_ref, p_ref, pp_ref, pn_ref, mod_ref, wout_ref, wbd_ref, ps_ref, g2_ref, wr_ref,
     x1_ref, h2_ref, aff_ref, affc_ref) = _side_cast(refs, 12, side_cast)
    i = pl.program_id(0)

    def centred_mean_minus_self(rows, prev, nxt, t0):
        r = rows.shape[0]
        ext = jnp.concatenate([prev, rows, nxt], axis=0)
        n_ext = r + 2 * HALO
        s2 = ext + pltpu.roll(ext, 1, 0)
        s4 = pltpu.roll(s2, 1, 0) + pltpu.roll(s2, n_ext - 1, 0)
        s8 = pltpu.roll(s4, 2, 0) + pltpu.roll(s4, n_ext - 2, 0)
        s16 = pltpu.roll(s8, 4, 0) + pltpu.roll(s8, n_ext - 4, 0)
        lane = lax.broadcasted_iota(jnp.int32, (r, POOL_WIDTH), 1)
        grp = lane // (POOL_WIDTH // 4)
        win = jnp.where(grp == 0, s2[HALO:HALO + r],
                        jnp.where(grp == 1, s4[HALO:HALO + r],
                                  jnp.where(grp == 2, s8[HALO:HALO + r], s16[HALO:HALO + r])))
        t = t0 + lax.broadcasted_iota(jnp.int32, (r, POOL_WIDTH), 0)
        left = jnp.where(grp == 0, 1, jnp.where(grp == 1, 2, jnp.where(grp == 2, 4, 8)))
        lo = jnp.maximum(t - left, 0)
        hi = jnp.minimum(t + left - 1, seq - 1) + 1
        return win / (hi - lo).astype(F32) - rows

    if tm <= seq:
        tiles_per_seq = seq // tm
        ti = i % tiles_per_seq
        pooled = centred_mean_minus_self(
            p_ref[...], jnp.where(ti == 0, 0.0, pp_ref[...]),
            jnp.where(ti == tiles_per_seq - 1, 0.0, pn_ref[...]), ti * tm)
    else:
        halo = jnp.zeros((HALO, POOL_WIDTH), F32)
        pooled = jnp.concatenate(
            [centred_mean_minus_self(p_ref[j * seq:(j + 1) * seq, :], halo, halo, 0)
             for j in range(tm // seq)], axis=0)
    pool = jnp.dot(pooled.astype(BF16), wbd_ref[...], preferred_element_type=F32) * ps_ref[...]

    cat = jnp.concatenate([pool.astype(BF16), a_ref[...]], axis=1)
    mix = jnp.dot(cat, wout_ref[...], preferred_element_type=F32)
    mod = mod_ref[0]
    gate1 = mod[:, 0:D_MODEL]
    shift2 = mod[:, D_MODEL:2 * D_MODEL]
    scale2 = mod[:, 2 * D_MODEL:3 * D_MODEL]
    x1 = x_ref[...] + gate1 * mix
    x1_ref[...] = x1
    ms = jnp.mean(x1 * x1, axis=1, keepdims=True)
    h2 = x1 * lax.rsqrt(ms + EPS) * g2_ref[...] * (1.0 + scale2) + shift2
    for s in range(ROW_TILES):
        h2_ref[pl.ds(s, tm, stride=ROW_TILES), :] = h2[:, s * LANES:(s + 1) * LANES]

    logits = lax.dot_general(wr_ref[...], h2, (((1,), (1,)), ((), ())),
                             precision=lax.Precision.HIGHEST, preferred_element_type=F32)
    e = jnp.exp(logits - jnp.max(logits, axis=0, keepdims=True))
    aff = e / jnp.sum(e, axis=0, keepdims=True)
    aff_ref[...] = aff
    for c in range(tm // LANES):
        affc_ref[c * N_EXPERTS:(c + 1) * N_EXPERTS, :] = aff[:, c * LANES:(c + 1) * LANES]


def _post_call(x2d, attn, p, mod3, wout_bf, wbd_bf, pool_scale, g2, wr_t, *, seq, tm, mod_base,
               mod_stride, cast_w=None):
    n = x2d.shape[0]
    assert cast_w is None or cast_w.shape[0] == n // tm, "one expert matrix per grid step"
    assert tm <= seq or mod_stride == 0, "a tile spanning sequences needs one modulation row"
    halo_per_tile = tm // HALO
    n_halo = n // HALO

    def mod_map(i):
        return (mod_base + mod_stride * (i * tm // seq), 0, 0)

    in_specs = [
            pl.BlockSpec((tm, D_MODEL), lambda i: (i, 0)),
            pl.BlockSpec((tm, ATTN_WIDTH), lambda i: (i, 0)),
            pl.BlockSpec((tm, POOL_WIDTH), lambda i: (i, 0)),
            pl.BlockSpec((HALO, POOL_WIDTH), lambda i: (jnp.maximum(i * halo_per_tile - 1, 0), 0)),
            pl.BlockSpec((HALO, POOL_WIDTH),
                         lambda i: (jnp.minimum((i + 1) * halo_per_tile, n_halo - 1), 0)),
            pl.BlockSpec((1, 1, 3 * D_MODEL), lambda i: mod_map(i)[:2] + (0,)),
            pl.BlockSpec((D_MODEL, D_MODEL), lambda i: (0, 0)),
            pl.BlockSpec((POOL_WIDTH, POOL_WIDTH), lambda i: (0, 0)),
            pl.BlockSpec((1, POOL_WIDTH), lambda i: (0, 0)),
            pl.BlockSpec((1, D_MODEL), lambda i: (0, 0)),
            pl.BlockSpec((N_EXPERTS, D_MODEL), lambda i: (0, 0)),
    ]
    out_specs = [
            pl.BlockSpec((tm, D_MODEL), lambda i: (i, 0)),
            pl.BlockSpec((tm * ROW_TILES, LANES), lambda i: (i, 0)),
            pl.BlockSpec((N_EXPERTS, tm), lambda i: (0, i)),
            pl.BlockSpec((tm // LANES * N_EXPERTS, LANES), lambda i: (i, 0)),
    ]
    out_shapes = [
            jax.ShapeDtypeStruct((n, D_MODEL), F32),
            jax.ShapeDtypeStruct((n * ROW_TILES, LANES), F32),
            jax.ShapeDtypeStruct((N_EXPERTS, n), F32),
            jax.ShapeDtypeStruct((n // LANES * N_EXPERTS, LANES), F32),
    ]
    args = [x2d, attn, p, p, p, mod3, wout_bf, wbd_bf, pool_scale, g2, wr_t]
    _with_side_cast(cast_w, lambda i: (i, 0, 0), in_specs, args, out_specs, out_shapes)
    return pl.pallas_call(
        functools.partial(_post_kernel, tm=tm, seq=seq, side_cast=cast_w is not None),
        grid=(n // tm,),
        in_specs=in_specs,
        out_specs=out_specs,
        out_shape=out_shapes,
        compiler_params=_cparams(("arbitrary",), VMEM_LIMIT),
        name="post",
    )(*args)


def _select_kernel(a_ref, ac_ref, idx_ref, gate_ref, *, n, cap):
    nc = n // LANES
    a = a_ref[...]
    thr = jnp.zeros((N_EXPERTS, 1), jnp.int32)
    for bit in range(30, -1, -1):
        cand = thr | (1 << bit)
        cnt = jnp.sum(jnp.where(a >= pltpu.bitcast(cand, F32), 1.0, 0.0), axis=1, keepdims=True)
        thr = jnp.where(cnt >= cap, cand, thr)
    thr_all = pltpu.bitcast(thr, F32)
    need_all = cap - jnp.sum(jnp.where(a > thr_all, 1.0, 0.0), axis=1, keepdims=True)

    r = lax.broadcasted_iota(jnp.int32, (LANES, LANES), 0)
    c = lax.broadcasted_iota(jnp.int32, (LANES, LANES), 1)
    upper = jnp.where(r <= c, 1.0, 0.0).astype(BF16)
    lower = jnp.where(c < r, 1.0, 0.0).astype(BF16)
    row_valid = r < nc
    chunk_col = lax.broadcasted_iota(jnp.int32, (LANES, 1), 0).astype(F32)
    slot = lax.broadcasted_iota(jnp.int32, (1, cap), 1).astype(F32)

    def lane_counts(mask):
        local = jnp.dot(mask.astype(BF16), upper, preferred_element_type=F32)
        total = jnp.broadcast_to(local[:, LANES - 1:LANES], (LANES, LANES))
        before = jnp.dot(lower, total.astype(BF16), preferred_element_type=F32)
        return local, total, before

    for e in range(N_EXPERTS):
        av = ac_ref[pl.ds(e, nc, stride=N_EXPERTS), :]
        if nc < LANES:
            av = jnp.concatenate([av, jnp.zeros((LANES - nc, LANES), F32)], axis=0)
        thr_e = thr_all[e:e + 1, :]
        above = jnp.where(row_valid & (av > thr_e), 1.0, 0.0)
        tied = jnp.where(row_valid & (av == thr_e), 1.0, 0.0)
        t_local, _, t_before = lane_counts(tied)
        sel = above + tied * jnp.where(t_local + t_before <= need_all[e:e + 1, :], 1.0, 0.0)
        s_local, s_total, s_before = lane_counts(sel)
        rank = jnp.where(sel > 0.0, s_local, 0.0)

        start = s_before[:, 0:1]
        stop = start + s_total[:, 0:1]
        onehot = jnp.where((slot >= start) & (slot < stop), 1.0, 0.0)
        chunk_of_slot = jnp.sum(onehot * chunk_col, axis=0, keepdims=True)
        start_of_slot = jnp.sum(onehot * start, axis=0, keepdims=True)

        at = av.T
        hi = at.astype(BF16)
        rest = at - hi.astype(F32)
        mid = rest.astype(BF16)
        lo = (rest - mid.astype(F32)).astype(BF16)
        lhs = jnp.concatenate([rank.T.astype(BF16), hi, mid, lo], axis=0)
        picked = jnp.dot(lhs, onehot.astype(BF16), preferred_element_type=F32)
        rank_p = picked[0:LANES]
        aff_p = picked[LANES:2 * LANES] + picked[2 * LANES:3 * LANES] + picked[3 * LANES:]
        hit = rank_p == (slot - start_of_slot + 1.0)
        lane_of_slot = jnp.sum(jnp.where(hit, chunk_col, 0.0), axis=0, keepdims=True)
        idx_ref[e:e + 1, :] = (chunk_of_slot * LANES + lane_of_slot).astype(jnp.int32)
        gate_ref[e:e + 1, :] = jnp.sum(jnp.where(hit, aff_p, 0.0), axis=0, keepdims=True)


def _select_call(aff_t, aff_c, *, cap):
    n = aff_t.shape[1]
    assert n % LANES == 0 and n // LANES <= LANES and cap % LANES == 0
    return pl.pallas_call(
        functools.partial(_select_kernel, n=n, cap=cap),
        grid=(1,),
        in_specs=[pl.BlockSpec((N_EXPERTS, n), lambda i: (0, 0)),
                  pl.BlockSpec(aff_c.shape, lambda i: (0, 0))],
        out_specs=[pl.BlockSpec((N_EXPERTS, cap), lambda i: (0, 0))] * 2,
        out_shape=[jax.ShapeDtypeStruct((N_EXPERTS, cap), jnp.int32),
                   jax.ShapeDtypeStruct((N_EXPERTS, cap), F32)],
        compiler_params=_cparams(("arbitrary",), VMEM_LIMIT),
        name="select",
    )(aff_t, aff_c)


def _moe_kernel(idx_ref, gate_ref, wg_ref, wu_ref, wd_ref, x_hbm, out_hbm,
                gbuf, ybuf, acc_ref, gsem, osem, *, tm, n_tiles_total):
    e = pl.program_id(0)
    t = pl.program_id(1)
    nt = pl.num_programs(1)
    step = e * nt + t
    group = 8
    last = n_tiles_total - 1
    rows = tm * ROW_TILES

    def gather_start(tile_step, k, dst_slot):
        tok = idx_ref[tile_step * tm + k]
        pltpu.make_async_copy(
            x_hbm.at[pl.ds(pl.multiple_of(tok * ROW_TILES, ROW_TILES), ROW_TILES), :],
            gbuf.at[dst_slot, pl.ds(k * ROW_TILES, ROW_TILES), :], gsem.at[dst_slot]).start()

    def gather_wait(dst_slot):
        pltpu.make_async_copy(x_hbm.at[pl.ds(0, rows), :], gbuf.at[dst_slot],
                              gsem.at[dst_slot]).wait()

    def scatter_add(tile_step, src_slot, k0):
        pending = []
        for r in range(group):
            k = k0 + r
            tok = idx_ref[tile_step * tm + k]
            off = pl.multiple_of(tok * ROW_TILES, ROW_TILES)
            src = pl.multiple_of(k * ROW_TILES, ROW_TILES)
            pending.append((off, acc_ref[pl.ds(off, ROW_TILES), :]
                            + ybuf[src_slot, pl.ds(src, ROW_TILES), :]))
        for off, val in pending:
            acc_ref[pl.ds(off, ROW_TILES), :] = val

    @pl.when(step == 0)
    def _():
        acc_ref[...] = jnp.zeros_like(acc_ref)
        ybuf[...] = jnp.zeros_like(ybuf)

        def body(k, _):
            gather_start(0, k, 0)
            return 0
        lax.fori_loop(0, tm, body, 0)

    def tile_body(slot):
        nxt = jnp.minimum(step + 1, last)
        prev = jnp.maximum(step - 1, 0)
        for k in range(tm):
            gather_start(nxt, k, 1 - slot)
        gather_wait(slot)
        xe = jnp.concatenate(
            [gbuf[slot, pl.ds(s, tm, stride=ROW_TILES), :] for s in range(ROW_TILES)],
            axis=1).astype(BF16)
        g = jnp.dot(xe, wg_ref[...], preferred_element_type=F32)
        for k0 in range(0, tm // 2, group):
            scatter_add(prev, 1 - slot, k0)
        u = jnp.dot(xe, wu_ref[...], preferred_element_type=F32)
        for k0 in range(tm // 2, tm, group):
            scatter_add(prev, 1 - slot, k0)
        hid = (g * jax.nn.sigmoid(g) * u).astype(BF16)
        y = jnp.dot(hid, wd_ref[...], preferred_element_type=F32)
        gate =jnp.broadcast_to(gate_ref[...], (LANES, tm)).T
        for s in range(ROW_TILES):
            ybuf[slot, pl.ds(s, tm, stride=ROW_TILES), :] = y[:, s * LANES:(s + 1) * LANES] * gate

        @pl.when(step == last)
        def _():
            gather_wait(1 - slot)

            def body(kk, _):
                scatter_add(step, slot, kk * group)
                return 0
            lax.fori_loop(0, tm // group, body, 0)
            cp = pltpu.make_async_copy(acc_ref, out_hbm, osem)
            cp.start()
            cp.wait()

    for parity in range(2):
        pl.when(step % 2 == parity)(functools.partial(tile_body, parity))


def _moe_call(idx_flat, gates3, wg_bf, wu_bf, wd_bf, h2, *, cap, tm):
    n_rows = h2.shape[0]
    nt = cap // tm
    grid_spec = pltpu.PrefetchScalarGridSpec(
        num_scalar_prefetch=1,
        grid=(N_EXPERTS, nt),
        in_specs=[
            pl.BlockSpec((None, 1, tm), lambda e, t, idx: (e * nt + t, 0, 0)),
            pl.BlockSpec((None, D_MODEL, D_MODEL), lambda e, t, idx: (e, 0, 0)),
            pl.BlockSpec((None, D_MODEL, D_MODEL), lambda e, t, idx: (e, 0, 0)),
            pl.BlockSpec((None, D_MODEL, D_MODEL), lambda e, t, idx: (e, 0, 0)),
            pl.BlockSpec(memory_space=pl.ANY),
        ],
        out_specs=pl.BlockSpec(memory_space=pl.ANY),
        scratch_shapes=[
            pltpu.VMEM((2, tm * ROW_TILES, LANES), F32),
            pltpu.VMEM((2, tm * ROW_TILES, LANES), F32),
            pltpu.VMEM((n_rows, LANES), F32),
            pltpu.SemaphoreType.DMA((2,)),
            pltpu.SemaphoreType.DMA(()),
        ],
    )
    return pl.pallas_call(
        functools.partial(_moe_kernel, tm=tm, n_tiles_total=N_EXPERTS * nt),
        grid_spec=grid_spec,
        out_shape=jax.ShapeDtypeStruct((n_rows, LANES), F32),
        compiler_params=_cparams(("arbitrary", "arbitrary"), VMEM_LIMIT),
        name="moe",
    )(idx_flat, gates3, wg_bf, wu_bf, wd_bf, h2)


def _final_kernel(x1_ref, moe_ref, mod_ref, o_ref, *, tm):
    moe = jnp.concatenate(
        [moe_ref[pl.ds(s, tm, stride=ROW_TILES), :] for s in range(ROW_TILES)], axis=1)
    o_ref[...] = x1_ref[...] + mod_ref[0] * moe


def _final_call(x1, moe_tiles, mod3, *, seq, tm, mod_base, mod_stride):
    n = x1.shape[0]
    assert tm <= seq or mod_stride == 0, "a tile spanning sequences needs one modulation row"
    gate2_block = 5

    def mod_map(i):
        return (mod_base + mod_stride * (i * tm // seq), 0, gate2_block)

    return pl.pallas_call(
        functools.partial(_final_kernel, tm=tm),
        grid=(n // tm,),
        in_specs=[pl.BlockSpec((tm, D_MODEL), lambda i: (i, 0)),
                  pl.BlockSpec((tm * ROW_TILES, LANES), lambda i: (i, 0)),
                  pl.BlockSpec((1, 1, D_MODEL), mod_map)],
        out_specs=pl.BlockSpec((tm, D_MODEL), lambda i: (i, 0)),
        out_shape=jax.ShapeDtypeStruct((n, D_MODEL), F32),
        compiler_params=_cparams(("arbitrary",)),
        name="final",
    )(x1, moe_tiles, mod3)


def _rope_tables(seq):
    t = np.arange(seq)
    row, col = t // GRID_W, t % GRID_W
    half = HEAD_DIM // 2
    freqs = 1.0 / (ROPE_BASE ** (np.arange(0, half, 2) / half))
    ang_r = row[:, None] * freqs[None, :]
    ang_c = col[:, None] * freqs[None, :]
    ang = np.concatenate([ang_r, ang_r, ang_c, ang_c], axis=-1)
    cos = np.tile(np.cos(ang), (1, LANES // HEAD_DIM))
    sin = np.tile(np.sin(ang), (1, LANES // HEAD_DIM))
    sign = np.where((np.arange(LANES) % (HEAD_DIM // 2)) < (HEAD_DIM // 4), -1.0, 1.0)
    return jnp.asarray(cos, F32), jnp.asarray(sin * sign[None, :], F32)


def _segment_matrix():
    seg = np.arange(SEG_BLOCK) // HEAD_DIM
    return jnp.asarray((seg[:, None] == seg[None, :]) / HEAD_DIM, BF16)


def _token_mixing(x, mod3, w, cache, *, mod_base, mod_stride, tm, tm_post, heads, tq, tk,
                  cast_pre=None, cast_attn=None, cast_post=None):
    batch, seq, _ = x.shape
    n = batch * seq
    x2d = x.reshape(n, D_MODEL)
    rope_tabs = _rope_tables(seq) if cache is not None else None
    casts = {}
    pre = list(_pre_call(x2d, mod3[:, :, :2 * D_MODEL], w["g1"], w["win"], w["seg"], w["qg"],
                         w["kg"], rope_tabs, seq=seq, tm=tm, mod_base=mod_base,
                         mod_stride=mod_stride, emit_f32_kv=cache is None, cast_w=cast_pre))
    if cast_pre is not None:
        casts["pre"] = pre.pop()
    p, q, k, v = pre[:4]
    attn = list(_attn_call(w["lam4"], w["subln"], q, k, v, cache, batch=batch, seq=seq,
                           heads=heads, tq=tq, tw=256, tk=tk, cast_w=cast_attn))
    if cast_attn is not None:
        casts["attn"] = attn.pop()
    post = list(_post_call(x2d, attn[0], p, mod3[:, :, 2 * D_MODEL:5 * D_MODEL], w["wout"],
                           w["wbd"], w["pool_scale"], w["g2"], w["wr_t"], seq=seq, tm=tm_post,
                           mod_base=mod_base, mod_stride=mod_stride, cast_w=cast_post))
    if cast_post is not None:
        casts["post"] = post.pop()
    x1, h2, aff_t, aff_c = post
    cap = CAPACITY_FACTOR * n // N_EXPERTS
    idx, gates = _select_call(aff_t, aff_c, cap=cap)
    return {"x1": x1, "h2": h2, "idx": idx, "gates": gates, "cap": cap, "kv": pre[4:],
            "casts": casts, "shape": (batch, seq), "tm_post": tm_post,
            "mod": (mod_base, mod_stride)}


def _channel_mixing(mixed, mod3, wg, wu, wd, *, moe_tm):
    batch, seq = mixed["shape"]
    cap = mixed["cap"]
    mod_base, mod_stride = mixed["mod"]
    moe_tiles = _moe_call(mixed["idx"].reshape(N_EXPERTS * cap),
                          mixed["gates"].reshape(-1, 1, moe_tm), wg, wu, wd, mixed["h2"],
                          cap=cap, tm=moe_tm)
    y = _final_call(mixed["x1"], moe_tiles, mod3, seq=seq, tm=mixed["tm_post"],
                    mod_base=mod_base, mod_stride=mod_stride)
    return y.reshape(batch, seq, D_MODEL)


def kernel(x_prompt, x_sample, cache_k, cache_v, c, c_ctx, norm1_g, norm2_g, w_ada, b_ada, w_in,
           q_norm_g, k_norm_g, lambda_q1, lambda_k1, lambda_q2, lambda_k2, subln_g, w_pool,
           pool_scale, w_out, w_router, w_gate, w_up, w_down):
    assert w_ada.shape[0] == 1, "single-layer stack"
    batch, seq, _ = x_prompt.shape
    dec_batch, dec_seq, _ = x_sample.shape

    pad = SUBLANES - 1 - dec_batch
    cvec = jnp.concatenate([c_ctx[None, :], c, jnp.zeros((pad, D_MODEL), F32)], axis=0)
    mod = _ada_call(cvec, w_ada[0], b_ada[0])
    mod3 = mod.reshape(SUBLANES, 1, 6 * D_MODEL)

    n_groups = w_pool.shape[1]
    grp = POOL_WIDTH // n_groups
    eye = jnp.eye(n_groups, dtype=F32)
    wbd = (w_pool[0][:, :, None, :] * eye[:, None, :, None]).reshape(POOL_WIDTH, POOL_WIDTH)

    w = {
        "g1": norm1_g[0].reshape(1, D_MODEL),
        "g2": norm2_g[0].reshape(1, D_MODEL),
        "win": w_in[0].astype(BF16),
        "seg": _segment_matrix(),
        "qg": jnp.tile(q_norm_g[0], ATTN_WIDTH // HEAD_DIM).reshape(1, ATTN_WIDTH),
        "kg": jnp.tile(k_norm_g[0], ATTN_WIDTH // HEAD_DIM).reshape(1, ATTN_WIDTH),
        "lam4": jnp.stack([lambda_q1[0], lambda_k1[0], lambda_q2[0], lambda_k2[0]], axis=0),
        "subln": subln_g[0].reshape(V_DIM, 1),
        "wbd": wbd.astype(BF16),
        "pool_scale": pool_scale[0].reshape(1, POOL_WIDTH),
        "wout": w_out[0].astype(BF16),
        "wr_t": w_router[0].T,
    }

    ctx = _token_mixing(x_prompt, mod3, w, None, mod_base=0, mod_stride=0, tm=256, tm_post=512,
                        heads=N_HEADS, tq=256, tk=256)
    past = cache_k.shape[2]
    cv = cache_v[:, 0].reshape(dec_batch * past, N_HEADS, V_DIM).transpose(1, 2, 0).astype(BF16)
    cv = jnp.concatenate([cv, jnp.ones((N_HEADS, V_EXT - V_DIM, dec_batch * past), BF16)], axis=1)
    cache = (cache_k[:, 0].reshape(dec_batch * past, ATTN_WIDTH).astype(BF16),
             cv.reshape(N_HEADS * V_EXT, dec_batch * past))
    lat = _token_mixing(x_sample, mod3, w, cache, mod_base=1, mod_stride=1, tm=512, tm_post=512,
                        heads=1, tq=1024, tk=512, cast_pre=w_gate[0], cast_attn=w_down[0],
                        cast_post=w_up[0])
    wg, wu, wd = lat["casts"]["pre"], lat["casts"]["post"], lat["casts"]["attn"]
    yp = _channel_mixing(ctx, mod3, wg, wu, wd, moe_tm=256)
    ys = _channel_mixing(lat, mod3, wg, wu, wd, moe_tm=256)
    k_ctx, v_ctx = ctx["kv"]
    ctx_k =(k_ctx.reshape(batch, N_HEADS, 2, HEAD_DIM, seq).transpose(0, 4, 1, 2, 3)
             .reshape(batch, 1, seq, N_HEADS, 2, HEAD_DIM))
    ctx_v = v_ctx.transpose(0, 2, 1, 3).reshape(batch, 1, seq, N_HEADS, V_DIM)
    return yp, ys, ctx_k, ctx_v
```

```python
import functools
import math

import numpy as np
import jax
import jax.numpy as jnp
from jax import lax
from jax.experimental import pallas as pl
from jax.experimental.pallas import tpu as pltpu

F32 = jnp.float32
BF16 = jnp.bfloat16

D_MODEL = 1024
POOL_WIDTH = 256
ATTN_WIDTH = 768
N_HEADS = 6
HEAD_DIM = 64
V_DIM = 128
IN_WIDTH = POOL_WIDTH + 3 * ATTN_WIDTH
N_EXPERTS = 16
CAPACITY_FACTOR = 2
GRID_W = 64
ROPE_BASE = 10000.0
EPS = 1e-6
LAMBDA_INIT = 0.8 - 0.6 * math.exp(-0.3 * 0)
LOG2E = math.log2(math.e)
V_EXT = V_DIM + 16
PRE_ROW_BLOCK = 256
SELECT_GROUP = 8
SEG_BLOCK = 256
SCORES_AHEAD = 3

LANES = 128
SUBLANES = 8
ROW_TILES = D_MODEL // LANES
HALO = 16
VMEM_LIMIT = 56 * 1024 * 1024


def _cparams(sem, vmem=None):
    return pltpu.CompilerParams(dimension_semantics=sem, vmem_limit_bytes=vmem)


def _ada_kernel(c_ref, w_ref, b_ref, o_ref):
    c = c_ref[...]
    s = c * jax.nn.sigmoid(c)
    o_ref[...] = jnp.dot(s.astype(BF16), w_ref[...].astype(BF16),
                         preferred_element_type=F32) + b_ref[...]


def _ada_call(cvec, w_ada, b_ada):
    rows, d = cvec.shape
    n = w_ada.shape[1]
    bn = 1536
    return pl.pallas_call(
        _ada_kernel,
        grid=(n // bn,),
        in_specs=[pl.BlockSpec((rows, d), lambda j: (0, 0)),
                  pl.BlockSpec((d, bn), lambda j: (0, j)),
                  pl.BlockSpec((1, bn), lambda j: (0, j))],
        out_specs=pl.BlockSpec((rows, bn), lambda j: (0, j)),
        out_shape=jax.ShapeDtypeStruct((rows, n), F32),
        compiler_params=_cparams(("arbitrary",)),
        name="ada",
    )(cvec, w_ada, b_ada.reshape(1, n))


def _segment_mean_square(a, seg_ref):
    sq = (a * a).astype(BF16)
    seg = seg_ref[...]
    return jnp.concatenate(
        [jnp.dot(sq[:, j:j + SEG_BLOCK], seg, preferred_element_type=F32)
         for j in range(0, a.shape[1], SEG_BLOCK)], axis=1)


def _rope(a, cos, sin_signed, first_half):
    parts = []
    for h in range(a.shape[1] // LANES):
        blk = a[:, h * LANES:(h + 1) * LANES]
        fwd = pltpu.roll(blk, LANES - HEAD_DIM // 4, 1)
        bwd = pltpu.roll(blk, HEAD_DIM // 4, 1)
        parts.append(blk * cos + jnp.where(first_half, fwd, bwd) * sin_signed)
    return jnp.concatenate(parts, axis=1)


def _side_cast(refs, n_inputs, enabled, step=None, steps_per_block=1):
    if not enabled:
        return refs
    src_ref, dst_ref = refs[n_inputs - 1], refs[-1]

    def convert():
        dst_ref[...] = src_ref[...].astype(BF16)

    if steps_per_block == 1:
        convert()
    else:
        pl.when(step % steps_per_block == 0)(convert)
    return refs[:n_inputs - 1] + refs[n_inputs:-1]


def _with_side_cast(w, index_map, in_specs, args, out_specs, out_shapes):
    if w is None:
        return
    _, rows, cols = w.shape
    in_specs.append(pl.BlockSpec((None, rows, cols), index_map))
    args.append(w)
    out_specs.append(pl.BlockSpec((None, rows, cols), index_map))
    out_shapes.append(jax.ShapeDtypeStruct(w.shape, BF16))


def _pre_kernel(*refs, rope, emit_f32_kv, side_cast, row_blocks):
    refs = _side_cast(refs, 7 + (2 if rope else 0) + 1, side_cast)
    x_ref, mod_ref, g1_ref, win_ref, seg_ref, qg_ref, kg_ref = refs[:7]
    pos = 7
    if rope:
        cos_ref, sin_ref = refs[pos:pos + 2]
        pos += 2
    p_ref, q_ref, k_ref, v_ref = refs[pos:pos + 4]
    pos += 4
    if emit_f32_kv:
        kf_ref, vf_ref = refs[pos:pos + 2]

    mod = mod_ref[0]
    shift1 = mod[:, :D_MODEL]
    scale1 = mod[:, D_MODEL:2 * D_MODEL]
    tm = x_ref.shape[0]
    rb = tm // row_blocks

    def project(j):
        x = x_ref[j * rb:(j + 1) * rb, :]
        ms = jnp.mean(x * x, axis=1, keepdims=True)
        h = x * lax.rsqrt(ms + EPS) * g1_ref[...] * (1.0 + scale1) + shift1
        return jnp.dot(h.astype(BF16), win_ref[...], preferred_element_type=F32)

    def head_stats(z):
        qz = z[:, POOL_WIDTH:POOL_WIDTH + ATTN_WIDTH]
        kz = z[:, POOL_WIDTH + ATTN_WIDTH:POOL_WIDTH + 2 * ATTN_WIDTH]
        return _segment_mean_square(qz, seg_ref), _segment_mean_square(kz, seg_ref)

    def finish(j, z, stats):
        rows = slice(j * rb, (j + 1) * rb)
        p_ref[rows, :] = z[:, :POOL_WIDTH]
        qz = z[:, POOL_WIDTH:POOL_WIDTH + ATTN_WIDTH]
        kz = z[:, POOL_WIDTH + ATTN_WIDTH:POOL_WIDTH + 2 * ATTN_WIDTH]
        vz = z[:, POOL_WIDTH + 2 * ATTN_WIDTH:]
        qn = qz * lax.rsqrt(stats[0] + EPS) * qg_ref[...]
        kn = kz * lax.rsqrt(stats[1] + EPS) * kg_ref[...]
        if rope:
            cos = cos_ref[rows, :]
            sin_signed = sin_ref[rows, :]
            lane = lax.broadcasted_iota(jnp.int32, cos.shape, 1)
            first_half = (lane % (HEAD_DIM // 2)) < (HEAD_DIM // 4)
            qn = _rope(qn, cos, sin_signed, first_half)
            kn = _rope(kn, cos, sin_signed, first_half)
        q_ref[:, rows] = (qn * (LOG2E / math.sqrt(HEAD_DIM))).T.astype(BF16)
        k_ref[rows, :] = kn.astype(BF16)
        vt = vz.T
        ones = jnp.ones((V_EXT - V_DIM, rb), BF16)
        for h in range(N_HEADS):
            v_ref[h * V_EXT:h * V_EXT + V_DIM, rows] = vt[h * V_DIM:(h + 1) * V_DIM, :].astype(BF16)
            v_ref[h * V_EXT + V_DIM:(h + 1) * V_EXT, rows] = ones
        if emit_f32_kv:
            kf_ref[0, :, rows] = kn.T
            for h in range(N_HEADS):
                vf_ref[0, h, rows, :] = vz[:, h * V_DIM:(h + 1) * V_DIM]

    z_prev = project(0)
    for j in range(row_blocks):
        stats = head_stats(z_prev)
        z_next = project(j + 1) if j + 1 < row_blocks else None
        finish(j, z_prev, stats)
        z_prev = z_next


def _pre_call(x2d, mod3, g1, win_bf, seg, qg, kg, rope_tabs, *, seq, tm, mod_base, mod_stride,
              emit_f32_kv, cast_w=None, row_blocks=1):
    n = x2d.shape[0]
    rope = rope_tabs is not None
    tiles_per_seq = seq // tm

    def mod_map(i):
        return (mod_base + mod_stride * (i // tiles_per_seq), 0, 0)

    in_specs = [
        pl.BlockSpec((tm, D_MODEL), lambda i: (i, 0)),
        pl.BlockSpec((1, 1, 2 * D_MODEL), mod_map),
        pl.BlockSpec((1, D_MODEL), lambda i: (0, 0)),
        pl.BlockSpec((D_MODEL, IN_WIDTH), lambda i: (0, 0)),
        pl.BlockSpec((SEG_BLOCK, SEG_BLOCK), lambda i: (0, 0)),
        pl.BlockSpec((1, ATTN_WIDTH), lambda i: (0, 0)),
        pl.BlockSpec((1, ATTN_WIDTH), lambda i: (0, 0)),
    ]
    args = [x2d, mod3, g1, win_bf, seg, qg, kg]
    if rope:
        in_specs += [pl.BlockSpec((tm, LANES), lambda i: (i % tiles_per_seq, 0))] * 2
        args += list(rope_tabs)
    out_shapes = [jax.ShapeDtypeStruct((n, POOL_WIDTH), F32)]
    out_specs = [pl.BlockSpec((tm, POOL_WIDTH), lambda i: (i, 0))]
    out_shapes += [jax.ShapeDtypeStruct((ATTN_WIDTH, n), BF16),
                   jax.ShapeDtypeStruct((n, ATTN_WIDTH), BF16),
                   jax.ShapeDtypeStruct((N_HEADS * V_EXT, n), BF16)]
    out_specs += [pl.BlockSpec((ATTN_WIDTH, tm), lambda i: (0, i)),
                  pl.BlockSpec((tm, ATTN_WIDTH), lambda i: (i, 0)),
                  pl.BlockSpec((N_HEADS * V_EXT, tm), lambda i: (0, i))]
    if emit_f32_kv:
        def seq_map(i):
            return (i // tiles_per_seq, 0, i % tiles_per_seq)

        out_shapes += [jax.ShapeDtypeStruct((n // seq, ATTN_WIDTH, seq), F32),
                       jax.ShapeDtypeStruct((n // seq, N_HEADS, seq, V_DIM), F32)]
        out_specs += [pl.BlockSpec((1, ATTN_WIDTH, tm), seq_map),
                      pl.BlockSpec((1, N_HEADS, tm, V_DIM),
                                   lambda i: (i // tiles_per_seq, 0, i % tiles_per_seq, 0))]
    if cast_w is not None:
        assert cast_w.shape[0] == n // tm, "one expert matrix per grid step"
    _with_side_cast(cast_w, lambda i: (i, 0, 0), in_specs, args, out_specs, out_shapes)
    return pl.pallas_call(
        functools.partial(_pre_kernel, rope=rope, emit_f32_kv=emit_f32_kv,
                          side_cast=cast_w is not None, row_blocks=row_blocks),
        grid=(n // tm,),
        in_specs=in_specs,
        out_specs=out_specs,
        out_shape=out_shapes,
        compiler_params=_cparams(("arbitrary",), VMEM_LIMIT),
        name="pre_rope" if rope else "pre",
    )(*args)


def _attn_kernel(*refs, has_cache, heads, tq, tw, tk, seq, cache_len, side_cast):
    step = ((pl.program_id(0) * pl.num_programs(1) + pl.program_id(1)) * pl.num_programs(2)
            + pl.program_id(2))
    refs = _side_cast(refs, (7 if has_cache else 5) + 1, side_cast > 0, step, side_cast)
    if has_cache:
        lam_ref, g_ref, q_ref, kc_ref, vc_ref, k_ref, v_ref, o_ref = refs
    else:
        lam_ref, g_ref, q_ref, k_ref, v_ref, o_ref = refs

    lv = lam_ref[...]
    lam = (jnp.exp(jnp.sum(lv[0:1] * lv[1:2], axis=1, keepdims=True))
           - jnp.exp(jnp.sum(lv[2:3] * lv[3:4], axis=1, keepdims=True)) + LAMBDA_INIT)

    row = lax.broadcasted_iota(jnp.int32, (LANES, tq), 0)
    zero = jnp.zeros((LANES, tq), BF16)

    def sub_queries(h):
        qt = q_ref[h * LANES:(h + 1) * LANES, :]
        return (jnp.where(row < HEAD_DIM, qt, zero), jnp.where(row >= HEAD_DIM, qt, zero))

    def scores(kb, q_one):
        return jnp.dot(kb, q_one, preferred_element_type=F32)

    def softmax_step(s, m):
        m_new = jnp.maximum(m, jnp.max(s, axis=0, keepdims=True))
        return m_new, jnp.exp2(m - m_new), jnp.exp2(s - m_new).astype(BF16)

    def accumulate(vb, p, alpha, acc):
        return alpha * acc + jnp.dot(vb, p, preferred_element_type=F32)

    chunks = []
    if has_cache:
        chunks += [(kc_ref, vc_ref, j) for j in range(cache_len // tk)]
    chunks += [(k_ref, v_ref, j) for j in range(seq // tk)]

    def keys(c, h):
        kr, _, j = chunks[c]
        return kr[j * tk:(j + 1) * tk, h * LANES:(h + 1) * LANES]

    def values(c, h):
        _, vr, j = chunks[c]
        return vr[h * V_EXT:(h + 1) * V_EXT, j * tk:(j + 1) * tk]

    chains = []
    for h in range(heads):
        q_sub = sub_queries(h)
        chains += [(h, q_sub[sub][:, w * tw:(w + 1) * tw])
                   for w in range(tq // tw) for sub in range(2)]
    per_head = len(chains) // heads
    items = [(c, ch) for h in range(heads) for c in range(len(chunks))
             for ch in range(h * per_head, (h + 1) * per_head)]
    m = [jnp.full((1, tw), -jnp.inf, F32)] * len(chains)
    acc = [jnp.zeros((V_EXT, tw), F32)] * len(chains)
    queue = [scores(keys(c, chains[ch][0]), chains[ch][1]) for c, ch in items[:SCORES_AHEAD]]
    for i, (c, ch) in enumerate(items):
        s_cur = queue.pop(0)
        if i + SCORES_AHEAD < len(items):
            nc, nch = items[i + SCORES_AHEAD]
            queue.append(scores(keys(nc, chains[nch][0]), chains[nch][1]))
        m[ch], alpha, p = softmax_step(s_cur, m[ch])
        acc[ch] = accumulate(values(c, chains[ch][0]), p, alpha, acc[ch])
    for h in range(heads):
        outs = []
        for w in range(tq // tw):
            a1, a2 = acc[h * per_head + 2 * w], acc[h * per_head + 2 * w + 1]
            outs.append(a1[:V_DIM] / a1[V_DIM:V_DIM + 1]
                        - lam * (a2[:V_DIM] / a2[V_DIM:V_DIM + 1]))
        o = outs[0] if len(outs) == 1 else jnp.concatenate(outs, axis=1)
        y = (o * lax.rsqrt(jnp.mean(o * o, axis=0, keepdims=True) + EPS) * g_ref[...]
             * (1.0 - LAMBDA_INIT))
        o_ref[:, h * LANES:(h + 1) * LANES] = y.T.astype(BF16)


def _attn_call(lam4, subln_col, qt, k, vt, cache, *, batch, seq, heads, tq, tw, tk, cast_w=None):
    has_cache = cache is not None
    nq = seq // tq
    in_specs = [
        pl.BlockSpec((4, HEAD_DIM), lambda b, h, i: (0, 0)),
        pl.BlockSpec((V_DIM, 1), lambda b, h, i: (0, 0)),
        pl.BlockSpec((heads * LANES, tq), lambda b, h, i: (h, b * nq + i)),
    ]
    args = [lam4, subln_col, qt]
    cache_len = 0
    if has_cache:
        kc, vct = cache
        cache_len = kc.shape[0] // batch
        in_specs += [pl.BlockSpec((cache_len, heads * LANES), lambda b, h, i: (b, h)),
                     pl.BlockSpec((heads * V_EXT, cache_len), lambda b, h, i: (h, b))]
        args += [kc, vct]
    in_specs += [pl.BlockSpec((seq, heads * LANES), lambda b, h, i: (b, h)),
                 pl.BlockSpec((heads * V_EXT, seq), lambda b, h, i: (h, b))]
    args += [k, vt]
    out_specs = [pl.BlockSpec((tq, heads * LANES), lambda b, h, i: (b * nq + i, h))]
    out_shapes = [jax.ShapeDtypeStruct((batch * seq, ATTN_WIDTH), BF16)]
    grid = (batch, N_HEADS // heads, nq)
    steps_per_block = 0
    if cast_w is not None:
        steps_per_block, rem = divmod(grid[0] * grid[1] * grid[2], cast_w.shape[0])
        assert rem == 0 and steps_per_block > 0, "whole number of grid steps per expert matrix"
    _with_side_cast(
        cast_w, lambda b, h, i: (((b * grid[1] + h) * grid[2] + i) // steps_per_block, 0, 0),
        in_specs, args, out_specs, out_shapes)
    return pl.pallas_call(
        functools.partial(_attn_kernel, has_cache=has_cache, heads=heads, tq=tq, tw=tw, tk=tk,
                          seq=seq, cache_len=cache_len, side_cast=steps_per_block),
        grid=grid,
        in_specs=in_specs,
        out_specs=out_specs,
        out_shape=out_shapes,
        compiler_params=_cparams(("arbitrary", "arbitrary", "arbitrary"), VMEM_LIMIT),
        name="attn_cache" if has_cache else "attn",
    )(*args)


def _post_kernel(*refs, tm, seq, side_cast):
    (x_ref, a_ref, p_ref, pp_ref, pn_ref, mod_ref, wout_ref, wbd_ref, ps_ref, g2_ref, wr_ref,
     x1_ref, h2_ref, aff_ref, affc_ref) = _side_cast(refs, 12, side_cast)
    i = pl.program_id(0)

    def centred_mean_minus_self(rows, prev, nxt, t0):
        r = rows.shape[0]
        ext = jnp.concatenate([prev, rows, nxt], axis=0)
        n_ext = r + 2 * HALO
        s2 = ext + pltpu.roll(ext, 1, 0)
        s4 = pltpu.roll(s2, 1, 0) + pltpu.roll(s2, n_ext - 1, 0)
        s8 = pltpu.roll(s4, 2, 0) + pltpu.roll(s4, n_ext - 2, 0)
        s16 = pltpu.roll(s8, 4, 0) + pltpu.roll(s8, n_ext - 4, 0)
        lane = lax.broadcasted_iota(jnp.int32, (r, POOL_WIDTH), 1)
        grp = lane // (POOL_WIDTH // 4)
        win = jnp.where(grp == 0, s2[HALO:HALO + r],
                        jnp.where(grp == 1, s4[HALO:HALO + r],
                                  jnp.where(grp == 2, s8[HALO:HALO + r], s16[HALO:HALO + r])))
        t = t0 + lax.broadcasted_iota(jnp.int32, (r, POOL_WIDTH), 0)
        left = jnp.where(grp == 0, 1, jnp.where(grp == 1, 2, jnp.where(grp == 2, 4, 8)))
        lo = jnp.maximum(t - left, 0)
        hi = jnp.minimum(t + left - 1, seq - 1) + 1
        return win / (hi - lo).astype(F32) - rows

    if tm <= seq:
        tiles_per_seq = seq // tm
        ti = i % tiles_per_seq
        pooled = centred_mean_minus_self(
            p_ref[...], jnp.where(ti == 0, 0.0, pp_ref[...]),
            jnp.where(ti == tiles_per_seq - 1, 0.0, pn_ref[...]), ti * tm)
    else:
        halo = jnp.zeros((HALO, POOL_WIDTH), F32)
        pooled = jnp.concatenate(
            [centred_mean_minus_self(p_ref[j * seq:(j + 1) * seq, :], halo, halo, 0)
             for j in range(tm // seq)], axis=0)
    pool = jnp.dot(pooled.astype(BF16), wbd_ref[...], preferred_element_type=F32) * ps_ref[...]

    cat = jnp.concatenate([pool.astype(BF16), a_ref[...]], axis=1)
    mix = jnp.dot(cat, wout_ref[...], preferred_element_type=F32)
    mod = mod_ref[0]
    gate1 = mod[:, 0:D_MODEL]
    shift2 = mod[:, D_MODEL:2 * D_MODEL]
    scale2 = mod[:, 2 * D_MODEL:3 * D_MODEL]
    x1 = x_ref[...] + gate1 * mix
    x1_ref[...] = x1
    ms = jnp.mean(x1 * x1, axis=1, keepdims=True)
    h2 = x1 * lax.rsqrt(ms + EPS) * g2_ref[...] * (1.0 + scale2) + shift2
    for s in range(ROW_TILES):
        h2_ref[pl.ds(s, tm, stride=ROW_TILES), :] = h2[:, s * LANES:(s + 1) * LANES]

    logits = lax.dot_general(wr_ref[...], h2, (((1,), (1,)), ((), ())),
                             precision=lax.Precision.HIGHEST, preferred_element_type=F32)
    e = jnp.exp(logits - jnp.max(logits, axis=0, keepdims=True))
    aff = e / jnp.sum(e, axis=0, keepdims=True)
    aff_ref[...] = aff
    for c in range(tm // LANES):
        affc_ref[c * N_EXPERTS:(c + 1) * N_EXPERTS, :] = aff[:, c * LANES:(c + 1) * LANES]


def _post_call(x2d, attn, p, mod3, wout_bf, wbd_bf, pool_scale, g2, wr_t, *, seq, tm, mod_base,
               mod_stride, cast_w=None):
    n = x2d.shape[0]
    assert cast_w is None or cast_w.shape[0] == n // tm, "one expert matrix per grid step"
    assert tm <= seq or mod_stride == 0, "a tile spanning sequences needs one modulation row"
    halo_per_tile = tm // HALO
    n_halo = n // HALO

    def mod_map(i):
        return (mod_base + mod_stride * (i * tm // seq), 0, 0)

    in_specs = [
            pl.BlockSpec((tm, D_MODEL), lambda i: (i, 0)),
            pl.BlockSpec((tm, ATTN_WIDTH), lambda i: (i, 0)),
            pl.BlockSpec((tm, POOL_WIDTH), lambda i: (i, 0)),
            pl.BlockSpec((HALO, POOL_WIDTH), lambda i: (jnp.maximum(i * halo_per_tile - 1, 0), 0)),
            pl.BlockSpec((HALO, POOL_WIDTH),
                         lambda i: (jnp.minimum((i + 1) * halo_per_tile, n_halo - 1), 0)),
            pl.BlockSpec((1, 1, 3 * D_MODEL), lambda i: mod_map(i)[:2] + (0,)),
            pl.BlockSpec((D_MODEL, D_MODEL), lambda i: (0, 0)),
            pl.BlockSpec((POOL_WIDTH, POOL_WIDTH), lambda i: (0, 0)),
            pl.BlockSpec((1, POOL_WIDTH), lambda i: (0, 0)),
            pl.BlockSpec((1, D_MODEL), lambda i: (0, 0)),
            pl.BlockSpec((N_EXPERTS, D_MODEL), lambda i: (0, 0)),
    ]
    out_specs = [
            pl.BlockSpec((tm, D_MODEL), lambda i: (i, 0)),
            pl.BlockSpec((tm * ROW_TILES, LANES), lambda i: (i, 0)),
            pl.BlockSpec((N_EXPERTS, tm), lambda i: (0, i)),
            pl.BlockSpec((tm // LANES * N_EXPERTS, LANES), lambda i: (i, 0)),
    ]
    out_shapes = [
            jax.ShapeDtypeStruct((n, D_MODEL), F32),
            jax.ShapeDtypeStruct((n * ROW_TILES, LANES), F32),
            jax.ShapeDtypeStruct((N_EXPERTS, n), F32),
            jax.ShapeDtypeStruct((n // LANES * N_EXPERTS, LANES), F32),
    ]
    args = [x2d, attn, p, p, p, mod3, wout_bf, wbd_bf, pool_scale, g2, wr_t]
    _with_side_cast(cast_w, lambda i: (i, 0, 0), in_specs, args, out_specs, out_shapes)
    return pl.pallas_call(
        functools.partial(_post_kernel, tm=tm, seq=seq, side_cast=cast_w is not None),
        grid=(n // tm,),
        in_specs=in_specs,
        out_specs=out_specs,
        out_shape=out_shapes,
        compiler_params=_cparams(("arbitrary",), VMEM_LIMIT),
        name="post",
    )(*args)


def _select_kernel(a_ref, ac_ref, idx_ref, gate_ref, *, n, cap):
    nc = n // LANES
    a = a_ref[...]
    thr = jnp.zeros((N_EXPERTS, 1), jnp.int32)
    for bit in range(30, -1, -1):
        cand = thr | (1 << bit)
        cnt = jnp.sum(jnp.where(a >= pltpu.bitcast(cand, F32), 1.0, 0.0), axis=1, keepdims=True)
        thr = jnp.where(cnt >= cap, cand, thr)
    thr_all = pltpu.bitcast(thr, F32)
    need_all = cap - jnp.sum(jnp.where(a > thr_all, 1.0, 0.0), axis=1, keepdims=True)

    r = lax.broadcasted_iota(jnp.int32, (LANES, LANES), 0)
    c = lax.broadcasted_iota(jnp.int32, (LANES, LANES), 1)
    upper = jnp.where(r <= c, 1.0, 0.0).astype(BF16)
    lower = jnp.where(c < r, 1.0, 0.0).astype(BF16)
    row_valid = r < nc
    chunk_col = lax.broadcasted_iota(jnp.int32, (LANES, 1), 0).astype(F32)
    slot = lax.broadcasted_iota(jnp.int32, (1, cap), 1).astype(F32)

    def lane_counts(mask):
        local = jnp.dot(mask.astype(BF16), upper, preferred_element_type=F32)
        total = jnp.broadcast_to(local[:, LANES - 1:LANES], (LANES, LANES))
        before = jnp.dot(lower, total.astype(BF16), preferred_element_type=F32)
        return local, total, before

    def ties(e):
        av = ac_ref[pl.ds(e, nc, stride=N_EXPERTS), :]
        if nc < LANES:
            av = jnp.concatenate([av, jnp.zeros((LANES - nc, LANES), F32)], axis=0)
        thr_e = thr_all[e:e + 1, :]
        above = jnp.where(row_valid & (av > thr_e), 1.0, 0.0)
        tied = jnp.where(row_valid & (av == thr_e), 1.0, 0.0)
        t_local, _, t_before = lane_counts(tied)
        return {"av": av, "above": above, "tied": tied, "tie_rank": t_local + t_before}

    def selection(e, st):
        sel = st["above"] + st["tied"] * jnp.where(st["tie_rank"] <= need_all[e:e + 1, :], 1.0, 0.0)
        s_local, s_total, s_before = lane_counts(sel)
        rank = jnp.where(sel > 0.0, s_local, 0.0)
        start = s_before[:, 0:1]
        return {"av": st["av"], "rank": rank, "start": start, "stop": start + s_total[:, 0:1]}

    def pick(e, st):
        start = st["start"]
        onehot = jnp.where((slot >= start) & (slot < st["stop"]), 1.0, 0.0)
        chunk_of_slot = jnp.sum(onehot * chunk_col, axis=0, keepdims=True)
        start_of_slot = jnp.sum(onehot * start, axis=0, keepdims=True)
        at = st["av"].T
        hi = at.astype(BF16)
        rest = at - hi.astype(F32)
        mid = rest.astype(BF16)
        lo = (rest - mid.astype(F32)).astype(BF16)
        lhs = jnp.concatenate([st["rank"].T.astype(BF16), hi, mid, lo], axis=0)
        picked = jnp.dot(lhs, onehot.astype(BF16), preferred_element_type=F32)
        return {"picked": picked, "chunk_of_slot": chunk_of_slot, "start_of_slot": start_of_slot}

    def emit(e, st):
        picked = st["picked"]
        rank_p = picked[0:LANES]
        aff_p = picked[LANES:2 * LANES] + picked[2 * LANES:3 * LANES] + picked[3 * LANES:]
        hit = rank_p == (slot - st["start_of_slot"] + 1.0)
        lane_of_slot = jnp.sum(jnp.where(hit, chunk_col, 0.0), axis=0, keepdims=True)
        idx_ref[e:e + 1, :] = (st["chunk_of_slot"] * LANES + lane_of_slot).astype(jnp.int32)
        gate_ref[e:e + 1, :] = jnp.sum(jnp.where(hit, aff_p, 0.0), axis=0, keepdims=True)

    for e0 in range(0, N_EXPERTS, SELECT_GROUP):
        group = range(e0, e0 + SELECT_GROUP)
        states = {e: ties(e) for e in group}
        states = {e: selection(e, states[e]) for e in group}
        states = {e: pick(e, states[e]) for e in group}
        for e in group:
            emit(e, states[e])


def _select_call(aff_t, aff_c, *, cap):
    n = aff_t.shape[1]
    assert n % LANES == 0 and n // LANES <= LANES and cap % LANES == 0
    return pl.pallas_call(
        functools.partial(_select_kernel, n=n, cap=cap),
        grid=(1,),
        in_specs=[pl.BlockSpec((N_EXPERTS, n), lambda i: (0, 0)),
                  pl.BlockSpec(aff_c.shape, lambda i: (0, 0))],
        out_specs=[pl.BlockSpec((N_EXPERTS, cap), lambda i: (0, 0))] * 2,
        out_shape=[jax.ShapeDtypeStruct((N_EXPERTS, cap), jnp.int32),
                   jax.ShapeDtypeStruct((N_EXPERTS, cap), F32)],
        compiler_params=_cparams(("arbitrary",), VMEM_LIMIT),
        name="select",
    )(aff_t, aff_c)


def _moe_kernel(idx_ref, gate_ref, wg_ref, wu_ref, wd_ref, x_hbm, out_hbm,
                gbuf, ybuf, acc_ref, gsem, osem, *, tm, n_tiles_total):
    e = pl.program_id(0)
    t = pl.program_id(1)
    nt = pl.num_programs(1)
    step = e * nt + t
    group = 8
    last = n_tiles_total - 1
    rows = tm * ROW_TILES

    def gather_start(tile_step, k, dst_slot):
        tok = idx_ref[tile_step * tm + k]
        pltpu.make_async_copy(
            x_hbm.at[pl.ds(pl.multiple_of(tok * ROW_TILES, ROW_TILES), ROW_TILES), :],
            gbuf.at[dst_slot, pl.ds(k * ROW_TILES, ROW_TILES), :], gsem.at[dst_slot]).start()

    def gather_wait(dst_slot):
        pltpu.make_async_copy(x_hbm.at[pl.ds(0, rows), :], gbuf.at[dst_slot],
                              gsem.at[dst_slot]).wait()

    def scatter_add(tile_step, src_slot, k0):
        pending = []
        for r in range(group):
            k = k0 + r
            tok = idx_ref[tile_step * tm + k]
            off = pl.multiple_of(tok * ROW_TILES, ROW_TILES)
            src = pl.multiple_of(k * ROW_TILES, ROW_TILES)
            pending.append((off, acc_ref[pl.ds(off, ROW_TILES), :]
                            + ybuf[src_slot, pl.ds(src, ROW_TILES), :]))
        for off, val in pending:
            acc_ref[pl.ds(off, ROW_TILES), :] = val

    @pl.when(step == 0)
    def _():
        acc_ref[...] = jnp.zeros_like(acc_ref)
        ybuf[...] = jnp.zeros_like(ybuf)

        def body(k, _):
            gather_start(0, k, 0)
            return 0
        lax.fori_loop(0, tm, body, 0)

    def tile_body(slot):
        nxt = jnp.minimum(step + 1, last)
        prev = jnp.maximum(step - 1, 0)
        for k in range(tm):
            gather_start(nxt, k, 1 - slot)
        gather_wait(slot)
        xe = jnp.concatenate(
            [gbuf[slot, pl.ds(s, tm, stride=ROW_TILES), :] for s in range(ROW_TILES)],
            axis=1).astype(BF16)
        g = jnp.dot(xe, wg_ref[...], preferred_element_type=F32)
        for k0 in range(0, tm // 2, group):
            scatter_add(prev, 1 - slot, k0)
        u = jnp.dot(xe, wu_ref[...], preferred_element_type=F32)
        for k0 in range(tm // 2, tm, group):
            scatter_add(prev, 1 - slot, k0)
        hid = (g * jax.nn.sigmoid(g) * u).astype(BF16)
        y = jnp.dot(hid, wd_ref[...], preferred_element_type=F32)
        gate =jnp.broadcast_to(gate_ref[...], (LANES, tm)).T
        for s in range(ROW_TILES):
            ybuf[slot, pl.ds(s, tm, stride=ROW_TILES), :] = y[:, s * LANES:(s + 1) * LANES] * gate

        @pl.when(step == last)
        def _():
            gather_wait(1 - slot)

            def body(kk, _):
                scatter_add(step, slot, kk * group)
                return 0
            lax.fori_loop(0, tm // group, body, 0)
            cp = pltpu.make_async_copy(acc_ref, out_hbm, osem)
            cp.start()
            cp.wait()

    for parity in range(2):
        pl.when(step % 2 == parity)(functools.partial(tile_body, parity))


def _moe_call(idx_flat, gates3, wg_bf, wu_bf, wd_bf, h2, *, cap, tm):
    n_rows = h2.shape[0]
    nt = cap // tm
    grid_spec = pltpu.PrefetchScalarGridSpec(
        num_scalar_prefetch=1,
        grid=(N_EXPERTS, nt),
        in_specs=[
            pl.BlockSpec((None, 1, tm), lambda e, t, idx: (e * nt + t, 0, 0)),
            pl.BlockSpec((None, D_MODEL, D_MODEL), lambda e, t, idx: (e, 0, 0)),
            pl.BlockSpec((None, D_MODEL, D_MODEL), lambda e, t, idx: (e, 0, 0)),
            pl.BlockSpec((None, D_MODEL, D_MODEL), lambda e, t, idx: (e, 0, 0)),
            pl.BlockSpec(memory_space=pl.ANY),
        ],
        out_specs=pl.BlockSpec(memory_space=pl.ANY),
        scratch_shapes=[
            pltpu.VMEM((2, tm * ROW_TILES, LANES), F32),
            pltpu.VMEM((2, tm * ROW_TILES, LANES), F32),
            pltpu.VMEM((n_rows, LANES), F32),
            pltpu.SemaphoreType.DMA((2,)),
            pltpu.SemaphoreType.DMA(()),
        ],
    )
    return pl.pallas_call(
        functools.partial(_moe_kernel, tm=tm, n_tiles_total=N_EXPERTS * nt),
        grid_spec=grid_spec,
        out_shape=jax.ShapeDtypeStruct((n_rows, LANES), F32),
        compiler_params=_cparams(("arbitrary", "arbitrary"), VMEM_LIMIT),
        name="moe",
    )(idx_flat, gates3, wg_bf, wu_bf, wd_bf, h2)


def _final_kernel(x1_ref, moe_ref, mod_ref, o_ref, *, tm):
    moe = jnp.concatenate(
        [moe_ref[pl.ds(s, tm, stride=ROW_TILES), :] for s in range(ROW_TILES)], axis=1)
    o_ref[...] = x1_ref[...] + mod_ref[0] * moe


def _final_call(x1, moe_tiles, mod3, *, seq, tm, mod_base, mod_stride):
    n = x1.shape[0]
    assert tm <= seq or mod_stride == 0, "a tile spanning sequences needs one modulation row"
    gate2_block = 5

    def mod_map(i):
        return (mod_base + mod_stride * (i * tm // seq), 0, gate2_block)

    return pl.pallas_call(
        functools.partial(_final_kernel, tm=tm),
        grid=(n // tm,),
        in_specs=[pl.BlockSpec((tm, D_MODEL), lambda i: (i, 0)),
                  pl.BlockSpec((tm * ROW_TILES, LANES), lambda i: (i, 0)),
                  pl.BlockSpec((1, 1, D_MODEL), mod_map)],
        out_specs=pl.BlockSpec((tm, D_MODEL), lambda i: (i, 0)),
        out_shape=jax.ShapeDtypeStruct((n, D_MODEL), F32),
        compiler_params=_cparams(("arbitrary",)),
        name="final",
    )(x1, moe_tiles, mod3)


def _rope_tables(seq):
    t = np.arange(seq)
    row, col = t // GRID_W, t % GRID_W
    half = HEAD_DIM // 2
    freqs = 1.0 / (ROPE_BASE ** (np.arange(0, half, 2) / half))
    ang_r = row[:, None] * freqs[None, :]
    ang_c = col[:, None] * freqs[None, :]
    ang = np.concatenate([ang_r, ang_r, ang_c, ang_c], axis=-1)
    cos = np.tile(np.cos(ang), (1, LANES // HEAD_DIM))
    sin = np.tile(np.sin(ang), (1, LANES // HEAD_DIM))
    sign = np.where((np.arange(LANES) % (HEAD_DIM // 2)) < (HEAD_DIM // 4), -1.0, 1.0)
    return jnp.asarray(cos, F32), jnp.asarray(sin * sign[None, :], F32)


def _segment_matrix():
    seg = np.arange(SEG_BLOCK) // HEAD_DIM
    return jnp.asarray((seg[:, None] == seg[None, :]) / HEAD_DIM, BF16)


def _token_mixing(x, mod3, w, cache, *, mod_base, mod_stride, tm, tm_post, heads, tq, tk,
                  cast_pre=None, cast_attn=None, cast_post=None):
    batch, seq, _ = x.shape
    n = batch * seq
    x2d = x.reshape(n, D_MODEL)
    rope_tabs = _rope_tables(seq) if cache is not None else None
    casts = {}
    pre = list(_pre_call(x2d, mod3[:, :, :2 * D_MODEL], w["g1"], w["win"], w["seg"], w["qg"],
                         w["kg"], rope_tabs, seq=seq, tm=tm, mod_base=mod_base,
                         mod_stride=mod_stride, emit_f32_kv=cache is None, cast_w=cast_pre,
                         row_blocks=tm // PRE_ROW_BLOCK))
    if cast_pre is not None:
        casts["pre"] = pre.pop()
    p, q, k, v = pre[:4]
    attn = list(_attn_call(w["lam4"], w["subln"], q, k, v, cache, batch=batch, seq=seq,
                           heads=heads, tq=tq, tw=256, tk=tk, cast_w=cast_attn))
    if cast_attn is not None:
        casts["attn"] = attn.pop()
    post = list(_post_call(x2d, attn[0], p, mod3[:, :, 2 * D_MODEL:5 * D_MODEL], w["wout"],
                           w["wbd"], w["pool_scale"], w["g2"], w["wr_t"], seq=seq, tm=tm_post,
                           mod_base=mod_base, mod_stride=mod_stride, cast_w=cast_post))
    if cast_post is not None:
        casts["post"] = post.pop()
    x1, h2, aff_t, aff_c = post
    cap = CAPACITY_FACTOR * n // N_EXPERTS
    idx, gates = _select_call(aff_t, aff_c, cap=cap)
    return {"x1": x1, "h2": h2, "idx": idx, "gates": gates, "cap": cap, "kv": pre[4:],
            "casts": casts, "shape": (batch, seq), "tm_post": tm_post,
            "mod": (mod_base, mod_stride)}


def _channel_mixing(mixed, mod3, wg, wu, wd, *, moe_tm):
    batch, seq = mixed["shape"]
    cap = mixed["cap"]
    mod_base, mod_stride = mixed["mod"]
    moe_tiles = _moe_call(mixed["idx"].reshape(N_EXPERTS * cap),
                          mixed["gates"].reshape(-1, 1, moe_tm), wg, wu, wd, mixed["h2"],
                          cap=cap, tm=moe_tm)
    y = _final_call(mixed["x1"], moe_tiles, mod3, seq=seq, tm=mixed["tm_post"],
                    mod_base=mod_base, mod_stride=mod_stride)
    return y.reshape(batch, seq, D_MODEL)


def kernel(x_prompt, x_sample, cache_k, cache_v, c, c_ctx, norm1_g, norm2_g, w_ada, b_ada, w_in,
           q_norm_g, k_norm_g, lambda_q1, lambda_k1, lambda_q2, lambda_k2, subln_g, w_pool,
           pool_scale, w_out, w_router, w_gate, w_up, w_down):
    assert w_ada.shape[0] == 1, "single-layer stack"
    batch, seq, _ = x_prompt.shape
    dec_batch, dec_seq, _ = x_sample.shape

    pad = SUBLANES - 1 - dec_batch
    cvec = jnp.concatenate([c_ctx[None, :], c, jnp.zeros((pad, D_MODEL), F32)], axis=0)
    mod = _ada_call(cvec, w_ada[0], b_ada[0])
    mod3 = mod.reshape(SUBLANES, 1, 6 * D_MODEL)

    n_groups = w_pool.shape[1]
    grp = POOL_WIDTH // n_groups
    eye = jnp.eye(n_groups, dtype=F32)
    wbd = (w_pool[0][:, :, None, :] * eye[:, None, :, None]).reshape(POOL_WIDTH, POOL_WIDTH)

    w = {
        "g1": norm1_g[0].reshape(1, D_MODEL),
        "g2": norm2_g[0].reshape(1, D_MODEL),
        "win": w_in[0].astype(BF16),
        "seg": _segment_matrix(),
        "qg": jnp.tile(q_norm_g[0], ATTN_WIDTH // HEAD_DIM).reshape(1, ATTN_WIDTH),
        "kg": jnp.tile(k_norm_g[0], ATTN_WIDTH // HEAD_DIM).reshape(1, ATTN_WIDTH),
        "lam4": jnp.stack([lambda_q1[0], lambda_k1[0], lambda_q2[0], lambda_k2[0]], axis=0),
        "subln": subln_g[0].reshape(V_DIM, 1),
        "wbd": wbd.astype(BF16),
        "pool_scale": pool_scale[0].reshape(1, POOL_WIDTH),
        "wout": w_out[0].astype(BF16),
        "wr_t": w_router[0].T,
    }

    ctx = _token_mixing(x_prompt, mod3, w, None, mod_base=0, mod_stride=0, tm=256, tm_post=512,
                        heads=N_HEADS, tq=256, tk=256)
    past = cache_k.shape[2]
    cv = cache_v[:, 0].reshape(dec_batch * past, N_HEADS, V_DIM).transpose(1, 2, 0).astype(BF16)
    cv = jnp.concatenate([cv, jnp.ones((N_HEADS, V_EXT - V_DIM, dec_batch * past), BF16)], axis=1)
    cache = (cache_k[:, 0].reshape(dec_batch * past, ATTN_WIDTH).astype(BF16),
             cv.reshape(N_HEADS * V_EXT, dec_batch * past))
    lat = _token_mixing(x_sample, mod3, w, cache, mod_base=1, mod_stride=1, tm=512, tm_post=512,
                        heads=1, tq=1024, tk=512, cast_pre=w_gate[0], cast_attn=w_down[0],
                        cast_post=w_up[0])
    wg, wu, wd = lat["casts"]["pre"], lat["casts"]["post"], lat["casts"]["attn"]
    yp = _channel_mixing(ctx, mod3, wg, wu, wd, moe_tm=512)
    ys = _channel_mixing(lat, mod3, wg, wu, wd, moe_tm=256)
    k_ctx, v_ctx = ctx["kv"]
    ctx_k =(k_ctx.reshape(batch, N_HEADS, 2, HEAD_DIM, seq).transpose(0, 4, 1, 2, 3)
             .reshape(batch, 1, seq, N_HEADS, 2, HEAD_DIM))
    ctx_v = v_ctx.transpose(0, 2, 1, 3).reshape(batch, 1, seq, N_HEADS, V_DIM)
    return yp, ys, ctx_k, ctx_v
```

```python
import functools
import math

import numpy as np
import jax
import jax.numpy as jnp
from jax import lax
from jax.experimental import pallas as pl
from jax.experimental.pallas import tpu as pltpu

F32 = jnp.float32
BF16 = jnp.bfloat16

D_MODEL = 1024
POOL_WIDTH = 256
ATTN_WIDTH = 768
N_HEADS = 6
HEAD_DIM = 64
V_DIM = 128
IN_WIDTH = POOL_WIDTH + 3 * ATTN_WIDTH
N_EXPERTS = 16
CAPACITY_FACTOR = 2
GRID_W = 64
ROPE_BASE = 10000.0
EPS = 1e-6
LAMBDA_INIT = 0.8 - 0.6 * math.exp(-0.3 * 0)
LOG2E = math.log2(math.e)
V_EXT = V_DIM + 16
PRE_ROW_BLOCK = 256
SELECT_GROUP = 8
SEG_BLOCK = 256
SCORES_AHEAD = 3

LANES = 128
SUBLANES = 8
ROW_TILES = D_MODEL // LANES
HALO = 16
VMEM_LIMIT = 56 * 1024 * 1024


def _cparams(sem, vmem=None):
    return pltpu.CompilerParams(dimension_semantics=sem, vmem_limit_bytes=vmem)


def _ada_kernel(c_ref, w_ref, b_ref, o_ref):
    c = c_ref[...]
    s = c * jax.nn.sigmoid(c)
    o_ref[...] = jnp.dot(s.astype(BF16), w_ref[...].astype(BF16),
                         preferred_element_type=F32) + b_ref[...]


def _ada_call(cvec, w_ada, b_ada):
    rows, d = cvec.shape
    n = w_ada.shape[1]
    bn = 1536
    return pl.pallas_call(
        _ada_kernel,
        grid=(n // bn,),
        in_specs=[pl.BlockSpec((rows, d), lambda j: (0, 0)),
                  pl.BlockSpec((d, bn), lambda j: (0, j)),
                  pl.BlockSpec((1, bn), lambda j: (0, j))],
        out_specs=pl.BlockSpec((rows, bn), lambda j: (0, j)),
        out_shape=jax.ShapeDtypeStruct((rows, n), F32),
        compiler_params=_cparams(("arbitrary",)),
        name="ada",
    )(cvec, w_ada, b_ada.reshape(1, n))


def _segment_mean_square(a, seg_ref):
    sq = (a * a).astype(BF16)
    seg = seg_ref[...]
    return jnp.concatenate(
        [jnp.dot(sq[:, j:j + SEG_BLOCK], seg, preferred_element_type=F32)
         for j in range(0, a.shape[1], SEG_BLOCK)], axis=1)


def _rope(a, cos, sin_signed, first_half):
    parts = []
    for h in range(a.shape[1] // LANES):
        blk = a[:, h * LANES:(h + 1) * LANES]
        fwd = pltpu.roll(blk, LANES - HEAD_DIM // 4, 1)
        bwd = pltpu.roll(blk, HEAD_DIM // 4, 1)
        parts.append(blk * cos + jnp.where(first_half, fwd, bwd) * sin_signed)
    return jnp.concatenate(parts, axis=1)


def _side_cast(refs, n_inputs, enabled, step=None, steps_per_block=1):
    if not enabled:
        return refs
    src_ref, dst_ref = refs[n_inputs - 1], refs[-1]

    def convert():
        dst_ref[...] = src_ref[...].astype(BF16)

    if steps_per_block == 1:
        convert()
    else:
        pl.when(step % steps_per_block == 0)(convert)
    return refs[:n_inputs - 1] + refs[n_inputs:-1]


def _with_side_cast(w, index_map, in_specs, args, out_specs, out_shapes):
    if w is None:
        return
    _, rows, cols = w.shape
    in_specs.append(pl.BlockSpec((None, rows, cols), index_map))
    args.append(w)
    out_specs.append(pl.BlockSpec((None, rows, cols), index_map))
    out_shapes.append(jax.ShapeDtypeStruct(w.shape, BF16))


def _pre_kernel(*refs, rope, emit_f32_kv, side_cast, row_blocks):
    refs = _side_cast(refs, 7 + (2 if rope else 0) + 1, side_cast)
    x_ref, mod_ref, g1_ref, win_ref, seg_ref, qg_ref, kg_ref = refs[:7]
    pos = 7
    if rope:
        cos_ref, sin_ref = refs[pos:pos + 2]
        pos += 2
    p_ref, q_ref, k_ref, v_ref = refs[pos:pos + 4]
    pos += 4
    if emit_f32_kv:
        kf_ref, vf_ref = refs[pos:pos + 2]

    mod = mod_ref[0]
    shift1 = mod[:, :D_MODEL]
    scale1 = mod[:, D_MODEL:2 * D_MODEL]
    tm = x_ref.shape[0]
    rb = tm // row_blocks

    def project(j):
        x = x_ref[j * rb:(j + 1) * rb, :]
        ms = jnp.mean(x * x, axis=1, keepdims=True)
        h = x * lax.rsqrt(ms + EPS) * g1_ref[...] * (1.0 + scale1) + shift1
        return jnp.dot(h.astype(BF16), win_ref[...], preferred_element_type=F32)

    def head_stats(z):
        qz = z[:, POOL_WIDTH:POOL_WIDTH + ATTN_WIDTH]
        kz = z[:, POOL_WIDTH + ATTN_WIDTH:POOL_WIDTH + 2 * ATTN_WIDTH]
        return _segment_mean_square(qz, seg_ref), _segment_mean_square(kz, seg_ref)

    def finish(j, z, stats):
        rows = slice(j * rb, (j + 1) * rb)
        p_ref[rows, :] = z[:, :POOL_WIDTH]
        qz = z[:, POOL_WIDTH:POOL_WIDTH + ATTN_WIDTH]
        kz = z[:, POOL_WIDTH + ATTN_WIDTH:POOL_WIDTH + 2 * ATTN_WIDTH]
        vz = z[:, POOL_WIDTH + 2 * ATTN_WIDTH:]
        qn = qz * lax.rsqrt(stats[0] + EPS) * qg_ref[...]
        kn = kz * lax.rsqrt(stats[1] + EPS) * kg_ref[...]
        if rope:
            cos = cos_ref[rows, :]
            sin_signed = sin_ref[rows, :]
            lane = lax.broadcasted_iota(jnp.int32, cos.shape, 1)
            first_half = (lane % (HEAD_DIM // 2)) < (HEAD_DIM // 4)
            qn = _rope(qn, cos, sin_signed, first_half)
            kn = _rope(kn, cos, sin_signed, first_half)
        q_ref[:, rows] = (qn * (LOG2E / math.sqrt(HEAD_DIM))).T.astype(BF16)
        k_ref[rows, :] = kn.astype(BF16)
        vt = vz.T
        ones = jnp.ones((V_EXT - V_DIM, rb), BF16)
        for h in range(N_HEADS):
            v_ref[h * V_EXT:h * V_EXT + V_DIM, rows] = vt[h * V_DIM:(h + 1) * V_DIM, :].astype(BF16)
            v_ref[h * V_EXT + V_DIM:(h + 1) * V_EXT, rows] = ones
        if emit_f32_kv:
            kf_ref[0, :, rows] = kn.T
            for h in range(N_HEADS):
                vf_ref[0, h, rows, :] = vz[:, h * V_DIM:(h + 1) * V_DIM]

    z_prev = project(0)
    for j in range(row_blocks):
        stats = head_stats(z_prev)
        z_next = project(j + 1) if j + 1 < row_blocks else None
        finish(j, z_prev, stats)
        z_prev = z_next


def _pre_call(x2d, mod3, g1, win_bf, seg, qg, kg, rope_tabs, *, seq, tm, mod_base, mod_stride,
              emit_f32_kv, cast_w=None, row_blocks=1):
    n = x2d.shape[0]
    rope = rope_tabs is not None
    tiles_per_seq = seq // tm

    def mod_map(i):
        return (mod_base + mod_stride * (i // tiles_per_seq), 0, 0)

    in_specs = [
        pl.BlockSpec((tm, D_MODEL), lambda i: (i, 0)),
        pl.BlockSpec((1, 1, 2 * D_MODEL), mod_map),
        pl.BlockSpec((1, D_MODEL), lambda i: (0, 0)),
        pl.BlockSpec((D_MODEL, IN_WIDTH), lambda i: (0, 0)),
        pl.BlockSpec((SEG_BLOCK, SEG_BLOCK), lambda i: (0, 0)),
        pl.BlockSpec((1, ATTN_WIDTH), lambda i: (0, 0)),
        pl.BlockSpec((1, ATTN_WIDTH), lambda i: (0, 0)),
    ]
    args = [x2d, mod3, g1, win_bf, seg, qg, kg]
    if rope:
        in_specs += [pl.BlockSpec((tm, LANES), lambda i: (i % tiles_per_seq, 0))] * 2
        args += list(rope_tabs)
    out_shapes = [jax.ShapeDtypeStruct((n, POOL_WIDTH), F32)]
    out_specs = [pl.BlockSpec((tm, POOL_WIDTH), lambda i: (i, 0))]
    out_shapes += [jax.ShapeDtypeStruct((ATTN_WIDTH, n), BF16),
                   jax.ShapeDtypeStruct((n, ATTN_WIDTH), BF16),
                   jax.ShapeDtypeStruct((N_HEADS * V_EXT, n), BF16)]
    out_specs += [pl.BlockSpec((ATTN_WIDTH, tm), lambda i: (0, i)),
                  pl.BlockSpec((tm, ATTN_WIDTH), lambda i: (i, 0)),
                  pl.BlockSpec((N_HEADS * V_EXT, tm), lambda i: (0, i))]
    if emit_f32_kv:
        def seq_map(i):
            return (i // tiles_per_seq, 0, i % tiles_per_seq)

        out_shapes += [jax.ShapeDtypeStruct((n // seq, ATTN_WIDTH, seq), F32),
                       jax.ShapeDtypeStruct((n // seq, N_HEADS, seq, V_DIM), F32)]
        out_specs += [pl.BlockSpec((1, ATTN_WIDTH, tm), seq_map),
                      pl.BlockSpec((1, N_HEADS, tm, V_DIM),
                                   lambda i: (i // tiles_per_seq, 0, i % tiles_per_seq, 0))]
    if cast_w is not None:
        assert cast_w.shape[0] == n // tm, "one expert matrix per grid step"
    _with_side_cast(cast_w, lambda i: (i, 0, 0), in_specs, args, out_specs, out_shapes)
    return pl.pallas_call(
        functools.partial(_pre_kernel, rope=rope, emit_f32_kv=emit_f32_kv,
                          side_cast=cast_w is not None, row_blocks=row_blocks),
        grid=(n // tm,),
        in_specs=in_specs,
        out_specs=out_specs,
        out_shape=out_shapes,
        compiler_params=_cparams(("arbitrary",), VMEM_LIMIT),
        name="pre_rope" if rope else "pre",
    )(*args)


def _attn_kernel(*refs, has_cache, heads, tq, tw, tk, seq, cache_len, side_cast):
    step = ((pl.program_id(0) * pl.num_programs(1) + pl.program_id(1)) * pl.num_programs(2)
            + pl.program_id(2))
    refs = _side_cast(refs, (7 if has_cache else 5) + 1, side_cast > 0, step, side_cast)
    if has_cache:
        lam_ref, g_ref, q_ref, kc_ref, vc_ref, k_ref, v_ref, o_ref = refs
    else:
        lam_ref, g_ref, q_ref, k_ref, v_ref, o_ref = refs

    lv = lam_ref[...]
    lam = (jnp.exp(jnp.sum(lv[0:1] * lv[1:2], axis=1, keepdims=True))
           - jnp.exp(jnp.sum(lv[2:3] * lv[3:4], axis=1, keepdims=True)) + LAMBDA_INIT)

    row = lax.broadcasted_iota(jnp.int32, (LANES, tq), 0)
    zero = jnp.zeros((LANES, tq), BF16)

    def sub_queries(h):
        qt = q_ref[h * LANES:(h + 1) * LANES, :]
        return (jnp.where(row < HEAD_DIM, qt, zero), jnp.where(row >= HEAD_DIM, qt, zero))

    def scores(kb, q_one):
        return jnp.dot(kb, q_one, preferred_element_type=F32)

    def softmax_step(s, m):
        m_new = jnp.maximum(m, jnp.max(s, axis=0, keepdims=True))
        return m_new, jnp.exp2(m - m_new), jnp.exp2(s - m_new).astype(BF16)

    def accumulate(vb, p, alpha, acc):
        return alpha * acc + jnp.dot(vb, p, preferred_element_type=F32)

    chunks = []
    if has_cache:
        chunks += [(kc_ref, vc_ref, j) for j in range(cache_len // tk)]
    chunks += [(k_ref, v_ref, j) for j in range(seq // tk)]

    def keys(c, h):
        kr, _, j = chunks[c]
        return kr[j * tk:(j + 1) * tk, h * LANES:(h + 1) * LANES]

    def values(c, h):
        _, vr, j = chunks[c]
        return vr[h * V_EXT:(h + 1) * V_EXT, j * tk:(j + 1) * tk]

    chains = []
    for h in range(heads):
        q_sub = sub_queries(h)
        chains += [(h, q_sub[sub][:, w * tw:(w + 1) * tw])
                   for w in range(tq // tw) for sub in range(2)]
    per_head = len(chains) // heads
    items = [(c, ch) for h in range(heads) for c in range(len(chunks))
             for ch in range(h * per_head, (h + 1) * per_head)]
    m = [jnp.full((1, tw), -jnp.inf, F32)] * len(chains)
    acc = [jnp.zeros((V_EXT, tw), F32)] * len(chains)
    queue = [scores(keys(c, chains[ch][0]), chains[ch][1]) for c, ch in items[:SCORES_AHEAD]]
    for i, (c, ch) in enumerate(items):
        s_cur = queue.pop(0)
        if i + SCORES_AHEAD < len(items):
            nc, nch = items[i + SCORES_AHEAD]
            queue.append(scores(keys(nc, chains[nch][0]), chains[nch][1]))
        m[ch], alpha, p = softmax_step(s_cur, m[ch])
        acc[ch] = accumulate(values(c, chains[ch][0]), p, alpha, acc[ch])
    for h in range(heads):
        outs = []
        for w in range(tq // tw):
            a1, a2 = acc[h * per_head + 2 * w], acc[h * per_head + 2 * w + 1]
            outs.append(a1[:V_DIM] / a1[V_DIM:V_DIM + 1]
                        - lam * (a2[:V_DIM] / a2[V_DIM:V_DIM + 1]))
        o = outs[0] if len(outs) == 1 else jnp.concatenate(outs, axis=1)
        y = (o * lax.rsqrt(jnp.mean(o * o, axis=0, keepdims=True) + EPS) * g_ref[...]
             * (1.0 - LAMBDA_INIT))
        o_ref[:, h * LANES:(h + 1) * LANES] = y.T.astype(BF16)


def _attn_call(lam4, subln_col, qt, k, vt, cache, *, batch, seq, heads, tq, tw, tk, cast_w=None):
    has_cache = cache is not None
    nq = seq // tq
    in_specs = [
        pl.BlockSpec((4, HEAD_DIM), lambda b, h, i: (0, 0)),
        pl.BlockSpec((V_DIM, 1), lambda b, h, i: (0, 0)),
        pl.BlockSpec((heads * LANES, tq), lambda b, h, i: (h, b * nq + i)),
    ]
    args = [lam4, subln_col, qt]
    cache_len = 0
    if has_cache:
        kc, vct = cache
        cache_len = kc.shape[0] // batch
        in_specs += [pl.BlockSpec((cache_len, heads * LANES), lambda b, h, i: (b, h)),
                     pl.BlockSpec((heads * V_EXT, cache_len), lambda b, h, i: (h, b))]
        args += [kc, vct]
    in_specs += [pl.BlockSpec((seq, heads * LANES), lambda b, h, i: (b, h)),
                 pl.BlockSpec((heads * V_EXT, seq), lambda b, h, i: (h, b))]
    args += [k, vt]
    out_specs = [pl.BlockSpec((tq, heads * LANES), lambda b, h, i: (b * nq + i, h))]
    out_shapes = [jax.ShapeDtypeStruct((batch * seq, ATTN_WIDTH), BF16)]
    grid = (batch, N_HEADS // heads, nq)
    steps_per_block = 0
    if cast_w is not None:
        steps_per_block, rem = divmod(grid[0] * grid[1] * grid[2], cast_w.shape[0])
        assert rem == 0 and steps_per_block > 0, "whole number of grid steps per expert matrix"
    _with_side_cast(
        cast_w, lambda b, h, i: (((b * grid[1] + h) * grid[2] + i) // steps_per_block, 0, 0),
        in_specs, args, out_specs, out_shapes)
    return pl.pallas_call(
        functools.partial(_attn_kernel, has_cache=has_cache, heads=heads, tq=tq, tw=tw, tk=tk,
                          seq=seq, cache_len=cache_len, side_cast=steps_per_block),
        grid=grid,
        in_specs=in_specs,
        out_specs=out_specs,
        out_shape=out_shapes,
        compiler_params=_cparams(("arbitrary", "arbitrary", "arbitrary"), VMEM_LIMIT),
        name="attn_cache" if has_cache else "attn",
    )(*args)


def _post_kernel(*refs, tm, seq, side_cast):
    (x_ref, a_ref, p_ref, pp_ref, pn_ref, mod_ref, wout_ref, wbd_ref, ps_ref, g2_ref, wr_ref,
     x1_ref, h2_ref, aff_ref, affc_ref) = _side_cast(refs, 12, side_cast)
    i = pl.program_id(0)

    def centred_mean_minus_self(rows, prev, nxt, t0):
        r = rows.shape[0]
        ext = jnp.concatenate([prev, rows, nxt], axis=0)
        n_ext = r + 2 * HALO
        s2 = ext + pltpu.roll(ext, 1, 0)
        s4 = pltpu.roll(s2, 1, 0) + pltpu.roll(s2, n_ext - 1, 0)
        s8 = pltpu.roll(s4, 2, 0) + pltpu.roll(s4, n_ext - 2, 0)
        s16 = pltpu.roll(s8, 4, 0) + pltpu.roll(s8, n_ext - 4, 0)
        lane = lax.broadcasted_iota(jnp.int32, (r, POOL_WIDTH), 1)
        grp = lane // (POOL_WIDTH // 4)
        win = jnp.where(grp == 0, s2[HALO:HALO + r],
                        jnp.where(grp == 1, s4[HALO:HALO + r],
                                  jnp.where(grp == 2, s8[HALO:HALO + r], s16[HALO:HALO + r])))
        t = t0 + lax.broadcasted_iota(jnp.int32, (r, POOL_WIDTH), 0)
        left = jnp.where(grp == 0, 1, jnp.where(grp == 1, 2, jnp.where(grp == 2, 4, 8)))
        lo = jnp.maximum(t - left, 0)
        hi = jnp.minimum(t + left - 1, seq - 1) + 1
        return win / (hi - lo).astype(F32) - rows

    if tm <= seq:
        tiles_per_seq = seq // tm
        ti = i % tiles_per_seq
        pooled = centred_mean_minus_self(
            p_ref[...], jnp.where(ti == 0, 0.0, pp_ref[...]),
            jnp.where(ti == tiles_per_seq - 1, 0.0, pn_ref[...]), ti * tm)
    else:
        halo = jnp.zeros((HALO, POOL_WIDTH), F32)
        pooled = jnp.concatenate(
            [centred_mean_minus_self(p_ref[j * seq:(j + 1) * seq, :], halo, halo, 0)
             for j in range(tm // seq)], axis=0)
    pool = jnp.dot(pooled.astype(BF16), wbd_ref[...], preferred_element_type=F32) * ps_ref[...]

    cat = jnp.concatenate([pool.astype(BF16), a_ref[...]], axis=1)
    mix = jnp.dot(cat, wout_ref[...], preferred_element_type=F32)
    mod = mod_ref[0]
    gate1 = mod[:, 0:D_MODEL]
    shift2 = mod[:, D_MODEL:2 * D_MODEL]
    scale2 = mod[:, 2 * D_MODEL:3 * D_MODEL]
    x1 = x_ref[...] + gate1 * mix
    x1_ref[...] = x1
    ms = jnp.mean(x1 * x1, axis=1, keepdims=True)
    h2 = x1 * lax.rsqrt(ms + EPS) * g2_ref[...] * (1.0 + scale2) + shift2
    for s in range(ROW_TILES):
        h2_ref[pl.ds(s, tm, stride=ROW_TILES), :] = h2[:, s * LANES:(s + 1) * LANES]

    logits = lax.dot_general(wr_ref[...], h2, (((1,), (1,)), ((), ())),
                             precision=lax.Precision.HIGHEST, preferred_element_type=F32)
    e = jnp.exp(logits - jnp.max(logits, axis=0, keepdims=True))
    aff = e / jnp.sum(e, axis=0, keepdims=True)
    aff_ref[...] = aff
    for c in range(tm // LANES):
        affc_ref[c * N_EXPERTS:(c + 1) * N_EXPERTS, :] = aff[:, c * LANES:(c + 1) * LANES]


def _post_call(x2d, attn, p, mod3, wout_bf, wbd_bf, pool_scale, g2, wr_t, *, seq, tm, mod_base,
               mod_stride, cast_w=None):
    n = x2d.shape[0]
    assert cast_w is None or cast_w.shape[0] == n // tm, "one expert matrix per grid step"
    assert tm <= seq or mod_stride == 0, "a tile spanning sequences needs one modulation row"
    halo_per_tile = tm // HALO
    n_halo = n // HALO

    def mod_map(i):
        return (mod_base + mod_stride * (i * tm // seq), 0, 0)

    in_specs = [
            pl.BlockSpec((tm, D_MODEL), lambda i: (i, 0)),
            pl.BlockSpec((tm, ATTN_WIDTH), lambda i: (i, 0)),
            pl.BlockSpec((tm, POOL_WIDTH), lambda i: (i, 0)),
            pl.BlockSpec((HALO, POOL_WIDTH), lambda i: (jnp.maximum(i * halo_per_tile - 1, 0), 0)),
            pl.BlockSpec((HALO, POOL_WIDTH),
                         lambda i: (jnp.minimum((i + 1) * halo_per_tile, n_halo - 1), 0)),
            pl.BlockSpec((1, 1, 3 * D_MODEL), lambda i: mod_map(i)[:2] + (0,)),
            pl.BlockSpec((D_MODEL, D_MODEL), lambda i: (0, 0)),
            pl.BlockSpec((POOL_WIDTH, POOL_WIDTH), lambda i: (0, 0)),
            pl.BlockSpec((1, POOL_WIDTH), lambda i: (0, 0)),
            pl.BlockSpec((1, D_MODEL), lambda i: (0, 0)),
            pl.BlockSpec((N_EXPERTS, D_MODEL), lambda i: (0, 0)),
    ]
    out_specs = [
            pl.BlockSpec((tm, D_MODEL), lambda i: (i, 0)),
            pl.BlockSpec((tm * ROW_TILES, LANES), lambda i: (i, 0)),
            pl.BlockSpec((N_EXPERTS, tm), lambda i: (0, i)),
            pl.BlockSpec((tm // LANES * N_EXPERTS, LANES), lambda i: (i, 0)),
    ]
    out_shapes = [
            jax.ShapeDtypeStruct((n, D_MODEL), F32),
            jax.ShapeDtypeStruct((n * ROW_TILES, LANES), F32),
            jax.ShapeDtypeStruct((N_EXPERTS, n), F32),
            jax.ShapeDtypeStruct((n // LANES * N_EXPERTS, LANES), F32),
    ]
    args = [x2d, attn, p, p, p, mod3, wout_bf, wbd_bf, pool_scale, g2, wr_t]
    _with_side_cast(cast_w, lambda i: (i, 0, 0), in_specs, args, out_specs, out_shapes)
    return pl.pallas_call(
        functools.partial(_post_kernel, tm=tm, seq=seq, side_cast=cast_w is not None),
        grid=(n // tm,),
        in_specs=in_specs,
        out_specs=out_specs,
        out_shape=out_shapes,
        compiler_params=_cparams(("arbitrary",), VMEM_LIMIT),
        name="post",
    )(*args)


def _select_kernel(a_ref, ac_ref, idx_ref, gate_ref, *, n, cap):
    nc = n // LANES
    a = a_ref[...]
    thr = jnp.zeros((N_EXPERTS, 1), jnp.int32)
    for bit in range(30, -1, -1):
        cand = thr | (1 << bit)
        cnt = jnp.sum(jnp.where(a >= pltpu.bitcast(cand, F32), 1.0, 0.0), axis=1, keepdims=True)
        thr = jnp.where(cnt >= cap, cand, thr)
    thr_all = pltpu.bitcast(thr, F32)
    need_all = cap - jnp.sum(jnp.where(a > thr_all, 1.0, 0.0), axis=1, keepdims=True)

    r = lax.broadcasted_iota(jnp.int32, (LANES, LANES), 0)
    c = lax.broadcasted_iota(jnp.int32, (LANES, LANES), 1)
    upper = jnp.where(r <= c, 1.0, 0.0).astype(BF16)
    lower = jnp.where(c < r, 1.0, 0.0).astype(BF16)
    row_valid = r < nc
    chunk_col = lax.broadcasted_iota(jnp.int32, (LANES, 1), 0).astype(F32)
    slot = lax.broadcasted_iota(jnp.int32, (1, cap), 1).astype(F32)

    def lane_counts(mask):
        local = jnp.dot(mask.astype(BF16), upper, preferred_element_type=F32)
        total = jnp.broadcast_to(local[:, LANES - 1:LANES], (LANES, LANES))
        before = jnp.dot(lower, total.astype(BF16), preferred_element_type=F32)
        return local, total, before

    def ties(e):
        av = ac_ref[pl.ds(e, nc, stride=N_EXPERTS), :]
        if nc < LANES:
            av = jnp.concatenate([av, jnp.zeros((LANES - nc, LANES), F32)], axis=0)
        thr_e = thr_all[e:e + 1, :]
        above = jnp.where(row_valid & (av > thr_e), 1.0, 0.0)
        tied = jnp.where(row_valid & (av == thr_e), 1.0, 0.0)
        t_local, _, t_before = lane_counts(tied)
        return {"av": av, "above": above, "tied": tied, "tie_rank": t_local + t_before}

    def selection(e, st):
        sel = st["above"] + st["tied"] * jnp.where(st["tie_rank"] <= need_all[e:e + 1, :], 1.0, 0.0)
        s_local, s_total, s_before = lane_counts(sel)
        rank = jnp.where(sel > 0.0, s_local, 0.0)
        start = s_before[:, 0:1]
        return {"av": st["av"], "rank": rank, "start": start, "stop": start + s_total[:, 0:1]}

    def pick(e, st):
        start = st["start"]
        onehot = jnp.where((slot >= start) & (slot < st["stop"]), 1.0, 0.0)
        chunk_of_slot = jnp.sum(onehot * chunk_col, axis=0, keepdims=True)
        start_of_slot = jnp.sum(onehot * start, axis=0, keepdims=True)
        at = st["av"].T
        hi = at.astype(BF16)
        rest = at - hi.astype(F32)
        mid = rest.astype(BF16)
        lo = (rest - mid.astype(F32)).astype(BF16)
        lhs = jnp.concatenate([st["rank"].T.astype(BF16), hi, mid, lo], axis=0)
        picked = jnp.dot(lhs, onehot.astype(BF16), preferred_element_type=F32)
        return {"picked": picked, "chunk_of_slot": chunk_of_slot, "start_of_slot": start_of_slot}

    def emit(e, st):
        picked = st["picked"]
        rank_p = picked[0:LANES]
        aff_p = picked[LANES:2 * LANES] + picked[2 * LANES:3 * LANES] + picked[3 * LANES:]
        hit = rank_p == (slot - st["start_of_slot"] + 1.0)
        lane_of_slot = jnp.sum(jnp.where(hit, chunk_col, 0.0), axis=0, keepdims=True)
        idx_ref[e:e + 1, :] = (st["chunk_of_slot"] * LANES + lane_of_slot).astype(jnp.int32)
        gate_ref[e:e + 1, :] = jnp.sum(jnp.where(hit, aff_p, 0.0), axis=0, keepdims=True)

    for e0 in range(0, N_EXPERTS, SELECT_GROUP):
        group = range(e0, e0 + SELECT_GROUP)
        states = {e: ties(e) for e in group}
        states = {e: selection(e, states[e]) for e in group}
        states = {e: pick(e, states[e]) for e in group}
        for e in group:
            emit(e, states[e])


def _select_call(aff_t, aff_c, *, cap):
    n = aff_t.shape[1]
    assert n % LANES == 0 and n // LANES <= LANES and cap % LANES == 0
    return pl.pallas_call(
        functools.partial(_select_kernel, n=n, cap=cap),
        grid=(1,),
        in_specs=[pl.BlockSpec((N_EXPERTS, n), lambda i: (0, 0)),
                  pl.BlockSpec(aff_c.shape, lambda i: (0, 0))],
        out_specs=[pl.BlockSpec((N_EXPERTS, cap), lambda i: (0, 0))] * 2,
        out_shape=[jax.ShapeDtypeStruct((N_EXPERTS, cap), jnp.int32),
                   jax.ShapeDtypeStruct((N_EXPERTS, cap), F32)],
        compiler_params=_cparams(("arbitrary",), VMEM_LIMIT),
        name="select",
    )(aff_t, aff_c)


def _moe_kernel(idx_ref, gate_prev_ref, gate_ref, wg_ref, wu_ref, wd_ref, x_hbm, out_hbm,
                gbuf, hbuf, ybuf, acc_ref, gsem, osem, *, tm, n_tiles_total):
    e = pl.program_id(0)
    t = pl.program_id(1)
    nt = pl.num_programs(1)
    step = e * nt + t
    group = 8
    last = n_tiles_total - 1
    rows = tm * ROW_TILES

    def gather_start(tile_step, k, dst_slot):
        tok = idx_ref[tile_step * tm + k]
        pltpu.make_async_copy(
            x_hbm.at[pl.ds(pl.multiple_of(tok * ROW_TILES, ROW_TILES), ROW_TILES), :],
            gbuf.at[dst_slot, pl.ds(k * ROW_TILES, ROW_TILES), :], gsem.at[dst_slot]).start()

    def gather_wait(dst_slot):
        pltpu.make_async_copy(x_hbm.at[pl.ds(0, rows), :], gbuf.at[dst_slot],
                              gsem.at[dst_slot]).wait()

    def scatter_add(tile_step, k0):
        pending = []
        for r in range(group):
            k = k0 + r
            tok = idx_ref[tile_step * tm + k]
            off = pl.multiple_of(tok * ROW_TILES, ROW_TILES)
            src = pl.multiple_of(k * ROW_TILES, ROW_TILES)
            pending.append((off, acc_ref[pl.ds(off, ROW_TILES), :]
                            + ybuf[pl.ds(src, ROW_TILES), :]))
        for off, val in pending:
            acc_ref[pl.ds(off, ROW_TILES), :] = val

    def down_project(src_slot, gates):
        y = jnp.dot(hbuf[src_slot], wd_ref[...], preferred_element_type=F32)
        gate = jnp.broadcast_to(gates[...], (LANES, tm)).T
        for s in range(ROW_TILES):
            ybuf[pl.ds(s, tm, stride=ROW_TILES), :] = y[:, s * LANES:(s + 1) * LANES] * gate

    @pl.when(step == 0)
    def _():
        acc_ref[...] = jnp.zeros_like(acc_ref)
        hbuf[...] = jnp.zeros_like(hbuf)

        def body(k, _):
            gather_start(0, k, 0)
            return 0
        lax.fori_loop(0, tm, body, 0)

    def tile_body(slot):
        nxt = jnp.minimum(step + 1, last)
        prev = jnp.maximum(step - 1, 0)
        for k in range(tm):
            gather_start(nxt, k, 1 - slot)
        down_project(1 - slot, gate_prev_ref)
        gather_wait(slot)
        xe = jnp.concatenate(
            [gbuf[slot, pl.ds(s, tm, stride=ROW_TILES), :] for s in range(ROW_TILES)],
            axis=1).astype(BF16)
        g = jnp.dot(xe, wg_ref[...], preferred_element_type=F32)
        for k0 in range(0, tm // 2, group):
            scatter_add(prev, k0)
        u = jnp.dot(xe, wu_ref[...], preferred_element_type=F32)
        for k0 in range(tm // 2, tm, group):
            scatter_add(prev, k0)
        hbuf[slot] = (g * jax.nn.sigmoid(g) * u).astype(BF16)

        @pl.when(step == last)
        def _():
            gather_wait(1 - slot)
            down_project(slot, gate_ref)

            def body(kk, _):
                scatter_add(step, kk * group)
                return 0
            lax.fori_loop(0, tm // group, body, 0)
            cp = pltpu.make_async_copy(acc_ref, out_hbm, osem)
            cp.start()
            cp.wait()

    for parity in range(2):
        pl.when(step % 2 == parity)(functools.partial(tile_body, parity))


def _moe_call(idx_flat, gates3, wg_bf, wu_bf, wd_bf, h2, *, cap, tm):
    n_rows = h2.shape[0]
    nt = cap // tm
    assert nt >= 2, "the one-step lag of w_down needs at least two tiles per expert"

    def prev_tile(e, t):
        return jnp.maximum(e * nt + t - 1, 0)

    grid_spec = pltpu.PrefetchScalarGridSpec(
        num_scalar_prefetch=1,
        grid=(N_EXPERTS, nt),
        in_specs=[
            pl.BlockSpec((None, 1, tm), lambda e, t, idx: (prev_tile(e, t), 0, 0)),
            pl.BlockSpec((None, 1, tm), lambda e, t, idx: (e * nt + t, 0, 0)),
            pl.BlockSpec((None, D_MODEL, D_MODEL), lambda e, t, idx: (e, 0, 0)),
            pl.BlockSpec((None, D_MODEL, D_MODEL), lambda e, t, idx: (e, 0, 0)),
            pl.BlockSpec((None, D_MODEL, D_MODEL), lambda e, t, idx: (prev_tile(e, t) // nt, 0, 0)),
            pl.BlockSpec(memory_space=pl.ANY),
        ],
        out_specs=pl.BlockSpec(memory_space=pl.ANY),
        scratch_shapes=[
            pltpu.VMEM((2, tm * ROW_TILES, LANES), F32),
            pltpu.VMEM((2, tm, D_MODEL), BF16),
            pltpu.VMEM((tm * ROW_TILES, LANES), F32),
            pltpu.VMEM((n_rows, LANES), F32),
            pltpu.SemaphoreType.DMA((2,)),
            pltpu.SemaphoreType.DMA(()),
        ],
    )
    return pl.pallas_call(
        functools.partial(_moe_kernel, tm=tm, n_tiles_total=N_EXPERTS * nt),
        grid_spec=grid_spec,
        out_shape=jax.ShapeDtypeStruct((n_rows, LANES), F32),
        compiler_params=_cparams(("arbitrary", "arbitrary"), VMEM_LIMIT),
        name="moe",
    )(idx_flat, gates3, gates3, wg_bf, wu_bf, wd_bf, h2)


def _final_kernel(x1_ref, moe_ref, mod_ref, o_ref, *, tm):
    moe = jnp.concatenate(
        [moe_ref[pl.ds(s, tm, stride=ROW_TILES), :] for s in range(ROW_TILES)], axis=1)
    o_ref[...] = x1_ref[...] + mod_ref[0] * moe


def _final_call(x1, moe_tiles, mod3, *, seq, tm, mod_base, mod_stride):
    n = x1.shape[0]
    assert tm <= seq or mod_stride == 0, "a tile spanning sequences needs one modulation row"
    gate2_block = 5

    def mod_map(i):
        return (mod_base + mod_stride * (i * tm // seq), 0, gate2_block)

    return pl.pallas_call(
        functools.partial(_final_kernel, tm=tm),
        grid=(n // tm,),
        in_specs=[pl.BlockSpec((tm, D_MODEL), lambda i: (i, 0)),
                  pl.BlockSpec((tm * ROW_TILES, LANES), lambda i: (i, 0)),
                  pl.BlockSpec((1, 1, D_MODEL), mod_map)],
        out_specs=pl.BlockSpec((tm, D_MODEL), lambda i: (i, 0)),
        out_shape=jax.ShapeDtypeStruct((n, D_MODEL), F32),
        compiler_params=_cparams(("arbitrary",)),
        name="final",
    )(x1, moe_tiles, mod3)


def _rope_tables(seq):
    t = np.arange(seq)
    row, col = t // GRID_W, t % GRID_W
    half = HEAD_DIM // 2
    freqs = 1.0 / (ROPE_BASE ** (np.arange(0, half, 2) / half))
    ang_r = row[:, None] * freqs[None, :]
    ang_c = col[:, None] * freqs[None, :]
    ang = np.concatenate([ang_r, ang_r, ang_c, ang_c], axis=-1)
    cos = np.tile(np.cos(ang), (1, LANES // HEAD_DIM))
    sin = np.tile(np.sin(ang), (1, LANES // HEAD_DIM))
    sign = np.where((np.arange(LANES) % (HEAD_DIM // 2)) < (HEAD_DIM // 4), -1.0, 1.0)
    return jnp.asarray(cos, F32), jnp.asarray(sin * sign[None, :], F32)


def _segment_matrix():
    seg = np.arange(SEG_BLOCK) // HEAD_DIM
    return jnp.asarray((seg[:, None] == seg[None, :]) / HEAD_DIM, BF16)


def _token_mixing(x, mod3, w, cache, *, mod_base, mod_stride, tm, tm_post, heads, tq, tk,
                  cast_pre=None, cast_attn=None, cast_post=None):
    batch, seq, _ = x.shape
    n = batch * seq
    x2d = x.reshape(n, D_MODEL)
    rope_tabs = _rope_tables(seq) if cache is not None else None
    casts = {}
    pre = list(_pre_call(x2d, mod3[:, :, :2 * D_MODEL], w["g1"], w["win"], w["seg"], w["qg"],
                         w["kg"], rope_tabs, seq=seq, tm=tm, mod_base=mod_base,
                         mod_stride=mod_stride, emit_f32_kv=cache is None, cast_w=cast_pre,
                         row_blocks=tm // PRE_ROW_BLOCK))
    if cast_pre is not None:
        casts["pre"] = pre.pop()
    p, q, k, v = pre[:4]
    attn = list(_attn_call(w["lam4"], w["subln"], q, k, v, cache, batch=batch, seq=seq,
                           heads=heads, tq=tq, tw=256, tk=tk, cast_w=cast_attn))
    if cast_attn is not None:
        casts["attn"] = attn.pop()
    post = list(_post_call(x2d, attn[0], p, mod3[:, :, 2 * D_MODEL:5 * D_MODEL], w["wout"],
                           w["wbd"], w["pool_scale"], w["g2"], w["wr_t"], seq=seq, tm=tm_post,
                           mod_base=mod_base, mod_stride=mod_stride, cast_w=cast_post))
    if cast_post is not None:
        casts["post"] = post.pop()
    x1, h2, aff_t, aff_c = post
    cap = CAPACITY_FACTOR * n // N_EXPERTS
    idx, gates = _select_call(aff_t, aff_c, cap=cap)
    return {"x1": x1, "h2": h2, "idx": idx, "gates": gates, "cap": cap, "kv": pre[4:],
            "casts": casts, "shape": (batch, seq), "tm_post": tm_post,
            "mod": (mod_base, mod_stride)}


def _channel_mixing(mixed, mod3, wg, wu, wd, *, moe_tm):
    batch, seq = mixed["shape"]
    cap = mixed["cap"]
    mod_base, mod_stride = mixed["mod"]
    moe_tiles = _moe_call(mixed["idx"].reshape(N_EXPERTS * cap),
                          mixed["gates"].reshape(-1, 1, moe_tm), wg, wu, wd, mixed["h2"],
                          cap=cap, tm=moe_tm)
    y = _final_call(mixed["x1"], moe_tiles, mod3, seq=seq, tm=mixed["tm_post"],
                    mod_base=mod_base, mod_stride=mod_stride)
    return y.reshape(batch, seq, D_MODEL)


def kernel(x_prompt, x_sample, cache_k, cache_v, c, c_ctx, norm1_g, norm2_g, w_ada, b_ada, w_in,
           q_norm_g, k_norm_g, lambda_q1, lambda_k1, lambda_q2, lambda_k2, subln_g, w_pool,
           pool_scale, w_out, w_router, w_gate, w_up, w_down):
    assert w_ada.shape[0] == 1, "single-layer stack"
    batch, seq, _ = x_prompt.shape
    dec_batch, dec_seq, _ = x_sample.shape

    pad = SUBLANES - 1 - dec_batch
    cvec = jnp.concatenate([c_ctx[None, :], c, jnp.zeros((pad, D_MODEL), F32)], axis=0)
    mod = _ada_call(cvec, w_ada[0], b_ada[0])
    mod3 = mod.reshape(SUBLANES, 1, 6 * D_MODEL)

    n_groups = w_pool.shape[1]
    grp = POOL_WIDTH // n_groups
    eye = jnp.eye(n_groups, dtype=F32)
    wbd = (w_pool[0][:, :, None, :] * eye[:, None, :, None]).reshape(POOL_WIDTH, POOL_WIDTH)

    w = {
        "g1": norm1_g[0].reshape(1, D_MODEL),
        "g2": norm2_g[0].reshape(1, D_MODEL),
        "win": w_in[0].astype(BF16),
        "seg": _segment_matrix(),
        "qg": jnp.tile(q_norm_g[0], ATTN_WIDTH // HEAD_DIM).reshape(1, ATTN_WIDTH),
        "kg": jnp.tile(k_norm_g[0], ATTN_WIDTH // HEAD_DIM).reshape(1, ATTN_WIDTH),
        "lam4": jnp.stack([lambda_q1[0], lambda_k1[0], lambda_q2[0], lambda_k2[0]], axis=0),
        "subln": subln_g[0].reshape(V_DIM, 1),
        "wbd": wbd.astype(BF16),
        "pool_scale": pool_scale[0].reshape(1, POOL_WIDTH),
        "wout": w_out[0].astype(BF16),
        "wr_t": w_router[0].T,
    }

    ctx = _token_mixing(x_prompt, mod3, w, None, mod_base=0, mod_stride=0, tm=256, tm_post=512,
                        heads=N_HEADS, tq=256, tk=256)
    past = cache_k.shape[2]
    cv = cache_v[:, 0].reshape(dec_batch * past, N_HEADS, V_DIM).transpose(1, 2, 0).astype(BF16)
    cv = jnp.concatenate([cv, jnp.ones((N_HEADS, V_EXT - V_DIM, dec_batch * past), BF16)], axis=1)
    cache = (cache_k[:, 0].reshape(dec_batch * past, ATTN_WIDTH).astype(BF16),
             cv.reshape(N_HEADS * V_EXT, dec_batch * past))
    lat = _token_mixing(x_sample, mod3, w, cache, mod_base=1, mod_stride=1, tm=512, tm_post=512,
                        heads=1, tq=1024, tk=512, cast_pre=w_gate[0], cast_attn=w_down[0],
                        cast_post=w_up[0])
    wg, wu, wd = lat["casts"]["pre"], lat["casts"]["post"], lat["casts"]["attn"]
    yp = _channel_mixing(ctx, mod3, wg, wu, wd, moe_tm=256)
    ys = _channel_mixing(lat, mod3, wg, wu, wd, moe_tm=256)
    k_ctx, v_ctx = ctx["kv"]
    ctx_k =(k_ctx.reshape(batch, N_HEADS, 2, HEAD_DIM, seq).transpose(0, 4, 1, 2, 3)
             .reshape(batch, 1, seq, N_HEADS, 2, HEAD_DIM))
    ctx_v = v_ctx.transpose(0, 2, 1, 3).reshape(batch, 1, seq, N_HEADS, V_DIM)
    return yp, ys, ctx_k, ctx_v
```

```python
import functools
import math

import numpy as np
import jax
import jax.numpy as jnp
from jax import lax
from jax.experimental import pallas as pl
from jax.experimental.pallas import tpu as pltpu

F32 = jnp.float32
BF16 = jnp.bfloat16

D_MODEL = 1024
POOL_WIDTH = 256
ATTN_WIDTH = 768
N_HEADS = 6
HEAD_DIM = 64
V_DIM = 128
IN_WIDTH = POOL_WIDTH + 3 * ATTN_WIDTH
N_EXPERTS = 16
CAPACITY_FACTOR = 2
GRID_W = 64
ROPE_BASE = 10000.0
EPS = 1e-6
LAMBDA_INIT = 0.8 - 0.6 * math.exp(-0.3 * 0)
LOG2E = math.log2(math.e)
V_EXT = V_DIM + 16
PRE_ROW_BLOCK = 256
SELECT_GROUP = 8
SEG_BLOCK = 256
SCORES_AHEAD = 3

LANES = 128
SUBLANES = 8
ROW_TILES = D_MODEL // LANES
HALO = 16
VMEM_LIMIT = 56 * 1024 * 1024


def _cparams(sem, vmem=None):
    return pltpu.CompilerParams(dimension_semantics=sem, vmem_limit_bytes=vmem)


def _ada_kernel(c_ref, w_ref, b_ref, o_ref):
    c = c_ref[...]
    s = c * jax.nn.sigmoid(c)
    o_ref[...] = jnp.dot(s.astype(BF16), w_ref[...].astype(BF16),
                         preferred_element_type=F32) + b_ref[...]


def _ada_call(cvec, w_ada, b_ada):
    rows, d = cvec.shape
    n = w_ada.shape[1]
    bn = 1536
    return pl.pallas_call(
        _ada_kernel,
        grid=(n // bn,),
        in_specs=[pl.BlockSpec((rows, d), lambda j: (0, 0)),
                  pl.BlockSpec((d, bn), lambda j: (0, j)),
                  pl.BlockSpec((1, bn), lambda j: (0, j))],
        out_specs=pl.BlockSpec((rows, bn), lambda j: (0, j)),
        out_shape=jax.ShapeDtypeStruct((rows, n), F32),
        compiler_params=_cparams(("arbitrary",)),
        name="ada",
    )(cvec, w_ada, b_ada.reshape(1, n))


def _segment_mean_square(a, seg_ref):
    sq = (a * a).astype(BF16)
    seg = seg_ref[...]
    return jnp.concatenate(
        [jnp.dot(sq[:, j:j + SEG_BLOCK], seg, preferred_element_type=F32)
         for j in range(0, a.shape[1], SEG_BLOCK)], axis=1)


def _rope(a, cos, sin_signed, first_half):
    parts = []
    for h in range(a.shape[1] // LANES):
        blk = a[:, h * LANES:(h + 1) * LANES]
        fwd = pltpu.roll(blk, LANES - HEAD_DIM // 4, 1)
        bwd = pltpu.roll(blk, HEAD_DIM // 4, 1)
        parts.append(blk * cos + jnp.where(first_half, fwd, bwd) * sin_signed)
    return jnp.concatenate(parts, axis=1)


def _side_cast(refs, n_inputs, enabled, step=None, steps_per_block=1):
    if not enabled:
        return refs
    src_ref, dst_ref = refs[n_inputs - 1], refs[-1]

    def convert():
        dst_ref[...] = src_ref[...].astype(BF16)

    if steps_per_block == 1:
        convert()
    else:
        pl.when(step % steps_per_block == 0)(convert)
    return refs[:n_inputs - 1] + refs[n_inputs:-1]


def _with_side_cast(w, index_map, in_specs, args, out_specs, out_shapes):
    if w is None:
        return
    _, rows, cols = w.shape
    in_specs.append(pl.BlockSpec((None, rows, cols), index_map))
    args.append(w)
    out_specs.append(pl.BlockSpec((None, rows, cols), index_map))
    out_shapes.append(jax.ShapeDtypeStruct(w.shape, BF16))


def _pre_kernel(*refs, rope, emit_f32_kv, side_cast, row_blocks):
    refs = _side_cast(refs, 7 + (2 if rope else 0) + 1, side_cast)
    x_ref, mod_ref, g1_ref, win_ref, seg_ref, qg_ref, kg_ref = refs[:7]
    pos = 7
    if rope:
        cos_ref, sin_ref = refs[pos:pos + 2]
        pos += 2
    p_ref, q_ref, k_ref, v_ref = refs[pos:pos + 4]
    pos += 4
    if emit_f32_kv:
        kf_ref, vf_ref = refs[pos:pos + 2]

    mod = mod_ref[0]
    shift1 = mod[:, :D_MODEL]
    scale1 = mod[:, D_MODEL:2 * D_MODEL]
    tm = x_ref.shape[0]
    rb = tm // row_blocks

    def project(j):
        x = x_ref[j * rb:(j + 1) * rb, :]
        ms = jnp.mean(x * x, axis=1, keepdims=True)
        h = x * lax.rsqrt(ms + EPS) * g1_ref[...] * (1.0 + scale1) + shift1
        return jnp.dot(h.astype(BF16), win_ref[...], preferred_element_type=F32)

    def head_stats(z):
        qz = z[:, POOL_WIDTH:POOL_WIDTH + ATTN_WIDTH]
        kz = z[:, POOL_WIDTH + ATTN_WIDTH:POOL_WIDTH + 2 * ATTN_WIDTH]
        return _segment_mean_square(qz, seg_ref), _segment_mean_square(kz, seg_ref)

    def finish(j, z, stats):
        rows = slice(j * rb, (j + 1) * rb)
        p_ref[rows, :] = z[:, :POOL_WIDTH]
        qz = z[:, POOL_WIDTH:POOL_WIDTH + ATTN_WIDTH]
        kz = z[:, POOL_WIDTH + ATTN_WIDTH:POOL_WIDTH + 2 * ATTN_WIDTH]
        vz = z[:, POOL_WIDTH + 2 * ATTN_WIDTH:]
        qn = qz * lax.rsqrt(stats[0] + EPS) * qg_ref[...]
        kn = kz * lax.rsqrt(stats[1] + EPS) * kg_ref[...]
        if rope:
            cos = cos_ref[rows, :]
            sin_signed = sin_ref[rows, :]
            lane = lax.broadcasted_iota(jnp.int32, cos.shape, 1)
            first_half = (lane % (HEAD_DIM // 2)) < (HEAD_DIM // 4)
            qn = _rope(qn, cos, sin_signed, first_half)
            kn = _rope(kn, cos, sin_signed, first_half)
        q_ref[:, rows] = (qn * (LOG2E / math.sqrt(HEAD_DIM))).T.astype(BF16)
        k_ref[rows, :] = kn.astype(BF16)
        vt = vz.T
        ones = jnp.ones((V_EXT - V_DIM, rb), BF16)
        for h in range(N_HEADS):
            v_ref[h * V_EXT:h * V_EXT + V_DIM, rows] = vt[h * V_DIM:(h + 1) * V_DIM, :].astype(BF16)
            v_ref[h * V_EXT + V_DIM:(h + 1) * V_EXT, rows] = ones
        if emit_f32_kv:
            kf_ref[0, :, rows] = kn.T
            for h in range(N_HEADS):
                vf_ref[0, h, rows, :] = vz[:, h * V_DIM:(h + 1) * V_DIM]

    z_prev = project(0)
    for j in range(row_blocks):
        stats = head_stats(z_prev)
        z_next = project(j + 1) if j + 1 < row_blocks else None
        finish(j, z_prev, stats)
        z_prev = z_next


def _pre_call(x2d, mod3, g1, win_bf, seg, qg, kg, rope_tabs, *, seq, tm, mod_base, mod_stride,
              emit_f32_kv, cast_w=None, row_blocks=1):
    n = x2d.shape[0]
    rope = rope_tabs is not None
    tiles_per_seq = seq // tm

    def mod_map(i):
        return (mod_base + mod_stride * (i // tiles_per_seq), 0, 0)

    in_specs = [
        pl.BlockSpec((tm, D_MODEL), lambda i: (i, 0)),
        pl.BlockSpec((1, 1, 2 * D_MODEL), mod_map),
        pl.BlockSpec((1, D_MODEL), lambda i: (0, 0)),
        pl.BlockSpec((D_MODEL, IN_WIDTH), lambda i: (0, 0)),
        pl.BlockSpec((SEG_BLOCK, SEG_BLOCK), lambda i: (0, 0)),
        pl.BlockSpec((1, ATTN_WIDTH), lambda i: (0, 0)),
        pl.BlockSpec((1, ATTN_WIDTH), lambda i: (0, 0)),
    ]
    args = [x2d, mod3, g1, win_bf, seg, qg, kg]
    if rope:
        in_specs += [pl.BlockSpec((tm, LANES), lambda i: (i % tiles_per_seq, 0))] * 2
        args += list(rope_tabs)
    out_shapes = [jax.ShapeDtypeStruct((n, POOL_WIDTH), F32)]
    out_specs = [pl.BlockSpec((tm, POOL_WIDTH), lambda i: (i, 0))]
    out_shapes += [jax.ShapeDtypeStruct((ATTN_WIDTH, n), BF16),
                   jax.ShapeDtypeStruct((n, ATTN_WIDTH), BF16),
                   jax.ShapeDtypeStruct((N_HEADS * V_EXT, n), BF16)]
    out_specs += [pl.BlockSpec((ATTN_WIDTH, tm), lambda i: (0, i)),
                  pl.BlockSpec((tm, ATTN_WIDTH), lambda i: (i, 0)),
                  pl.BlockSpec((N_HEADS * V_EXT, tm), lambda i: (0, i))]
    if emit_f32_kv:
        def seq_map(i):
            return (i // tiles_per_seq, 0, i % tiles_per_seq)

        out_shapes += [jax.ShapeDtypeStruct((n // seq, ATTN_WIDTH, seq), F32),
                       jax.ShapeDtypeStruct((n // seq, N_HEADS, seq, V_DIM), F32)]
        out_specs += [pl.BlockSpec((1, ATTN_WIDTH, tm), seq_map),
                      pl.BlockSpec((1, N_HEADS, tm, V_DIM),
                                   lambda i: (i // tiles_per_seq, 0, i % tiles_per_seq, 0))]
    if cast_w is not None:
        assert cast_w.shape[0] == n // tm, "one expert matrix per grid step"
    _with_side_cast(cast_w, lambda i: (i, 0, 0), in_specs, args, out_specs, out_shapes)
    return pl.pallas_call(
        functools.partial(_pre_kernel, rope=rope, emit_f32_kv=emit_f32_kv,
                          side_cast=cast_w is not None, row_blocks=row_blocks),
        grid=(n // tm,),
        in_specs=in_specs,
        out_specs=out_specs,
        out_shape=out_shapes,
        compiler_params=_cparams(("arbitrary",), VMEM_LIMIT),
        name="pre_rope" if rope else "pre",
    )(*args)


def _attn_kernel(*refs, has_cache, heads, tq, tw, tk, seq, cache_len, side_cast):
    step = ((pl.program_id(0) * pl.num_programs(1) + pl.program_id(1)) * pl.num_programs(2)
            + pl.program_id(2))
    refs = _side_cast(refs, (7 if has_cache else 5) + 1, side_cast > 0, step, side_cast)
    if has_cache:
        lam_ref, g_ref, q_ref, kc_ref, vc_ref, k_ref, v_ref, o_ref = refs
    else:
        lam_ref, g_ref, q_ref, k_ref, v_ref, o_ref = refs

    lv = lam_ref[...]
    lam = (jnp.exp(jnp.sum(lv[0:1] * lv[1:2], axis=1, keepdims=True))
           - jnp.exp(jnp.sum(lv[2:3] * lv[3:4], axis=1, keepdims=True)) + LAMBDA_INIT)

    row = lax.broadcasted_iota(jnp.int32, (LANES, tq), 0)
    zero = jnp.zeros((LANES, tq), BF16)

    def sub_queries(h):
        qt = q_ref[h * LANES:(h + 1) * LANES, :]
        return (jnp.where(row < HEAD_DIM, qt, zero), jnp.where(row >= HEAD_DIM, qt, zero))

    def scores(kb, q_one):
        return jnp.dot(kb, q_one, preferred_element_type=F32)

    def softmax_step(s, m):
        m_new = jnp.maximum(m, jnp.max(s, axis=0, keepdims=True))
        return m_new, jnp.exp2(m - m_new), jnp.exp2(s - m_new).astype(BF16)

    def accumulate(vb, p, alpha, acc):
        return alpha * acc + jnp.dot(vb, p, preferred_element_type=F32)

    chunks = []
    if has_cache:
        chunks += [(kc_ref, vc_ref, j) for j in range(cache_len // tk)]
    chunks += [(k_ref, v_ref, j) for j in range(seq // tk)]

    def keys(c, h):
        kr, _, j = chunks[c]
        return kr[j * tk:(j + 1) * tk, h * LANES:(h + 1) * LANES]

    def values(c, h):
        _, vr, j = chunks[c]
        return vr[h * V_EXT:(h + 1) * V_EXT, j * tk:(j + 1) * tk]

    chains = []
    for h in range(heads):
        q_sub = sub_queries(h)
        chains += [(h, q_sub[sub][:, w * tw:(w + 1) * tw])
                   for w in range(tq // tw) for sub in range(2)]
    per_head = len(chains) // heads
    items = [(c, ch) for h in range(heads) for c in range(len(chunks))
             for ch in range(h * per_head, (h + 1) * per_head)]
    m = [jnp.full((1, tw), -jnp.inf, F32)] * len(chains)
    acc = [jnp.zeros((V_EXT, tw), F32)] * len(chains)
    queue = [scores(keys(c, chains[ch][0]), chains[ch][1]) for c, ch in items[:SCORES_AHEAD]]
    for i, (c, ch) in enumerate(items):
        s_cur = queue.pop(0)
        if i + SCORES_AHEAD < len(items):
            nc, nch = items[i + SCORES_AHEAD]
            queue.append(scores(keys(nc, chains[nch][0]), chains[nch][1]))
        m[ch], alpha, p = softmax_step(s_cur, m[ch])
        acc[ch] = accumulate(values(c, chains[ch][0]), p, alpha, acc[ch])
    for h in range(heads):
        outs = []
        for w in range(tq // tw):
            a1, a2 = acc[h * per_head + 2 * w], acc[h * per_head + 2 * w + 1]
            outs.append(a1[:V_DIM] / a1[V_DIM:V_DIM + 1]
                        - lam * (a2[:V_DIM] / a2[V_DIM:V_DIM + 1]))
        o = outs[0] if len(outs) == 1 else jnp.concatenate(outs, axis=1)
        y = (o * lax.rsqrt(jnp.mean(o * o, axis=0, keepdims=True) + EPS) * g_ref[...]
             * (1.0 - LAMBDA_INIT))
        o_ref[:, h * LANES:(h + 1) * LANES] = y.T.astype(BF16)


def _attn_call(lam4, subln_col, qt, k, vt, cache, *, batch, seq, heads, tq, tw, tk, cast_w=None):
    has_cache = cache is not None
    nq = seq // tq
    in_specs = [
        pl.BlockSpec((4, HEAD_DIM), lambda b, h, i: (0, 0)),
        pl.BlockSpec((V_DIM, 1), lambda b, h, i: (0, 0)),
        pl.BlockSpec((heads * LANES, tq), lambda b, h, i: (h, b * nq + i)),
    ]
    args = [lam4, subln_col, qt]
    cache_len = 0
    if has_cache:
        kc, vct = cache
        cache_len = kc.shape[0] // batch
        in_specs += [pl.BlockSpec((cache_len, heads * LANES), lambda b, h, i: (b, h)),
                     pl.BlockSpec((heads * V_EXT, cache_len), lambda b, h, i: (h, b))]
        args += [kc, vct]
    in_specs += [pl.BlockSpec((seq, heads * LANES), lambda b, h, i: (b, h)),
                 pl.BlockSpec((heads * V_EXT, seq), lambda b, h, i: (h, b))]
    args += [k, vt]
    out_specs = [pl.BlockSpec((tq, heads * LANES), lambda b, h, i: (b * nq + i, h))]
    out_shapes = [jax.ShapeDtypeStruct((batch * seq, ATTN_WIDTH), BF16)]
    grid = (batch, N_HEADS // heads, nq)
    steps_per_block = 0
    if cast_w is not None:
        steps_per_block, rem = divmod(grid[0] * grid[1] * grid[2], cast_w.shape[0])
        assert rem == 0 and steps_per_block > 0, "whole number of grid steps per expert matrix"
    _with_side_cast(
        cast_w, lambda b, h, i: (((b * grid[1] + h) * grid[2] + i) // steps_per_block, 0, 0),
        in_specs, args, out_specs, out_shapes)
    return pl.pallas_call(
        functools.partial(_attn_kernel, has_cache=has_cache, heads=heads, tq=tq, tw=tw, tk=tk,
                          seq=seq, cache_len=cache_len, side_cast=steps_per_block),
        grid=grid,
        in_specs=in_specs,
        out_specs=out_specs,
        out_shape=out_shapes,
        compiler_params=_cparams(("arbitrary", "arbitrary", "arbitrary"), VMEM_LIMIT),
        name="attn_cache" if has_cache else "attn",
    )(*args)


def _post_kernel(*refs, tm, seq, side_cast):
    (x_ref, a_ref, p_ref, pp_ref, pn_ref, mod_ref, wout_ref, wbd_ref, ps_ref, g2_ref, wr_ref,
     x1_ref, h2_ref, aff_ref, affc_ref) = _side_cast(refs, 12, side_cast)
    i = pl.program_id(0)

    def centred_mean_minus_self(rows, prev, nxt, t0):
        r = rows.shape[0]
        ext = jnp.concatenate([prev, rows, nxt], axis=0)
        n_ext = r + 2 * HALO
        s2 = ext + pltpu.roll(ext, 1, 0)
        s4 = pltpu.roll(s2, 1, 0) + pltpu.roll(s2, n_ext - 1, 0)
        s8 = pltpu.roll(s4, 2, 0) + pltpu.roll(s4, n_ext - 2, 0)
        s16 = pltpu.roll(s8, 4, 0) + pltpu.roll(s8, n_ext - 4, 0)
        lane = lax.broadcasted_iota(jnp.int32, (r, POOL_WIDTH), 1)
        grp = lane // (POOL_WIDTH // 4)
        win = jnp.where(grp == 0, s2[HALO:HALO + r],
                        jnp.where(grp == 1, s4[HALO:HALO + r],
                                  jnp.where(grp == 2, s8[HALO:HALO + r], s16[HALO:HALO + r])))
        t = t0 + lax.broadcasted_iota(jnp.int32, (r, POOL_WIDTH), 0)
        left = jnp.where(grp == 0, 1, jnp.where(grp == 1, 2, jnp.where(grp == 2, 4, 8)))
        lo = jnp.maximum(t - left, 0)
        hi = jnp.minimum(t + left - 1, seq - 1) + 1
        return win / (hi - lo).astype(F32) - rows

    if tm <= seq:
        tiles_per_seq = seq // tm
        ti = i % tiles_per_seq
        pooled = centred_mean_minus_self(
            p_ref[...], jnp.where(ti == 0, 0.0, pp_ref[...]),
            jnp.where(ti == tiles_per_seq - 1, 0.0, pn_ref[...]), ti * tm)
    else:
        halo = jnp.zeros((HALO, POOL_WIDTH), F32)
        pooled = jnp.concatenate(
            [centred_mean_minus_self(p_ref[j * seq:(j + 1) * seq, :], halo, halo, 0)
             for j in range(tm // seq)], axis=0)
    pool = jnp.dot(pooled.astype(BF16), wbd_ref[...], preferred_element_type=F32) * ps_ref[...]

    cat = jnp.concatenate([pool.astype(BF16), a_ref[...]], axis=1)
    mix = jnp.dot(cat, wout_ref[...], preferred_element_type=F32)
    mod = mod_ref[0]
    gate1 = mod[:, 0:D_MODEL]
    shift2 = mod[:, D_MODEL:2 * D_MODEL]
    scale2 = mod[:, 2 * D_MODEL:3 * D_MODEL]
    x1 = x_ref[...] + gate1 * mix
    x1_ref[...] = x1
    ms = jnp.mean(x1 * x1, axis=1, keepdims=True)
    h2 = x1 * lax.rsqrt(ms + EPS) * g2_ref[...] * (1.0 + scale2) + shift2
    for s in range(ROW_TILES):
        h2_ref[pl.ds(s, tm, stride=ROW_TILES), :] = h2[:, s * LANES:(s + 1) * LANES]

    logits = lax.dot_general(wr_ref[...], h2, (((1,), (1,)), ((), ())),
                             precision=lax.Precision.HIGHEST, preferred_element_type=F32)
    e = jnp.exp(logits - jnp.max(logits, axis=0, keepdims=True))
    aff = e / jnp.sum(e, axis=0, keepdims=True)
    aff_ref[...] = aff
    for c in range(tm // LANES):
        affc_ref[c * N_EXPERTS:(c + 1) * N_EXPERTS, :] = aff[:, c * LANES:(c + 1) * LANES]


def _post_call(x2d, attn, p, mod3, wout_bf, wbd_bf, pool_scale, g2, wr_t, *, seq, tm, mod_base,
               mod_stride, cast_w=None):
    n = x2d.shape[0]
    assert cast_w is None or cast_w.shape[0] == n // tm, "one expert matrix per grid step"
    assert tm <= seq or mod_stride == 0, "a tile spanning sequences needs one modulation row"
    halo_per_tile = tm // HALO
    n_halo = n // HALO

    def mod_map(i):
        return (mod_base + mod_stride * (i * tm // seq), 0, 0)

    in_specs = [
            pl.BlockSpec((tm, D_MODEL), lambda i: (i, 0)),
            pl.BlockSpec((tm, ATTN_WIDTH), lambda i: (i, 0)),
            pl.BlockSpec((tm, POOL_WIDTH), lambda i: (i, 0)),
            pl.BlockSpec((HALO, POOL_WIDTH), lambda i: (jnp.maximum(i * halo_per_tile - 1, 0), 0)),
            pl.BlockSpec((HALO, POOL_WIDTH),
                         lambda i: (jnp.minimum((i + 1) * halo_per_tile, n_halo - 1), 0)),
            pl.BlockSpec((1, 1, 3 * D_MODEL), lambda i: mod_map(i)[:2] + (0,)),
            pl.BlockSpec((D_MODEL, D_MODEL), lambda i: (0, 0)),
            pl.BlockSpec((POOL_WIDTH, POOL_WIDTH), lambda i: (0, 0)),
            pl.BlockSpec((1, POOL_WIDTH), lambda i: (0, 0)),
            pl.BlockSpec((1, D_MODEL), lambda i: (0, 0)),
            pl.BlockSpec((N_EXPERTS, D_MODEL), lambda i: (0, 0)),
    ]
    out_specs = [
            pl.BlockSpec((tm, D_MODEL), lambda i: (i, 0)),
            pl.BlockSpec((tm * ROW_TILES, LANES), lambda i: (i, 0)),
            pl.BlockSpec((N_EXPERTS, tm), lambda i: (0, i)),
            pl.BlockSpec((tm // LANES * N_EXPERTS, LANES), lambda i: (i, 0)),
    ]
    out_shapes = [
            jax.ShapeDtypeStruct((n, D_MODEL), F32),
            jax.ShapeDtypeStruct((n * ROW_TILES, LANES), F32),
            jax.ShapeDtypeStruct((N_EXPERTS, n), F32),
            jax.ShapeDtypeStruct((n // LANES * N_EXPERTS, LANES), F32),
    ]
    args = [x2d, attn, p, p, p, mod3, wout_bf, wbd_bf, pool_scale, g2, wr_t]
    _with_side_cast(cast_w, lambda i: (i, 0, 0), in_specs, args, out_specs, out_shapes)
    return pl.pallas_call(
        functools.partial(_post_kernel, tm=tm, seq=seq, side_cast=cast_w is not None),
        grid=(n // tm,),
        in_specs=in_specs,
        out_specs=out_specs,
        out_shape=out_shapes,
        compiler_params=_cparams(("arbitrary",), VMEM_LIMIT),
        name="post",
    )(*args)


def _select_kernel(a_ref, ac_ref, idx_ref, gate_ref, *, n, cap):
    nc = n // LANES
    a = a_ref[...]
    thr = jnp.zeros((N_EXPERTS, 1), jnp.int32)
    for bit in range(30, -1, -1):
        cand = thr | (1 << bit)
        cnt = jnp.sum(jnp.where(a >= pltpu.bitcast(cand, F32), 1.0, 0.0), axis=1, keepdims=True)
        thr = jnp.where(cnt >= cap, cand, thr)
    thr_all = pltpu.bitcast(thr, F32)
    need_all = cap - jnp.sum(jnp.where(a > thr_all, 1.0, 0.0), axis=1, keepdims=True)

    r = lax.broadcasted_iota(jnp.int32, (LANES, LANES), 0)
    c = lax.broadcasted_iota(jnp.int32, (LANES, LANES), 1)
    upper = jnp.where(r <= c, 1.0, 0.0).astype(BF16)
    lower = jnp.where(c < r, 1.0, 0.0).astype(BF16)
    row_valid = r < nc
    chunk_col = lax.broadcasted_iota(jnp.int32, (LANES, 1), 0).astype(F32)
    slot = lax.broadcasted_iota(jnp.int32, (1, cap), 1).astype(F32)

    def lane_counts(mask):
        local = jnp.dot(mask.astype(BF16), upper, preferred_element_type=F32)
        total = jnp.broadcast_to(local[:, LANES - 1:LANES], (LANES, LANES))
        before = jnp.dot(lower, total.astype(BF16), preferred_element_type=F32)
        return local, total, before

    def ties(e):
        av = ac_ref[pl.ds(e, nc, stride=N_EXPERTS), :]
        if nc < LANES:
            av = jnp.concatenate([av, jnp.zeros((LANES - nc, LANES), F32)], axis=0)
        thr_e = thr_all[e:e + 1, :]
        above = jnp.where(row_valid & (av > thr_e), 1.0, 0.0)
        tied = jnp.where(row_valid & (av == thr_e), 1.0, 0.0)
        t_local, _, t_before = lane_counts(tied)
        return {"av": av, "above": above, "tied": tied, "tie_rank": t_local + t_before}

    def selection(e, st):
        sel = st["above"] + st["tied"] * jnp.where(st["tie_rank"] <= need_all[e:e + 1, :], 1.0, 0.0)
        s_local, s_total, s_before = lane_counts(sel)
        rank = jnp.where(sel > 0.0, s_local, 0.0)
        start = s_before[:, 0:1]
        return {"av": st["av"], "rank": rank, "start": start, "stop": start + s_total[:, 0:1]}

    def pick(e, st):
        start = st["start"]
        onehot = jnp.where((slot >= start) & (slot < st["stop"]), 1.0, 0.0)
        chunk_of_slot = jnp.sum(onehot * chunk_col, axis=0, keepdims=True)
        start_of_slot = jnp.sum(onehot * start, axis=0, keepdims=True)
        at = st["av"].T
        hi = at.astype(BF16)
        rest = at - hi.astype(F32)
        mid = rest.astype(BF16)
        lo = (rest - mid.astype(F32)).astype(BF16)
        lhs = jnp.concatenate([st["rank"].T.astype(BF16), hi, mid, lo], axis=0)
        picked = jnp.dot(lhs, onehot.astype(BF16), preferred_element_type=F32)
        return {"picked": picked, "chunk_of_slot": chunk_of_slot, "start_of_slot": start_of_slot}

    def emit(e, st):
        picked = st["picked"]
        rank_p = picked[0:LANES]
        aff_p = picked[LANES:2 * LANES] + picked[2 * LANES:3 * LANES] + picked[3 * LANES:]
        hit = rank_p == (slot - st["start_of_slot"] + 1.0)
        lane_of_slot = jnp.sum(jnp.where(hit, chunk_col, 0.0), axis=0, keepdims=True)
        idx_ref[e:e + 1, :] = (st["chunk_of_slot"] * LANES + lane_of_slot).astype(jnp.int32)
        gate_ref[e:e + 1, :] = jnp.sum(jnp.where(hit, aff_p, 0.0), axis=0, keepdims=True)

    for e0 in range(0, N_EXPERTS, SELECT_GROUP):
        group = range(e0, e0 + SELECT_GROUP)
        states = {e: ties(e) for e in group}
        states = {e: selection(e, states[e]) for e in group}
        states = {e: pick(e, states[e]) for e in group}
        for e in group:
            emit(e, states[e])


def _select_call(aff_t, aff_c, *, cap):
    n = aff_t.shape[1]
    assert n % LANES == 0 and n // LANES <= LANES and cap % LANES == 0
    return pl.pallas_call(
        functools.partial(_select_kernel, n=n, cap=cap),
        grid=(1,),
        in_specs=[pl.BlockSpec((N_EXPERTS, n), lambda i: (0, 0)),
                  pl.BlockSpec(aff_c.shape, lambda i: (0, 0))],
        out_specs=[pl.BlockSpec((N_EXPERTS, cap), lambda i: (0, 0))] * 2,
        out_shape=[jax.ShapeDtypeStruct((N_EXPERTS, cap), jnp.int32),
                   jax.ShapeDtypeStruct((N_EXPERTS, cap), F32)],
        compiler_params=_cparams(("arbitrary",), VMEM_LIMIT),
        name="select",
    )(aff_t, aff_c)


def _moe_kernel(idx_ref, gate_ref, wg_ref, wu_ref, wd_ref, x_hbm, out_hbm,
                gbuf, ybuf, acc_ref, gsem, osem, *, tm, n_tiles_total):
    e = pl.program_id(0)
    t = pl.program_id(1)
    nt = pl.num_programs(1)
    step = e * nt + t
    group = 8
    last = n_tiles_total - 1
    rows = tm * ROW_TILES

    def gather_start(tile_step, k, dst_slot, priority=0):
        tok = idx_ref[tile_step * tm + k]
        pltpu.make_async_copy(
            x_hbm.at[pl.ds(pl.multiple_of(tok * ROW_TILES, ROW_TILES), ROW_TILES), :],
            gbuf.at[dst_slot, pl.ds(k * ROW_TILES, ROW_TILES), :], gsem.at[dst_slot]
        ).start(priority=priority)

    def gather_wait(dst_slot):
        pltpu.make_async_copy(x_hbm.at[pl.ds(0, rows), :], gbuf.at[dst_slot],
                              gsem.at[dst_slot]).wait()

    def scatter_add(tile_step, src_slot, k0):
        pending = []
        for r in range(group):
            k = k0 + r
            tok = idx_ref[tile_step * tm + k]
            off = pl.multiple_of(tok * ROW_TILES, ROW_TILES)
            src = pl.multiple_of(k * ROW_TILES, ROW_TILES)
            pending.append((off, acc_ref[pl.ds(off, ROW_TILES), :]
                            + ybuf[src_slot, pl.ds(src, ROW_TILES), :]))
        for off, val in pending:
            acc_ref[pl.ds(off, ROW_TILES), :] = val

    @pl.when(step == 0)
    def _():
        acc_ref[...] = jnp.zeros_like(acc_ref)
        ybuf[...] = jnp.zeros_like(ybuf)

        def body(k, _):
            gather_start(0, k, 0)
            return 0
        lax.fori_loop(0, tm, body, 0)

    def tile_body(slot):
        nxt = jnp.minimum(step + 1, last)
        prev = jnp.maximum(step - 1, 0)
        for k in range(tm):
            gather_start(nxt, k, 1 - slot, priority=k % 2)
        gather_wait(slot)
        xe = jnp.concatenate(
            [gbuf[slot, pl.ds(s, tm, stride=ROW_TILES), :] for s in range(ROW_TILES)],
            axis=1).astype(BF16)
        g = jnp.dot(xe, wg_ref[...], preferred_element_type=F32)
        for k0 in range(0, tm // 2, group):
            scatter_add(prev, 1 - slot, k0)
        u = jnp.dot(xe, wu_ref[...], preferred_element_type=F32)
        for k0 in range(tm // 2, tm, group):
            scatter_add(prev, 1 - slot, k0)
        hid = (g * jax.nn.sigmoid(g) * u).astype(BF16)
        y = jnp.dot(hid, wd_ref[...], preferred_element_type=F32)
        gate = jnp.broadcast_to(gate_ref[...], (LANES, tm)).T
        for s in range(ROW_TILES):
            ybuf[slot, pl.ds(s, tm, stride=ROW_TILES), :] = y[:, s * LANES:(s + 1) * LANES] * gate

        @pl.when(step == last)
        def _():
            gather_wait(1 - slot)

            def body(kk, _):
                scatter_add(step, slot, kk * group)
                return 0
            lax.fori_loop(0, tm // group, body, 0)
            cp = pltpu.make_async_copy(acc_ref, out_hbm, osem)
            cp.start()
            cp.wait()

    for parity in range(2):
        pl.when(step % 2 == parity)(functools.partial(tile_body, parity))


def _moe_call(idx_flat, gates3, wg_bf, wu_bf, wd_bf, h2, *, cap, tm):
    n_rows = h2.shape[0]
    nt = cap // tm
    grid_spec = pltpu.PrefetchScalarGridSpec(
        num_scalar_prefetch=1,
        grid=(N_EXPERTS, nt),
        in_specs=[
            pl.BlockSpec((None, 1, tm), lambda e, t, idx: (e * nt + t, 0, 0)),
            pl.BlockSpec((None, D_MODEL, D_MODEL), lambda e, t, idx: (e, 0, 0)),
            pl.BlockSpec((None, D_MODEL, D_MODEL), lambda e, t, idx: (e, 0, 0)),
            pl.BlockSpec((None, D_MODEL, D_MODEL), lambda e, t, idx: (e, 0, 0)),
            pl.BlockSpec(memory_space=pl.ANY),
        ],
        out_specs=pl.BlockSpec(memory_space=pl.ANY),
        scratch_shapes=[
            pltpu.VMEM((2, tm * ROW_TILES, LANES), F32),
            pltpu.VMEM((2, tm * ROW_TILES, LANES), F32),
            pltpu.VMEM((n_rows, LANES), F32),
            pltpu.SemaphoreType.DMA((2,)),
            pltpu.SemaphoreType.DMA(()),
        ],
    )
    return pl.pallas_call(
        functools.partial(_moe_kernel, tm=tm, n_tiles_total=N_EXPERTS * nt),
        grid_spec=grid_spec,
        out_shape=jax.ShapeDtypeStruct((n_rows, LANES), F32),
        compiler_params=_cparams(("arbitrary", "arbitrary"), VMEM_LIMIT),
        name="moe",
    )(idx_flat, gates3, wg_bf, wu_bf, wd_bf, h2)


def _final_kernel(x1_ref, moe_ref, mod_ref, o_ref, *, tm):
    moe = jnp.concatenate(
        [moe_ref[pl.ds(s, tm, stride=ROW_TILES), :] for s in range(ROW_TILES)], axis=1)
    o_ref[...] = x1_ref[...] + mod_ref[0] * moe


def _final_call(x1, moe_tiles, mod3, *, seq, tm, mod_base, mod_stride):
    n = x1.shape[0]
    assert tm <= seq or mod_stride == 0, "a tile spanning sequences needs one modulation row"
    gate2_block = 5

    def mod_map(i):
        return (mod_base + mod_stride * (i * tm // seq), 0, gate2_block)

    return pl.pallas_call(
        functools.partial(_final_kernel, tm=tm),
        grid=(n // tm,),
        in_specs=[pl.BlockSpec((tm, D_MODEL), lambda i: (i, 0)),
                  pl.BlockSpec((tm * ROW_TILES, LANES), lambda i: (i, 0)),
                  pl.BlockSpec((1, 1, D_MODEL), mod_map)],
        out_specs=pl.BlockSpec((tm, D_MODEL), lambda i: (i, 0)),
        out_shape=jax.ShapeDtypeStruct((n, D_MODEL), F32),
        compiler_params=_cparams(("arbitrary",)),
        name="final",
    )(x1, moe_tiles, mod3)


def _rope_tables(seq):
    t = np.arange(seq)
    row, col = t // GRID_W, t % GRID_W
    half = HEAD_DIM // 2
    freqs = 1.0 / (ROPE_BASE ** (np.arange(0, half, 2) / half))
    ang_r = row[:, None] * freqs[None, :]
    ang_c = col[:, None] * freqs[None, :]
    ang = np.concatenate([ang_r, ang_r, ang_c, ang_c], axis=-1)
    cos = np.tile(np.cos(ang), (1, LANES // HEAD_DIM))
    sin = np.tile(np.sin(ang), (1, LANES // HEAD_DIM))
    sign = np.where((np.arange(LANES) % (HEAD_DIM // 2)) < (HEAD_DIM // 4), -1.0, 1.0)
    return jnp.asarray(cos, F32), jnp.asarray(sin * sign[None, :], F32)


def _segment_matrix():
    seg = np.arange(SEG_BLOCK) // HEAD_DIM
    return jnp.asarray((seg[:, None] == seg[None, :]) / HEAD_DIM, BF16)


def _token_mixing(x, mod3, w, cache, *, mod_base, mod_stride, tm, tm_post, heads, tq, tk,
                  cast_pre=None, cast_attn=None, cast_post=None):
    batch, seq, _ = x.shape
    n = batch * seq
    x2d = x.reshape(n, D_MODEL)
    rope_tabs = _rope_tables(seq) if cache is not None else None
    casts = {}
    pre = list(_pre_call(x2d, mod3[:, :, :2 * D_MODEL], w["g1"], w["win"], w["seg"], w["qg"],
                         w["kg"], rope_tabs, seq=seq, tm=tm, mod_base=mod_base,
                         mod_stride=mod_stride, emit_f32_kv=cache is None, cast_w=cast_pre,
                         row_blocks=tm // PRE_ROW_BLOCK))
    if cast_pre is not None:
        casts["pre"] = pre.pop()
    p, q, k, v = pre[:4]
    attn = list(_attn_call(w["lam4"], w["subln"], q, k, v, cache, batch=batch, seq=seq,
                           heads=heads, tq=tq, tw=256, tk=tk, cast_w=cast_attn))
    if cast_attn is not None:
        casts["attn"] = attn.pop()
    post = list(_post_call(x2d, attn[0], p, mod3[:, :, 2 * D_MODEL:5 * D_MODEL], w["wout"],
                           w["wbd"], w["pool_scale"], w["g2"], w["wr_t"], seq=seq, tm=tm_post,
                           mod_base=mod_base, mod_stride=mod_stride, cast_w=cast_post))
    if cast_post is not None:
        casts["post"] = post.pop()
    x1, h2, aff_t, aff_c = post
    cap = CAPACITY_FACTOR * n // N_EXPERTS
    idx, gates = _select_call(aff_t, aff_c, cap=cap)
    return {"x1": x1, "h2": h2, "idx": idx, "gates": gates, "cap": cap, "kv": pre[4:],
            "casts": casts, "shape": (batch, seq), "tm_post": tm_post,
            "mod": (mod_base, mod_stride)}


def _channel_mixing(mixed, mod3, wg, wu, wd, *, moe_tm):
    batch, seq = mixed["shape"]
    cap = mixed["cap"]
    mod_base, mod_stride = mixed["mod"]
    moe_tiles = _moe_call(mixed["idx"].reshape(N_EXPERTS * cap),
                          mixed["gates"].reshape(-1, 1, moe_tm), wg, wu, wd, mixed["h2"],
                          cap=cap, tm=moe_tm)
    y = _final_call(mixed["x1"], moe_tiles, mod3, seq=seq, tm=mixed["tm_post"],
                    mod_base=mod_base, mod_stride=mod_stride)
    return y.reshape(batch, seq, D_MODEL)


def kernel(x_prompt, x_sample, cache_k, cache_v, c, c_ctx, norm1_g, norm2_g, w_ada, b_ada, w_in,
           q_norm_g, k_norm_g, lambda_q1, lambda_k1, lambda_q2, lambda_k2, subln_g, w_pool,
           pool_scale, w_out, w_router, w_gate, w_up, w_down):
    assert w_ada.shape[0] == 1, "single-layer stack"
    batch, seq, _ = x_prompt.shape
    dec_batch, dec_seq, _ = x_sample.shape

    pad = SUBLANES - 1 - dec_batch
    cvec = jnp.concatenate([c_ctx[None, :], c, jnp.zeros((pad, D_MODEL), F32)], axis=0)
    mod = _ada_call(cvec, w_ada[0], b_ada[0])
    mod3 = mod.reshape(SUBLANES, 1, 6 * D_MODEL)

    n_groups = w_pool.shape[1]
    grp = POOL_WIDTH // n_groups
    eye = jnp.eye(n_groups, dtype=F32)
    wbd = (w_pool[0][:, :, None, :] * eye[:, None, :, None]).reshape(POOL_WIDTH, POOL_WIDTH)

    w = {
        "g1": norm1_g[0].reshape(1, D_MODEL),
        "g2": norm2_g[0].reshape(1, D_MODEL),
        "win": w_in[0].astype(BF16),
        "seg": _segment_matrix(),
        "qg": jnp.tile(q_norm_g[0], ATTN_WIDTH // HEAD_DIM).reshape(1, ATTN_WIDTH),
        "kg": jnp.tile(k_norm_g[0], ATTN_WIDTH // HEAD_DIM).reshape(1, ATTN_WIDTH),
        "lam4": jnp.stack([lambda_q1[0], lambda_k1[0], lambda_q2[0], lambda_k2[0]], axis=0),
        "subln": subln_g[0].reshape(V_DIM, 1),
        "wbd": wbd.astype(BF16),
        "pool_scale": pool_scale[0].reshape(1, POOL_WIDTH),
        "wout": w_out[0].astype(BF16),
        "wr_t": w_router[0].T,
    }

    ctx = _token_mixing(x_prompt, mod3, w, None, mod_base=0, mod_stride=0, tm=256, tm_post=512,
                        heads=N_HEADS, tq=256, tk=256)
    past = cache_k.shape[2]
    cv = cache_v[:, 0].reshape(dec_batch * past, N_HEADS, V_DIM).transpose(1, 2, 0).astype(BF16)
    cv = jnp.concatenate([cv, jnp.ones((N_HEADS, V_EXT - V_DIM, dec_batch * past), BF16)], axis=1)
    cache = (cache_k[:, 0].reshape(dec_batch * past, ATTN_WIDTH).astype(BF16),
             cv.reshape(N_HEADS * V_EXT, dec_batch * past))
    lat = _token_mixing(x_sample, mod3, w, cache, mod_base=1, mod_stride=1, tm=512, tm_post=512,
                        heads=1, tq=1024, tk=512, cast_pre=w_gate[0], cast_attn=w_down[0],
                        cast_post=w_up[0])
    wg, wu, wd = lat["casts"]["pre"], lat["casts"]["post"], lat["casts"]["attn"]
    yp = _channel_mixing(ctx, mod3, wg, wu, wd, moe_tm=512)
    ys = _channel_mixing(lat, mod3, wg, wu, wd, moe_tm=256)
    k_ctx, v_ctx = ctx["kv"]
    ctx_k =(k_ctx.reshape(batch, N_HEADS, 2, HEAD_DIM, seq).transpose(0, 4, 1, 2, 3)
             .reshape(batch, 1, seq, N_HEADS, 2, HEAD_DIM))
    ctx_v = v_ctx.transpose(0, 2, 1, 3).reshape(batch, 1, seq, N_HEADS, V_DIM)
    return yp, ys, ctx_k, ctx_v
```

```python
import functools
import math

import numpy as np
import jax
import jax.numpy as jnp
from jax import lax
from jax.experimental import pallas as pl
from jax.experimental.pallas import tpu as pltpu

F32 = jnp.float32
BF16 = jnp.bfloat16

D_MODEL = 1024
POOL_WIDTH = 256
ATTN_WIDTH = 768
N_HEADS = 6
HEAD_DIM = 64
V_DIM = 128
IN_WIDTH = POOL_WIDTH + 3 * ATTN_WIDTH
N_EXPERTS = 16
CAPACITY_FACTOR = 2
GRID_W = 64
ROPE_BASE = 10000.0
EPS = 1e-6
LAMBDA_INIT = 0.8 - 0.6 * math.exp(-0.3 * 0)
LOG2E = math.log2(math.e)
V_EXT = V_DIM + 16
PRE_ROW_BLOCK = 256
SELECT_GROUP = 8
SEG_BLOCK = 256
SCORES_AHEAD = 3

LANES = 128
SUBLANES = 8
ROW_TILES = D_MODEL // LANES
HALO = 16
VMEM_LIMIT = 56 * 1024 * 1024


def _cparams(sem, vmem=None):
    return pltpu.CompilerParams(dimension_semantics=sem, vmem_limit_bytes=vmem)


def _ada_kernel(c_ref, w_ref, b_ref, o_ref):
    c = c_ref[...]
    s = c * jax.nn.sigmoid(c)
    o_ref[...] = jnp.dot(s.astype(BF16), w_ref[...].astype(BF16),
                         preferred_element_type=F32) + b_ref[...]


def _ada_call(cvec, w_ada, b_ada):
    rows, d = cvec.shape
    n = w_ada.shape[1]
    bn = 1536
    return pl.pallas_call(
        _ada_kernel,
        grid=(n // bn,),
        in_specs=[pl.BlockSpec((rows, d), lambda j: (0, 0)),
                  pl.BlockSpec((d, bn), lambda j: (0, j)),
                  pl.BlockSpec((1, bn), lambda j: (0, j))],
        out_specs=pl.BlockSpec((rows, bn), lambda j: (0, j)),
        out_shape=jax.ShapeDtypeStruct((rows, n), F32),
        compiler_params=_cparams(("arbitrary",)),
        name="ada",
    )(cvec, w_ada, b_ada.reshape(1, n))


def _segment_mean_square(a, seg_ref):
    sq = (a * a).astype(BF16)
    seg = seg_ref[...]
    return jnp.concatenate(
        [jnp.dot(sq[:, j:j + SEG_BLOCK], seg, preferred_element_type=F32)
         for j in range(0, a.shape[1], SEG_BLOCK)], axis=1)


def _rope(a, cos, sin_signed, first_half):
    parts = []
    for h in range(a.shape[1] // LANES):
        blk = a[:, h * LANES:(h + 1) * LANES]
        fwd = pltpu.roll(blk, LANES - HEAD_DIM // 4, 1)
        bwd = pltpu.roll(blk, HEAD_DIM // 4, 1)
        parts.append(blk * cos + jnp.where(first_half, fwd, bwd) * sin_signed)
    return jnp.concatenate(parts, axis=1)


def _side_cast(refs, n_inputs, enabled, step=None, steps_per_block=1):
    if not enabled:
        return refs
    src_ref, dst_ref = refs[n_inputs - 1], refs[-1]

    def convert():
        dst_ref[...] = src_ref[...].astype(BF16)

    if steps_per_block == 1:
        convert()
    else:
        pl.when(step % steps_per_block == 0)(convert)
    return refs[:n_inputs - 1] + refs[n_inputs:-1]


def _with_side_cast(w, index_map, in_specs, args, out_specs, out_shapes):
    if w is None:
        return
    _, rows, cols = w.shape
    in_specs.append(pl.BlockSpec((None, rows, cols), index_map))
    args.append(w)
    out_specs.append(pl.BlockSpec((None, rows, cols), index_map))
    out_shapes.append(jax.ShapeDtypeStruct(w.shape, BF16))


def _pre_kernel(*refs, rope, emit_f32_kv, side_cast, row_blocks):
    refs = _side_cast(refs, 7 + (2 if rope else 0) + 1, side_cast)
    x_ref, mod_ref, g1_ref, win_ref, seg_ref, qg_ref, kg_ref = refs[:7]
    pos = 7
    if rope:
        cos_ref, sin_ref = refs[pos:pos + 2]
        pos += 2
    p_ref, q_ref, k_ref, v_ref = refs[pos:pos + 4]
    pos += 4
    if emit_f32_kv:
        kf_ref, vf_ref = refs[pos:pos + 2]

    mod = mod_ref[0]
    shift1 = mod[:, :D_MODEL]
    scale1 = mod[:, D_MODEL:2 * D_MODEL]
    tm = x_ref.shape[0]
    rb = tm // row_blocks

    def project(j):
        x = x_ref[j * rb:(j + 1) * rb, :]
        ms = jnp.mean(x * x, axis=1, keepdims=True)
        h = x * lax.rsqrt(ms + EPS) * g1_ref[...] * (1.0 + scale1) + shift1
        return jnp.dot(h.astype(BF16), win_ref[...], preferred_element_type=F32)

    def head_stats(z):
        qz = z[:, POOL_WIDTH:POOL_WIDTH + ATTN_WIDTH]
        kz = z[:, POOL_WIDTH + ATTN_WIDTH:POOL_WIDTH + 2 * ATTN_WIDTH]
        return _segment_mean_square(qz, seg_ref), _segment_mean_square(kz, seg_ref)

    def finish(j, z, stats):
        rows = slice(j * rb, (j + 1) * rb)
        p_ref[rows, :] = z[:, :POOL_WIDTH]
        qz = z[:, POOL_WIDTH:POOL_WIDTH + ATTN_WIDTH]
        kz = z[:, POOL_WIDTH + ATTN_WIDTH:POOL_WIDTH + 2 * ATTN_WIDTH]
        vz = z[:, POOL_WIDTH + 2 * ATTN_WIDTH:]
        qn = qz * lax.rsqrt(stats[0] + EPS) * qg_ref[...]
        kn = kz * lax.rsqrt(stats[1] + EPS) * kg_ref[...]
        if rope:
            cos = cos_ref[rows, :]
            sin_signed = sin_ref[rows, :]
            lane = lax.broadcasted_iota(jnp.int32, cos.shape, 1)
            first_half = (lane % (HEAD_DIM // 2)) < (HEAD_DIM // 4)
            qn = _rope(qn, cos, sin_signed, first_half)
            kn = _rope(kn, cos, sin_signed, first_half)
        q_ref[:, rows] = (qn * (LOG2E / math.sqrt(HEAD_DIM))).T.astype(BF16)
        k_ref[rows, :] = kn.astype(BF16)
        vt = vz.T
        ones = jnp.ones((V_EXT - V_DIM, rb), BF16)
        for h in range(N_HEADS):
            v_ref[h * V_EXT:h * V_EXT + V_DIM, rows] = vt[h * V_DIM:(h + 1) * V_DIM, :].astype(BF16)
            v_ref[h * V_EXT + V_DIM:(h + 1) * V_EXT, rows] = ones
        if emit_f32_kv:
            kf_ref[0, :, rows] = kn.T
            for h in range(N_HEADS):
                vf_ref[0, h, rows, :] = vz[:, h * V_DIM:(h + 1) * V_DIM]

    z_prev = project(0)
    for j in range(row_blocks):
        stats = head_stats(z_prev)
        z_next = project(j + 1) if j + 1 < row_blocks else None
        finish(j, z_prev, stats)
        z_prev = z_next


def _pre_call(x2d, mod3, g1, win_bf, seg, qg, kg, rope_tabs, *, seq, tm, mod_base, mod_stride,
              emit_f32_kv, cast_w=None, row_blocks=1):
    n = x2d.shape[0]
    rope = rope_tabs is not None
    tiles_per_seq = seq // tm

    def mod_map(i):
        return (mod_base + mod_stride * (i // tiles_per_seq), 0, 0)

    in_specs = [
        pl.BlockSpec((tm, D_MODEL), lambda i: (i, 0)),
        pl.BlockSpec((1, 1, 2 * D_MODEL), mod_map),
        pl.BlockSpec((1, D_MODEL), lambda i: (0, 0)),
        pl.BlockSpec((D_MODEL, IN_WIDTH), lambda i: (0, 0)),
        pl.BlockSpec((SEG_BLOCK, SEG_BLOCK), lambda i: (0, 0)),
        pl.BlockSpec((1, ATTN_WIDTH), lambda i: (0, 0)),
        pl.BlockSpec((1, ATTN_WIDTH), lambda i: (0, 0)),
    ]
    args = [x2d, mod3, g1, win_bf, seg, qg, kg]
    if rope:
        in_specs += [pl.BlockSpec((tm, LANES), lambda i: (i % tiles_per_seq, 0))] * 2
        args += list(rope_tabs)
    out_shapes = [jax.ShapeDtypeStruct((n, POOL_WIDTH), F32)]
    out_specs = [pl.BlockSpec((tm, POOL_WIDTH), lambda i: (i, 0))]
    out_shapes += [jax.ShapeDtypeStruct((ATTN_WIDTH, n), BF16),
                   jax.ShapeDtypeStruct((n, ATTN_WIDTH), BF16),
                   jax.ShapeDtypeStruct((N_HEADS * V_EXT, n), BF16)]
    out_specs += [pl.BlockSpec((ATTN_WIDTH, tm), lambda i: (0, i)),
                  pl.BlockSpec((tm, ATTN_WIDTH), lambda i: (i, 0)),
                  pl.BlockSpec((N_HEADS * V_EXT, tm), lambda i: (0, i))]
    if emit_f32_kv:
        def seq_map(i):
            return (i // tiles_per_seq, 0, i % tiles_per_seq)

        out_shapes += [jax.ShapeDtypeStruct((n // seq, ATTN_WIDTH, seq), F32),
                       jax.ShapeDtypeStruct((n // seq, N_HEADS, seq, V_DIM), F32)]
        out_specs += [pl.BlockSpec((1, ATTN_WIDTH, tm), seq_map),
                      pl.BlockSpec((1, N_HEADS, tm, V_DIM),
                                   lambda i: (i // tiles_per_seq, 0, i % tiles_per_seq, 0))]
    if cast_w is not None:
        assert cast_w.shape[0] == n // tm, "one expert matrix per grid step"
    _with_side_cast(cast_w, lambda i: (i, 0, 0), in_specs, args, out_specs, out_shapes)
    return pl.pallas_call(
        functools.partial(_pre_kernel, rope=rope, emit_f32_kv=emit_f32_kv,
                          side_cast=cast_w is not None, row_blocks=row_blocks),
        grid=(n // tm,),
        in_specs=in_specs,
        out_specs=out_specs,
        out_shape=out_shapes,
        compiler_params=_cparams(("arbitrary",), VMEM_LIMIT),
        name="pre_rope" if rope else "pre",
    )(*args)


def _attn_kernel(*refs, has_cache, heads, tq, tw, tk, seq, cache_len, side_cast):
    step = ((pl.program_id(0) * pl.num_programs(1) + pl.program_id(1)) * pl.num_programs(2)
            + pl.program_id(2))
    refs = _side_cast(refs, (7 if has_cache else 5) + 1, side_cast > 0, step, side_cast)
    if has_cache:
        lam_ref, g_ref, q_ref, kc_ref, vc_ref, k_ref, v_ref, o_ref = refs
    else:
        lam_ref, g_ref, q_ref, k_ref, v_ref, o_ref = refs

    lv = lam_ref[...]
    lam = (jnp.exp(jnp.sum(lv[0:1] * lv[1:2], axis=1, keepdims=True))
           - jnp.exp(jnp.sum(lv[2:3] * lv[3:4], axis=1, keepdims=True)) + LAMBDA_INIT)

    row = lax.broadcasted_iota(jnp.int32, (LANES, tq), 0)
    zero = jnp.zeros((LANES, tq), BF16)

    def sub_queries(h):
        qt = q_ref[h * LANES:(h + 1) * LANES, :]
        return (jnp.where(row < HEAD_DIM, qt, zero), jnp.where(row >= HEAD_DIM, qt, zero))

    def scores(kb, q_one):
        return jnp.dot(kb, q_one, preferred_element_type=F32)

    def softmax_step(s, m):
        m_new = jnp.maximum(m, jnp.max(s, axis=0, keepdims=True))
        return m_new, jnp.exp2(m - m_new), jnp.exp2(s - m_new).astype(BF16)

    def accumulate(vb, p, alpha, acc):
        return alpha * acc + jnp.dot(vb, p, preferred_element_type=F32)

    chunks = []
    if has_cache:
        chunks += [(kc_ref, vc_ref, j) for j in range(cache_len // tk)]
    chunks += [(k_ref, v_ref, j) for j in range(seq // tk)]

    def keys(c, h):
        kr, _, j = chunks[c]
        return kr[j * tk:(j + 1) * tk, h * LANES:(h + 1) * LANES]

    def values(c, h):
        _, vr, j = chunks[c]
        return vr[h * V_EXT:(h + 1) * V_EXT, j * tk:(j + 1) * tk]

    chains = []
    for h in range(heads):
        q_sub = sub_queries(h)
        chains += [(h, q_sub[sub][:, w * tw:(w + 1) * tw])
                   for w in range(tq // tw) for sub in range(2)]
    per_head = len(chains) // heads
    items = [(c, ch) for h in range(heads) for c in range(len(chunks))
             for ch in range(h * per_head, (h + 1) * per_head)]
    m = [jnp.full((1, tw), -jnp.inf, F32)] * len(chains)
    acc = [jnp.zeros((V_EXT, tw), F32)] * len(chains)
    queue = [scores(keys(c, chains[ch][0]), chains[ch][1]) for c, ch in items[:SCORES_AHEAD]]
    for i, (c, ch) in enumerate(items):
        s_cur = queue.pop(0)
        if i + SCORES_AHEAD < len(items):
            nc, nch = items[i + SCORES_AHEAD]
            queue.append(scores(keys(nc, chains[nch][0]), chains[nch][1]))
        m[ch], alpha, p = softmax_step(s_cur, m[ch])
        acc[ch] = accumulate(values(c, chains[ch][0]), p, alpha, acc[ch])
    for h in range(heads):
        outs = []
        for w in range(tq // tw):
            a1, a2 = acc[h * per_head + 2 * w], acc[h * per_head + 2 * w + 1]
            outs.append(a1[:V_DIM] / a1[V_DIM:V_DIM + 1]
                        - lam * (a2[:V_DIM] / a2[V_DIM:V_DIM + 1]))
        o = outs[0] if len(outs) == 1 else jnp.concatenate(outs, axis=1)
        y = (o * lax.rsqrt(jnp.mean(o * o, axis=0, keepdims=True) + EPS) * g_ref[...]
             * (1.0 - LAMBDA_INIT))
        o_ref[:, h * LANES:(h + 1) * LANES] = y.T.astype(BF16)


def _attn_call(lam4, subln_col, qt, k, vt, cache, *, batch, seq, heads, tq, tw, tk, cast_w=None):
    has_cache = cache is not None
    nq = seq // tq
    in_specs = [
        pl.BlockSpec((4, HEAD_DIM), lambda b, h, i: (0, 0)),
        pl.BlockSpec((V_DIM, 1), lambda b, h, i: (0, 0)),
        pl.BlockSpec((heads * LANES, tq), lambda b, h, i: (h, b * nq + i)),
    ]
    args = [lam4, subln_col, qt]
    cache_len = 0
    if has_cache:
        kc, vct = cache
        cache_len = kc.shape[0] // batch
        in_specs += [pl.BlockSpec((cache_len, heads * LANES), lambda b, h, i: (b, h)),
                     pl.BlockSpec((heads * V_EXT, cache_len), lambda b, h, i: (h, b))]
        args += [kc, vct]
    in_specs += [pl.BlockSpec((seq, heads * LANES), lambda b, h, i: (b, h)),
                 pl.BlockSpec((heads * V_EXT, seq), lambda b, h, i: (h, b))]
    args += [k, vt]
    out_specs = [pl.BlockSpec((tq, heads * LANES), lambda b, h, i: (b * nq + i, h))]
    out_shapes = [jax.ShapeDtypeStruct((batch * seq, ATTN_WIDTH), BF16)]
    grid = (batch, N_HEADS // heads, nq)
    steps_per_block = 0
    if cast_w is not None:
        steps_per_block, rem = divmod(grid[0] * grid[1] * grid[2], cast_w.shape[0])
        assert rem == 0 and steps_per_block > 0, "whole number of grid steps per expert matrix"
    _with_side_cast(
        cast_w, lambda b, h, i: (((b * grid[1] + h) * grid[2] + i) // steps_per_block, 0, 0),
        in_specs, args, out_specs, out_shapes)
    return pl.pallas_call(
        functools.partial(_attn_kernel, has_cache=has_cache, heads=heads, tq=tq, tw=tw, tk=tk,
                          seq=seq, cache_len=cache_len, side_cast=steps_per_block),
        grid=grid,
        in_specs=in_specs,
        out_specs=out_specs,
        out_shape=out_shapes,
        compiler_params=_cparams(("arbitrary", "arbitrary", "arbitrary"), VMEM_LIMIT),
        name="attn_cache" if has_cache else "attn",
    )(*args)


def _post_kernel(*refs, tm, seq, side_cast):
    (x_ref, a_ref, p_ref, pp_ref, pn_ref, mod_ref, wout_ref, wbd_ref, ps_ref, g2_ref, wr_ref,
     x1_ref, h2_ref, aff_ref, affc_ref) = _side_cast(refs, 12, side_cast)
    i = pl.program_id(0)

    def centred_mean_minus_self(rows, prev, nxt, t0):
        r = rows.shape[0]
        ext = jnp.concatenate([prev, rows, nxt], axis=0)
        n_ext = r + 2 * HALO
        s2 = ext + pltpu.roll(ext, 1, 0)
        s4 = pltpu.roll(s2, 1, 0) + pltpu.roll(s2, n_ext - 1, 0)
        s8 = pltpu.roll(s4, 2, 0) + pltpu.roll(s4, n_ext - 2, 0)
        s16 = pltpu.roll(s8, 4, 0) + pltpu.roll(s8, n_ext - 4, 0)
        lane = lax.broadcasted_iota(jnp.int32, (r, POOL_WIDTH), 1)
        grp = lane // (POOL_WIDTH // 4)
        win = jnp.where(grp == 0, s2[HALO:HALO + r],
                        jnp.where(grp == 1, s4[HALO:HALO + r],
                                  jnp.where(grp == 2, s8[HALO:HALO + r], s16[HALO:HALO + r])))
        t = t0 + lax.broadcasted_iota(jnp.int32, (r, POOL_WIDTH), 0)
        left = jnp.where(grp == 0, 1, jnp.where(grp == 1, 2, jnp.where(grp == 2, 4, 8)))
        lo = jnp.maximum(t - left, 0)
        hi = jnp.minimum(t + left - 1, seq - 1) + 1
        return win / (hi - lo).astype(F32) - rows

    if tm <= seq:
        tiles_per_seq = seq // tm
        ti = i % tiles_per_seq
        pooled = centred_mean_minus_self(
            p_ref[...], jnp.where(ti == 0, 0.0, pp_ref[...]),
            jnp.where(ti == tiles_per_seq - 1, 0.0, pn_ref[...]), ti * tm)
    else:
        halo = jnp.zeros((HALO, POOL_WIDTH), F32)
        pooled = jnp.concatenate(
            [centred_mean_minus_self(p_ref[j * seq:(j + 1) * seq, :], halo, halo, 0)
             for j in range(tm // seq)], axis=0)
    pool = jnp.dot(pooled.astype(BF16), wbd_ref[...], preferred_element_type=F32) * ps_ref[...]

    cat = jnp.concatenate([pool.astype(BF16), a_ref[...]], axis=1)
    mix = jnp.dot(cat, wout_ref[...], preferred_element_type=F32)
    mod = mod_ref[0]
    gate1 = mod[:, 0:D_MODEL]
    shift2 = mod[:, D_MODEL:2 * D_MODEL]
    scale2 = mod[:, 2 * D_MODEL:3 * D_MODEL]
    x1 = x_ref[...] + gate1 * mix
    x1_ref[...] = x1
    ms = jnp.mean(x1 * x1, axis=1, keepdims=True)
    h2 = x1 * lax.rsqrt(ms + EPS) * g2_ref[...] * (1.0 + scale2) + shift2
    for s in range(ROW_TILES):
        h2_ref[pl.ds(s, tm, stride=ROW_TILES), :] = h2[:, s * LANES:(s + 1) * LANES]

    def pieces(v):
        hi = v.astype(BF16)
        return hi, (v - hi.astype(F32)).astype(BF16)

    def contract(a, b):
        return lax.dot_general(a, b, (((1,), (1,)), ((), ())), preferred_element_type=F32)

    w_hi, w_lo = pieces(wr_ref[...])
    h_hi, h_lo = pieces(h2)
    logits = contract(w_hi, h_hi) + (contract(w_hi, h_lo) + contract(w_lo, h_hi))
    e = jnp.exp(logits - jnp.max(logits, axis=0, keepdims=True))
    aff = e / jnp.sum(e, axis=0, keepdims=True)
    aff_ref[...] = aff
    for c in range(tm // LANES):
        affc_ref[c * N_EXPERTS:(c + 1) * N_EXPERTS, :] = aff[:, c * LANES:(c + 1) * LANES]


def _post_call(x2d, attn, p, mod3, wout_bf, wbd_bf, pool_scale, g2, wr_t, *, seq, tm, mod_base,
               mod_stride, cast_w=None):
    n = x2d.shape[0]
    assert cast_w is None or cast_w.shape[0] == n // tm, "one expert matrix per grid step"
    assert tm <= seq or mod_stride == 0, "a tile spanning sequences needs one modulation row"
    halo_per_tile = tm // HALO
    n_halo = n // HALO

    def mod_map(i):
        return (mod_base + mod_stride * (i * tm // seq), 0, 0)

    in_specs = [
            pl.BlockSpec((tm, D_MODEL), lambda i: (i, 0)),
            pl.BlockSpec((tm, ATTN_WIDTH), lambda i: (i, 0)),
            pl.BlockSpec((tm, POOL_WIDTH), lambda i: (i, 0)),
            pl.BlockSpec((HALO, POOL_WIDTH), lambda i: (jnp.maximum(i * halo_per_tile - 1, 0), 0)),
            pl.BlockSpec((HALO, POOL_WIDTH),
                         lambda i: (jnp.minimum((i + 1) * halo_per_tile, n_halo - 1), 0)),
            pl.BlockSpec((1, 1, 3 * D_MODEL), lambda i: mod_map(i)[:2] + (0,)),
            pl.BlockSpec((D_MODEL, D_MODEL), lambda i: (0, 0)),
            pl.BlockSpec((POOL_WIDTH, POOL_WIDTH), lambda i: (0, 0)),
            pl.BlockSpec((1, POOL_WIDTH), lambda i: (0, 0)),
            pl.BlockSpec((1, D_MODEL), lambda i: (0, 0)),
            pl.BlockSpec((N_EXPERTS, D_MODEL), lambda i: (0, 0)),
    ]
    out_specs = [
            pl.BlockSpec((tm, D_MODEL), lambda i: (i, 0)),
            pl.BlockSpec((tm * ROW_TILES, LANES), lambda i: (i, 0)),
            pl.BlockSpec((N_EXPERTS, tm), lambda i: (0, i)),
            pl.BlockSpec((tm // LANES * N_EXPERTS, LANES), lambda i: (i, 0)),
    ]
    out_shapes = [
            jax.ShapeDtypeStruct((n, D_MODEL), F32),
            jax.ShapeDtypeStruct((n * ROW_TILES, LANES), F32),
            jax.ShapeDtypeStruct((N_EXPERTS, n), F32),
            jax.ShapeDtypeStruct((n // LANES * N_EXPERTS, LANES), F32),
    ]
    args = [x2d, attn, p, p, p, mod3, wout_bf, wbd_bf, pool_scale, g2, wr_t]
    _with_side_cast(cast_w, lambda i: (i, 0, 0), in_specs, args, out_specs, out_shapes)
    return pl.pallas_call(
        functools.partial(_post_kernel, tm=tm, seq=seq, side_cast=cast_w is not None),
        grid=(n // tm,),
        in_specs=in_specs,
        out_specs=out_specs,
        out_shape=out_shapes,
        compiler_params=_cparams(("arbitrary",), VMEM_LIMIT),
        name="post",
    )(*args)


def _select_kernel(a_ref, ac_ref, idx_ref, gate_ref, *, n, cap):
    nc = n // LANES
    a = a_ref[...]
    thr = jnp.zeros((N_EXPERTS, 1), jnp.int32)
    for bit in range(30, -1, -1):
        cand = thr | (1 << bit)
        cnt = jnp.sum(jnp.where(a >= pltpu.bitcast(cand, F32), 1.0, 0.0), axis=1, keepdims=True)
        thr = jnp.where(cnt >= cap, cand, thr)
    thr_all = pltpu.bitcast(thr, F32)
    need_all = cap - jnp.sum(jnp.where(a > thr_all, 1.0, 0.0), axis=1, keepdims=True)

    r = lax.broadcasted_iota(jnp.int32, (LANES, LANES), 0)
    c = lax.broadcasted_iota(jnp.int32, (LANES, LANES), 1)
    upper = jnp.where(r <= c, 1.0, 0.0).astype(BF16)
    lower = jnp.where(c < r, 1.0, 0.0).astype(BF16)
    row_valid = r < nc
    chunk_col = lax.broadcasted_iota(jnp.int32, (LANES, 1), 0).astype(F32)
    slot = lax.broadcasted_iota(jnp.int32, (1, cap), 1).astype(F32)

    def lane_counts(mask):
        local = jnp.dot(mask.astype(BF16), upper, preferred_element_type=F32)
        total = jnp.broadcast_to(local[:, LANES - 1:LANES], (LANES, LANES))
        before = jnp.dot(lower, total.astype(BF16), preferred_element_type=F32)
        return local, total, before

    def ties(e):
        av = ac_ref[pl.ds(e, nc, stride=N_EXPERTS), :]
        if nc < LANES:
            av = jnp.concatenate([av, jnp.zeros((LANES - nc, LANES), F32)], axis=0)
        thr_e = thr_all[e:e + 1, :]
        above = jnp.where(row_valid & (av > thr_e), 1.0, 0.0)
        tied = jnp.where(row_valid & (av == thr_e), 1.0, 0.0)
        t_local, _, t_before = lane_counts(tied)
        return {"av": av, "above": above, "tied": tied, "tie_rank": t_local + t_before}

    def selection(e, st):
        sel = st["above"] + st["tied"] * jnp.where(st["tie_rank"] <= need_all[e:e + 1, :], 1.0, 0.0)
        s_local, s_total, s_before = lane_counts(sel)
        rank = jnp.where(sel > 0.0, s_local, 0.0)
        start = s_before[:, 0:1]
        return {"av": st["av"], "rank": rank, "start": start, "stop": start + s_total[:, 0:1]}

    def pick(e, st):
        start = st["start"]
        onehot = jnp.where((slot >= start) & (slot < st["stop"]), 1.0, 0.0)
        chunk_of_slot = jnp.sum(onehot * chunk_col, axis=0, keepdims=True)
        start_of_slot = jnp.sum(onehot * start, axis=0, keepdims=True)
        at = st["av"].T
        hi = at.astype(BF16)
        rest = at - hi.astype(F32)
        mid = rest.astype(BF16)
        lo = (rest - mid.astype(F32)).astype(BF16)
        lhs = jnp.concatenate([st["rank"].T.astype(BF16), hi, mid, lo], axis=0)
        picked = jnp.dot(lhs, onehot.astype(BF16), preferred_element_type=F32)
        return {"picked": picked, "chunk_of_slot": chunk_of_slot, "start_of_slot": start_of_slot}

    def emit(e, st):
        picked = st["picked"]
        rank_p = picked[0:LANES]
        aff_p = picked[LANES:2 * LANES] + picked[2 * LANES:3 * LANES] + picked[3 * LANES:]
        hit = rank_p == (slot - st["start_of_slot"] + 1.0)
        lane_of_slot = jnp.sum(jnp.where(hit, chunk_col, 0.0), axis=0, keepdims=True)
        idx_ref[e:e + 1, :] = (st["chunk_of_slot"] * LANES + lane_of_slot).astype(jnp.int32)
        gate_ref[e:e + 1, :] = jnp.sum(jnp.where(hit, aff_p, 0.0), axis=0, keepdims=True)

    for e0 in range(0, N_EXPERTS, SELECT_GROUP):
        group = range(e0, e0 + SELECT_GROUP)
        states = {e: ties(e) for e in group}
        states = {e: selection(e, states[e]) for e in group}
        states = {e: pick(e, states[e]) for e in group}
        for e in group:
            emit(e, states[e])


def _select_call(aff_t, aff_c, *, cap):
    n = aff_t.shape[1]
    assert n % LANES == 0 and n // LANES <= LANES and cap % LANES == 0
    return pl.pallas_call(
        functools.partial(_select_kernel, n=n, cap=cap),
        grid=(1,),
        in_specs=[pl.BlockSpec((N_EXPERTS, n), lambda i: (0, 0)),
                  pl.BlockSpec(aff_c.shape, lambda i: (0, 0))],
        out_specs=[pl.BlockSpec((N_EXPERTS, cap), lambda i: (0, 0))] * 2,
        out_shape=[jax.ShapeDtypeStruct((N_EXPERTS, cap), jnp.int32),
                   jax.ShapeDtypeStruct((N_EXPERTS, cap), F32)],
        compiler_params=_cparams(("arbitrary",), VMEM_LIMIT),
        name="select",
    )(aff_t, aff_c)


def _moe_kernel(idx_ref, gate_ref, wg_ref, wu_ref, wd_ref, x_hbm, out_hbm,
                gbuf, ybuf, acc_ref, gsem, osem, *, tm, n_tiles_total):
    e = pl.program_id(0)
    t = pl.program_id(1)
    nt = pl.num_programs(1)
    step = e * nt + t
    group = 8
    last = n_tiles_total - 1
    rows = tm * ROW_TILES

    def gather_start(tile_step, k, dst_slot):
        tok = idx_ref[tile_step * tm + k]
        pltpu.make_async_copy(
            x_hbm.at[pl.ds(pl.multiple_of(tok * ROW_TILES, ROW_TILES), ROW_TILES), :],
            gbuf.at[dst_slot, pl.ds(k * ROW_TILES, ROW_TILES), :], gsem.at[dst_slot]).start()

    def gather_wait(dst_slot):
        pltpu.make_async_copy(x_hbm.at[pl.ds(0, rows), :], gbuf.at[dst_slot],
                              gsem.at[dst_slot]).wait()

    def scatter_add(tile_step, src_slot, k0):
        pending = []
        for r in range(group):
            k = k0 + r
            tok = idx_ref[tile_step * tm + k]
            off = pl.multiple_of(tok * ROW_TILES, ROW_TILES)
            src = pl.multiple_of(k * ROW_TILES, ROW_TILES)
            pending.append((off, acc_ref[pl.ds(off, ROW_TILES), :]
                            + ybuf[src_slot, pl.ds(src, ROW_TILES), :]))
        for off, val in pending:
            acc_ref[pl.ds(off, ROW_TILES), :] = val

    @pl.when(step == 0)
    def _():
        acc_ref[...] = jnp.zeros_like(acc_ref)
        ybuf[...] = jnp.zeros_like(ybuf)

        def body(k, _):
            gather_start(0, k, 0)
            return 0
        lax.fori_loop(0, tm, body, 0)

    def tile_body(slot):
        nxt = jnp.minimum(step + 1, last)
        prev = jnp.maximum(step - 1, 0)
        for k in range(tm):
            gather_start(nxt, k, 1 - slot)
        gather_wait(slot)
        xe = jnp.concatenate(
            [gbuf[slot, pl.ds(s, tm, stride=ROW_TILES), :] for s in range(ROW_TILES)],
            axis=1).astype(BF16)
        g = jnp.dot(xe, wg_ref[...], preferred_element_type=F32)
        for k0 in range(0, tm // 2, group):
            scatter_add(prev, 1 - slot, k0)
        u = jnp.dot(xe, wu_ref[...], preferred_element_type=F32)
        for k0 in range(tm // 2, tm, group):
            scatter_add(prev, 1 - slot, k0)
        hid = (g * jax.nn.sigmoid(g) * u).astype(BF16)
        y = jnp.dot(hid, wd_ref[...], preferred_element_type=F32)
        gate = jnp.broadcast_to(gate_ref[...], (LANES, tm)).T
        for s in range(ROW_TILES):
            ybuf[slot, pl.ds(s, tm, stride=ROW_TILES), :] = y[:, s * LANES:(s + 1) * LANES] * gate

        @pl.when(step == last)
        def _():
            gather_wait(1 - slot)

            def body(kk, _):
                scatter_add(step, slot, kk * group)
                return 0
            lax.fori_loop(0, tm // group, body, 0)
            cp = pltpu.make_async_copy(acc_ref, out_hbm, osem)
            cp.start()
            cp.wait()

    for parity in range(2):
        pl.when(step % 2 == parity)(functools.partial(tile_body, parity))


def _moe_call(idx_flat, gates3, wg_bf, wu_bf, wd_bf, h2, *, cap, tm):
    n_rows = h2.shape[0]
    nt = cap // tm
    grid_spec = pltpu.PrefetchScalarGridSpec(
        num_scalar_prefetch=1,
        grid=(N_EXPERTS, nt),
        in_specs=[
            pl.BlockSpec((None, 1, tm), lambda e, t, idx: (e * nt + t, 0, 0)),
            pl.BlockSpec((None, D_MODEL, D_MODEL), lambda e, t, idx: (e, 0, 0)),
            pl.BlockSpec((None, D_MODEL, D_MODEL), lambda e, t, idx: (e, 0, 0)),
            pl.BlockSpec((None, D_MODEL, D_MODEL), lambda e, t, idx: (e, 0, 0)),
            pl.BlockSpec(memory_space=pl.ANY),
        ],
        out_specs=pl.BlockSpec(memory_space=pl.ANY),
        scratch_shapes=[
            pltpu.VMEM((2, tm * ROW_TILES, LANES), F32),
            pltpu.VMEM((2, tm * ROW_TILES, LANES), F32),
            pltpu.VMEM((n_rows, LANES), F32),
            pltpu.SemaphoreType.DMA((2,)),
            pltpu.SemaphoreType.DMA(()),
        ],
    )
    return pl.pallas_call(
        functools.partial(_moe_kernel, tm=tm, n_tiles_total=N_EXPERTS * nt),
        grid_spec=grid_spec,
        out_shape=jax.ShapeDtypeStruct((n_rows, LANES), F32),
        compiler_params=_cparams(("arbitrary", "arbitrary"), VMEM_LIMIT),
        name="moe",
    )(idx_flat, gates3, wg_bf, wu_bf, wd_bf, h2)


def _final_kernel(x1_ref, moe_ref, mod_ref, o_ref, *, tm):
    moe = jnp.concatenate(
        [moe_ref[pl.ds(s, tm, stride=ROW_TILES), :] for s in range(ROW_TILES)], axis=1)
    o_ref[...] = x1_ref[...] + mod_ref[0] * moe


def _final_call(x1, moe_tiles, mod3, *, seq, tm, mod_base, mod_stride):
    n = x1.shape[0]
    assert tm <= seq or mod_stride == 0, "a tile spanning sequences needs one modulation row"
    gate2_block = 5

    def mod_map(i):
        return (mod_base + mod_stride * (i * tm // seq), 0, gate2_block)

    return pl.pallas_call(
        functools.partial(_final_kernel, tm=tm),
        grid=(n // tm,),
        in_specs=[pl.BlockSpec((tm, D_MODEL), lambda i: (i, 0)),
                  pl.BlockSpec((tm * ROW_TILES, LANES), lambda i: (i, 0)),
                  pl.BlockSpec((1, 1, D_MODEL), mod_map)],
        out_specs=pl.BlockSpec((tm, D_MODEL), lambda i: (i, 0)),
        out_shape=jax.ShapeDtypeStruct((n, D_MODEL), F32),
        compiler_params=_cparams(("arbitrary",)),
        name="final",
    )(x1, moe_tiles, mod3)


def _rope_tables(seq):
    t = np.arange(seq)
    row, col = t // GRID_W, t % GRID_W
    half = HEAD_DIM // 2
    freqs = 1.0 / (ROPE_BASE ** (np.arange(0, half, 2) / half))
    ang_r = row[:, None] * freqs[None, :]
    ang_c = col[:, None] * freqs[None, :]
    ang = np.concatenate([ang_r, ang_r, ang_c, ang_c], axis=-1)
    cos = np.tile(np.cos(ang), (1, LANES // HEAD_DIM))
    sin = np.tile(np.sin(ang), (1, LANES // HEAD_DIM))
    sign = np.where((np.arange(LANES) % (HEAD_DIM // 2)) < (HEAD_DIM // 4), -1.0, 1.0)
    return jnp.asarray(cos, F32), jnp.asarray(sin * sign[None, :], F32)


def _segment_matrix():
    seg = np.arange(SEG_BLOCK) // HEAD_DIM
    return jnp.asarray((seg[:, None] == seg[None, :]) / HEAD_DIM, BF16)


def _token_mixing(x, mod3, w, cache, *, mod_base, mod_stride, tm, tm_post, heads, tq, tk,
                  cast_pre=None, cast_attn=None, cast_post=None):
    batch, seq, _ = x.shape
    n = batch * seq
    x2d = x.reshape(n, D_MODEL)
    rope_tabs = _rope_tables(seq) if cache is not None else None
    casts = {}
    pre = list(_pre_call(x2d, mod3[:, :, :2 * D_MODEL], w["g1"], w["win"], w["seg"], w["qg"],
                         w["kg"], rope_tabs, seq=seq, tm=tm, mod_base=mod_base,
                         mod_stride=mod_stride, emit_f32_kv=cache is None, cast_w=cast_pre,
                         row_blocks=tm // PRE_ROW_BLOCK))
    if cast_pre is not None:
        casts["pre"] = pre.pop()
    p, q, k, v = pre[:4]
    attn = list(_attn_call(w["lam4"], w["subln"], q, k, v, cache, batch=batch, seq=seq,
                           heads=heads, tq=tq, tw=256, tk=tk, cast_w=cast_attn))
    if cast_attn is not None:
        casts["attn"] = attn.pop()
    post = list(_post_call(x2d, attn[0], p, mod3[:, :, 2 * D_MODEL:5 * D_MODEL], w["wout"],
                           w["wbd"], w["pool_scale"], w["g2"], w["wr_t"], seq=seq, tm=tm_post,
                           mod_base=mod_base, mod_stride=mod_stride, cast_w=cast_post))
    if cast_post is not None:
        casts["post"] = post.pop()
    x1, h2, aff_t, aff_c = post
    cap = CAPACITY_FACTOR * n // N_EXPERTS
    idx, gates = _select_call(aff_t, aff_c, cap=cap)
    return {"x1": x1, "h2": h2, "idx": idx, "gates": gates, "cap": cap, "kv": pre[4:],
            "casts": casts, "shape": (batch, seq), "tm_post": tm_post,
            "mod": (mod_base, mod_stride)}


def _channel_mixing(mixed, mod3, wg, wu, wd, *, moe_tm):
    batch, seq = mixed["shape"]
    cap = mixed["cap"]
    mod_base, mod_stride = mixed["mod"]
    moe_tiles = _moe_call(mixed["idx"].reshape(N_EXPERTS * cap),
                          mixed["gates"].reshape(-1, 1, moe_tm), wg, wu, wd, mixed["h2"],
                          cap=cap, tm=moe_tm)
    y = _final_call(mixed["x1"], moe_tiles, mod3, seq=seq, tm=mixed["tm_post"],
                    mod_base=mod_base, mod_stride=mod_stride)
    return y.reshape(batch, seq, D_MODEL)


def kernel(x_prompt, x_sample, cache_k, cache_v, c, c_ctx, norm1_g, norm2_g, w_ada, b_ada, w_in,
           q_norm_g, k_norm_g, lambda_q1, lambda_k1, lambda_q2, lambda_k2, subln_g, w_pool,
           pool_scale, w_out, w_router, w_gate, w_up, w_down):
    assert w_ada.shape[0] == 1, "single-layer stack"
    batch, seq, _ = x_prompt.shape
    dec_batch, dec_seq, _ = x_sample.shape

    pad = SUBLANES - 1 - dec_batch
    cvec = jnp.concatenate([c_ctx[None, :], c, jnp.zeros((pad, D_MODEL), F32)], axis=0)
    mod = _ada_call(cvec, w_ada[0], b_ada[0])
    mod3 = mod.reshape(SUBLANES, 1, 6 * D_MODEL)

    n_groups = w_pool.shape[1]
    grp = POOL_WIDTH // n_groups
    eye = jnp.eye(n_groups, dtype=F32)
    wbd = (w_pool[0][:, :, None, :] * eye[:, None, :, None]).reshape(POOL_WIDTH, POOL_WIDTH)

    w = {
        "g1": norm1_g[0].reshape(1, D_MODEL),
        "g2": norm2_g[0].reshape(1, D_MODEL),
        "win": w_in[0].astype(BF16),
        "seg": _segment_matrix(),
        "qg": jnp.tile(q_norm_g[0], ATTN_WIDTH // HEAD_DIM).reshape(1, ATTN_WIDTH),
        "kg": jnp.tile(k_norm_g[0], ATTN_WIDTH // HEAD_DIM).reshape(1, ATTN_WIDTH),
        "lam4": jnp.stack([lambda_q1[0], lambda_k1[0], lambda_q2[0], lambda_k2[0]], axis=0),
        "subln": subln_g[0].reshape(V_DIM, 1),
        "wbd": wbd.astype(BF16),
        "pool_scale": pool_scale[0].reshape(1, POOL_WIDTH),
        "wout": w_out[0].astype(BF16),
        "wr_t": w_router[0].T,
    }

    ctx = _token_mixing(x_prompt, mod3, w, None, mod_base=0, mod_stride=0, tm=256, tm_post=512,
                        heads=N_HEADS, tq=256, tk=256)
    past = cache_k.shape[2]
    cv = cache_v[:, 0].reshape(dec_batch * past, N_HEADS, V_DIM).transpose(1, 2, 0).astype(BF16)
    cv = jnp.concatenate([cv, jnp.ones((N_HEADS, V_EXT - V_DIM, dec_batch * past), BF16)], axis=1)
    cache = (cache_k[:, 0].reshape(dec_batch * past, ATTN_WIDTH).astype(BF16),
             cv.reshape(N_HEADS * V_EXT, dec_batch * past))
    lat = _token_mixing(x_sample, mod3, w, cache, mod_base=1, mod_stride=1, tm=512, tm_post=512,
                        heads=1, tq=1024, tk=512, cast_pre=w_gate[0], cast_attn=w_down[0],
                        cast_post=w_up[0])
    wg, wu, wd = lat["casts"]["pre"], lat["casts"]["post"], lat["casts"]["attn"]
    yp = _channel_mixing(ctx, mod3, wg, wu, wd, moe_tm=512)
    ys = _channel_mixing(lat, mod3, wg, wu, wd, moe_tm=256)
    k_ctx, v_ctx = ctx["kv"]
    ctx_k =(k_ctx.reshape(batch, N_HEADS, 2, HEAD_DIM, seq).transpose(0, 4, 1, 2, 3)
             .reshape(batch, 1, seq, N_HEADS, 2, HEAD_DIM))
    ctx_v = v_ctx.transpose(0, 2, 1, 3).reshape(batch, 1, seq, N_HEADS, V_DIM)
    return yp, ys, ctx_k, ctx_v
```

```python
import functools
import math

import numpy as np
import jax
import jax.numpy as jnp
from jax import lax
from jax.experimental import pallas as pl
from jax.experimental.pallas import tpu as pltpu

F32 = jnp.float32
BF16 = jnp.bfloat16

D_MODEL = 1024
POOL_WIDTH = 256
POOL_WINDOWS = (2, 4, 8, 16)
assert POOL_WIDTH == 2 * 128 and POOL_WINDOWS == (2, 4, 8, 16)
ATTN_WIDTH = 768
N_HEADS = 6
HEAD_DIM = 64
V_DIM = 128
IN_WIDTH = POOL_WIDTH + 3 * ATTN_WIDTH
N_EXPERTS = 16
CAPACITY_FACTOR = 2
GRID_W = 64
ROPE_BASE = 10000.0
EPS = 1e-6
LAMBDA_INIT = 0.8 - 0.6 * math.exp(-0.3 * 0)
LOG2E = math.log2(math.e)
V_EXT = V_DIM + 16
PRE_ROW_BLOCK = 128
SELECT_GROUP = 8
SEG_BLOCK = 256
SCORES_AHEAD = 3

LANES = 128
SUBLANES = 8
ROW_TILES = D_MODEL // LANES
HALO = 16
VMEM_LIMIT = 56 * 1024 * 1024


def _cparams(sem, vmem=None):
    return pltpu.CompilerParams(dimension_semantics=sem, vmem_limit_bytes=vmem)


def _ada_kernel(c_ref, w_ref, b_ref, o_ref):
    c = c_ref[...]
    s = c * jax.nn.sigmoid(c)
    o_ref[...] = jnp.dot(s.astype(BF16), w_ref[...].astype(BF16),
                         preferred_element_type=F32) + b_ref[...]


def _ada_call(cvec, w_ada, b_ada):
    rows, d = cvec.shape
    n = w_ada.shape[1]
    bn = 1536
    return pl.pallas_call(
        _ada_kernel,
        grid=(n // bn,),
        in_specs=[pl.BlockSpec((rows, d), lambda j: (0, 0)),
                  pl.BlockSpec((d, bn), lambda j: (0, j)),
                  pl.BlockSpec((1, bn), lambda j: (0, j))],
        out_specs=pl.BlockSpec((rows, bn), lambda j: (0, j)),
        out_shape=jax.ShapeDtypeStruct((rows, n), F32),
        compiler_params=_cparams(("arbitrary",)),
        name="ada",
    )(cvec, w_ada, b_ada.reshape(1, n))


def _segment_mean_square(a, seg_ref):
    sq = (a * a).astype(BF16)
    seg = seg_ref[...]
    return jnp.concatenate(
        [jnp.dot(sq[:, j:j + SEG_BLOCK], seg, preferred_element_type=F32)
         for j in range(0, a.shape[1], SEG_BLOCK)], axis=1)


def _rope(a, cos, sin_signed, first_half):
    parts = []
    for h in range(a.shape[1] // LANES):
        blk = a[:, h * LANES:(h + 1) * LANES]
        fwd = pltpu.roll(blk, LANES - HEAD_DIM // 4, 1)
        bwd = pltpu.roll(blk, HEAD_DIM // 4, 1)
        parts.append(blk * cos + jnp.where(first_half, fwd, bwd) * sin_signed)
    return jnp.concatenate(parts, axis=1)


def _side_cast(refs, n_inputs, enabled, step=None, steps_per_block=1):
    if not enabled:
        return refs
    src_ref, dst_ref = refs[n_inputs - 1], refs[-1]

    def convert():
        dst_ref[...] = src_ref[...].astype(BF16)

    if steps_per_block == 1:
        convert()
    else:
        pl.when(step % steps_per_block == 0)(convert)
    return refs[:n_inputs - 1] + refs[n_inputs:-1]


def _with_side_cast(w, index_map, in_specs, args, out_specs, out_shapes):
    if w is None:
        return
    _, rows, cols = w.shape
    in_specs.append(pl.BlockSpec((None, rows, cols), index_map))
    args.append(w)
    out_specs.append(pl.BlockSpec((None, rows, cols), index_map))
    out_shapes.append(jax.ShapeDtypeStruct(w.shape, BF16))


def _pre_kernel(*refs, rope, emit_f32_kv, side_cast, row_blocks):
    refs = _side_cast(refs, 7 + (2 if rope else 0) + 1, side_cast)
    x_ref, mod_ref, g1_ref, win_ref, seg_ref, qg_ref, kg_ref = refs[:7]
    pos = 7
    if rope:
        cos_ref, sin_ref = refs[pos:pos + 2]
        pos += 2
    p_ref, q_ref, k_ref, v_ref = refs[pos:pos + 4]
    pos += 4
    if emit_f32_kv:
        kf_ref, vf_ref = refs[pos:pos + 2]

    mod = mod_ref[0]
    shift1 = mod[:, :D_MODEL]
    scale1 = mod[:, D_MODEL:2 * D_MODEL]
    tm = x_ref.shape[0]
    rb = tm // row_blocks

    def project(j):
        x = x_ref[j * rb:(j + 1) * rb, :]
        ms = jnp.mean(x * x, axis=1, keepdims=True)
        h = x * lax.rsqrt(ms + EPS) * g1_ref[...] * (1.0 + scale1) + shift1
        return jnp.dot(h.astype(BF16), win_ref[...], preferred_element_type=F32)

    def head_stats(z):
        qz = z[:, POOL_WIDTH:POOL_WIDTH + ATTN_WIDTH]
        kz = z[:, POOL_WIDTH + ATTN_WIDTH:POOL_WIDTH + 2 * ATTN_WIDTH]
        return _segment_mean_square(qz, seg_ref), _segment_mean_square(kz, seg_ref)

    def finish(j, z, stats):
        rows = slice(j * rb, (j + 1) * rb)
        p_ref[rows, :] = z[:, :POOL_WIDTH]
        qz = z[:, POOL_WIDTH:POOL_WIDTH + ATTN_WIDTH]
        kz = z[:, POOL_WIDTH + ATTN_WIDTH:POOL_WIDTH + 2 * ATTN_WIDTH]
        vz = z[:, POOL_WIDTH + 2 * ATTN_WIDTH:]
        qn = qz * lax.rsqrt(stats[0] + EPS) * qg_ref[...]
        kn = kz * lax.rsqrt(stats[1] + EPS) * kg_ref[...]
        if rope:
            cos = cos_ref[rows, :]
            sin_signed = sin_ref[rows, :]
            lane = lax.broadcasted_iota(jnp.int32, cos.shape, 1)
            first_half = (lane % (HEAD_DIM // 2)) < (HEAD_DIM // 4)
            qn = _rope(qn, cos, sin_signed, first_half)
            kn = _rope(kn, cos, sin_signed, first_half)
        q_ref[:, rows] = (qn * (LOG2E / math.sqrt(HEAD_DIM))).T.astype(BF16)
        k_ref[rows, :] = kn.astype(BF16)
        vt = vz.T
        ones = jnp.ones((V_EXT - V_DIM, rb), BF16)
        for h in range(N_HEADS):
            v_ref[h * V_EXT:h * V_EXT + V_DIM, rows] = vt[h * V_DIM:(h + 1) * V_DIM, :].astype(BF16)
            v_ref[h * V_EXT + V_DIM:(h + 1) * V_EXT, rows] = ones
        if emit_f32_kv:
            kf_ref[0, :, rows] = kn.T
            for h in range(N_HEADS):
                vf_ref[0, h, rows, :] = vz[:, h * V_DIM:(h + 1) * V_DIM]

    z_prev = project(0)
    for j in range(row_blocks):
        stats = head_stats(z_prev)
        z_next = project(j + 1) if j + 1 < row_blocks else None
        finish(j, z_prev, stats)
        z_prev = z_next


def _pre_call(x2d, mod3, g1, win_bf, seg, qg, kg, rope_tabs, *, seq, tm, mod_base, mod_stride,
              emit_f32_kv, cast_w=None, row_blocks=1):
    n = x2d.shape[0]
    rope = rope_tabs is not None
    tiles_per_seq = seq // tm

    def mod_map(i):
        return (mod_base + mod_stride * (i // tiles_per_seq), 0, 0)

    in_specs = [
        pl.BlockSpec((tm, D_MODEL), lambda i: (i, 0)),
        pl.BlockSpec((1, 1, 2 * D_MODEL), mod_map),
        pl.BlockSpec((1, D_MODEL), lambda i: (0, 0)),
        pl.BlockSpec((D_MODEL, IN_WIDTH), lambda i: (0, 0)),
        pl.BlockSpec((SEG_BLOCK, SEG_BLOCK), lambda i: (0, 0)),
        pl.BlockSpec((1, ATTN_WIDTH), lambda i: (0, 0)),
        pl.BlockSpec((1, ATTN_WIDTH), lambda i: (0, 0)),
    ]
    args = [x2d, mod3, g1, win_bf, seg, qg, kg]
    if rope:
        in_specs += [pl.BlockSpec((tm, LANES), lambda i: (i % tiles_per_seq, 0))] * 2
        args += list(rope_tabs)
    out_shapes = [jax.ShapeDtypeStruct((n, POOL_WIDTH), F32)]
    out_specs = [pl.BlockSpec((tm, POOL_WIDTH), lambda i: (i, 0))]
    out_shapes += [jax.ShapeDtypeStruct((ATTN_WIDTH, n), BF16),
                   jax.ShapeDtypeStruct((n, ATTN_WIDTH), BF16),
                   jax.ShapeDtypeStruct((N_HEADS * V_EXT, n), BF16)]
    out_specs += [pl.BlockSpec((ATTN_WIDTH, tm), lambda i: (0, i)),
                  pl.BlockSpec((tm, ATTN_WIDTH), lambda i: (i, 0)),
                  pl.BlockSpec((N_HEADS * V_EXT, tm), lambda i: (0, i))]
    if emit_f32_kv:
        def seq_map(i):
            return (i // tiles_per_seq, 0, i % tiles_per_seq)

        out_shapes += [jax.ShapeDtypeStruct((n // seq, ATTN_WIDTH, seq), F32),
                       jax.ShapeDtypeStruct((n // seq, N_HEADS, seq, V_DIM), F32)]
        out_specs += [pl.BlockSpec((1, ATTN_WIDTH, tm), seq_map),
                      pl.BlockSpec((1, N_HEADS, tm, V_DIM),
                                   lambda i: (i // tiles_per_seq, 0, i % tiles_per_seq, 0))]
    if cast_w is not None:
        assert cast_w.shape[0] == n // tm, "one expert matrix per grid step"
    _with_side_cast(cast_w, lambda i: (i, 0, 0), in_specs, args, out_specs, out_shapes)
    return pl.pallas_call(
        functools.partial(_pre_kernel, rope=rope, emit_f32_kv=emit_f32_kv,
                          side_cast=cast_w is not None, row_blocks=row_blocks),
        grid=(n // tm,),
        in_specs=in_specs,
        out_specs=out_specs,
        out_shape=out_shapes,
        compiler_params=_cparams(("arbitrary",), VMEM_LIMIT),
        name="pre_rope" if rope else "pre",
    )(*args)


def _attn_kernel(*refs, has_cache, heads, tq, tw, tk, seq, cache_len, side_cast):
    step = ((pl.program_id(0) * pl.num_programs(1) + pl.program_id(1)) * pl.num_programs(2)
            + pl.program_id(2))
    refs = _side_cast(refs, (7 if has_cache else 5) + 1, side_cast > 0, step, side_cast)
    if has_cache:
        lam_ref, g_ref, q_ref, kc_ref, vc_ref, k_ref, v_ref, o_ref = refs
    else:
        lam_ref, g_ref, q_ref, k_ref, v_ref, o_ref = refs

    lv = lam_ref[...]
    lam = (jnp.exp(jnp.sum(lv[0:1] * lv[1:2], axis=1, keepdims=True))
           - jnp.exp(jnp.sum(lv[2:3] * lv[3:4], axis=1, keepdims=True)) + LAMBDA_INIT)

    row = lax.broadcasted_iota(jnp.int32, (LANES, tq), 0)
    zero = jnp.zeros((LANES, tq), BF16)

    def sub_queries(h):
        qt = q_ref[h * LANES:(h + 1) * LANES, :]
        return (jnp.where(row < HEAD_DIM, qt, zero), jnp.where(row >= HEAD_DIM, qt, zero))

    def scores(kb, q_one):
        return jnp.dot(kb, q_one, preferred_element_type=F32)

    def softmax_step(s, m):
        m_new = jnp.maximum(m, jnp.max(s, axis=0, keepdims=True))
        return m_new, jnp.exp2(m - m_new), jnp.exp2(s - m_new).astype(BF16)

    def accumulate(vb, p, alpha, acc):
        return alpha * acc + jnp.dot(vb, p, preferred_element_type=F32)

    chunks = []
    if has_cache:
        chunks += [(kc_ref, vc_ref, j) for j in range(cache_len // tk)]
    chunks += [(k_ref, v_ref, j) for j in range(seq // tk)]

    def keys(c, h):
        kr, _, j = chunks[c]
        return kr[j * tk:(j + 1) * tk, h * LANES:(h + 1) * LANES]

    def values(c, h):
        _, vr, j = chunks[c]
        return vr[h * V_EXT:(h + 1) * V_EXT, j * tk:(j + 1) * tk]

    chains = []
    for h in range(heads):
        q_sub = sub_queries(h)
        chains += [(h, q_sub[sub][:, w * tw:(w + 1) * tw])
                   for w in range(tq // tw) for sub in range(2)]
    per_head = len(chains) // heads
    items = [(c, ch) for h in range(heads) for c in range(len(chunks))
             for ch in range(h * per_head, (h + 1) * per_head)]
    m = [jnp.full((1, tw), -jnp.inf, F32)] * len(chains)
    acc = [jnp.zeros((V_EXT, tw), F32)] * len(chains)
    queue = [scores(keys(c, chains[ch][0]), chains[ch][1]) for c, ch in items[:SCORES_AHEAD]]
    for i, (c, ch) in enumerate(items):
        s_cur = queue.pop(0)
        if i + SCORES_AHEAD < len(items):
            nc, nch = items[i + SCORES_AHEAD]
            queue.append(scores(keys(nc, chains[nch][0]), chains[nch][1]))
        m[ch], alpha, p = softmax_step(s_cur, m[ch])
        acc[ch] = accumulate(values(c, chains[ch][0]), p, alpha, acc[ch])
    for h in range(heads):
        outs = []
        for w in range(tq // tw):
            a1, a2 = acc[h * per_head + 2 * w], acc[h * per_head + 2 * w + 1]
            outs.append(a1[:V_DIM] / a1[V_DIM:V_DIM + 1]
                        - lam * (a2[:V_DIM] / a2[V_DIM:V_DIM + 1]))
        o = outs[0] if len(outs) == 1 else jnp.concatenate(outs, axis=1)
        y = (o * lax.rsqrt(jnp.mean(o * o, axis=0, keepdims=True) + EPS) * g_ref[...]
             * (1.0 - LAMBDA_INIT))
        o_ref[:, h * LANES:(h + 1) * LANES] = y.T.astype(BF16)


def _attn_call(lam4, subln_col, qt, k, vt, cache, *, batch, seq, heads, tq, tw, tk, cast_w=None):
    has_cache = cache is not None
    nq = seq // tq
    in_specs = [
        pl.BlockSpec((4, HEAD_DIM), lambda b, h, i: (0, 0)),
        pl.BlockSpec((V_DIM, 1), lambda b, h, i: (0, 0)),
        pl.BlockSpec((heads * LANES, tq), lambda b, h, i: (h, b * nq + i)),
    ]
    args = [lam4, subln_col, qt]
    cache_len = 0
    if has_cache:
        kc, vct = cache
        cache_len = kc.shape[0] // batch
        in_specs += [pl.BlockSpec((cache_len, heads * LANES), lambda b, h, i: (b, h)),
                     pl.BlockSpec((heads * V_EXT, cache_len), lambda b, h, i: (h, b))]
        args += [kc, vct]
    in_specs += [pl.BlockSpec((seq, heads * LANES), lambda b, h, i: (b, h)),
                 pl.BlockSpec((heads * V_EXT, seq), lambda b, h, i: (h, b))]
    args += [k, vt]
    out_specs = [pl.BlockSpec((tq, heads * LANES), lambda b, h, i: (b * nq + i, h))]
    out_shapes = [jax.ShapeDtypeStruct((batch * seq, ATTN_WIDTH), BF16)]
    grid = (batch, N_HEADS // heads, nq)
    steps_per_block = 0
    if cast_w is not None:
        steps_per_block, rem = divmod(grid[0] * grid[1] * grid[2], cast_w.shape[0])
        assert rem == 0 and steps_per_block > 0, "whole number of grid steps per expert matrix"
    _with_side_cast(
        cast_w, lambda b, h, i: (((b * grid[1] + h) * grid[2] + i) // steps_per_block, 0, 0),
        in_specs, args, out_specs, out_shapes)
    return pl.pallas_call(
        functools.partial(_attn_kernel, has_cache=has_cache, heads=heads, tq=tq, tw=tw, tk=tk,
                          seq=seq, cache_len=cache_len, side_cast=steps_per_block),
        grid=grid,
        in_specs=in_specs,
        out_specs=out_specs,
        out_shape=out_shapes,
        compiler_params=_cparams(("arbitrary", "arbitrary", "arbitrary"), VMEM_LIMIT),
        name="attn_cache" if has_cache else "attn",
    )(*args)


def _post_kernel(*refs, tm, seq, side_cast):
    (x_ref, a_ref, p_ref, pp_ref, pn_ref, inv_ref, mod_ref, wout_ref, wbd_ref, ps_ref, g2_ref,
     wr_ref, x1_ref, h2_ref, aff_ref, affc_ref) = _side_cast(refs, 13, side_cast)
    i = pl.program_id(0)

    def centred_mean_minus_self(rows, prev, nxt, inv_count):
        r = rows.shape[0]
        ext = jnp.concatenate([prev, rows, nxt], axis=0)
        n_ext = r + 2 * HALO
        s2 = ext + pltpu.roll(ext, 1, 0)
        s4 = pltpu.roll(s2, 1, 0) + pltpu.roll(s2, n_ext - 1, 0)
        s4_hi = s4[:, LANES:]
        s8 = pltpu.roll(s4_hi, 2, 0) + pltpu.roll(s4_hi, n_ext - 2, 0)
        s16 = pltpu.roll(s8, 4, 0) + pltpu.roll(s8, n_ext - 4, 0)
        first = lax.broadcasted_iota(jnp.int32, (r, LANES), 1) < POOL_WIDTH // len(POOL_WINDOWS)
        inner = slice(HALO, HALO + r)
        win = jnp.concatenate([jnp.where(first, s2[inner, :LANES], s4[inner, :LANES]),
                               jnp.where(first, s8[inner], s16[inner])], axis=1)
        return win * inv_count - rows

    if tm <= seq:
        tiles_per_seq = seq // tm
        ti = i % tiles_per_seq
        pooled = centred_mean_minus_self(
            p_ref[...], jnp.where(ti == 0, 0.0, pp_ref[...]),
            jnp.where(ti == tiles_per_seq - 1, 0.0, pn_ref[...]), inv_ref[...])
    else:
        halo = jnp.zeros((HALO, POOL_WIDTH), F32)
        pooled = jnp.concatenate(
            [centred_mean_minus_self(p_ref[j * seq:(j + 1) * seq, :], halo, halo, inv_ref[...])
             for j in range(tm // seq)], axis=0)
    pool = jnp.dot(pooled.astype(BF16), wbd_ref[...], preferred_element_type=F32) * ps_ref[...]

    cat = jnp.concatenate([pool.astype(BF16), a_ref[...]], axis=1)
    mix = jnp.dot(cat, wout_ref[...], preferred_element_type=F32)
    mod = mod_ref[0]
    gate1 = mod[:, 0:D_MODEL]
    shift2 = mod[:, D_MODEL:2 * D_MODEL]
    scale2 = mod[:, 2 * D_MODEL:3 * D_MODEL]
    x1 = x_ref[...] + gate1 * mix
    x1_ref[...] = x1
    ms = jnp.mean(x1 * x1, axis=1, keepdims=True)
    h2 = x1 * lax.rsqrt(ms + EPS) * g2_ref[...] * (1.0 + scale2) + shift2
    for s in range(ROW_TILES):
        h2_ref[pl.ds(s, tm, stride=ROW_TILES), :] = h2[:, s * LANES:(s + 1) * LANES]

    def pieces(v):
        hi = v.astype(BF16)
        return hi, (v - hi.astype(F32)).astype(BF16)

    def contract(a, b):
        return lax.dot_general(a, b, (((1,), (1,)), ((), ())), preferred_element_type=F32)

    w_hi, w_lo = pieces(wr_ref[...])
    h_hi, h_lo = pieces(h2)
    logits = contract(w_hi, h_hi) + (contract(w_hi, h_lo) + contract(w_lo, h_hi))
    e = jnp.exp(logits - jnp.max(logits, axis=0, keepdims=True))
    aff = e / jnp.sum(e, axis=0, keepdims=True)
    aff_ref[...] = aff
    for c in range(tm // LANES):
        affc_ref[c * N_EXPERTS:(c + 1) * N_EXPERTS, :] = aff[:, c * LANES:(c + 1) * LANES]


def _inverse_window_counts(seq):
    t = np.arange(seq)[:, None]
    left = np.repeat(np.array(POOL_WINDOWS) // 2, POOL_WIDTH // len(POOL_WINDOWS))[None, :]
    lo = np.maximum(t - left, 0)
    hi = np.minimum(t + left - 1, seq - 1) + 1
    return jnp.asarray(1.0 / (hi - lo), F32)


def _post_call(x2d, attn, p, mod3, wout_bf, wbd_bf, pool_scale, g2, wr_t, *, seq, tm, mod_base,
               mod_stride, cast_w=None):
    n = x2d.shape[0]
    assert cast_w is None or cast_w.shape[0] == n // tm, "one expert matrix per grid step"
    assert tm <= seq or mod_stride == 0, "a tile spanning sequences needs one modulation row"
    halo_per_tile = tm // HALO
    n_halo = n // HALO

    def mod_map(i):
        return (mod_base + mod_stride * (i * tm // seq), 0, 0)

    in_specs = [
            pl.BlockSpec((tm, D_MODEL), lambda i: (i, 0)),
            pl.BlockSpec((tm, ATTN_WIDTH), lambda i: (i, 0)),
            pl.BlockSpec((tm, POOL_WIDTH), lambda i: (i, 0)),
            pl.BlockSpec((HALO, POOL_WIDTH), lambda i: (jnp.maximum(i * halo_per_tile - 1, 0), 0)),
            pl.BlockSpec((HALO, POOL_WIDTH),
                         lambda i: (jnp.minimum((i + 1) * halo_per_tile, n_halo - 1), 0)),
            (pl.BlockSpec((tm, POOL_WIDTH), lambda i: (i % (seq // tm), 0)) if tm <= seq
             else pl.BlockSpec((seq, POOL_WIDTH), lambda i: (0, 0))),
            pl.BlockSpec((1, 1, 3 * D_MODEL), lambda i: mod_map(i)[:2] + (0,)),
            pl.BlockSpec((D_MODEL, D_MODEL), lambda i: (0, 0)),
            pl.BlockSpec((POOL_WIDTH, POOL_WIDTH), lambda i: (0, 0)),
            pl.BlockSpec((1, POOL_WIDTH), lambda i: (0, 0)),
            pl.BlockSpec((1, D_MODEL), lambda i: (0, 0)),
            pl.BlockSpec((N_EXPERTS, D_MODEL), lambda i: (0, 0)),
    ]
    out_specs = [
            pl.BlockSpec((tm, D_MODEL), lambda i: (i, 0)),
            pl.BlockSpec((tm * ROW_TILES, LANES), lambda i: (i, 0)),
            pl.BlockSpec((N_EXPERTS, tm), lambda i: (0, i)),
            pl.BlockSpec((tm // LANES * N_EXPERTS, LANES), lambda i: (i, 0)),
    ]
    out_shapes = [
            jax.ShapeDtypeStruct((n, D_MODEL), F32),
            jax.ShapeDtypeStruct((n * ROW_TILES, LANES), F32),
            jax.ShapeDtypeStruct((N_EXPERTS, n), F32),
            jax.ShapeDtypeStruct((n // LANES * N_EXPERTS, LANES), F32),
    ]
    args = [x2d, attn, p, p, p, _inverse_window_counts(seq), mod3, wout_bf, wbd_bf, pool_scale, g2,
            wr_t]
    _with_side_cast(cast_w, lambda i: (i, 0, 0), in_specs, args, out_specs, out_shapes)
    return pl.pallas_call(
        functools.partial(_post_kernel, tm=tm, seq=seq, side_cast=cast_w is not None),
        grid=(n // tm,),
        in_specs=in_specs,
        out_specs=out_specs,
        out_shape=out_shapes,
        compiler_params=_cparams(("arbitrary",), VMEM_LIMIT),
        name="post",
    )(*args)


def _select_kernel(a_ref, ac_ref, idx_ref, gate_ref, *, n, cap):
    nc = n // LANES
    a = a_ref[...]
    thr = jnp.zeros((N_EXPERTS, 1), jnp.int32)
    for bit in range(30, -1, -1):
        cand = thr | (1 << bit)
        cnt = jnp.sum(jnp.where(a >= pltpu.bitcast(cand, F32), 1.0, 0.0), axis=1, keepdims=True)
        thr = jnp.where(cnt >= cap, cand, thr)
    thr_all = pltpu.bitcast(thr, F32)
    need_all = cap - jnp.sum(jnp.where(a > thr_all, 1.0, 0.0), axis=1, keepdims=True)

    r = lax.broadcasted_iota(jnp.int32, (LANES, LANES), 0)
    c = lax.broadcasted_iota(jnp.int32, (LANES, LANES), 1)
    upper = jnp.where(r <= c, 1.0, 0.0).astype(BF16)
    lower = jnp.where(c < r, 1.0, 0.0).astype(BF16)
    row_valid = r < nc
    chunk_col = lax.broadcasted_iota(jnp.int32, (LANES, 1), 0).astype(F32)
    slot = lax.broadcasted_iota(jnp.int32, (1, cap), 1).astype(F32)

    def lane_counts(mask):
        local = jnp.dot(mask.astype(BF16), upper, preferred_element_type=F32)
        total = jnp.broadcast_to(local[:, LANES - 1:LANES], (LANES, LANES))
        before = jnp.dot(lower, total.astype(BF16), preferred_element_type=F32)
        return local, total, before

    def ties(e):
        av = ac_ref[pl.ds(e, nc, stride=N_EXPERTS), :]
        if nc < LANES:
            av = jnp.concatenate([av, jnp.zeros((LANES - nc, LANES), F32)], axis=0)
        thr_e = thr_all[e:e + 1, :]
        above = jnp.where(row_valid & (av > thr_e), 1.0, 0.0)
        tied = jnp.where(row_valid & (av == thr_e), 1.0, 0.0)
        t_local, _, t_before = lane_counts(tied)
        return {"av": av, "above": above, "tied": tied, "tie_rank": t_local + t_before}

    def selection(e, st):
        sel = st["above"] + st["tied"] * jnp.where(st["tie_rank"] <= need_all[e:e + 1, :], 1.0, 0.0)
        s_local, s_total, s_before = lane_counts(sel)
        rank = jnp.where(sel > 0.0, s_local, 0.0)
        start = s_before[:, 0:1]
        return {"av": st["av"], "rank": rank, "start": start, "stop": start + s_total[:, 0:1]}

    def pick(e, st):
        start = st["start"]
        onehot = jnp.where((slot >= start) & (slot < st["stop"]), 1.0, 0.0)
        chunk_of_slot = jnp.sum(onehot * chunk_col, axis=0, keepdims=True)
        start_of_slot = jnp.sum(onehot * start, axis=0, keepdims=True)
        at = st["av"].T
        hi = at.astype(BF16)
        rest = at - hi.astype(F32)
        mid = rest.astype(BF16)
        lo = (rest - mid.astype(F32)).astype(BF16)
        lhs = jnp.concatenate([st["rank"].T.astype(BF16), hi, mid, lo], axis=0)
        picked = jnp.dot(lhs, onehot.astype(BF16), preferred_element_type=F32)
        return {"picked": picked, "chunk_of_slot": chunk_of_slot, "start_of_slot": start_of_slot}

    def emit(e, st):
        picked = st["picked"]
        rank_p = picked[0:LANES]
        aff_p = picked[LANES:2 * LANES] + picked[2 * LANES:3 * LANES] + picked[3 * LANES:]
        hit = rank_p == (slot - st["start_of_slot"] + 1.0)
        lane_of_slot = jnp.sum(jnp.where(hit, chunk_col, 0.0), axis=0, keepdims=True)
        idx_ref[e:e + 1, :] = (st["chunk_of_slot"] * LANES + lane_of_slot).astype(jnp.int32)
        gate_ref[e:e + 1, :] = jnp.sum(jnp.where(hit, aff_p, 0.0), axis=0, keepdims=True)

    for e0 in range(0, N_EXPERTS, SELECT_GROUP):
        group = range(e0, e0 + SELECT_GROUP)
        states = {e: ties(e) for e in group}
        states = {e: selection(e, states[e]) for e in group}
        states = {e: pick(e, states[e]) for e in group}
        for e in group:
            emit(e, states[e])


def _select_call(aff_t, aff_c, *, cap):
    n = aff_t.shape[1]
    assert n % LANES == 0 and n // LANES <= LANES and cap % LANES == 0
    return pl.pallas_call(
        functools.partial(_select_kernel, n=n, cap=cap),
        grid=(1,),
        in_specs=[pl.BlockSpec((N_EXPERTS, n), lambda i: (0, 0)),
                  pl.BlockSpec(aff_c.shape, lambda i: (0, 0))],
        out_specs=[pl.BlockSpec((N_EXPERTS, cap), lambda i: (0, 0))] * 2,
        out_shape=[jax.ShapeDtypeStruct((N_EXPERTS, cap), jnp.int32),
                   jax.ShapeDtypeStruct((N_EXPERTS, cap), F32)],
        compiler_params=_cparams(("arbitrary",), VMEM_LIMIT),
        name="select",
    )(aff_t, aff_c)


def _moe_kernel(idx_ref, gate_ref, wg_ref, wu_ref, wd_ref, x_hbm, out_hbm,
                gbuf, ybuf, acc_ref, gsem, osem, *, tm, n_tiles_total):
    e = pl.program_id(0)
    t = pl.program_id(1)
    nt = pl.num_programs(1)
    step = e * nt + t
    group = 8
    last = n_tiles_total - 1
    rows = tm * ROW_TILES

    def gather_start(tile_step, k, dst_slot):
        tok = idx_ref[tile_step * tm + k]
        pltpu.make_async_copy(
            x_hbm.at[pl.ds(pl.multiple_of(tok * ROW_TILES, ROW_TILES), ROW_TILES), :],
            gbuf.at[dst_slot, pl.ds(k * ROW_TILES, ROW_TILES), :], gsem.at[dst_slot]).start()

    def gather_wait(dst_slot):
        pltpu.make_async_copy(x_hbm.at[pl.ds(0, rows), :], gbuf.at[dst_slot],
                              gsem.at[dst_slot]).wait()

    def scatter_add(tile_step, src_slot, k0):
        pending = []
        for r in range(group):
            k = k0 + r
            tok = idx_ref[tile_step * tm + k]
            off = pl.multiple_of(tok * ROW_TILES, ROW_TILES)
            src = pl.multiple_of(k * ROW_TILES, ROW_TILES)
            pending.append((off, acc_ref[pl.ds(off, ROW_TILES), :]
                            + ybuf[src_slot, pl.ds(src, ROW_TILES), :]))
        for off, val in pending:
            acc_ref[pl.ds(off, ROW_TILES), :] = val

    @pl.when(step == 0)
    def _():
        acc_ref[...] = jnp.zeros_like(acc_ref)
        ybuf[...] = jnp.zeros_like(ybuf)

        def body(k, _):
            gather_start(0, k, 0)
            return 0
        lax.fori_loop(0, tm, body, 0)

    def tile_body(slot):
        nxt = jnp.minimum(step + 1, last)
        prev = jnp.maximum(step - 1, 0)
        for k in range(tm):
            gather_start(nxt, k, 1 - slot)
        gather_wait(slot)
        xe = jnp.concatenate(
            [gbuf[slot, pl.ds(s, tm, stride=ROW_TILES), :] for s in range(ROW_TILES)],
            axis=1).astype(BF16)
        g = jnp.dot(xe, wg_ref[...], preferred_element_type=F32)
        for k0 in range(0, tm // 2, group):
            scatter_add(prev, 1 - slot, k0)
        u = jnp.dot(xe, wu_ref[...], preferred_element_type=F32)
        for k0 in range(tm // 2, tm, group):
            scatter_add(prev, 1 - slot, k0)
        hid = (g * jax.nn.sigmoid(g) * u).astype(BF16)
        y = jnp.dot(hid, wd_ref[...], preferred_element_type=F32)
        gate = jnp.broadcast_to(gate_ref[...], (LANES, tm)).T
        for s in range(ROW_TILES):
            ybuf[slot, pl.ds(s, tm, stride=ROW_TILES), :] = y[:, s * LANES:(s + 1) * LANES] * gate

        @pl.when(step == last)
        def _():
            gather_wait(1 - slot)

            def body(kk, _):
                scatter_add(step, slot, kk * group)
                return 0
            lax.fori_loop(0, tm // group, body, 0)
            cp = pltpu.make_async_copy(acc_ref, out_hbm, osem)
            cp.start()
            cp.wait()

    for parity in range(2):
        pl.when(step % 2 == parity)(functools.partial(tile_body, parity))


def _moe_call(idx_flat, gates3, wg_bf, wu_bf, wd_bf, h2, *, cap, tm):
    n_rows = h2.shape[0]
    nt = cap // tm
    grid_spec = pltpu.PrefetchScalarGridSpec(
        num_scalar_prefetch=1,
        grid=(N_EXPERTS, nt),
        in_specs=[
            pl.BlockSpec((None, 1, tm), lambda e, t, idx: (e * nt + t, 0, 0)),
            pl.BlockSpec((None, D_MODEL, D_MODEL), lambda e, t, idx: (e, 0, 0)),
            pl.BlockSpec((None, D_MODEL, D_MODEL), lambda e, t, idx: (e, 0, 0)),
            pl.BlockSpec((None, D_MODEL, D_MODEL), lambda e, t, idx: (e, 0, 0)),
            pl.BlockSpec(memory_space=pl.ANY),
        ],
        out_specs=pl.BlockSpec(memory_space=pl.ANY),
        scratch_shapes=[
            pltpu.VMEM((2, tm * ROW_TILES, LANES), F32),
            pltpu.VMEM((2, tm * ROW_TILES, LANES), F32),
            pltpu.VMEM((n_rows, LANES), F32),
            pltpu.SemaphoreType.DMA((2,)),
            pltpu.SemaphoreType.DMA(()),
        ],
    )
    return pl.pallas_call(
        functools.partial(_moe_kernel, tm=tm, n_tiles_total=N_EXPERTS * nt),
        grid_spec=grid_spec,
        out_shape=jax.ShapeDtypeStruct((n_rows, LANES), F32),
        compiler_params=_cparams(("arbitrary", "arbitrary"), VMEM_LIMIT),
        name="moe",
    )(idx_flat, gates3, wg_bf, wu_bf, wd_bf, h2)


def _final_kernel(x1_ref, moe_ref, mod_ref, o_ref, *, tm):
    moe = jnp.concatenate(
        [moe_ref[pl.ds(s, tm, stride=ROW_TILES), :] for s in range(ROW_TILES)], axis=1)
    o_ref[...] = x1_ref[...] + mod_ref[0] * moe


def _final_call(x1, moe_tiles, mod3, *, seq, tm, mod_base, mod_stride):
    n = x1.shape[0]
    assert tm <= seq or mod_stride == 0, "a tile spanning sequences needs one modulation row"
    gate2_block = 5

    def mod_map(i):
        return (mod_base + mod_stride * (i * tm // seq), 0, gate2_block)

    return pl.pallas_call(
        functools.partial(_final_kernel, tm=tm),
        grid=(n // tm,),
        in_specs=[pl.BlockSpec((tm, D_MODEL), lambda i: (i, 0)),
                  pl.BlockSpec((tm * ROW_TILES, LANES), lambda i: (i, 0)),
                  pl.BlockSpec((1, 1, D_MODEL), mod_map)],
        out_specs=pl.BlockSpec((tm, D_MODEL), lambda i: (i, 0)),
        out_shape=jax.ShapeDtypeStruct((n, D_MODEL), F32),
        compiler_params=_cparams(("arbitrary",)),
        name="final",
    )(x1, moe_tiles, mod3)


def _rope_tables(seq):
    t = np.arange(seq)
    row, col = t // GRID_W, t % GRID_W
    half = HEAD_DIM // 2
    freqs = 1.0 / (ROPE_BASE ** (np.arange(0, half, 2) / half))
    ang_r = row[:, None] * freqs[None, :]
    ang_c = col[:, None] * freqs[None, :]
    ang = np.concatenate([ang_r, ang_r, ang_c, ang_c], axis=-1)
    cos = np.tile(np.cos(ang), (1, LANES // HEAD_DIM))
    sin = np.tile(np.sin(ang), (1, LANES // HEAD_DIM))
    sign = np.where((np.arange(LANES) % (HEAD_DIM // 2)) < (HEAD_DIM // 4), -1.0, 1.0)
    return jnp.asarray(cos, F32), jnp.asarray(sin * sign[None, :], F32)


def _segment_matrix():
    seg = np.arange(SEG_BLOCK) // HEAD_DIM
    return jnp.asarray((seg[:, None] == seg[None, :]) / HEAD_DIM, BF16)


def _token_mixing(x, mod3, w, cache, *, mod_base, mod_stride, tm, tm_post, heads, tq, tk,
                  cast_pre=None, cast_attn=None, cast_post=None):
    batch, seq, _ = x.shape
    n = batch * seq
    x2d = x.reshape(n, D_MODEL)
    rope_tabs = _rope_tables(seq) if cache is not None else None
    casts = {}
    pre = list(_pre_call(x2d, mod3[:, :, :2 * D_MODEL], w["g1"], w["win"], w["seg"], w["qg"],
                         w["kg"], rope_tabs, seq=seq, tm=tm, mod_base=mod_base,
                         mod_stride=mod_stride, emit_f32_kv=cache is None, cast_w=cast_pre,
                         row_blocks=tm // PRE_ROW_BLOCK))
    if cast_pre is not None:
        casts["pre"] = pre.pop()
    p, q, k, v = pre[:4]
    attn = list(_attn_call(w["lam4"], w["subln"], q, k, v, cache, batch=batch, seq=seq,
                           heads=heads, tq=tq, tw=256, tk=tk, cast_w=cast_attn))
    if cast_attn is not None:
        casts["attn"] = attn.pop()
    post = list(_post_call(x2d, attn[0], p, mod3[:, :, 2 * D_MODEL:5 * D_MODEL], w["wout"],
                           w["wbd"], w["pool_scale"], w["g2"], w["wr_t"], seq=seq, tm=tm_post,
                           mod_base=mod_base, mod_stride=mod_stride, cast_w=cast_post))
    if cast_post is not None:
        casts["post"] = post.pop()
    x1, h2, aff_t, aff_c = post
    cap = CAPACITY_FACTOR * n // N_EXPERTS
    idx, gates = _select_call(aff_t, aff_c, cap=cap)
    return {"x1": x1, "h2": h2, "idx": idx, "gates": gates, "cap": cap, "kv": pre[4:],
            "casts": casts, "shape": (batch, seq), "tm_post": tm_post,
            "mod": (mod_base, mod_stride)}


def _channel_mixing(mixed, mod3, wg, wu, wd, *, moe_tm):
    batch, seq = mixed["shape"]
    cap = mixed["cap"]
    mod_base, mod_stride = mixed["mod"]
    moe_tiles = _moe_call(mixed["idx"].reshape(N_EXPERTS * cap),
                          mixed["gates"].reshape(-1, 1, moe_tm), wg, wu, wd, mixed["h2"],
                          cap=cap, tm=moe_tm)
    y = _final_call(mixed["x1"], moe_tiles, mod3, seq=seq, tm=mixed["tm_post"],
                    mod_base=mod_base, mod_stride=mod_stride)
    return y.reshape(batch, seq, D_MODEL)


def kernel(x_prompt, x_sample, cache_k, cache_v, c, c_ctx, norm1_g, norm2_g, w_ada, b_ada, w_in,
           q_norm_g, k_norm_g, lambda_q1, lambda_k1, lambda_q2, lambda_k2, subln_g, w_pool,
           pool_scale, w_out, w_router, w_gate, w_up, w_down):
    assert w_ada.shape[0] == 1, "single-layer stack"
    batch, seq, _ = x_prompt.shape
    dec_batch, dec_seq, _ = x_sample.shape

    pad = SUBLANES - 1 - dec_batch
    cvec = jnp.concatenate([c_ctx[None, :], c, jnp.zeros((pad, D_MODEL), F32)], axis=0)
    mod = _ada_call(cvec, w_ada[0], b_ada[0])
    mod3 = mod.reshape(SUBLANES, 1, 6 * D_MODEL)

    n_groups = w_pool.shape[1]
    grp = POOL_WIDTH // n_groups
    eye = jnp.eye(n_groups, dtype=F32)
    wbd = (w_pool[0][:, :, None, :] * eye[:, None, :, None]).reshape(POOL_WIDTH, POOL_WIDTH)

    w = {
        "g1": norm1_g[0].reshape(1, D_MODEL),
        "g2": norm2_g[0].reshape(1, D_MODEL),
        "win": w_in[0].astype(BF16),
        "seg": _segment_matrix(),
        "qg": jnp.tile(q_norm_g[0], ATTN_WIDTH // HEAD_DIM).reshape(1, ATTN_WIDTH),
        "kg": jnp.tile(k_norm_g[0], ATTN_WIDTH // HEAD_DIM).reshape(1, ATTN_WIDTH),
        "lam4": jnp.stack([lambda_q1[0], lambda_k1[0], lambda_q2[0], lambda_k2[0]], axis=0),
        "subln": subln_g[0].reshape(V_DIM, 1),
        "wbd": wbd.astype(BF16),
        "pool_scale": pool_scale[0].reshape(1, POOL_WIDTH),
        "wout": w_out[0].astype(BF16),
        "wr_t": w_router[0].T,
    }

    ctx = _token_mixing(x_prompt, mod3, w, None, mod_base=0, mod_stride=0, tm=256, tm_post=512,
                        heads=N_HEADS, tq=256, tk=256)
    past = cache_k.shape[2]
    cv = cache_v[:, 0].reshape(dec_batch * past, N_HEADS, V_DIM).transpose(1, 2, 0).astype(BF16)
    cv = jnp.concatenate([cv, jnp.ones((N_HEADS, V_EXT - V_DIM, dec_batch * past), BF16)], axis=1)
    cache = (cache_k[:, 0].reshape(dec_batch * past, ATTN_WIDTH).astype(BF16),
             cv.reshape(N_HEADS * V_EXT, dec_batch * past))
    lat = _token_mixing(x_sample, mod3, w, cache, mod_base=1, mod_stride=1, tm=512, tm_post=512,
                        heads=1, tq=1024, tk=512, cast_pre=w_gate[0], cast_attn=w_down[0],
                        cast_post=w_up[0])
    wg, wu, wd = lat["casts"]["pre"], lat["casts"]["post"], lat["casts"]["attn"]
    yp = _channel_mixing(ctx, mod3, wg, wu, wd, moe_tm=512)
    ys = _channel_mixing(lat, mod3, wg, wu, wd, moe_tm=256)
    k_ctx, v_ctx = ctx["kv"]
    ctx_k =(k_ctx.reshape(batch, N_HEADS, 2, HEAD_DIM, seq).transpose(0, 4, 1, 2, 3)
             .reshape(batch, 1, seq, N_HEADS, 2, HEAD_DIM))
    ctx_v = v_ctx.transpose(0, 2, 1, 3).reshape(batch, 1, seq, N_HEADS, V_DIM)
    return yp, ys, ctx_k, ctx_v
```

```python
import functools
import math

import numpy as np
import jax
import jax.numpy as jnp
from jax import lax
from jax.experimental import pallas as pl
from jax.experimental.pallas import tpu as pltpu

F32 = jnp.float32
BF16 = jnp.bfloat16

D_MODEL = 1024
POOL_WIDTH = 256
POOL_WINDOWS = (2, 4, 8, 16)
assert POOL_WIDTH == 2 * 128 and POOL_WINDOWS == (2, 4, 8, 16)
ATTN_WIDTH = 768
N_HEADS = 6
HEAD_DIM = 64
V_DIM = 128
IN_WIDTH = POOL_WIDTH + 3 * ATTN_WIDTH
N_EXPERTS = 16
CAPACITY_FACTOR = 2
GRID_W = 64
ROPE_BASE = 10000.0
EPS = 1e-6
LAMBDA_INIT = 0.8 - 0.6 * math.exp(-0.3 * 0)
LOG2E = math.log2(math.e)
V_EXT = V_DIM + 16
PRE_ROW_BLOCK = 128
SELECT_GROUP = 8
SEG_BLOCK = 256
SCORES_AHEAD = 3

LANES = 128
SUBLANES = 8
ROW_TILES = D_MODEL // LANES
HALO = 16
VMEM_LIMIT = 56 * 1024 * 1024


def _cparams(sem, vmem=None):
    return pltpu.CompilerParams(dimension_semantics=sem, vmem_limit_bytes=vmem)


def _ada_kernel(c_ref, w_ref, b_ref, o_ref):
    c = c_ref[...]
    s = c * jax.nn.sigmoid(c)
    o_ref[...] = jnp.dot(s.astype(BF16), w_ref[...].astype(BF16),
                         preferred_element_type=F32) + b_ref[...]


def _ada_call(cvec, w_ada, b_ada):
    rows, d = cvec.shape
    n = w_ada.shape[1]
    bn = 1536
    return pl.pallas_call(
        _ada_kernel,
        grid=(n // bn,),
        in_specs=[pl.BlockSpec((rows, d), lambda j: (0, 0)),
                  pl.BlockSpec((d, bn), lambda j: (0, j)),
                  pl.BlockSpec((1, bn), lambda j: (0, j))],
        out_specs=pl.BlockSpec((rows, bn), lambda j: (0, j)),
        out_shape=jax.ShapeDtypeStruct((rows, n), F32),
        compiler_params=_cparams(("arbitrary",)),
        name="ada",
    )(cvec, w_ada, b_ada.reshape(1, n))


def _segment_mean_square(a, seg_ref):
    sq = (a * a).astype(BF16)
    seg = seg_ref[...]
    return jnp.concatenate(
        [jnp.dot(sq[:, j:j + SEG_BLOCK], seg, preferred_element_type=F32)
         for j in range(0, a.shape[1], SEG_BLOCK)], axis=1)


def _rope(a, cos, sin_signed, first_half):
    parts = []
    for h in range(a.shape[1] // LANES):
        blk = a[:, h * LANES:(h + 1) * LANES]
        fwd = pltpu.roll(blk, LANES - HEAD_DIM // 4, 1)
        bwd = pltpu.roll(blk, HEAD_DIM // 4, 1)
        parts.append(blk * cos + jnp.where(first_half, fwd, bwd) * sin_signed)
    return jnp.concatenate(parts, axis=1)


def _side_cast(refs, n_inputs, count, step=None, steps_per_block=1):
    if not count:
        return refs
    srcs, dsts = refs[n_inputs - count:n_inputs], refs[len(refs) - count:]

    def convert():
        for src_ref, dst_ref in zip(srcs, dsts):
            dst_ref[...] = src_ref[...].astype(BF16)

    if steps_per_block == 1:
        convert()
    else:
        pl.when(step % steps_per_block == 0)(convert)
    return refs[:n_inputs - count] + refs[n_inputs:len(refs) - count]


def _with_side_cast(ws, index_map, in_specs, args, out_specs, out_shapes):
    for w in ws:
        _, rows, cols = w.shape
        in_specs.append(pl.BlockSpec((None, rows, cols), index_map))
        args.append(w)
    for w in ws:
        _, rows, cols = w.shape
        out_specs.append(pl.BlockSpec((None, rows, cols), index_map))
        out_shapes.append(jax.ShapeDtypeStruct(w.shape, BF16))


def _pre_kernel(*refs, rope, emit_f32_kv, side_cast, row_blocks):
    refs = _side_cast(refs, 7 + (2 if rope else 0) + side_cast, side_cast)
    x_ref, mod_ref, g1_ref, win_ref, seg_ref, qg_ref, kg_ref = refs[:7]
    pos = 7
    if rope:
        cos_ref, sin_ref = refs[pos:pos + 2]
        pos += 2
    p_ref, q_ref, k_ref, v_ref = refs[pos:pos + 4]
    pos += 4
    if emit_f32_kv:
        kf_ref, vf_ref = refs[pos:pos + 2]

    mod = mod_ref[0]
    shift1 = mod[:, :D_MODEL]
    scale1 = mod[:, D_MODEL:2 * D_MODEL]
    tm = x_ref.shape[0]
    rb = tm // row_blocks

    def project(j):
        x = x_ref[j * rb:(j + 1) * rb, :]
        ms = jnp.mean(x * x, axis=1, keepdims=True)
        h = x * lax.rsqrt(ms + EPS) * g1_ref[...] * (1.0 + scale1) + shift1
        return jnp.dot(h.astype(BF16), win_ref[...], preferred_element_type=F32)

    def head_stats(z):
        qz = z[:, POOL_WIDTH:POOL_WIDTH + ATTN_WIDTH]
        kz = z[:, POOL_WIDTH + ATTN_WIDTH:POOL_WIDTH + 2 * ATTN_WIDTH]
        return _segment_mean_square(qz, seg_ref), _segment_mean_square(kz, seg_ref)

    def finish(j, z, stats):
        rows = slice(j * rb, (j + 1) * rb)
        p_ref[rows, :] = z[:, :POOL_WIDTH]
        qz = z[:, POOL_WIDTH:POOL_WIDTH + ATTN_WIDTH]
        kz = z[:, POOL_WIDTH + ATTN_WIDTH:POOL_WIDTH + 2 * ATTN_WIDTH]
        vz = z[:, POOL_WIDTH + 2 * ATTN_WIDTH:]
        qn = qz * lax.rsqrt(stats[0] + EPS) * qg_ref[...]
        kn = kz * lax.rsqrt(stats[1] + EPS) * kg_ref[...]
        if rope:
            cos = cos_ref[rows, :]
            sin_signed = sin_ref[rows, :]
            lane = lax.broadcasted_iota(jnp.int32, cos.shape, 1)
            first_half = (lane % (HEAD_DIM // 2)) < (HEAD_DIM // 4)
            qn = _rope(qn, cos, sin_signed, first_half)
            kn = _rope(kn, cos, sin_signed, first_half)
        q_ref[:, rows] = (qn * (LOG2E / math.sqrt(HEAD_DIM))).T.astype(BF16)
        k_ref[rows, :] = kn.astype(BF16)
        vt = vz.T
        ones = jnp.ones((V_EXT - V_DIM, rb), BF16)
        for h in range(N_HEADS):
            v_ref[h * V_EXT:h * V_EXT + V_DIM, rows] = vt[h * V_DIM:(h + 1) * V_DIM, :].astype(BF16)
            v_ref[h * V_EXT + V_DIM:(h + 1) * V_EXT, rows] = ones
        if emit_f32_kv:
            kf_ref[0, :, rows] = kn.T
            for h in range(N_HEADS):
                vf_ref[0, h, rows, :] = vz[:, h * V_DIM:(h + 1) * V_DIM]

    z_prev = project(0)
    for j in range(row_blocks):
        stats = head_stats(z_prev)
        z_next = project(j + 1) if j + 1 < row_blocks else None
        finish(j, z_prev, stats)
        z_prev = z_next


def _pre_call(x2d, mod3, g1, win_bf, seg, qg, kg, rope_tabs, *, seq, tm, mod_base, mod_stride,
              emit_f32_kv, cast_w=(), row_blocks=1):
    n = x2d.shape[0]
    rope = rope_tabs is not None
    tiles_per_seq = seq // tm

    def mod_map(i):
        return (mod_base + mod_stride * (i // tiles_per_seq), 0, 0)

    in_specs = [
        pl.BlockSpec((tm, D_MODEL), lambda i: (i, 0)),
        pl.BlockSpec((1, 1, 2 * D_MODEL), mod_map),
        pl.BlockSpec((1, D_MODEL), lambda i: (0, 0)),
        pl.BlockSpec((D_MODEL, IN_WIDTH), lambda i: (0, 0)),
        pl.BlockSpec((SEG_BLOCK, SEG_BLOCK), lambda i: (0, 0)),
        pl.BlockSpec((1, ATTN_WIDTH), lambda i: (0, 0)),
        pl.BlockSpec((1, ATTN_WIDTH), lambda i: (0, 0)),
    ]
    args = [x2d, mod3, g1, win_bf, seg, qg, kg]
    if rope:
        in_specs += [pl.BlockSpec((tm, LANES), lambda i: (i % tiles_per_seq, 0))] * 2
        args += list(rope_tabs)
    out_shapes = [jax.ShapeDtypeStruct((n, POOL_WIDTH), F32)]
    out_specs = [pl.BlockSpec((tm, POOL_WIDTH), lambda i: (i, 0))]
    out_shapes += [jax.ShapeDtypeStruct((ATTN_WIDTH, n), BF16),
                   jax.ShapeDtypeStruct((n, ATTN_WIDTH), BF16),
                   jax.ShapeDtypeStruct((N_HEADS * V_EXT, n), BF16)]
    out_specs += [pl.BlockSpec((ATTN_WIDTH, tm), lambda i: (0, i)),
                  pl.BlockSpec((tm, ATTN_WIDTH), lambda i: (i, 0)),
                  pl.BlockSpec((N_HEADS * V_EXT, tm), lambda i: (0, i))]
    if emit_f32_kv:
        def seq_map(i):
            return (i // tiles_per_seq, 0, i % tiles_per_seq)

        out_shapes += [jax.ShapeDtypeStruct((n // seq, ATTN_WIDTH, seq), F32),
                       jax.ShapeDtypeStruct((n // seq, N_HEADS, seq, V_DIM), F32)]
        out_specs += [pl.BlockSpec((1, ATTN_WIDTH, tm), seq_map),
                      pl.BlockSpec((1, N_HEADS, tm, V_DIM),
                                   lambda i: (i // tiles_per_seq, 0, i % tiles_per_seq, 0))]
    assert all(w.shape[0] == n // tm for w in cast_w), "one expert matrix per grid step"
    _with_side_cast(cast_w, lambda i: (i, 0, 0), in_specs, args, out_specs, out_shapes)
    return pl.pallas_call(
        functools.partial(_pre_kernel, rope=rope, emit_f32_kv=emit_f32_kv,
                          side_cast=len(cast_w), row_blocks=row_blocks),
        grid=(n // tm,),
        in_specs=in_specs,
        out_specs=out_specs,
        out_shape=out_shapes,
        compiler_params=_cparams(("arbitrary",), VMEM_LIMIT),
        name="pre_rope" if rope else "pre",
    )(*args)


def _attn_kernel(*refs, has_cache, heads, tq, tw, tk, seq, cache_len, side_cast,
                 cast_steps_per_block):
    step = ((pl.program_id(0) * pl.num_programs(1) + pl.program_id(1)) * pl.num_programs(2)
            + pl.program_id(2))
    refs = _side_cast(refs, (7 if has_cache else 5) + side_cast, side_cast, step,
                      cast_steps_per_block)
    if has_cache:
        lam_ref, g_ref, q_ref, kc_ref, vc_ref, k_ref, v_ref, o_ref = refs
    else:
        lam_ref, g_ref, q_ref, k_ref, v_ref, o_ref = refs

    lv = lam_ref[...]
    lam = (jnp.exp(jnp.sum(lv[0:1] * lv[1:2], axis=1, keepdims=True))
           - jnp.exp(jnp.sum(lv[2:3] * lv[3:4], axis=1, keepdims=True)) + LAMBDA_INIT)

    row = lax.broadcasted_iota(jnp.int32, (LANES, tq), 0)
    zero = jnp.zeros((LANES, tq), BF16)

    def sub_queries(h):
        qt = q_ref[h * LANES:(h + 1) * LANES, :]
        return (jnp.where(row < HEAD_DIM, qt, zero), jnp.where(row >= HEAD_DIM, qt, zero))

    def scores(kb, q_one):
        return jnp.dot(kb, q_one, preferred_element_type=F32)

    def softmax_step(s, m):
        m_new = jnp.maximum(m, jnp.max(s, axis=0, keepdims=True))
        return m_new, jnp.exp2(m - m_new), jnp.exp2(s - m_new).astype(BF16)

    def accumulate(vb, p, alpha, acc):
        return alpha * acc + jnp.dot(vb, p, preferred_element_type=F32)

    chunks = []
    if has_cache:
        chunks += [(kc_ref, vc_ref, j) for j in range(cache_len // tk)]
    chunks += [(k_ref, v_ref, j) for j in range(seq // tk)]

    def keys(c, h):
        kr, _, j = chunks[c]
        return kr[j * tk:(j + 1) * tk, h * LANES:(h + 1) * LANES]

    def values(c, h):
        _, vr, j = chunks[c]
        return vr[h * V_EXT:(h + 1) * V_EXT, j * tk:(j + 1) * tk]

    chains = []
    for h in range(heads):
        q_sub = sub_queries(h)
        chains += [(h, q_sub[sub][:, w * tw:(w + 1) * tw])
                   for w in range(tq // tw) for sub in range(2)]
    per_head = len(chains) // heads
    items = [(c, ch) for h in range(heads) for c in range(len(chunks))
             for ch in range(h * per_head, (h + 1) * per_head)]
    m = [jnp.full((1, tw), -jnp.inf, F32)] * len(chains)
    acc = [jnp.zeros((V_EXT, tw), F32)] * len(chains)
    queue = [scores(keys(c, chains[ch][0]), chains[ch][1]) for c, ch in items[:SCORES_AHEAD]]
    for i, (c, ch) in enumerate(items):
        s_cur = queue.pop(0)
        if i + SCORES_AHEAD < len(items):
            nc, nch = items[i + SCORES_AHEAD]
            queue.append(scores(keys(nc, chains[nch][0]), chains[nch][1]))
        m[ch], alpha, p = softmax_step(s_cur, m[ch])
        acc[ch] = accumulate(values(c, chains[ch][0]), p, alpha, acc[ch])
    for h in range(heads):
        outs = []
        for w in range(tq // tw):
            a1, a2 = acc[h * per_head + 2 * w], acc[h * per_head + 2 * w + 1]
            outs.append(a1[:V_DIM] / a1[V_DIM:V_DIM + 1]
                        - lam * (a2[:V_DIM] / a2[V_DIM:V_DIM + 1]))
        o = outs[0] if len(outs) == 1 else jnp.concatenate(outs, axis=1)
        y = (o * lax.rsqrt(jnp.mean(o * o, axis=0, keepdims=True) + EPS) * g_ref[...]
             * (1.0 - LAMBDA_INIT))
        o_ref[:, h * LANES:(h + 1) * LANES] = y.T.astype(BF16)


def _attn_call(lam4, subln_col, qt, k, vt, cache, *, batch, seq, heads, tq, tw, tk, cast_w=()):
    has_cache = cache is not None
    nq = seq // tq
    in_specs = [
        pl.BlockSpec((4, HEAD_DIM), lambda b, h, i: (0, 0)),
        pl.BlockSpec((V_DIM, 1), lambda b, h, i: (0, 0)),
        pl.BlockSpec((heads * LANES, tq), lambda b, h, i: (h, b * nq + i)),
    ]
    args = [lam4, subln_col, qt]
    cache_len = 0
    if has_cache:
        kc, vct = cache
        cache_len = kc.shape[0] // batch
        in_specs += [pl.BlockSpec((cache_len, heads * LANES), lambda b, h, i: (b, h)),
                     pl.BlockSpec((heads * V_EXT, cache_len), lambda b, h, i: (h, b))]
        args += [kc, vct]
    in_specs += [pl.BlockSpec((seq, heads * LANES), lambda b, h, i: (b, h)),
                 pl.BlockSpec((heads * V_EXT, seq), lambda b, h, i: (h, b))]
    args += [k, vt]
    out_specs = [pl.BlockSpec((tq, heads * LANES), lambda b, h, i: (b * nq + i, h))]
    out_shapes = [jax.ShapeDtypeStruct((batch * seq, ATTN_WIDTH), BF16)]
    grid = (batch, N_HEADS // heads, nq)
    steps_per_block = 1
    if cast_w:
        steps_per_block, rem = divmod(grid[0] * grid[1] * grid[2], cast_w[0].shape[0])
        assert rem == 0 and steps_per_block > 0, "whole number of grid steps per expert matrix"
        assert all(w.shape == cast_w[0].shape for w in cast_w)
    _with_side_cast(
        cast_w, lambda b, h, i: (((b * grid[1] + h) * grid[2] + i) // steps_per_block, 0, 0),
        in_specs, args, out_specs, out_shapes)
    return pl.pallas_call(
        functools.partial(_attn_kernel, has_cache=has_cache, heads=heads, tq=tq, tw=tw, tk=tk,
                          seq=seq, cache_len=cache_len, side_cast=len(cast_w),
                          cast_steps_per_block=steps_per_block),
        grid=grid,
        in_specs=in_specs,
        out_specs=out_specs,
        out_shape=out_shapes,
        compiler_params=_cparams(("arbitrary", "arbitrary", "arbitrary"), VMEM_LIMIT),
        name="attn_cache" if has_cache else "attn",
    )(*args)


def _post_kernel(*refs, tm, seq, side_cast):
    (x_ref, a_ref, p_ref, pp_ref, pn_ref, inv_ref, mod_ref, wout_ref, wbd_ref, ps_ref, g2_ref,
     wr_ref, x1_ref, h2_ref, aff_ref, affc_ref) = _side_cast(refs, 12 + side_cast, side_cast)
    i = pl.program_id(0)

    def centred_mean_minus_self(rows, prev, nxt, inv_count):
        r = rows.shape[0]
        ext = jnp.concatenate([prev, rows, nxt], axis=0)
        n_ext = r + 2 * HALO
        s2 = ext + pltpu.roll(ext, 1, 0)
        s4 = pltpu.roll(s2, 1, 0) + pltpu.roll(s2, n_ext - 1, 0)
        s4_hi = s4[:, LANES:]
        s8 = pltpu.roll(s4_hi, 2, 0) + pltpu.roll(s4_hi, n_ext - 2, 0)
        s16 = pltpu.roll(s8, 4, 0) + pltpu.roll(s8, n_ext - 4, 0)
        first = lax.broadcasted_iota(jnp.int32, (r, LANES), 1) < POOL_WIDTH // len(POOL_WINDOWS)
        inner = slice(HALO, HALO + r)
        win = jnp.concatenate([jnp.where(first, s2[inner, :LANES], s4[inner, :LANES]),
                               jnp.where(first, s8[inner], s16[inner])], axis=1)
        return win * inv_count - rows

    if tm <= seq:
        tiles_per_seq = seq // tm
        ti = i % tiles_per_seq
        pooled = centred_mean_minus_self(
            p_ref[...], jnp.where(ti == 0, 0.0, pp_ref[...]),
            jnp.where(ti == tiles_per_seq - 1, 0.0, pn_ref[...]), inv_ref[...])
    else:
        halo = jnp.zeros((HALO, POOL_WIDTH), F32)
        pooled = jnp.concatenate(
            [centred_mean_minus_self(p_ref[j * seq:(j + 1) * seq, :], halo, halo, inv_ref[...])
             for j in range(tm // seq)], axis=0)
    pool = jnp.dot(pooled.astype(BF16), wbd_ref[...], preferred_element_type=F32) * ps_ref[...]

    cat = jnp.concatenate([pool.astype(BF16), a_ref[...]], axis=1)
    mix = jnp.dot(cat, wout_ref[...], preferred_element_type=F32)
    mod = mod_ref[0]
    gate1 = mod[:, 0:D_MODEL]
    shift2 = mod[:, D_MODEL:2 * D_MODEL]
    scale2 = mod[:, 2 * D_MODEL:3 * D_MODEL]
    x1 = x_ref[...] + gate1 * mix
    x1_ref[...] = x1
    ms = jnp.mean(x1 * x1, axis=1, keepdims=True)
    h2 = x1 * lax.rsqrt(ms + EPS) * g2_ref[...] * (1.0 + scale2) + shift2
    for s in range(ROW_TILES):
        h2_ref[pl.ds(s, tm, stride=ROW_TILES), :] = h2[:, s * LANES:(s + 1) * LANES]

    def pieces(v):
        hi = v.astype(BF16)
        return hi, (v - hi.astype(F32)).astype(BF16)

    def contract(a, b):
        return lax.dot_general(a, b, (((1,), (1,)), ((), ())), preferred_element_type=F32)

    w_hi, w_lo = pieces(wr_ref[...])
    h_hi, h_lo = pieces(h2)
    logits = contract(w_hi, h_hi) + (contract(w_hi, h_lo) + contract(w_lo, h_hi))
    e = jnp.exp(logits - jnp.max(logits, axis=0, keepdims=True))
    aff = e / jnp.sum(e, axis=0, keepdims=True)
    aff_ref[...] = aff
    for c in range(tm // LANES):
        affc_ref[c * N_EXPERTS:(c + 1) * N_EXPERTS, :] = aff[:, c * LANES:(c + 1) * LANES]


def _inverse_window_counts(seq):
    t = np.arange(seq)[:, None]
    left = np.repeat(np.array(POOL_WINDOWS) // 2, POOL_WIDTH // len(POOL_WINDOWS))[None, :]
    lo = np.maximum(t - left, 0)
    hi = np.minimum(t + left - 1, seq - 1) + 1
    return jnp.asarray(1.0 / (hi - lo), F32)


def _post_call(x2d, attn, p, mod3, wout_bf, wbd_bf, pool_scale, g2, wr_t, *, seq, tm, mod_base,
               mod_stride, cast_w=()):
    n = x2d.shape[0]
    assert all(w.shape[0] == n // tm for w in cast_w), "one expert matrix per grid step"
    assert tm <= seq or mod_stride == 0, "a tile spanning sequences needs one modulation row"
    halo_per_tile = tm // HALO
    n_halo = n // HALO

    def mod_map(i):
        return (mod_base + mod_stride * (i * tm // seq), 0, 0)

    in_specs = [
            pl.BlockSpec((tm, D_MODEL), lambda i: (i, 0)),
            pl.BlockSpec((tm, ATTN_WIDTH), lambda i: (i, 0)),
            pl.BlockSpec((tm, POOL_WIDTH), lambda i: (i, 0)),
            pl.BlockSpec((HALO, POOL_WIDTH), lambda i: (jnp.maximum(i * halo_per_tile - 1, 0), 0)),
            pl.BlockSpec((HALO, POOL_WIDTH),
                         lambda i: (jnp.minimum((i + 1) * halo_per_tile, n_halo - 1), 0)),
            (pl.BlockSpec((tm, POOL_WIDTH), lambda i: (i % (seq // tm), 0)) if tm <= seq
             else pl.BlockSpec((seq, POOL_WIDTH), lambda i: (0, 0))),
            pl.BlockSpec((1, 1, 3 * D_MODEL), lambda i: mod_map(i)[:2] + (0,)),
            pl.BlockSpec((D_MODEL, D_MODEL), lambda i: (0, 0)),
            pl.BlockSpec((POOL_WIDTH, POOL_WIDTH), lambda i: (0, 0)),
            pl.BlockSpec((1, POOL_WIDTH), lambda i: (0, 0)),
            pl.BlockSpec((1, D_MODEL), lambda i: (0, 0)),
            pl.BlockSpec((N_EXPERTS, D_MODEL), lambda i: (0, 0)),
    ]
    out_specs = [
            pl.BlockSpec((tm, D_MODEL), lambda i: (i, 0)),
            pl.BlockSpec((tm * ROW_TILES, LANES), lambda i: (i, 0)),
            pl.BlockSpec((N_EXPERTS, tm), lambda i: (0, i)),
            pl.BlockSpec((tm // LANES * N_EXPERTS, LANES), lambda i: (i, 0)),
    ]
    out_shapes = [
            jax.ShapeDtypeStruct((n, D_MODEL), F32),
            jax.ShapeDtypeStruct((n * ROW_TILES, LANES), F32),
            jax.ShapeDtypeStruct((N_EXPERTS, n), F32),
            jax.ShapeDtypeStruct((n // LANES * N_EXPERTS, LANES), F32),
    ]
    args = [x2d, attn, p, p, p, _inverse_window_counts(seq), mod3, wout_bf, wbd_bf, pool_scale, g2,
            wr_t]
    _with_side_cast(cast_w, lambda i: (i, 0, 0), in_specs, args, out_specs, out_shapes)
    return pl.pallas_call(
        functools.partial(_post_kernel, tm=tm, seq=seq, side_cast=len(cast_w)),
        grid=(n // tm,),
        in_specs=in_specs,
        out_specs=out_specs,
        out_shape=out_shapes,
        compiler_params=_cparams(("arbitrary",), VMEM_LIMIT),
        name="post",
    )(*args)


def _select_kernel(a_ref, ac_ref, idx_ref, gate_ref, *, n, cap):
    nc = n // LANES
    a = a_ref[...]
    thr = jnp.zeros((N_EXPERTS, 1), jnp.int32)
    for bit in range(30, -1, -1):
        cand = thr | (1 << bit)
        cnt = jnp.sum(jnp.where(a >= pltpu.bitcast(cand, F32), 1.0, 0.0), axis=1, keepdims=True)
        thr = jnp.where(cnt >= cap, cand, thr)
    thr_all = pltpu.bitcast(thr, F32)
    need_all = cap - jnp.sum(jnp.where(a > thr_all, 1.0, 0.0), axis=1, keepdims=True)

    r = lax.broadcasted_iota(jnp.int32, (LANES, LANES), 0)
    c = lax.broadcasted_iota(jnp.int32, (LANES, LANES), 1)
    upper = jnp.where(r <= c, 1.0, 0.0).astype(BF16)
    lower = jnp.where(c < r, 1.0, 0.0).astype(BF16)
    row_valid = r < nc
    chunk_col = lax.broadcasted_iota(jnp.int32, (LANES, 1), 0).astype(F32)
    slot = lax.broadcasted_iota(jnp.int32, (1, cap), 1).astype(F32)

    def lane_counts(mask):
        local = jnp.dot(mask.astype(BF16), upper, preferred_element_type=F32)
        total = jnp.broadcast_to(local[:, LANES - 1:LANES], (LANES, LANES))
        before = jnp.dot(lower, total.astype(BF16), preferred_element_type=F32)
        return local, total, before

    def ties(e):
        av = ac_ref[pl.ds(e, nc, stride=N_EXPERTS), :]
        if nc < LANES:
            av = jnp.concatenate([av, jnp.zeros((LANES - nc, LANES), F32)], axis=0)
        thr_e = thr_all[e:e + 1, :]
        above = jnp.where(row_valid & (av > thr_e), 1.0, 0.0)
        tied = jnp.where(row_valid & (av == thr_e), 1.0, 0.0)
        t_local, _, t_before = lane_counts(tied)
        return {"av": av, "above": above, "tied": tied, "tie_rank": t_local + t_before}

    def selection(e, st):
        sel = st["above"] + st["tied"] * jnp.where(st["tie_rank"] <= need_all[e:e + 1, :], 1.0, 0.0)
        s_local, s_total, s_before = lane_counts(sel)
        rank = jnp.where(sel > 0.0, s_local, 0.0)
        start = s_before[:, 0:1]
        return {"av": st["av"], "rank": rank, "start": start, "stop": start + s_total[:, 0:1]}

    def pick(e, st):
        start = st["start"]
        onehot = jnp.where((slot >= start) & (slot < st["stop"]), 1.0, 0.0)
        chunk_of_slot = jnp.sum(onehot * chunk_col, axis=0, keepdims=True)
        start_of_slot = jnp.sum(onehot * start, axis=0, keepdims=True)
        at = st["av"].T
        hi = at.astype(BF16)
        rest = at - hi.astype(F32)
        mid = rest.astype(BF16)
        lo = (rest - mid.astype(F32)).astype(BF16)
        lhs = jnp.concatenate([st["rank"].T.astype(BF16), hi, mid, lo], axis=0)
        picked = jnp.dot(lhs, onehot.astype(BF16), preferred_element_type=F32)
        return {"picked": picked, "chunk_of_slot": chunk_of_slot, "start_of_slot": start_of_slot}

    def emit(e, st):
        picked = st["picked"]
        rank_p = picked[0:LANES]
        aff_p = picked[LANES:2 * LANES] + picked[2 * LANES:3 * LANES] + picked[3 * LANES:]
        hit = rank_p == (slot - st["start_of_slot"] + 1.0)
        lane_of_slot = jnp.sum(jnp.where(hit, chunk_col, 0.0), axis=0, keepdims=True)
        idx_ref[e:e + 1, :] = (st["chunk_of_slot"] * LANES + lane_of_slot).astype(jnp.int32)
        gate_ref[e:e + 1, :] = jnp.sum(jnp.where(hit, aff_p, 0.0), axis=0, keepdims=True)

    for e0 in range(0, N_EXPERTS, SELECT_GROUP):
        group = range(e0, e0 + SELECT_GROUP)
        states = {e: ties(e) for e in group}
        states = {e: selection(e, states[e]) for e in group}
        states = {e: pick(e, states[e]) for e in group}
        for e in group:
            emit(e, states[e])


def _select_call(aff_t, aff_c, *, cap):
    n = aff_t.shape[1]
    assert n % LANES == 0 and n // LANES <= LANES and cap % LANES == 0
    return pl.pallas_call(
        functools.partial(_select_kernel, n=n, cap=cap),
        grid=(1,),
        in_specs=[pl.BlockSpec((N_EXPERTS, n), lambda i: (0, 0)),
                  pl.BlockSpec(aff_c.shape, lambda i: (0, 0))],
        out_specs=[pl.BlockSpec((N_EXPERTS, cap), lambda i: (0, 0))] * 2,
        out_shape=[jax.ShapeDtypeStruct((N_EXPERTS, cap), jnp.int32),
                   jax.ShapeDtypeStruct((N_EXPERTS, cap), F32)],
        compiler_params=_cparams(("arbitrary",), VMEM_LIMIT),
        name="select",
    )(aff_t, aff_c)


def _moe_kernel(idx_ref, gate_ref, wg_ref, wu_ref, wd_ref, x_hbm, out_hbm,
                gbuf, ybuf, acc_ref, gsem, osem, *, tm, n_tiles_total):
    e = pl.program_id(0)
    t = pl.program_id(1)
    nt = pl.num_programs(1)
    step = e * nt + t
    group = 8
    last = n_tiles_total - 1
    rows = tm * ROW_TILES

    def gather_start(tile_step, k, dst_slot):
        tok = idx_ref[tile_step * tm + k]
        pltpu.make_async_copy(
            x_hbm.at[pl.ds(pl.multiple_of(tok * ROW_TILES, ROW_TILES), ROW_TILES), :],
            gbuf.at[dst_slot, pl.ds(k * ROW_TILES, ROW_TILES), :], gsem.at[dst_slot]).start()

    def gather_wait(dst_slot):
        pltpu.make_async_copy(x_hbm.at[pl.ds(0, rows), :], gbuf.at[dst_slot],
                              gsem.at[dst_slot]).wait()

    def scatter_add(tile_step, src_slot, k0):
        pending = []
        for r in range(group):
            k = k0 + r
            tok = idx_ref[tile_step * tm + k]
            off = pl.multiple_of(tok * ROW_TILES, ROW_TILES)
            src = pl.multiple_of(k * ROW_TILES, ROW_TILES)
            pending.append((off, acc_ref[pl.ds(off, ROW_TILES), :]
                            + ybuf[src_slot, pl.ds(src, ROW_TILES), :]))
        for off, val in pending:
            acc_ref[pl.ds(off, ROW_TILES), :] = val

    @pl.when(step == 0)
    def _():
        acc_ref[...] = jnp.zeros_like(acc_ref)
        ybuf[...] = jnp.zeros_like(ybuf)

        def body(k, _):
            gather_start(0, k, 0)
            return 0
        lax.fori_loop(0, tm, body, 0)

    def tile_body(slot):
        nxt = jnp.minimum(step + 1, last)
        prev = jnp.maximum(step - 1, 0)
        for k in range(tm):
            gather_start(nxt, k, 1 - slot)
        gather_wait(slot)
        xe = jnp.concatenate(
            [gbuf[slot, pl.ds(s, tm, stride=ROW_TILES), :] for s in range(ROW_TILES)],
            axis=1).astype(BF16)
        g = jnp.dot(xe, wg_ref[...], preferred_element_type=F32)
        for k0 in range(0, tm // 2, group):
            scatter_add(prev, 1 - slot, k0)
        u = jnp.dot(xe, wu_ref[...], preferred_element_type=F32)
        for k0 in range(tm // 2, tm, group):
            scatter_add(prev, 1 - slot, k0)
        hid = (g * jax.nn.sigmoid(g) * u).astype(BF16)
        y = jnp.dot(hid, wd_ref[...], preferred_element_type=F32)
        gate = jnp.broadcast_to(gate_ref[...], (LANES, tm)).T
        for s in range(ROW_TILES):
            ybuf[slot, pl.ds(s, tm, stride=ROW_TILES), :] = y[:, s * LANES:(s + 1) * LANES] * gate

        @pl.when(step == last)
        def _():
            gather_wait(1 - slot)

            def body(kk, _):
                scatter_add(step, slot, kk * group)
                return 0
            lax.fori_loop(0, tm // group, body, 0)
            cp = pltpu.make_async_copy(acc_ref, out_hbm, osem)
            cp.start()
            cp.wait()

    for parity in range(2):
        pl.when(step % 2 == parity)(functools.partial(tile_body, parity))


def _moe_call(idx_flat, gates3, wg_bf, wu_bf, wd_bf, h2, *, cap, tm):
    n_rows = h2.shape[0]
    nt = cap // tm
    grid_spec = pltpu.PrefetchScalarGridSpec(
        num_scalar_prefetch=1,
        grid=(N_EXPERTS, nt),
        in_specs=[
            pl.BlockSpec((None, 1, tm), lambda e, t, idx: (e * nt + t, 0, 0)),
            pl.BlockSpec((None, D_MODEL, D_MODEL), lambda e, t, idx: (e, 0, 0)),
            pl.BlockSpec((None, D_MODEL, D_MODEL), lambda e, t, idx: (e, 0, 0)),
            pl.BlockSpec((None, D_MODEL, D_MODEL), lambda e, t, idx: (e, 0, 0)),
            pl.BlockSpec(memory_space=pl.ANY),
        ],
        out_specs=pl.BlockSpec(memory_space=pl.ANY),
        scratch_shapes=[
            pltpu.VMEM((2, tm * ROW_TILES, LANES), F32),
            pltpu.VMEM((2, tm * ROW_TILES, LANES), F32),
            pltpu.VMEM((n_rows, LANES), F32),
            pltpu.SemaphoreType.DMA((2,)),
            pltpu.SemaphoreType.DMA(()),
        ],
    )
    return pl.pallas_call(
        functools.partial(_moe_kernel, tm=tm, n_tiles_total=N_EXPERTS * nt),
        grid_spec=grid_spec,
        out_shape=jax.ShapeDtypeStruct((n_rows, LANES), F32),
        compiler_params=_cparams(("arbitrary", "arbitrary"), VMEM_LIMIT),
        name="moe",
    )(idx_flat, gates3, wg_bf, wu_bf, wd_bf, h2)


def _final_kernel(x1_ref, moe_ref, mod_ref, o_ref, *, tm):
    moe = jnp.concatenate(
        [moe_ref[pl.ds(s, tm, stride=ROW_TILES), :] for s in range(ROW_TILES)], axis=1)
    o_ref[...] = x1_ref[...] + mod_ref[0] * moe


def _final_call(x1, moe_tiles, mod3, *, seq, tm, mod_base, mod_stride):
    n = x1.shape[0]
    assert tm <= seq or mod_stride == 0, "a tile spanning sequences needs one modulation row"
    gate2_block = 5

    def mod_map(i):
        return (mod_base + mod_stride * (i * tm // seq), 0, gate2_block)

    return pl.pallas_call(
        functools.partial(_final_kernel, tm=tm),
        grid=(n // tm,),
        in_specs=[pl.BlockSpec((tm, D_MODEL), lambda i: (i, 0)),
                  pl.BlockSpec((tm * ROW_TILES, LANES), lambda i: (i, 0)),
                  pl.BlockSpec((1, 1, D_MODEL), mod_map)],
        out_specs=pl.BlockSpec((tm, D_MODEL), lambda i: (i, 0)),
        out_shape=jax.ShapeDtypeStruct((n, D_MODEL), F32),
        compiler_params=_cparams(("arbitrary",)),
        name="final",
    )(x1, moe_tiles, mod3)


def _rope_tables(seq):
    t = np.arange(seq)
    row, col = t // GRID_W, t % GRID_W
    half = HEAD_DIM // 2
    freqs = 1.0 / (ROPE_BASE ** (np.arange(0, half, 2) / half))
    ang_r = row[:, None] * freqs[None, :]
    ang_c = col[:, None] * freqs[None, :]
    ang = np.concatenate([ang_r, ang_r, ang_c, ang_c], axis=-1)
    cos = np.tile(np.cos(ang), (1, LANES // HEAD_DIM))
    sin = np.tile(np.sin(ang), (1, LANES // HEAD_DIM))
    sign = np.where((np.arange(LANES) % (HEAD_DIM // 2)) < (HEAD_DIM // 4), -1.0, 1.0)
    return jnp.asarray(cos, F32), jnp.asarray(sin * sign[None, :], F32)


def _segment_matrix():
    seg = np.arange(SEG_BLOCK) // HEAD_DIM
    return jnp.asarray((seg[:, None] == seg[None, :]) / HEAD_DIM, BF16)


def _token_mixing(x, mod3, w, cache, *, mod_base, mod_stride, tm, tm_post, heads, tq, tk,
                  cast_pre=(), cast_attn=(), cast_post=()):
    batch, seq, _ = x.shape
    n = batch * seq
    x2d = x.reshape(n, D_MODEL)
    rope_tabs = _rope_tables(seq) if cache is not None else None
    casts = {}
    pre = list(_pre_call(x2d, mod3[:, :, :2 * D_MODEL], w["g1"], w["win"], w["seg"], w["qg"],
                         w["kg"], rope_tabs, seq=seq, tm=tm, mod_base=mod_base,
                         mod_stride=mod_stride, emit_f32_kv=cache is None, cast_w=cast_pre,
                         row_blocks=tm // PRE_ROW_BLOCK))
    casts["pre"] = [pre.pop() for _ in cast_pre][::-1]
    p, q, k, v = pre[:4]
    attn = list(_attn_call(w["lam4"], w["subln"], q, k, v, cache, batch=batch, seq=seq,
                           heads=heads, tq=tq, tw=256, tk=tk, cast_w=cast_attn))
    casts["attn"] = [attn.pop() for _ in cast_attn][::-1]
    post = list(_post_call(x2d, attn[0], p, mod3[:, :, 2 * D_MODEL:5 * D_MODEL], w["wout"],
                           w["wbd"], w["pool_scale"], w["g2"], w["wr_t"], seq=seq, tm=tm_post,
                           mod_base=mod_base, mod_stride=mod_stride, cast_w=cast_post))
    casts["post"] = [post.pop() for _ in cast_post][::-1]
    x1, h2, aff_t, aff_c = post
    cap = CAPACITY_FACTOR * n // N_EXPERTS
    idx, gates = _select_call(aff_t, aff_c, cap=cap)
    return {"x1": x1, "h2": h2, "idx": idx, "gates": gates, "cap": cap, "kv": pre[4:],
            "casts": casts, "shape": (batch, seq), "tm_post": tm_post,
            "mod": (mod_base, mod_stride)}


def _channel_mixing(mixed, mod3, wg, wu, wd, *, moe_tm):
    batch, seq = mixed["shape"]
    cap = mixed["cap"]
    mod_base, mod_stride = mixed["mod"]
    moe_tiles = _moe_call(mixed["idx"].reshape(N_EXPERTS * cap),
                          mixed["gates"].reshape(-1, 1, moe_tm), wg, wu, wd, mixed["h2"],
                          cap=cap, tm=moe_tm)
    y = _final_call(mixed["x1"], moe_tiles, mod3, seq=seq, tm=mixed["tm_post"],
                    mod_base=mod_base, mod_stride=mod_stride)
    return y.reshape(batch, seq, D_MODEL)


def kernel(x_prompt, x_sample, cache_k, cache_v, c, c_ctx, norm1_g, norm2_g, w_ada, b_ada, w_in,
           q_norm_g, k_norm_g, lambda_q1, lambda_k1, lambda_q2, lambda_k2, subln_g, w_pool,
           pool_scale, w_out, w_router, w_gate, w_up, w_down):
    assert w_ada.shape[0] == 1, "single-layer stack"
    batch, seq, _ = x_prompt.shape
    dec_batch, dec_seq, _ = x_sample.shape

    pad = SUBLANES - 1 - dec_batch
    cvec = jnp.concatenate([c_ctx[None, :], c, jnp.zeros((pad, D_MODEL), F32)], axis=0)
    mod = _ada_call(cvec, w_ada[0], b_ada[0])
    mod3 = mod.reshape(SUBLANES, 1, 6 * D_MODEL)

    n_groups = w_pool.shape[1]
    grp = POOL_WIDTH // n_groups
    eye = jnp.eye(n_groups, dtype=F32)
    wbd = (w_pool[0][:, :, None, :] * eye[:, None, :, None]).reshape(POOL_WIDTH, POOL_WIDTH)

    w = {
        "g1": norm1_g[0].reshape(1, D_MODEL),
        "g2": norm2_g[0].reshape(1, D_MODEL),
        "win": w_in[0].astype(BF16),
        "seg": _segment_matrix(),
        "qg": jnp.tile(q_norm_g[0], ATTN_WIDTH // HEAD_DIM).reshape(1, ATTN_WIDTH),
        "kg": jnp.tile(k_norm_g[0], ATTN_WIDTH // HEAD_DIM).reshape(1, ATTN_WIDTH),
        "lam4": jnp.stack([lambda_q1[0], lambda_k1[0], lambda_q2[0], lambda_k2[0]], axis=0),
        "subln": subln_g[0].reshape(V_DIM, 1),
        "wbd": wbd.astype(BF16),
        "pool_scale": pool_scale[0].reshape(1, POOL_WIDTH),
        "wout": w_out[0].astype(BF16),
        "wr_t": w_router[0].T,
    }

    ctx = _token_mixing(x_prompt, mod3, w, None, mod_base=0, mod_stride=0, tm=256, tm_post=512,
                        heads=N_HEADS, tq=256, tk=256)
    past = cache_k.shape[2]
    cv = cache_v[:, 0].reshape(dec_batch * past, N_HEADS, V_DIM).transpose(1, 2, 0).astype(BF16)
    cv = jnp.concatenate([cv, jnp.ones((N_HEADS, V_EXT - V_DIM, dec_batch * past), BF16)], axis=1)
    cache = (cache_k[:, 0].reshape(dec_batch * past, ATTN_WIDTH).astype(BF16),
             cv.reshape(N_HEADS * V_EXT, dec_batch * past))
    lat = _token_mixing(x_sample, mod3, w, cache, mod_base=1, mod_stride=1, tm=512, tm_post=512,
                        heads=1, tq=1024, tk=512, cast_pre=(w_gate[0],),
                        cast_attn=(w_up[0], w_down[0]))
    (wg,), (wu, wd) = lat["casts"]["pre"], lat["casts"]["attn"]
    yp = _channel_mixing(ctx, mod3, wg, wu, wd, moe_tm=512)
    ys = _channel_mixing(lat, mod3, wg, wu, wd, moe_tm=256)
    k_ctx, v_ctx = ctx["kv"]
    ctx_k =(k_ctx.reshape(batch, N_HEADS, 2, HEAD_DIM, seq).transpose(0, 4, 1, 2, 3)
             .reshape(batch, 1, seq, N_HEADS, 2, HEAD_DIM))
    ctx_v = v_ctx.transpose(0, 2, 1, 3).reshape(batch, 1, seq, N_HEADS, V_DIM)
    return yp, ys, ctx_k, ctx_v
```

```python
import functools
import math

import numpy as np
import jax
import jax.numpy as jnp
from jax import lax
from jax.experimental import pallas as pl
from jax.experimental.pallas import tpu as pltpu

F32 = jnp.float32
BF16 = jnp.bfloat16

D_MODEL = 1024
POOL_WIDTH = 256
POOL_WINDOWS = (2, 4, 8, 16)
assert POOL_WIDTH == 2 * 128 and POOL_WINDOWS == (2, 4, 8, 16)
ATTN_WIDTH = 768
N_HEADS = 6
HEAD_DIM = 64
V_DIM = 128
IN_WIDTH = POOL_WIDTH + 3 * ATTN_WIDTH
N_EXPERTS = 16
CAPACITY_FACTOR = 2
GRID_W = 64
ROPE_BASE = 10000.0
EPS = 1e-6
LAMBDA_INIT = 0.8 - 0.6 * math.exp(-0.3 * 0)
LOG2E = math.log2(math.e)
V_EXT = V_DIM + 16
PRE_ROW_BLOCK = 128
SELECT_GROUP = 8
SEG_BLOCK = 256
SCORES_AHEAD = 3

LANES = 128
SUBLANES = 8
ROW_TILES = D_MODEL // LANES
HALO = 16
VMEM_LIMIT = 56 * 1024 * 1024


def _cparams(sem, vmem=None):
    return pltpu.CompilerParams(dimension_semantics=sem, vmem_limit_bytes=vmem)


def _ada_kernel(c_ref, w_ref, b_ref, o_ref):
    c = c_ref[...]
    s = c * jax.nn.sigmoid(c)
    o_ref[...] = jnp.dot(s.astype(BF16), w_ref[...].astype(BF16),
                         preferred_element_type=F32) + b_ref[...]


def _ada_call(cvec, w_ada, b_ada):
    rows, d = cvec.shape
    n = w_ada.shape[1]
    bn = 1536
    return pl.pallas_call(
        _ada_kernel,
        grid=(n // bn,),
        in_specs=[pl.BlockSpec((rows, d), lambda j: (0, 0)),
                  pl.BlockSpec((d, bn), lambda j: (0, j)),
                  pl.BlockSpec((1, bn), lambda j: (0, j))],
        out_specs=pl.BlockSpec((rows, bn), lambda j: (0, j)),
        out_shape=jax.ShapeDtypeStruct((rows, n), F32),
        compiler_params=_cparams(("arbitrary",)),
        name="ada",
    )(cvec, w_ada, b_ada.reshape(1, n))


def _segment_mean_square(a, seg_ref):
    sq = (a * a).astype(BF16)
    seg = seg_ref[...]
    return jnp.concatenate(
        [jnp.dot(sq[:, j:j + SEG_BLOCK], seg, preferred_element_type=F32)
         for j in range(0, a.shape[1], SEG_BLOCK)], axis=1)


def _rope(a, cos, sin_signed, first_half):
    parts = []
    for h in range(a.shape[1] // LANES):
        blk = a[:, h * LANES:(h + 1) * LANES]
        fwd = pltpu.roll(blk, LANES - HEAD_DIM // 4, 1)
        bwd = pltpu.roll(blk, HEAD_DIM // 4, 1)
        parts.append(blk * cos + jnp.where(first_half, fwd, bwd) * sin_signed)
    return jnp.concatenate(parts, axis=1)


def _side_cast(refs, n_inputs, count, step=None, steps_per_block=1):
    if not count:
        return refs
    srcs, dsts = refs[n_inputs - count:n_inputs], refs[len(refs) - count:]

    def convert():
        for src_ref, dst_ref in zip(srcs, dsts):
            dst_ref[...] = src_ref[...].astype(BF16)

    if steps_per_block == 1:
        convert()
    else:
        pl.when(step % steps_per_block == 0)(convert)
    return refs[:n_inputs - count] + refs[n_inputs:len(refs) - count]


def _with_side_cast(ws, index_map, in_specs, args, out_specs, out_shapes):
    for w in ws:
        _, rows, cols = w.shape
        in_specs.append(pl.BlockSpec((None, rows, cols), index_map))
        args.append(w)
    for w in ws:
        _, rows, cols = w.shape
        out_specs.append(pl.BlockSpec((None, rows, cols), index_map))
        out_shapes.append(jax.ShapeDtypeStruct(w.shape, BF16))


def _pre_kernel(*refs, rope, emit_f32_kv, side_cast, row_blocks):
    refs = _side_cast(refs, 7 + (2 if rope else 0) + side_cast, side_cast)
    x_ref, mod_ref, g1_ref, win_ref, seg_ref, qg_ref, kg_ref = refs[:7]
    pos = 7
    if rope:
        cos_ref, sin_ref = refs[pos:pos + 2]
        pos += 2
    p_ref, q_ref, k_ref, v_ref = refs[pos:pos + 4]
    pos += 4
    if emit_f32_kv:
        kf_ref, vf_ref = refs[pos:pos + 2]

    mod = mod_ref[0]
    shift1 = mod[:, :D_MODEL]
    scale1 = mod[:, D_MODEL:2 * D_MODEL]
    tm = x_ref.shape[0]
    rb = tm // row_blocks

    def project(j):
        x = x_ref[j * rb:(j + 1) * rb, :]
        ms = jnp.mean(x * x, axis=1, keepdims=True)
        h = x * lax.rsqrt(ms + EPS) * g1_ref[...] * (1.0 + scale1) + shift1
        return jnp.dot(h.astype(BF16), win_ref[...], preferred_element_type=F32)

    def head_stats(z):
        qz = z[:, POOL_WIDTH:POOL_WIDTH + ATTN_WIDTH]
        kz = z[:, POOL_WIDTH + ATTN_WIDTH:POOL_WIDTH + 2 * ATTN_WIDTH]
        return _segment_mean_square(qz, seg_ref), _segment_mean_square(kz, seg_ref)

    def finish(j, z, stats):
        rows = slice(j * rb, (j + 1) * rb)
        p_ref[rows, :] = z[:, :POOL_WIDTH]
        qz = z[:, POOL_WIDTH:POOL_WIDTH + ATTN_WIDTH]
        kz = z[:, POOL_WIDTH + ATTN_WIDTH:POOL_WIDTH + 2 * ATTN_WIDTH]
        vz = z[:, POOL_WIDTH + 2 * ATTN_WIDTH:]
        qn = qz * lax.rsqrt(stats[0] + EPS) * qg_ref[...]
        kn = kz * lax.rsqrt(stats[1] + EPS) * kg_ref[...]
        if rope:
            cos = cos_ref[rows, :]
            sin_signed = sin_ref[rows, :]
            lane = lax.broadcasted_iota(jnp.int32, cos.shape, 1)
            first_half = (lane % (HEAD_DIM // 2)) < (HEAD_DIM // 4)
            qn = _rope(qn, cos, sin_signed, first_half)
            kn = _rope(kn, cos, sin_signed, first_half)
        q_ref[:, rows] = (qn * (LOG2E / math.sqrt(HEAD_DIM))).T.astype(BF16)
        k_ref[rows, :] = kn.astype(BF16)
        vt = vz.T
        ones = jnp.ones((V_EXT - V_DIM, rb), BF16)
        for h in range(N_HEADS):
            v_ref[h * V_EXT:h * V_EXT + V_DIM, rows] = vt[h * V_DIM:(h + 1) * V_DIM, :].astype(BF16)
            v_ref[h * V_EXT + V_DIM:(h + 1) * V_EXT, rows] = ones
        if emit_f32_kv:
            seq_rows = kf_ref.shape[2]
            sj, r0 = divmod(j * rb, seq_rows)
            local = slice(r0, r0 + rb)
            kf_ref[sj, :, local] = kn.T
            for h in range(N_HEADS):
                vf_ref[sj, h, local, :] = vz[:, h * V_DIM:(h + 1) * V_DIM]

    z_prev = project(0)
    for j in range(row_blocks):
        stats = head_stats(z_prev)
        z_next = project(j + 1) if j + 1 < row_blocks else None
        finish(j, z_prev, stats)
        z_prev = z_next


def _pre_call(x2d, mod3, g1, win_bf, seg, qg, kg, rope_tabs, *, seq, tm, mod_base, mod_stride,
              emit_f32_kv, cast_w=(), row_blocks=1):
    n = x2d.shape[0]
    rope = rope_tabs is not None
    tiles_per_seq = max(seq // tm, 1)
    seqs_per_tile = max(tm // seq, 1)
    assert tm <= seq or (mod_stride == 0 and not rope), "tiles spanning sequences: context only"
    assert tm // row_blocks <= seq

    def mod_map(i):
        return (mod_base + mod_stride * (i * tm // seq), 0, 0)

    in_specs = [
        pl.BlockSpec((tm, D_MODEL), lambda i: (i, 0)),
        pl.BlockSpec((1, 1, 2 * D_MODEL), mod_map),
        pl.BlockSpec((1, D_MODEL), lambda i: (0, 0)),
        pl.BlockSpec((D_MODEL, IN_WIDTH), lambda i: (0, 0)),
        pl.BlockSpec((SEG_BLOCK, SEG_BLOCK), lambda i: (0, 0)),
        pl.BlockSpec((1, ATTN_WIDTH), lambda i: (0, 0)),
        pl.BlockSpec((1, ATTN_WIDTH), lambda i: (0, 0)),
    ]
    args = [x2d, mod3, g1, win_bf, seg, qg, kg]
    if rope:
        in_specs += [pl.BlockSpec((tm, LANES), lambda i: (i % tiles_per_seq, 0))] * 2
        args += list(rope_tabs)
    out_shapes = [jax.ShapeDtypeStruct((n, POOL_WIDTH), F32)]
    out_specs = [pl.BlockSpec((tm, POOL_WIDTH), lambda i: (i, 0))]
    out_shapes += [jax.ShapeDtypeStruct((ATTN_WIDTH, n), BF16),
                   jax.ShapeDtypeStruct((n, ATTN_WIDTH), BF16),
                   jax.ShapeDtypeStruct((N_HEADS * V_EXT, n), BF16)]
    out_specs += [pl.BlockSpec((ATTN_WIDTH, tm), lambda i: (0, i)),
                  pl.BlockSpec((tm, ATTN_WIDTH), lambda i: (i, 0)),
                  pl.BlockSpec((N_HEADS * V_EXT, tm), lambda i: (0, i))]
    if emit_f32_kv:
        rows = min(tm, seq)

        out_shapes += [jax.ShapeDtypeStruct((n // seq, ATTN_WIDTH, seq), F32),
                       jax.ShapeDtypeStruct((n // seq, N_HEADS, seq, V_DIM), F32)]
        out_specs += [pl.BlockSpec((seqs_per_tile, ATTN_WIDTH, rows),
                                   lambda i: (i * tm // seq // seqs_per_tile, 0, i % tiles_per_seq)),
                      pl.BlockSpec((seqs_per_tile, N_HEADS, rows, V_DIM),
                                   lambda i: (i * tm // seq // seqs_per_tile, 0,
                                              i % tiles_per_seq, 0))]
    assert all(w.shape[0] == n // tm for w in cast_w), "one expert matrix per grid step"
    _with_side_cast(cast_w, lambda i: (i, 0, 0), in_specs, args, out_specs, out_shapes)
    return pl.pallas_call(
        functools.partial(_pre_kernel, rope=rope, emit_f32_kv=emit_f32_kv,
                          side_cast=len(cast_w), row_blocks=row_blocks),
        grid=(n // tm,),
        in_specs=in_specs,
        out_specs=out_specs,
        out_shape=out_shapes,
        compiler_params=_cparams(("arbitrary",), VMEM_LIMIT),
        name="pre_rope" if rope else "pre",
    )(*args)


def _attn_kernel(*refs, has_cache, heads, tq, tw, tk, seq, cache_len, side_cast,
                 cast_steps_per_block):
    step = ((pl.program_id(0) * pl.num_programs(1) + pl.program_id(1)) * pl.num_programs(2)
            + pl.program_id(2))
    refs = _side_cast(refs, (7 if has_cache else 5) + side_cast, side_cast, step,
                      cast_steps_per_block)
    if has_cache:
        lam_ref, g_ref, q_ref, kc_ref, vc_ref, k_ref, v_ref, o_ref = refs
    else:
        lam_ref, g_ref, q_ref, k_ref, v_ref, o_ref = refs

    lv = lam_ref[...]
    lam = (jnp.exp(jnp.sum(lv[0:1] * lv[1:2], axis=1, keepdims=True))
           - jnp.exp(jnp.sum(lv[2:3] * lv[3:4], axis=1, keepdims=True)) + LAMBDA_INIT)

    row = lax.broadcasted_iota(jnp.int32, (LANES, tq), 0)
    zero = jnp.zeros((LANES, tq), BF16)

    def sub_queries(h):
        qt = q_ref[h * LANES:(h + 1) * LANES, :]
        return (jnp.where(row < HEAD_DIM, qt, zero), jnp.where(row >= HEAD_DIM, qt, zero))

    def scores(kb, q_one):
        return jnp.dot(kb, q_one, preferred_element_type=F32)

    def softmax_step(s, m):
        m_new = jnp.maximum(m, jnp.max(s, axis=0, keepdims=True))
        return m_new, jnp.exp2(m - m_new), jnp.exp2(s - m_new).astype(BF16)

    def accumulate(vb, p, alpha, acc):
        return alpha * acc + jnp.dot(vb, p, preferred_element_type=F32)

    chunks = []
    if has_cache:
        chunks += [(kc_ref, vc_ref, j) for j in range(cache_len // tk)]
    chunks += [(k_ref, v_ref, j) for j in range(seq // tk)]

    def keys(c, h):
        kr, _, j = chunks[c]
        return kr[j * tk:(j + 1) * tk, h * LANES:(h + 1) * LANES]

    def values(c, h):
        _, vr, j = chunks[c]
        return vr[h * V_EXT:(h + 1) * V_EXT, j * tk:(j + 1) * tk]

    chains = []
    for h in range(heads):
        q_sub = sub_queries(h)
        chains += [(h, q_sub[sub][:, w * tw:(w + 1) * tw])
                   for w in range(tq // tw) for sub in range(2)]
    per_head = len(chains) // heads
    items = [(c, ch) for h in range(heads) for c in range(len(chunks))
             for ch in range(h * per_head, (h + 1) * per_head)]
    m = [jnp.full((1, tw), -jnp.inf, F32)] * len(chains)
    acc = [jnp.zeros((V_EXT, tw), F32)] * len(chains)
    queue = [scores(keys(c, chains[ch][0]), chains[ch][1]) for c, ch in items[:SCORES_AHEAD]]
    for i, (c, ch) in enumerate(items):
        s_cur = queue.pop(0)
        if i + SCORES_AHEAD < len(items):
            nc, nch = items[i + SCORES_AHEAD]
            queue.append(scores(keys(nc, chains[nch][0]), chains[nch][1]))
        m[ch], alpha, p = softmax_step(s_cur, m[ch])
        acc[ch] = accumulate(values(c, chains[ch][0]), p, alpha, acc[ch])
    for h in range(heads):
        outs = []
        for w in range(tq // tw):
            a1, a2 = acc[h * per_head + 2 * w], acc[h * per_head + 2 * w + 1]
            outs.append(a1[:V_DIM] / a1[V_DIM:V_DIM + 1]
                        - lam * (a2[:V_DIM] / a2[V_DIM:V_DIM + 1]))
        o = outs[0] if len(outs) == 1 else jnp.concatenate(outs, axis=1)
        y = (o * lax.rsqrt(jnp.mean(o * o, axis=0, keepdims=True) + EPS) * g_ref[...]
             * (1.0 - LAMBDA_INIT))
        o_ref[:, h * LANES:(h + 1) * LANES] = y.T.astype(BF16)


def _attn_call(lam4, subln_col, qt, k, vt, cache, *, batch, seq, heads, tq, tw, tk, cast_w=()):
    has_cache = cache is not None
    nq = seq // tq
    in_specs = [
        pl.BlockSpec((4, HEAD_DIM), lambda b, h, i: (0, 0)),
        pl.BlockSpec((V_DIM, 1), lambda b, h, i: (0, 0)),
        pl.BlockSpec((heads * LANES, tq), lambda b, h, i: (h, b * nq + i)),
    ]
    args = [lam4, subln_col, qt]
    cache_len = 0
    if has_cache:
        kc, vct = cache
        cache_len = kc.shape[0] // batch
        in_specs += [pl.BlockSpec((cache_len, heads * LANES), lambda b, h, i: (b, h)),
                     pl.BlockSpec((heads * V_EXT, cache_len), lambda b, h, i: (h, b))]
        args += [kc, vct]
    in_specs += [pl.BlockSpec((seq, heads * LANES), lambda b, h, i: (b, h)),
                 pl.BlockSpec((heads * V_EXT, seq), lambda b, h, i: (h, b))]
    args += [k, vt]
    out_specs = [pl.BlockSpec((tq, heads * LANES), lambda b, h, i: (b * nq + i, h))]
    out_shapes = [jax.ShapeDtypeStruct((batch * seq, ATTN_WIDTH), BF16)]
    grid = (batch, N_HEADS // heads, nq)
    steps_per_block = 1
    if cast_w:
        steps_per_block, rem = divmod(grid[0] * grid[1] * grid[2], cast_w[0].shape[0])
        assert rem == 0 and steps_per_block > 0, "whole number of grid steps per expert matrix"
        assert all(w.shape == cast_w[0].shape for w in cast_w)
    _with_side_cast(
        cast_w, lambda b, h, i: (((b * grid[1] + h) * grid[2] + i) // steps_per_block, 0, 0),
        in_specs, args, out_specs, out_shapes)
    return pl.pallas_call(
        functools.partial(_attn_kernel, has_cache=has_cache, heads=heads, tq=tq, tw=tw, tk=tk,
                          seq=seq, cache_len=cache_len, side_cast=len(cast_w),
                          cast_steps_per_block=steps_per_block),
        grid=grid,
        in_specs=in_specs,
        out_specs=out_specs,
        out_shape=out_shapes,
        compiler_params=_cparams(("arbitrary", "arbitrary", "arbitrary"), VMEM_LIMIT),
        name="attn_cache" if has_cache else "attn",
    )(*args)


def _post_kernel(*refs, tm, seq, side_cast):
    (x_ref, a_ref, p_ref, pp_ref, pn_ref, inv_ref, mod_ref, wout_ref, wbd_ref, ps_ref, g2_ref,
     wr_ref, x1_ref, h2_ref, aff_ref, affc_ref) = _side_cast(refs, 12 + side_cast, side_cast)
    i = pl.program_id(0)

    def centred_mean_minus_self(rows, prev, nxt, inv_count):
        r = rows.shape[0]
        ext = jnp.concatenate([prev, rows, nxt], axis=0)
        n_ext = r + 2 * HALO
        s2 = ext + pltpu.roll(ext, 1, 0)
        s4 = pltpu.roll(s2, 1, 0) + pltpu.roll(s2, n_ext - 1, 0)
        s4_hi = s4[:, LANES:]
        s8 = pltpu.roll(s4_hi, 2, 0) + pltpu.roll(s4_hi, n_ext - 2, 0)
        s16 = pltpu.roll(s8, 4, 0) + pltpu.roll(s8, n_ext - 4, 0)
        first = lax.broadcasted_iota(jnp.int32, (r, LANES), 1) < POOL_WIDTH // len(POOL_WINDOWS)
        inner = slice(HALO, HALO + r)
        win = jnp.concatenate([jnp.where(first, s2[inner, :LANES], s4[inner, :LANES]),
                               jnp.where(first, s8[inner], s16[inner])], axis=1)
        return win * inv_count - rows

    if tm <= seq:
        tiles_per_seq = seq // tm
        ti = i % tiles_per_seq
        pooled = centred_mean_minus_self(
            p_ref[...], jnp.where(ti == 0, 0.0, pp_ref[...]),
            jnp.where(ti == tiles_per_seq - 1, 0.0, pn_ref[...]), inv_ref[...])
    else:
        halo = jnp.zeros((HALO, POOL_WIDTH), F32)
        pooled = jnp.concatenate(
            [centred_mean_minus_self(p_ref[j * seq:(j + 1) * seq, :], halo, halo, inv_ref[...])
             for j in range(tm // seq)], axis=0)
    pool = jnp.dot(pooled.astype(BF16), wbd_ref[...], preferred_element_type=F32) * ps_ref[...]

    cat = jnp.concatenate([pool.astype(BF16), a_ref[...]], axis=1)
    mix = jnp.dot(cat, wout_ref[...], preferred_element_type=F32)
    mod = mod_ref[0]
    gate1 = mod[:, 0:D_MODEL]
    shift2 = mod[:, D_MODEL:2 * D_MODEL]
    scale2 = mod[:, 2 * D_MODEL:3 * D_MODEL]
    x1 = x_ref[...] + gate1 * mix
    x1_ref[...] = x1
    ms = jnp.mean(x1 * x1, axis=1, keepdims=True)
    h2 = x1 * lax.rsqrt(ms + EPS) * g2_ref[...] * (1.0 + scale2) + shift2
    for s in range(ROW_TILES):
        h2_ref[pl.ds(s, tm, stride=ROW_TILES), :] = h2[:, s * LANES:(s + 1) * LANES]

    def pieces(v):
        hi = v.astype(BF16)
        return hi, (v - hi.astype(F32)).astype(BF16)

    def contract(a, b):
        return lax.dot_general(a, b, (((1,), (1,)), ((), ())), preferred_element_type=F32)

    w_hi, w_lo = pieces(wr_ref[...])
    h_hi, h_lo = pieces(h2)
    logits = contract(w_hi, h_hi) + (contract(w_hi, h_lo) + contract(w_lo, h_hi))
    e = jnp.exp(logits - jnp.max(logits, axis=0, keepdims=True))
    aff = e / jnp.sum(e, axis=0, keepdims=True)
    aff_ref[...] = aff
    for c in range(tm // LANES):
        affc_ref[c * N_EXPERTS:(c + 1) * N_EXPERTS, :] = aff[:, c * LANES:(c + 1) * LANES]


def _inverse_window_counts(seq):
    t = np.arange(seq)[:, None]
    left = np.repeat(np.array(POOL_WINDOWS) // 2, POOL_WIDTH // len(POOL_WINDOWS))[None, :]
    lo = np.maximum(t - left, 0)
    hi = np.minimum(t + left - 1, seq - 1) + 1
    return jnp.asarray(1.0 / (hi - lo), F32)


def _post_call(x2d, attn, p, mod3, wout_bf, wbd_bf, pool_scale, g2, wr_t, *, seq, tm, mod_base,
               mod_stride, cast_w=()):
    n = x2d.shape[0]
    assert all(w.shape[0] == n // tm for w in cast_w), "one expert matrix per grid step"
    assert tm <= seq or mod_stride == 0, "a tile spanning sequences needs one modulation row"
    halo_per_tile = tm // HALO
    n_halo = n // HALO

    def mod_map(i):
        return (mod_base + mod_stride * (i * tm // seq), 0, 0)

    in_specs = [
            pl.BlockSpec((tm, D_MODEL), lambda i: (i, 0)),
            pl.BlockSpec((tm, ATTN_WIDTH), lambda i: (i, 0)),
            pl.BlockSpec((tm, POOL_WIDTH), lambda i: (i, 0)),
            pl.BlockSpec((HALO, POOL_WIDTH), lambda i: (jnp.maximum(i * halo_per_tile - 1, 0), 0)),
            pl.BlockSpec((HALO, POOL_WIDTH),
                         lambda i: (jnp.minimum((i + 1) * halo_per_tile, n_halo - 1), 0)),
            (pl.BlockSpec((tm, POOL_WIDTH), lambda i: (i % (seq // tm), 0)) if tm <= seq
             else pl.BlockSpec((seq, POOL_WIDTH), lambda i: (0, 0))),
            pl.BlockSpec((1, 1, 3 * D_MODEL), lambda i: mod_map(i)[:2] + (0,)),
            pl.BlockSpec((D_MODEL, D_MODEL), lambda i: (0, 0)),
            pl.BlockSpec((POOL_WIDTH, POOL_WIDTH), lambda i: (0, 0)),
            pl.BlockSpec((1, POOL_WIDTH), lambda i: (0, 0)),
            pl.BlockSpec((1, D_MODEL), lambda i: (0, 0)),
            pl.BlockSpec((N_EXPERTS, D_MODEL), lambda i: (0, 0)),
    ]
    out_specs = [
            pl.BlockSpec((tm, D_MODEL), lambda i: (i, 0)),
            pl.BlockSpec((tm * ROW_TILES, LANES), lambda i: (i, 0)),
            pl.BlockSpec((N_EXPERTS, tm), lambda i: (0, i)),
            pl.BlockSpec((tm // LANES * N_EXPERTS, LANES), lambda i: (i, 0)),
    ]
    out_shapes = [
            jax.ShapeDtypeStruct((n, D_MODEL), F32),
            jax.ShapeDtypeStruct((n * ROW_TILES, LANES), F32),
            jax.ShapeDtypeStruct((N_EXPERTS, n), F32),
            jax.ShapeDtypeStruct((n // LANES * N_EXPERTS, LANES), F32),
    ]
    args = [x2d, attn, p, p, p, _inverse_window_counts(seq), mod3, wout_bf, wbd_bf, pool_scale, g2,
            wr_t]
    _with_side_cast(cast_w, lambda i: (i, 0, 0), in_specs, args, out_specs, out_shapes)
    return pl.pallas_call(
        functools.partial(_post_kernel, tm=tm, seq=seq, side_cast=len(cast_w)),
        grid=(n // tm,),
        in_specs=in_specs,
        out_specs=out_specs,
        out_shape=out_shapes,
        compiler_params=_cparams(("arbitrary",), VMEM_LIMIT),
        name="post",
    )(*args)


def _select_kernel(a_ref, ac_ref, idx_ref, gate_ref, *, n, cap):
    nc = n // LANES
    a = a_ref[...]
    thr = jnp.zeros((N_EXPERTS, 1), jnp.int32)
    for bit in range(30, -1, -1):
        cand = thr | (1 << bit)
        cnt = jnp.sum(jnp.where(a >= pltpu.bitcast(cand, F32), 1.0, 0.0), axis=1, keepdims=True)
        thr = jnp.where(cnt >= cap, cand, thr)
    thr_all = pltpu.bitcast(thr, F32)
    need_all = cap - jnp.sum(jnp.where(a > thr_all, 1.0, 0.0), axis=1, keepdims=True)

    r = lax.broadcasted_iota(jnp.int32, (LANES, LANES), 0)
    c = lax.broadcasted_iota(jnp.int32, (LANES, LANES), 1)
    upper = jnp.where(r <= c, 1.0, 0.0).astype(BF16)
    lower = jnp.where(c < r, 1.0, 0.0).astype(BF16)
    row_valid = r < nc
    chunk_col = lax.broadcasted_iota(jnp.int32, (LANES, 1), 0).astype(F32)
    slot = lax.broadcasted_iota(jnp.int32, (1, cap), 1).astype(F32)

    def lane_counts(mask):
        local = jnp.dot(mask.astype(BF16), upper, preferred_element_type=F32)
        total = jnp.broadcast_to(local[:, LANES - 1:LANES], (LANES, LANES))
        before = jnp.dot(lower, total.astype(BF16), preferred_element_type=F32)
        return local, total, before

    def ties(e):
        av = ac_ref[pl.ds(e, nc, stride=N_EXPERTS), :]
        if nc < LANES:
            av = jnp.concatenate([av, jnp.zeros((LANES - nc, LANES), F32)], axis=0)
        thr_e = thr_all[e:e + 1, :]
        above = jnp.where(row_valid & (av > thr_e), 1.0, 0.0)
        tied = jnp.where(row_valid & (av == thr_e), 1.0, 0.0)
        t_local, _, t_before = lane_counts(tied)
        return {"av": av, "above": above, "tied": tied, "tie_rank": t_local + t_before}

    def selection(e, st):
        sel = st["above"] + st["tied"] * jnp.where(st["tie_rank"] <= need_all[e:e + 1, :], 1.0, 0.0)
        s_local, s_total, s_before = lane_counts(sel)
        rank = jnp.where(sel > 0.0, s_local, 0.0)
        start = s_before[:, 0:1]
        return {"av": st["av"], "rank": rank, "start": start, "stop": start + s_total[:, 0:1]}

    def pick(e, st):
        start = st["start"]
        onehot = jnp.where((slot >= start) & (slot < st["stop"]), 1.0, 0.0)
        chunk_of_slot = jnp.sum(onehot * chunk_col, axis=0, keepdims=True)
        start_of_slot = jnp.sum(onehot * start, axis=0, keepdims=True)
        at = st["av"].T
        hi = at.astype(BF16)
        rest = at - hi.astype(F32)
        mid = rest.astype(BF16)
        lo = (rest - mid.astype(F32)).astype(BF16)
        lhs = jnp.concatenate([st["rank"].T.astype(BF16), hi, mid, lo], axis=0)
        picked = jnp.dot(lhs, onehot.astype(BF16), preferred_element_type=F32)
        return {"picked": picked, "chunk_of_slot": chunk_of_slot, "start_of_slot": start_of_slot}

    def emit(e, st):
        picked = st["picked"]
        rank_p = picked[0:LANES]
        aff_p = picked[LANES:2 * LANES] + picked[2 * LANES:3 * LANES] + picked[3 * LANES:]
        hit = rank_p == (slot - st["start_of_slot"] + 1.0)
        lane_of_slot = jnp.sum(jnp.where(hit, chunk_col, 0.0), axis=0, keepdims=True)
        idx_ref[e:e + 1, :] = (st["chunk_of_slot"] * LANES + lane_of_slot).astype(jnp.int32)
        gate_ref[e:e + 1, :] = jnp.sum(jnp.where(hit, aff_p, 0.0), axis=0, keepdims=True)

    for e0 in range(0, N_EXPERTS, SELECT_GROUP):
        group = range(e0, e0 + SELECT_GROUP)
        states = {e: ties(e) for e in group}
        states = {e: selection(e, states[e]) for e in group}
        states = {e: pick(e, states[e]) for e in group}
        for e in group:
            emit(e, states[e])


def _select_call(aff_t, aff_c, *, cap):
    n = aff_t.shape[1]
    assert n % LANES == 0 and n // LANES <= LANES and cap % LANES == 0
    return pl.pallas_call(
        functools.partial(_select_kernel, n=n, cap=cap),
        grid=(1,),
        in_specs=[pl.BlockSpec((N_EXPERTS, n), lambda i: (0, 0)),
                  pl.BlockSpec(aff_c.shape, lambda i: (0, 0))],
        out_specs=[pl.BlockSpec((N_EXPERTS, cap), lambda i: (0, 0))] * 2,
        out_shape=[jax.ShapeDtypeStruct((N_EXPERTS, cap), jnp.int32),
                   jax.ShapeDtypeStruct((N_EXPERTS, cap), F32)],
        compiler_params=_cparams(("arbitrary",), VMEM_LIMIT),
        name="select",
    )(aff_t, aff_c)


def _moe_kernel(idx_ref, gate_ref, wg_ref, wu_ref, wd_ref, x_hbm, out_hbm,
                gbuf, ybuf, acc_ref, gsem, osem, *, tm, n_tiles_total):
    e = pl.program_id(0)
    t = pl.program_id(1)
    nt = pl.num_programs(1)
    step = e * nt + t
    group = 8
    last = n_tiles_total - 1
    rows = tm * ROW_TILES

    def gather_start(tile_step, k, dst_slot):
        tok = idx_ref[tile_step * tm + k]
        pltpu.make_async_copy(
            x_hbm.at[pl.ds(pl.multiple_of(tok * ROW_TILES, ROW_TILES), ROW_TILES), :],
            gbuf.at[dst_slot, pl.ds(k * ROW_TILES, ROW_TILES), :], gsem.at[dst_slot]).start()

    def gather_wait(dst_slot):
        pltpu.make_async_copy(x_hbm.at[pl.ds(0, rows), :], gbuf.at[dst_slot],
                              gsem.at[dst_slot]).wait()

    def scatter_add(tile_step, src_slot, k0):
        pending = []
        for r in range(group):
            k = k0 + r
            tok = idx_ref[tile_step * tm + k]
            off = pl.multiple_of(tok * ROW_TILES, ROW_TILES)
            src = pl.multiple_of(k * ROW_TILES, ROW_TILES)
            pending.append((off, acc_ref[pl.ds(off, ROW_TILES), :]
                            + ybuf[src_slot, pl.ds(src, ROW_TILES), :]))
        for off, val in pending:
            acc_ref[pl.ds(off, ROW_TILES), :] = val

    @pl.when(step == 0)
    def _():
        acc_ref[...] = jnp.zeros_like(acc_ref)
        ybuf[...] = jnp.zeros_like(ybuf)

        def body(k, _):
            gather_start(0, k, 0)
            return 0
        lax.fori_loop(0, tm, body, 0)

    def tile_body(slot):
        nxt = jnp.minimum(step + 1, last)
        prev = jnp.maximum(step - 1, 0)
        for k in range(tm):
            gather_start(nxt, k, 1 - slot)
        gather_wait(slot)
        xe = jnp.concatenate(
            [gbuf[slot, pl.ds(s, tm, stride=ROW_TILES), :] for s in range(ROW_TILES)],
            axis=1).astype(BF16)
        g = jnp.dot(xe, wg_ref[...], preferred_element_type=F32)
        for k0 in range(0, tm // 2, group):
            scatter_add(prev, 1 - slot, k0)
        u = jnp.dot(xe, wu_ref[...], preferred_element_type=F32)
        for k0 in range(tm // 2, tm, group):
            scatter_add(prev, 1 - slot, k0)
        hid = (g * jax.nn.sigmoid(g) * u).astype(BF16)
        y = jnp.dot(hid, wd_ref[...], preferred_element_type=F32)
        gate = jnp.broadcast_to(gate_ref[...], (LANES, tm)).T
        for s in range(ROW_TILES):
            ybuf[slot, pl.ds(s, tm, stride=ROW_TILES), :] = y[:, s * LANES:(s + 1) * LANES] * gate

        @pl.when(step == last)
        def _():
            gather_wait(1 - slot)

            def body(kk, _):
                scatter_add(step, slot, kk * group)
                return 0
            lax.fori_loop(0, tm // group, body, 0)
            cp = pltpu.make_async_copy(acc_ref, out_hbm, osem)
            cp.start()
            cp.wait()

    for parity in range(2):
        pl.when(step % 2 == parity)(functools.partial(tile_body, parity))


def _moe_call(idx_flat, gates3, wg_bf, wu_bf, wd_bf, h2, *, cap, tm):
    n_rows = h2.shape[0]
    nt = cap // tm
    grid_spec = pltpu.PrefetchScalarGridSpec(
        num_scalar_prefetch=1,
        grid=(N_EXPERTS, nt),
        in_specs=[
            pl.BlockSpec((None, 1, tm), lambda e, t, idx: (e * nt + t, 0, 0)),
            pl.BlockSpec((None, D_MODEL, D_MODEL), lambda e, t, idx: (e, 0, 0)),
            pl.BlockSpec((None, D_MODEL, D_MODEL), lambda e, t, idx: (e, 0, 0)),
            pl.BlockSpec((None, D_MODEL, D_MODEL), lambda e, t, idx: (e, 0, 0)),
            pl.BlockSpec(memory_space=pl.ANY),
        ],
        out_specs=pl.BlockSpec(memory_space=pl.ANY),
        scratch_shapes=[
            pltpu.VMEM((2, tm * ROW_TILES, LANES), F32),
            pltpu.VMEM((2, tm * ROW_TILES, LANES), F32),
            pltpu.VMEM((n_rows, LANES), F32),
            pltpu.SemaphoreType.DMA((2,)),
            pltpu.SemaphoreType.DMA(()),
        ],
    )
    return pl.pallas_call(
        functools.partial(_moe_kernel, tm=tm, n_tiles_total=N_EXPERTS * nt),
        grid_spec=grid_spec,
        out_shape=jax.ShapeDtypeStruct((n_rows, LANES), F32),
        compiler_params=_cparams(("arbitrary", "arbitrary"), VMEM_LIMIT),
        name="moe",
    )(idx_flat, gates3, wg_bf, wu_bf, wd_bf, h2)


def _final_kernel(x1_ref, moe_ref, mod_ref, o_ref, *, tm):
    moe = jnp.concatenate(
        [moe_ref[pl.ds(s, tm, stride=ROW_TILES), :] for s in range(ROW_TILES)], axis=1)
    o_ref[...] = x1_ref[...] + mod_ref[0] * moe


def _final_call(x1, moe_tiles, mod3, *, seq, tm, mod_base, mod_stride):
    n = x1.shape[0]
    assert tm <= seq or mod_stride == 0, "a tile spanning sequences needs one modulation row"
    gate2_block = 5

    def mod_map(i):
        return (mod_base + mod_stride * (i * tm // seq), 0, gate2_block)

    return pl.pallas_call(
        functools.partial(_final_kernel, tm=tm),
        grid=(n // tm,),
        in_specs=[pl.BlockSpec((tm, D_MODEL), lambda i: (i, 0)),
                  pl.BlockSpec((tm * ROW_TILES, LANES), lambda i: (i, 0)),
                  pl.BlockSpec((1, 1, D_MODEL), mod_map)],
        out_specs=pl.BlockSpec((tm, D_MODEL), lambda i: (i, 0)),
        out_shape=jax.ShapeDtypeStruct((n, D_MODEL), F32),
        compiler_params=_cparams(("arbitrary",)),
        name="final",
    )(x1, moe_tiles, mod3)


def _rope_tables(seq):
    t = np.arange(seq)
    row, col = t // GRID_W, t % GRID_W
    half = HEAD_DIM // 2
    freqs = 1.0 / (ROPE_BASE ** (np.arange(0, half, 2) / half))
    ang_r = row[:, None] * freqs[None, :]
    ang_c = col[:, None] * freqs[None, :]
    ang = np.concatenate([ang_r, ang_r, ang_c, ang_c], axis=-1)
    cos = np.tile(np.cos(ang), (1, LANES // HEAD_DIM))
    sin = np.tile(np.sin(ang), (1, LANES // HEAD_DIM))
    sign = np.where((np.arange(LANES) % (HEAD_DIM // 2)) < (HEAD_DIM // 4), -1.0, 1.0)
    return jnp.asarray(cos, F32), jnp.asarray(sin * sign[None, :], F32)


def _segment_matrix():
    seg = np.arange(SEG_BLOCK) // HEAD_DIM
    return jnp.asarray((seg[:, None] == seg[None, :]) / HEAD_DIM, BF16)


def _token_mixing(x, mod3, w, cache, *, mod_base, mod_stride, tm, tm_post, heads, tq, tk,
                  cast_pre=(), cast_attn=(), cast_post=()):
    batch, seq, _ = x.shape
    n = batch * seq
    x2d = x.reshape(n, D_MODEL)
    rope_tabs = _rope_tables(seq) if cache is not None else None
    casts = {}
    pre = list(_pre_call(x2d, mod3[:, :, :2 * D_MODEL], w["g1"], w["win"], w["seg"], w["qg"],
                         w["kg"], rope_tabs, seq=seq, tm=tm, mod_base=mod_base,
                         mod_stride=mod_stride, emit_f32_kv=cache is None, cast_w=cast_pre,
                         row_blocks=tm // PRE_ROW_BLOCK))
    casts["pre"] = [pre.pop() for _ in cast_pre][::-1]
    p, q, k, v = pre[:4]
    attn = list(_attn_call(w["lam4"], w["subln"], q, k, v, cache, batch=batch, seq=seq,
                           heads=heads, tq=tq, tw=256, tk=tk, cast_w=cast_attn))
    casts["attn"] = [attn.pop() for _ in cast_attn][::-1]
    post = list(_post_call(x2d, attn[0], p, mod3[:, :, 2 * D_MODEL:5 * D_MODEL], w["wout"],
                           w["wbd"], w["pool_scale"], w["g2"], w["wr_t"], seq=seq, tm=tm_post,
                           mod_base=mod_base, mod_stride=mod_stride, cast_w=cast_post))
    casts["post"] = [post.pop() for _ in cast_post][::-1]
    x1, h2, aff_t, aff_c = post
    cap = CAPACITY_FACTOR * n // N_EXPERTS
    idx, gates = _select_call(aff_t, aff_c, cap=cap)
    return {"x1": x1, "h2": h2, "idx": idx, "gates": gates, "cap": cap, "kv": pre[4:],
            "casts": casts, "shape": (batch, seq), "tm_post": tm_post,
            "mod": (mod_base, mod_stride)}


def _channel_mixing(mixed, mod3, wg, wu, wd, *, moe_tm):
    batch, seq = mixed["shape"]
    cap = mixed["cap"]
    mod_base, mod_stride = mixed["mod"]
    moe_tiles = _moe_call(mixed["idx"].reshape(N_EXPERTS * cap),
                          mixed["gates"].reshape(-1, 1, moe_tm), wg, wu, wd, mixed["h2"],
                          cap=cap, tm=moe_tm)
    y = _final_call(mixed["x1"], moe_tiles, mod3, seq=seq, tm=mixed["tm_post"],
                    mod_base=mod_base, mod_stride=mod_stride)
    return y.reshape(batch, seq, D_MODEL)


def kernel(x_prompt, x_sample, cache_k, cache_v, c, c_ctx, norm1_g, norm2_g, w_ada, b_ada, w_in,
           q_norm_g, k_norm_g, lambda_q1, lambda_k1, lambda_q2, lambda_k2, subln_g, w_pool,
           pool_scale, w_out, w_router, w_gate, w_up, w_down):
    assert w_ada.shape[0] == 1, "single-layer stack"
    batch, seq, _ = x_prompt.shape
    dec_batch, dec_seq, _ = x_sample.shape

    pad = SUBLANES - 1 - dec_batch
    cvec = jnp.concatenate([c_ctx[None, :], c, jnp.zeros((pad, D_MODEL), F32)], axis=0)
    mod = _ada_call(cvec, w_ada[0], b_ada[0])
    mod3 = mod.reshape(SUBLANES, 1, 6 * D_MODEL)

    n_groups = w_pool.shape[1]
    grp = POOL_WIDTH // n_groups
    eye = jnp.eye(n_groups, dtype=F32)
    wbd = (w_pool[0][:, :, None, :] * eye[:, None, :, None]).reshape(POOL_WIDTH, POOL_WIDTH)

    w = {
        "g1": norm1_g[0].reshape(1, D_MODEL),
        "g2": norm2_g[0].reshape(1, D_MODEL),
        "win": w_in[0].astype(BF16),
        "seg": _segment_matrix(),
        "qg": jnp.tile(q_norm_g[0], ATTN_WIDTH // HEAD_DIM).reshape(1, ATTN_WIDTH),
        "kg": jnp.tile(k_norm_g[0], ATTN_WIDTH // HEAD_DIM).reshape(1, ATTN_WIDTH),
        "lam4": jnp.stack([lambda_q1[0], lambda_k1[0], lambda_q2[0], lambda_k2[0]], axis=0),
        "subln": subln_g[0].reshape(V_DIM, 1),
        "wbd": wbd.astype(BF16),
        "pool_scale": pool_scale[0].reshape(1, POOL_WIDTH),
        "wout": w_out[0].astype(BF16),
        "wr_t": w_router[0].T,
    }

    ctx = _token_mixing(x_prompt, mod3, w, None, mod_base=0, mod_stride=0, tm=512, tm_post=512,
                        heads=N_HEADS, tq=256, tk=256)
    past = cache_k.shape[2]
    cv = cache_v[:, 0].reshape(dec_batch * past, N_HEADS, V_DIM).transpose(1, 2, 0).astype(BF16)
    cv = jnp.concatenate([cv, jnp.ones((N_HEADS, V_EXT - V_DIM, dec_batch * past), BF16)], axis=1)
    cache = (cache_k[:, 0].reshape(dec_batch * past, ATTN_WIDTH).astype(BF16),
             cv.reshape(N_HEADS * V_EXT, dec_batch * past))
    lat = _token_mixing(x_sample, mod3, w, cache, mod_base=1, mod_stride=1, tm=512, tm_post=512,
                        heads=1, tq=1024, tk=512, cast_pre=(w_gate[0],),
                        cast_attn=(w_up[0], w_down[0]))
    (wg,), (wu, wd) = lat["casts"]["pre"], lat["casts"]["attn"]
    yp = _channel_mixing(ctx, mod3, wg, wu, wd, moe_tm=512)
    ys = _channel_mixing(lat, mod3, wg, wu, wd, moe_tm=256)
    k_ctx, v_ctx = ctx["kv"]
    ctx_k =(k_ctx.reshape(batch, N_HEADS, 2, HEAD_DIM, seq).transpose(0, 4, 1, 2, 3)
             .reshape(batch, 1, seq, N_HEADS, 2, HEAD_DIM))
    ctx_v = v_ctx.transpose(0, 2, 1, 3).reshape(batch, 1, seq, N_HEADS, V_DIM)
    return yp, ys, ctx_k, ctx_v
```

```python
import functools
import math

import numpy as np
import jax
import jax.numpy as jnp
from jax import lax
from jax.experimental import pallas as pl
from jax.experimental.pallas import tpu as pltpu

F32 = jnp.float32
BF16 = jnp.bfloat16

LANES = 128
SUBLANES = 8
MXU_TILE = 256
VMEM_LIMIT = 56 * 1024 * 1024

D_MODEL = 1024
POOL_WIDTH = 256
POOL_WINDOWS = (2, 4, 8, 16)
assert POOL_WIDTH == 2 * LANES and POOL_WINDOWS == (2, 4, 8, 16)
ATTN_WIDTH = 768
N_HEADS = 6
HEAD_DIM = 64
V_DIM = 128
IN_WIDTH = POOL_WIDTH + 3 * ATTN_WIDTH
N_EXPERTS = 16
CAPACITY_FACTOR = 2
GRID_W = 64
ROPE_BASE = 10000.0
EPS = 1e-6
LAMBDA_INIT = 0.8 - 0.6 * math.exp(-0.3 * 0)
LOG2E = math.log2(math.e)
V_EXT = V_DIM + 2 * SUBLANES
ROW_TILES = D_MODEL // LANES
HALO = 2 * SUBLANES
SEG_BLOCK = MXU_TILE
ADA_BLOCK = 1536
PRE_ROW_BLOCK = 128
SELECT_GROUP = 8
SCORES_AHEAD = 3
SCATTER_GROUP = 8


def _cparams(sem, vmem=None):
    return pltpu.CompilerParams(dimension_semantics=sem, vmem_limit_bytes=vmem)


def _ada_kernel(c_ref, w_ref, b_ref, o_ref):
    c = c_ref[...]
    s = c * jax.nn.sigmoid(c)
    o_ref[...] = jnp.dot(s.astype(BF16), w_ref[...].astype(BF16),
                         preferred_element_type=F32) + b_ref[...]


def _ada_call(cvec, w_ada, b_ada):
    rows, d = cvec.shape
    n = w_ada.shape[1]
    bn = ADA_BLOCK
    return pl.pallas_call(
        _ada_kernel,
        grid=(n // bn,),
        in_specs=[pl.BlockSpec((rows, d), lambda j: (0, 0)),
                  pl.BlockSpec((d, bn), lambda j: (0, j)),
                  pl.BlockSpec((1, bn), lambda j: (0, j))],
        out_specs=pl.BlockSpec((rows, bn), lambda j: (0, j)),
        out_shape=jax.ShapeDtypeStruct((rows, n), F32),
        compiler_params=_cparams(("arbitrary",)),
        name="ada",
    )(cvec, w_ada, b_ada.reshape(1, n))


def _segment_mean_square(a, seg_ref):
    sq = (a * a).astype(BF16)
    seg = seg_ref[...]
    return jnp.concatenate(
        [jnp.dot(sq[:, j:j + SEG_BLOCK], seg, preferred_element_type=F32)
         for j in range(0, a.shape[1], SEG_BLOCK)], axis=1)


def _rope(a, cos, sin_signed, first_half):
    parts = []
    for h in range(a.shape[1] // LANES):
        blk = a[:, h * LANES:(h + 1) * LANES]
        fwd = pltpu.roll(blk, LANES - HEAD_DIM // 4, 1)
        bwd = pltpu.roll(blk, HEAD_DIM // 4, 1)
        parts.append(blk * cos + jnp.where(first_half, fwd, bwd) * sin_signed)
    return jnp.concatenate(parts, axis=1)


def _side_cast(refs, n_inputs, count, step=None, steps_per_block=1):
    if not count:
        return refs
    srcs, dsts = refs[n_inputs - count:n_inputs], refs[len(refs) - count:]

    def convert():
        for src_ref, dst_ref in zip(srcs, dsts):
            dst_ref[...] = src_ref[...].astype(BF16)

    if steps_per_block == 1:
        convert()
    else:
        pl.when(step % steps_per_block == 0)(convert)
    return refs[:n_inputs - count] + refs[n_inputs:len(refs) - count]


def _with_side_cast(ws, index_map, in_specs, args, out_specs, out_shapes):
    for w in ws:
        _, rows, cols = w.shape
        in_specs.append(pl.BlockSpec((None, rows, cols), index_map))
        args.append(w)
    for w in ws:
        _, rows, cols = w.shape
        out_specs.append(pl.BlockSpec((None, rows, cols), index_map))
        out_shapes.append(jax.ShapeDtypeStruct(w.shape, BF16))


def _pre_kernel(*refs, rope, emit_f32_kv, side_cast, row_blocks):
    refs = _side_cast(refs, 7 + (2 if rope else 0) + side_cast, side_cast)
    x_ref, mod_ref, g1_ref, win_ref, seg_ref, qg_ref, kg_ref = refs[:7]
    pos = 7
    if rope:
        cos_ref, sin_ref = refs[pos:pos + 2]
        pos += 2
    p_ref, q_ref, k_ref, v_ref = refs[pos:pos + 4]
    pos += 4
    if emit_f32_kv:
        kf_ref, vf_ref = refs[pos:pos + 2]

    mod = mod_ref[0]
    shift1 = mod[:, :D_MODEL]
    scale1 = mod[:, D_MODEL:2 * D_MODEL]
    tm = x_ref.shape[0]
    rb = tm // row_blocks

    def project(j):
        x = x_ref[j * rb:(j + 1) * rb, :]
        ms = jnp.mean(x * x, axis=1, keepdims=True)
        h = x * lax.rsqrt(ms + EPS) * g1_ref[...] * (1.0 + scale1) + shift1
        return jnp.dot(h.astype(BF16), win_ref[...], preferred_element_type=F32)

    def head_stats(z):
        qz = z[:, POOL_WIDTH:POOL_WIDTH + ATTN_WIDTH]
        kz = z[:, POOL_WIDTH + ATTN_WIDTH:POOL_WIDTH + 2 * ATTN_WIDTH]
        return _segment_mean_square(qz, seg_ref), _segment_mean_square(kz, seg_ref)

    def finish(j, z, stats):
        rows = slice(j * rb, (j + 1) * rb)
        p_ref[rows, :] = z[:, :POOL_WIDTH]
        qz = z[:, POOL_WIDTH:POOL_WIDTH + ATTN_WIDTH]
        kz = z[:, POOL_WIDTH + ATTN_WIDTH:POOL_WIDTH + 2 * ATTN_WIDTH]
        vz = z[:, POOL_WIDTH + 2 * ATTN_WIDTH:]
        qn = qz * lax.rsqrt(stats[0] + EPS) * qg_ref[...]
        kn = kz * lax.rsqrt(stats[1] + EPS) * kg_ref[...]
        if rope:
            cos = cos_ref[rows, :]
            sin_signed = sin_ref[rows, :]
            lane = lax.broadcasted_iota(jnp.int32, cos.shape, 1)
            first_half = (lane % (HEAD_DIM // 2)) < (HEAD_DIM // 4)
            qn = _rope(qn, cos, sin_signed, first_half)
            kn = _rope(kn, cos, sin_signed, first_half)
        q_ref[:, rows] = (qn * (LOG2E / math.sqrt(HEAD_DIM))).T.astype(BF16)
        k_ref[rows, :] = kn.astype(BF16)
        vt = vz.T
        ones = jnp.ones((V_EXT - V_DIM, rb), BF16)
        for h in range(N_HEADS):
            v_ref[h * V_EXT:h * V_EXT + V_DIM, rows] = vt[h * V_DIM:(h + 1) * V_DIM, :].astype(BF16)
            v_ref[h * V_EXT + V_DIM:(h + 1) * V_EXT, rows] = ones
        if emit_f32_kv:
            kf_ref[0, :, rows] = kn.T
            for h in range(N_HEADS):
                vf_ref[0, h, rows, :] = vz[:, h * V_DIM:(h + 1) * V_DIM]

    z_prev = project(0)
    for j in range(row_blocks):
        stats = head_stats(z_prev)
        z_next = project(j + 1) if j + 1 < row_blocks else None
        finish(j, z_prev, stats)
        z_prev = z_next


def _pre_call(x2d, mod3, g1, win_bf, seg, qg, kg, rope_tabs, *, seq, tm, mod_base, mod_stride,
              emit_f32_kv, cast_w=(), row_blocks=1):
    n = x2d.shape[0]
    rope = rope_tabs is not None
    tiles_per_seq = seq // tm

    def mod_map(i):
        return (mod_base + mod_stride * (i // tiles_per_seq), 0, 0)

    in_specs = [
        pl.BlockSpec((tm, D_MODEL), lambda i: (i, 0)),
        pl.BlockSpec((1, 1, 2 * D_MODEL), mod_map),
        pl.BlockSpec((1, D_MODEL), lambda i: (0, 0)),
        pl.BlockSpec((D_MODEL, IN_WIDTH), lambda i: (0, 0)),
        pl.BlockSpec((SEG_BLOCK, SEG_BLOCK), lambda i: (0, 0)),
        pl.BlockSpec((1, ATTN_WIDTH), lambda i: (0, 0)),
        pl.BlockSpec((1, ATTN_WIDTH), lambda i: (0, 0)),
    ]
    args = [x2d, mod3, g1, win_bf, seg, qg, kg]
    if rope:
        in_specs += [pl.BlockSpec((tm, LANES), lambda i: (i % tiles_per_seq, 0))] * 2
        args += list(rope_tabs)
    out_shapes = [jax.ShapeDtypeStruct((n, POOL_WIDTH), F32)]
    out_specs = [pl.BlockSpec((tm, POOL_WIDTH), lambda i: (i, 0))]
    out_shapes += [jax.ShapeDtypeStruct((ATTN_WIDTH, n), BF16),
                   jax.ShapeDtypeStruct((n, ATTN_WIDTH), BF16),
                   jax.ShapeDtypeStruct((N_HEADS * V_EXT, n), BF16)]
    out_specs += [pl.BlockSpec((ATTN_WIDTH, tm), lambda i: (0, i)),
                  pl.BlockSpec((tm, ATTN_WIDTH), lambda i: (i, 0)),
                  pl.BlockSpec((N_HEADS * V_EXT, tm), lambda i: (0, i))]
    if emit_f32_kv:
        def seq_map(i):
            return (i // tiles_per_seq, 0, i % tiles_per_seq)

        out_shapes += [jax.ShapeDtypeStruct((n // seq, ATTN_WIDTH, seq), F32),
                       jax.ShapeDtypeStruct((n // seq, N_HEADS, seq, V_DIM), F32)]
        out_specs += [pl.BlockSpec((1, ATTN_WIDTH, tm), seq_map),
                      pl.BlockSpec((1, N_HEADS, tm, V_DIM),
                                   lambda i: (i // tiles_per_seq, 0, i % tiles_per_seq, 0))]
    assert all(w.shape[0] == n // tm for w in cast_w), "one expert matrix per grid step"
    _with_side_cast(cast_w, lambda i: (i, 0, 0), in_specs, args, out_specs, out_shapes)
    return pl.pallas_call(
        functools.partial(_pre_kernel, rope=rope, emit_f32_kv=emit_f32_kv,
                          side_cast=len(cast_w), row_blocks=row_blocks),
        grid=(n // tm,),
        in_specs=in_specs,
        out_specs=out_specs,
        out_shape=out_shapes,
        compiler_params=_cparams(("arbitrary",), VMEM_LIMIT),
        name="pre_rope" if rope else "pre",
    )(*args)


def _attn_kernel(*refs, has_cache, heads, tq, tw, tk, seq, cache_len, side_cast,
                 cast_steps_per_block):
    step = ((pl.program_id(0) * pl.num_programs(1) + pl.program_id(1)) * pl.num_programs(2)
            + pl.program_id(2))
    refs = _side_cast(refs, (7 if has_cache else 5) + side_cast, side_cast, step,
                      cast_steps_per_block)
    if has_cache:
        lam_ref, g_ref, q_ref, kc_ref, vc_ref, k_ref, v_ref, o_ref = refs
    else:
        lam_ref, g_ref, q_ref, k_ref, v_ref, o_ref = refs

    lv = lam_ref[...]
    lam = (jnp.exp(jnp.sum(lv[0:1] * lv[1:2], axis=1, keepdims=True))
           - jnp.exp(jnp.sum(lv[2:3] * lv[3:4], axis=1, keepdims=True)) + LAMBDA_INIT)

    row = lax.broadcasted_iota(jnp.int32, (LANES, tq), 0)
    zero = jnp.zeros((LANES, tq), BF16)

    def sub_queries(h):
        qt = q_ref[h * LANES:(h + 1) * LANES, :]
        return (jnp.where(row < HEAD_DIM, qt, zero), jnp.where(row >= HEAD_DIM, qt, zero))

    def scores(kb, q_one):
        return jnp.dot(kb, q_one, preferred_element_type=F32)

    def softmax_step(s, m):
        m_new = jnp.maximum(m, jnp.max(s, axis=0, keepdims=True))
        return m_new, jnp.exp2(m - m_new), jnp.exp2(s - m_new).astype(BF16)

    def accumulate(vb, p, alpha, acc):
        return alpha * acc + jnp.dot(vb, p, preferred_element_type=F32)

    chunks = []
    if has_cache:
        chunks += [(kc_ref, vc_ref, j) for j in range(cache_len // tk)]
    chunks += [(k_ref, v_ref, j) for j in range(seq // tk)]

    def keys(c, h):
        kr, _, j = chunks[c]
        return kr[j * tk:(j + 1) * tk, h * LANES:(h + 1) * LANES]

    def values(c, h):
        _, vr, j = chunks[c]
        return vr[h * V_EXT:(h + 1) * V_EXT, j * tk:(j + 1) * tk]

    chains = []
    for h in range(heads):
        q_sub = sub_queries(h)
        chains += [(h, q_sub[sub][:, w * tw:(w + 1) * tw])
                   for w in range(tq // tw) for sub in range(2)]
    per_head = len(chains) // heads
    items = [(c, ch) for h in range(heads) for c in range(len(chunks))
             for ch in range(h * per_head, (h + 1) * per_head)]
    m = [jnp.full((1, tw), -jnp.inf, F32)] * len(chains)
    acc = [jnp.zeros((V_EXT, tw), F32)] * len(chains)
    queue = [scores(keys(c, chains[ch][0]), chains[ch][1]) for c, ch in items[:SCORES_AHEAD]]
    for i, (c, ch) in enumerate(items):
        s_cur = queue.pop(0)
        if i + SCORES_AHEAD < len(items):
            nc, nch = items[i + SCORES_AHEAD]
            queue.append(scores(keys(nc, chains[nch][0]), chains[nch][1]))
        m[ch], alpha, p = softmax_step(s_cur, m[ch])
        acc[ch] = accumulate(values(c, chains[ch][0]), p, alpha, acc[ch])
    for h in range(heads):
        outs = []
        for w in range(tq // tw):
            a1, a2 = acc[h * per_head + 2 * w], acc[h * per_head + 2 * w + 1]
            outs.append(a1[:V_DIM] / a1[V_DIM:V_DIM + 1]
                        - lam * (a2[:V_DIM] / a2[V_DIM:V_DIM + 1]))
        o = outs[0] if len(outs) == 1 else jnp.concatenate(outs, axis=1)
        y = (o * lax.rsqrt(jnp.mean(o * o, axis=0, keepdims=True) + EPS) * g_ref[...]
             * (1.0 - LAMBDA_INIT))
        o_ref[:, h * LANES:(h + 1) * LANES] = y.T.astype(BF16)


def _attn_call(lam4, subln_col, qt, k, vt, cache, *, batch, seq, heads, tq, tw, tk, cast_w=()):
    has_cache = cache is not None
    nq = seq // tq
    in_specs = [
        pl.BlockSpec((4, HEAD_DIM), lambda b, h, i: (0, 0)),
        pl.BlockSpec((V_DIM, 1), lambda b, h, i: (0, 0)),
        pl.BlockSpec((heads * LANES, tq), lambda b, h, i: (h, b * nq + i)),
    ]
    args = [lam4, subln_col, qt]
    cache_len = 0
    if has_cache:
        kc, vct = cache
        cache_len = kc.shape[0] // batch
        in_specs += [pl.BlockSpec((cache_len, heads * LANES), lambda b, h, i: (b, h)),
                     pl.BlockSpec((heads * V_EXT, cache_len), lambda b, h, i: (h, b))]
        args += [kc, vct]
    in_specs += [pl.BlockSpec((seq, heads * LANES), lambda b, h, i: (b, h)),
                 pl.BlockSpec((heads * V_EXT, seq), lambda b, h, i: (h, b))]
    args += [k, vt]
    out_specs = [pl.BlockSpec((tq, heads * LANES), lambda b, h, i: (b * nq + i, h))]
    out_shapes = [jax.ShapeDtypeStruct((batch * seq, ATTN_WIDTH), BF16)]
    grid = (batch, N_HEADS // heads, nq)
    steps_per_block = 1
    if cast_w:
        steps_per_block, rem = divmod(grid[0] * grid[1] * grid[2], cast_w[0].shape[0])
        assert rem == 0 and steps_per_block > 0, "whole number of grid steps per expert matrix"
        assert all(w.shape == cast_w[0].shape for w in cast_w)
    _with_side_cast(
        cast_w, lambda b, h, i: (((b * grid[1] + h) * grid[2] + i) // steps_per_block, 0, 0),
        in_specs, args, out_specs, out_shapes)
    return pl.pallas_call(
        functools.partial(_attn_kernel, has_cache=has_cache, heads=heads, tq=tq, tw=tw, tk=tk,
                          seq=seq, cache_len=cache_len, side_cast=len(cast_w),
                          cast_steps_per_block=steps_per_block),
        grid=grid,
        in_specs=in_specs,
        out_specs=out_specs,
        out_shape=out_shapes,
        compiler_params=_cparams(("arbitrary", "arbitrary", "arbitrary"), VMEM_LIMIT),
        name="attn_cache" if has_cache else "attn",
    )(*args)


def _post_kernel(x_ref, a_ref, p_ref, pp_ref, pn_ref, inv_ref, mod_ref, wout_ref, wbd_ref, ps_ref,
                 g2_ref, wr_ref, x1_ref, h2_ref, aff_ref, affc_ref, *, tm, seq):
    i = pl.program_id(0)

    def centred_mean_minus_self(rows, prev, nxt, inv_count):
        r = rows.shape[0]
        ext = jnp.concatenate([prev, rows, nxt], axis=0)
        n_ext = r + 2 * HALO
        s2 = ext + pltpu.roll(ext, 1, 0)
        s4 = pltpu.roll(s2, 1, 0) + pltpu.roll(s2, n_ext - 1, 0)
        s4_hi = s4[:, LANES:]
        s8 = pltpu.roll(s4_hi, 2, 0) + pltpu.roll(s4_hi, n_ext - 2, 0)
        s16 = pltpu.roll(s8, 4, 0) + pltpu.roll(s8, n_ext - 4, 0)
        first = lax.broadcasted_iota(jnp.int32, (r, LANES), 1) < POOL_WIDTH // len(POOL_WINDOWS)
        inner = slice(HALO, HALO + r)
        win = jnp.concatenate([jnp.where(first, s2[inner, :LANES], s4[inner, :LANES]),
                               jnp.where(first, s8[inner], s16[inner])], axis=1)
        return win * inv_count - rows

    if tm <= seq:
        tiles_per_seq = seq // tm
        ti = i % tiles_per_seq
        pooled = centred_mean_minus_self(
            p_ref[...], jnp.where(ti == 0, 0.0, pp_ref[...]),
            jnp.where(ti == tiles_per_seq - 1, 0.0, pn_ref[...]), inv_ref[...])
    else:
        halo = jnp.zeros((HALO, POOL_WIDTH), F32)
        pooled = jnp.concatenate(
            [centred_mean_minus_self(p_ref[j * seq:(j + 1) * seq, :], halo, halo, inv_ref[...])
             for j in range(tm // seq)], axis=0)
    pool = jnp.dot(pooled.astype(BF16), wbd_ref[...], preferred_element_type=F32) * ps_ref[...]

    cat = jnp.concatenate([pool.astype(BF16), a_ref[...]], axis=1)
    mix = jnp.dot(cat, wout_ref[...], preferred_element_type=F32)
    mod = mod_ref[0]
    gate1 = mod[:, 0:D_MODEL]
    shift2 = mod[:, D_MODEL:2 * D_MODEL]
    scale2 = mod[:, 2 * D_MODEL:3 * D_MODEL]
    x1 = x_ref[...] + gate1 * mix
    x1_ref[...] = x1
    ms = jnp.mean(x1 * x1, axis=1, keepdims=True)
    h2 = x1 * lax.rsqrt(ms + EPS) * g2_ref[...] * (1.0 + scale2) + shift2
    for s in range(ROW_TILES):
        h2_ref[pl.ds(s, tm, stride=ROW_TILES), :] = h2[:, s * LANES:(s + 1) * LANES]

    def pieces(v):
        hi = v.astype(BF16)
        return hi, (v - hi.astype(F32)).astype(BF16)

    def contract(a, b):
        return lax.dot_general(a, b, (((1,), (1,)), ((), ())), preferred_element_type=F32)

    w_hi, w_lo = pieces(wr_ref[...])
    h_hi, h_lo = pieces(h2)
    logits = contract(w_hi, h_hi) + (contract(w_hi, h_lo) + contract(w_lo, h_hi))
    e = jnp.exp(logits - jnp.max(logits, axis=0, keepdims=True))
    aff = e / jnp.sum(e, axis=0, keepdims=True)
    aff_ref[...] = aff
    for c in range(tm // LANES):
        affc_ref[c * N_EXPERTS:(c + 1) * N_EXPERTS, :] = aff[:, c * LANES:(c + 1) * LANES]


def _inverse_window_counts(seq):
    t = np.arange(seq)[:, None]
    left = np.repeat(np.array(POOL_WINDOWS) // 2, POOL_WIDTH // len(POOL_WINDOWS))[None, :]
    lo = np.maximum(t - left, 0)
    hi = np.minimum(t + left - 1, seq - 1) + 1
    return jnp.asarray(1.0 / (hi - lo), F32)


def _post_call(x2d, attn, p, mod3, wout_bf, wbd_bf, pool_scale, g2, wr_t, *, seq, tm, mod_base,
               mod_stride):
    n = x2d.shape[0]
    assert tm <= seq or mod_stride == 0, "a tile spanning sequences needs one modulation row"
    halo_per_tile = tm // HALO
    n_halo = n // HALO

    def mod_map(i):
        return (mod_base + mod_stride * (i * tm // seq), 0, 0)

    in_specs = [
            pl.BlockSpec((tm, D_MODEL), lambda i: (i, 0)),
            pl.BlockSpec((tm, ATTN_WIDTH), lambda i: (i, 0)),
            pl.BlockSpec((tm, POOL_WIDTH), lambda i: (i, 0)),
            pl.BlockSpec((HALO, POOL_WIDTH), lambda i: (jnp.maximum(i * halo_per_tile - 1, 0), 0)),
            pl.BlockSpec((HALO, POOL_WIDTH),
                         lambda i: (jnp.minimum((i + 1) * halo_per_tile, n_halo - 1), 0)),
            (pl.BlockSpec((tm, POOL_WIDTH), lambda i: (i % (seq // tm), 0)) if tm <= seq
             else pl.BlockSpec((seq, POOL_WIDTH), lambda i: (0, 0))),
            pl.BlockSpec((1, 1, 3 * D_MODEL), lambda i: mod_map(i)[:2] + (0,)),
            pl.BlockSpec((D_MODEL, D_MODEL), lambda i: (0, 0)),
            pl.BlockSpec((POOL_WIDTH, POOL_WIDTH), lambda i: (0, 0)),
            pl.BlockSpec((1, POOL_WIDTH), lambda i: (0, 0)),
            pl.BlockSpec((1, D_MODEL), lambda i: (0, 0)),
            pl.BlockSpec((N_EXPERTS, D_MODEL), lambda i: (0, 0)),
    ]
    out_specs = [
            pl.BlockSpec((tm, D_MODEL), lambda i: (i, 0)),
            pl.BlockSpec((tm * ROW_TILES, LANES), lambda i: (i, 0)),
            pl.BlockSpec((N_EXPERTS, tm), lambda i: (0, i)),
            pl.BlockSpec((tm // LANES * N_EXPERTS, LANES), lambda i: (i, 0)),
    ]
    out_shapes = [
            jax.ShapeDtypeStruct((n, D_MODEL), F32),
            jax.ShapeDtypeStruct((n * ROW_TILES, LANES), F32),
            jax.ShapeDtypeStruct((N_EXPERTS, n), F32),
            jax.ShapeDtypeStruct((n // LANES * N_EXPERTS, LANES), F32),
    ]
    args = [x2d, attn, p, p, p, _inverse_window_counts(seq), mod3, wout_bf, wbd_bf, pool_scale, g2,
            wr_t]
    return pl.pallas_call(
        functools.partial(_post_kernel, tm=tm, seq=seq),
        grid=(n // tm,),
        in_specs=in_specs,
        out_specs=out_specs,
        out_shape=out_shapes,
        compiler_params=_cparams(("arbitrary",), VMEM_LIMIT),
        name="post",
    )(*args)


def _select_kernel(a_ref, ac_ref, idx_ref, gate_ref, *, n, cap):
    nc = n // LANES
    a = a_ref[...]
    thr = jnp.zeros((N_EXPERTS, 1), jnp.int32)
    for bit in range(30, -1, -1):
        cand = thr | (1 << bit)
        cnt = jnp.sum(jnp.where(a >= pltpu.bitcast(cand, F32), 1.0, 0.0), axis=1, keepdims=True)
        thr = jnp.where(cnt >= cap, cand, thr)
    thr_all = pltpu.bitcast(thr, F32)
    need_all = cap - jnp.sum(jnp.where(a > thr_all, 1.0, 0.0), axis=1, keepdims=True)

    r = lax.broadcasted_iota(jnp.int32, (LANES, LANES), 0)
    c = lax.broadcasted_iota(jnp.int32, (LANES, LANES), 1)
    upper = jnp.where(r <= c, 1.0, 0.0).astype(BF16)
    lower = jnp.where(c < r, 1.0, 0.0).astype(BF16)
    row_valid = r < nc
    chunk_col = lax.broadcasted_iota(jnp.int32, (LANES, 1), 0).astype(F32)
    slot = lax.broadcasted_iota(jnp.int32, (1, cap), 1).astype(F32)

    def lane_counts(mask):
        local = jnp.dot(mask.astype(BF16), upper, preferred_element_type=F32)
        total = jnp.broadcast_to(local[:, LANES - 1:LANES], (LANES, LANES))
        before = jnp.dot(lower, total.astype(BF16), preferred_element_type=F32)
        return local, total, before

    def ties(e):
        av = ac_ref[pl.ds(e, nc, stride=N_EXPERTS), :]
        if nc < LANES:
            av = jnp.concatenate([av, jnp.zeros((LANES - nc, LANES), F32)], axis=0)
        thr_e = thr_all[e:e + 1, :]
        above = jnp.where(row_valid & (av > thr_e), 1.0, 0.0)
        tied = jnp.where(row_valid & (av == thr_e), 1.0, 0.0)
        t_local, _, t_before = lane_counts(tied)
        return {"av": av, "above": above, "tied": tied, "tie_rank": t_local + t_before}

    def selection(e, st):
        sel = st["above"] + st["tied"] * jnp.where(st["tie_rank"] <= need_all[e:e + 1, :], 1.0, 0.0)
        s_local, s_total, s_before = lane_counts(sel)
        rank = jnp.where(sel > 0.0, s_local, 0.0)
        start = s_before[:, 0:1]
        return {"av": st["av"], "rank": rank, "start": start, "stop": start + s_total[:, 0:1]}

    def pick(e, st):
        start = st["start"]
        onehot = jnp.where((slot >= start) & (slot < st["stop"]), 1.0, 0.0)
        chunk_of_slot = jnp.sum(onehot * chunk_col, axis=0, keepdims=True)
        start_of_slot = jnp.sum(onehot * start, axis=0, keepdims=True)
        at = st["av"].T
        hi = at.astype(BF16)
        rest = at - hi.astype(F32)
        mid = rest.astype(BF16)
        lo = (rest - mid.astype(F32)).astype(BF16)
        lhs = jnp.concatenate([st["rank"].T.astype(BF16), hi, mid, lo], axis=0)
        picked = jnp.dot(lhs, onehot.astype(BF16), preferred_element_type=F32)
        return {"picked": picked, "chunk_of_slot": chunk_of_slot, "start_of_slot": start_of_slot}

    def emit(e, st):
        picked = st["picked"]
        rank_p = picked[0:LANES]
        aff_p = picked[LANES:2 * LANES] + picked[2 * LANES:3 * LANES] + picked[3 * LANES:]
        hit = rank_p == (slot - st["start_of_slot"] + 1.0)
        lane_of_slot = jnp.sum(jnp.where(hit, chunk_col, 0.0), axis=0, keepdims=True)
        idx_ref[e:e + 1, :] = (st["chunk_of_slot"] * LANES + lane_of_slot).astype(jnp.int32)
        gate_ref[e:e + 1, :] = jnp.sum(jnp.where(hit, aff_p, 0.0), axis=0, keepdims=True)

    for e0 in range(0, N_EXPERTS, SELECT_GROUP):
        group = range(e0, e0 + SELECT_GROUP)
        states = {e: ties(e) for e in group}
        states = {e: selection(e, states[e]) for e in group}
        states = {e: pick(e, states[e]) for e in group}
        for e in group:
            emit(e, states[e])


def _select_call(aff_t, aff_c, *, cap):
    n = aff_t.shape[1]
    assert n % LANES == 0 and n // LANES <= LANES and cap % LANES == 0
    return pl.pallas_call(
        functools.partial(_select_kernel, n=n, cap=cap),
        grid=(1,),
        in_specs=[pl.BlockSpec((N_EXPERTS, n), lambda i: (0, 0)),
                  pl.BlockSpec(aff_c.shape, lambda i: (0, 0))],
        out_specs=[pl.BlockSpec((N_EXPERTS, cap), lambda i: (0, 0))] * 2,
        out_shape=[jax.ShapeDtypeStruct((N_EXPERTS, cap), jnp.int32),
                   jax.ShapeDtypeStruct((N_EXPERTS, cap), F32)],
        compiler_params=_cparams(("arbitrary",), VMEM_LIMIT),
        name="select",
    )(aff_t, aff_c)


def _moe_kernel(idx_ref, gate_ref, wg_ref, wu_ref, wd_ref, x_hbm, out_hbm,
                gbuf, ybuf, acc_ref, gsem, osem, *, tm, n_tiles_total):
    e = pl.program_id(0)
    t = pl.program_id(1)
    nt = pl.num_programs(1)
    step = e * nt + t
    group = SCATTER_GROUP
    last = n_tiles_total - 1
    rows = tm * ROW_TILES

    def gather_start(tile_step, k, dst_slot):
        tok = idx_ref[tile_step * tm + k]
        pltpu.make_async_copy(
            x_hbm.at[pl.ds(pl.multiple_of(tok * ROW_TILES, ROW_TILES), ROW_TILES), :],
            gbuf.at[dst_slot, pl.ds(k * ROW_TILES, ROW_TILES), :], gsem.at[dst_slot]).start()

    def gather_wait(dst_slot):
        pltpu.make_async_copy(x_hbm.at[pl.ds(0, rows), :], gbuf.at[dst_slot],
                              gsem.at[dst_slot]).wait()

    def scatter_add(tile_step, src_slot, k0):
        pending = []
        for r in range(group):
            k = k0 + r
            tok = idx_ref[tile_step * tm + k]
            off = pl.multiple_of(tok * ROW_TILES, ROW_TILES)
            src = pl.multiple_of(k * ROW_TILES, ROW_TILES)
            pending.append((off, acc_ref[pl.ds(off, ROW_TILES), :]
                            + ybuf[src_slot, pl.ds(src, ROW_TILES), :]))
        for off, val in pending:
            acc_ref[pl.ds(off, ROW_TILES), :] = val

    @pl.when(step == 0)
    def _():
        acc_ref[...] = jnp.zeros_like(acc_ref)
        ybuf[...] = jnp.zeros_like(ybuf)

        def body(k, _):
            gather_start(0, k, 0)
            return 0
        lax.fori_loop(0, tm, body, 0)

    def tile_body(slot):
        nxt = jnp.minimum(step + 1, last)
        prev = jnp.maximum(step - 1, 0)
        for k in range(tm):
            gather_start(nxt, k, 1 - slot)
        gather_wait(slot)
        xe = jnp.concatenate(
            [gbuf[slot, pl.ds(s, tm, stride=ROW_TILES), :] for s in range(ROW_TILES)],
            axis=1).astype(BF16)
        g = jnp.dot(xe, wg_ref[...], preferred_element_type=F32)
        for k0 in range(0, tm // 2, group):
            scatter_add(prev, 1 - slot, k0)
        u = jnp.dot(xe, wu_ref[...], preferred_element_type=F32)
        for k0 in range(tm // 2, tm, group):
            scatter_add(prev, 1 - slot, k0)
        hid = (g * jax.nn.sigmoid(g) * u).astype(BF16)
        y = jnp.dot(hid, wd_ref[...], preferred_element_type=F32)
        gate = jnp.broadcast_to(gate_ref[...], (LANES, tm)).T
        for s in range(ROW_TILES):
            ybuf[slot, pl.ds(s, tm, stride=ROW_TILES), :] = y[:, s * LANES:(s + 1) * LANES] * gate

        @pl.when(step == last)
        def _():
            gather_wait(1 - slot)

            def body(kk, _):
                scatter_add(step, slot, kk * group)
                return 0
            lax.fori_loop(0, tm // group, body, 0)
            cp = pltpu.make_async_copy(acc_ref, out_hbm, osem)
            cp.start()
            cp.wait()

    for parity in range(2):
        pl.when(step % 2 == parity)(functools.partial(tile_body, parity))


def _moe_call(idx_flat, gates3, wg_bf, wu_bf, wd_bf, h2, *, cap, tm):
    n_rows = h2.shape[0]
    nt = cap // tm
    grid_spec = pltpu.PrefetchScalarGridSpec(
        num_scalar_prefetch=1,
        grid=(N_EXPERTS, nt),
        in_specs=[
            pl.BlockSpec((None, 1, tm), lambda e, t, idx: (e * nt + t, 0, 0)),
            pl.BlockSpec((None, D_MODEL, D_MODEL), lambda e, t, idx: (e, 0, 0)),
            pl.BlockSpec((None, D_MODEL, D_MODEL), lambda e, t, idx: (e, 0, 0)),
            pl.BlockSpec((None, D_MODEL, D_MODEL), lambda e, t, idx: (e, 0, 0)),
            pl.BlockSpec(memory_space=pl.ANY),
        ],
        out_specs=pl.BlockSpec(memory_space=pl.ANY),
        scratch_shapes=[
            pltpu.VMEM((2, tm * ROW_TILES, LANES), F32),
            pltpu.VMEM((2, tm * ROW_TILES, LANES), F32),
            pltpu.VMEM((n_rows, LANES), F32),
            pltpu.SemaphoreType.DMA((2,)),
            pltpu.SemaphoreType.DMA(()),
        ],
    )
    return pl.pallas_call(
        functools.partial(_moe_kernel, tm=tm, n_tiles_total=N_EXPERTS * nt),
        grid_spec=grid_spec,
        out_shape=jax.ShapeDtypeStruct((n_rows, LANES), F32),
        compiler_params=_cparams(("arbitrary", "arbitrary"), VMEM_LIMIT),
        name="moe",
    )(idx_flat, gates3, wg_bf, wu_bf, wd_bf, h2)


def _final_kernel(x1_ref, moe_ref, mod_ref, o_ref, *, tm):
    moe = jnp.concatenate(
        [moe_ref[pl.ds(s, tm, stride=ROW_TILES), :] for s in range(ROW_TILES)], axis=1)
    o_ref[...] = x1_ref[...] + mod_ref[0] * moe


def _final_call(x1, moe_tiles, mod3, *, seq, tm, mod_base, mod_stride):
    n = x1.shape[0]
    assert tm <= seq or mod_stride == 0, "a tile spanning sequences needs one modulation row"
    gate2_block = 5

    def mod_map(i):
        return (mod_base + mod_stride * (i * tm // seq), 0, gate2_block)

    return pl.pallas_call(
        functools.partial(_final_kernel, tm=tm),
        grid=(n // tm,),
        in_specs=[pl.BlockSpec((tm, D_MODEL), lambda i: (i, 0)),
                  pl.BlockSpec((tm * ROW_TILES, LANES), lambda i: (i, 0)),
                  pl.BlockSpec((1, 1, D_MODEL), mod_map)],
        out_specs=pl.BlockSpec((tm, D_MODEL), lambda i: (i, 0)),
        out_shape=jax.ShapeDtypeStruct((n, D_MODEL), F32),
        compiler_params=_cparams(("arbitrary",)),
        name="final",
    )(x1, moe_tiles, mod3)


def _rope_tables(seq):
    t = np.arange(seq)
    row, col = t // GRID_W, t % GRID_W
    half = HEAD_DIM // 2
    freqs = 1.0 / (ROPE_BASE ** (np.arange(0, half, 2) / half))
    ang_r = row[:, None] * freqs[None, :]
    ang_c = col[:, None] * freqs[None, :]
    ang = np.concatenate([ang_r, ang_r, ang_c, ang_c], axis=-1)
    cos = np.tile(np.cos(ang), (1, LANES // HEAD_DIM))
    sin = np.tile(np.sin(ang), (1, LANES // HEAD_DIM))
    sign = np.where((np.arange(LANES) % (HEAD_DIM // 2)) < (HEAD_DIM // 4), -1.0, 1.0)
    return jnp.asarray(cos, F32), jnp.asarray(sin * sign[None, :], F32)


def _segment_matrix():
    seg = np.arange(SEG_BLOCK) // HEAD_DIM
    return jnp.asarray((seg[:, None] == seg[None, :]) / HEAD_DIM, BF16)


def _token_mixing(x, mod3, w, cache, *, mod_base, mod_stride, tm, tm_post, heads, tq, tk,
                  cast_pre=(), cast_attn=()):
    batch, seq, _ = x.shape
    n = batch * seq
    x2d = x.reshape(n, D_MODEL)
    rope_tabs = _rope_tables(seq) if cache is not None else None
    casts = {}
    pre = list(_pre_call(x2d, mod3[:, :, :2 * D_MODEL], w["g1"], w["win"], w["seg"], w["qg"],
                         w["kg"], rope_tabs, seq=seq, tm=tm, mod_base=mod_base,
                         mod_stride=mod_stride, emit_f32_kv=cache is None, cast_w=cast_pre,
                         row_blocks=tm // PRE_ROW_BLOCK))
    casts["pre"] = [pre.pop() for _ in cast_pre][::-1]
    p, q, k, v = pre[:4]
    attn = list(_attn_call(w["lam4"], w["subln"], q, k, v, cache, batch=batch, seq=seq,
                           heads=heads, tq=tq, tw=256, tk=tk, cast_w=cast_attn))
    casts["attn"] = [attn.pop() for _ in cast_attn][::-1]
    x1, h2, aff_t, aff_c = _post_call(
        x2d, attn[0], p, mod3[:, :, 2 * D_MODEL:5 * D_MODEL], w["wout"], w["wbd"],
        w["pool_scale"], w["g2"], w["wr_t"], seq=seq, tm=tm_post, mod_base=mod_base,
        mod_stride=mod_stride)
    cap = CAPACITY_FACTOR * n // N_EXPERTS
    idx, gates = _select_call(aff_t, aff_c, cap=cap)
    return {"x1": x1, "h2": h2, "idx": idx, "gates": gates, "cap": cap, "kv": pre[4:],
            "casts": casts, "shape": (batch, seq), "tm_post": tm_post,
            "mod": (mod_base, mod_stride)}


def _channel_mixing(mixed, mod3, wg, wu, wd, *, moe_tm):
    batch, seq = mixed["shape"]
    cap = mixed["cap"]
    mod_base, mod_stride = mixed["mod"]
    moe_tiles = _moe_call(mixed["idx"].reshape(N_EXPERTS * cap),
                          mixed["gates"].reshape(-1, 1, moe_tm), wg, wu, wd, mixed["h2"],
                          cap=cap, tm=moe_tm)
    y = _final_call(mixed["x1"], moe_tiles, mod3, seq=seq, tm=mixed["tm_post"],
                    mod_base=mod_base, mod_stride=mod_stride)
    return y.reshape(batch, seq, D_MODEL)


def kernel(x_prompt, x_sample, cache_k, cache_v, c, c_ctx, norm1_g, norm2_g, w_ada, b_ada, w_in,
           q_norm_g, k_norm_g, lambda_q1, lambda_k1, lambda_q2, lambda_k2, subln_g, w_pool,
           pool_scale, w_out, w_router, w_gate, w_up, w_down):
    assert w_ada.shape[0] == 1, "single-layer stack"
    batch, seq, _ = x_prompt.shape
    dec_batch, dec_seq, _ = x_sample.shape

    pad = SUBLANES - 1 - dec_batch
    cvec = jnp.concatenate([c_ctx[None, :], c, jnp.zeros((pad, D_MODEL), F32)], axis=0)
    mod = _ada_call(cvec, w_ada[0], b_ada[0])
    mod3 = mod.reshape(SUBLANES, 1, 6 * D_MODEL)

    n_groups = w_pool.shape[1]
    grp = POOL_WIDTH // n_groups
    eye = jnp.eye(n_groups, dtype=F32)
    wbd = (w_pool[0][:, :, None, :] * eye[:, None, :, None]).reshape(POOL_WIDTH, POOL_WIDTH)

    w = {
        "g1": norm1_g[0].reshape(1, D_MODEL),
        "g2": norm2_g[0].reshape(1, D_MODEL),
        "win": w_in[0].astype(BF16),
        "seg": _segment_matrix(),
        "qg": jnp.tile(q_norm_g[0], ATTN_WIDTH // HEAD_DIM).reshape(1, ATTN_WIDTH),
        "kg": jnp.tile(k_norm_g[0], ATTN_WIDTH // HEAD_DIM).reshape(1, ATTN_WIDTH),
        "lam4": jnp.stack([lambda_q1[0], lambda_k1[0], lambda_q2[0], lambda_k2[0]], axis=0),
        "subln": subln_g[0].reshape(V_DIM, 1),
        "wbd": wbd.astype(BF16),
        "pool_scale": pool_scale[0].reshape(1, POOL_WIDTH),
        "wout": w_out[0].astype(BF16),
        "wr_t": w_router[0].T,
    }

    ctx = _token_mixing(x_prompt, mod3, w, None, mod_base=0, mod_stride=0, tm=256, tm_post=512,
                        heads=N_HEADS, tq=256, tk=256)
    past = cache_k.shape[2]
    cv = cache_v[:, 0].reshape(dec_batch * past, N_HEADS, V_DIM).transpose(1, 2, 0).astype(BF16)
    cv = jnp.concatenate([cv, jnp.ones((N_HEADS, V_EXT - V_DIM, dec_batch * past), BF16)], axis=1)
    cache = (cache_k[:, 0].reshape(dec_batch * past, ATTN_WIDTH).astype(BF16),
             cv.reshape(N_HEADS * V_EXT, dec_batch * past))
    lat = _token_mixing(x_sample, mod3, w, cache, mod_base=1, mod_stride=1, tm=512, tm_post=512,
                        heads=1, tq=1024, tk=512, cast_pre=(w_gate[0],),
                        cast_attn=(w_up[0], w_down[0]))
    (wg,), (wu, wd) = lat["casts"]["pre"], lat["casts"]["attn"]
    yp = _channel_mixing(ctx, mod3, wg, wu, wd, moe_tm=512)
    ys = _channel_mixing(lat, mod3, wg, wu, wd, moe_tm=256)
    k_ctx, v_ctx = ctx["kv"]
    ctx_k =(k_ctx.reshape(batch, N_HEADS, 2, HEAD_DIM, seq).transpose(0, 4, 1, 2, 3)
             .reshape(batch, 1, seq, N_HEADS, 2, HEAD_DIM))
    ctx_v = v_ctx.transpose(0, 2, 1, 3).reshape(batch, 1, seq, N_HEADS, V_DIM)
    return yp, ys, ctx_k, ctx_v
```

```python
import functools
import math

import numpy as np
import jax
import jax.numpy as jnp
from jax import lax
from jax.experimental import pallas as pl
from jax.experimental.pallas import tpu as pltpu

F32 = jnp.float32
BF16 = jnp.bfloat16

LANES = 128
SUBLANES = 8
MXU_TILE = 256
VMEM_LIMIT = 56 * 1024 * 1024

D_MODEL = 1024
POOL_WIDTH = 256
POOL_WINDOWS = (2, 4, 8, 16)
assert POOL_WIDTH == 2 * LANES and POOL_WINDOWS == (2, 4, 8, 16)
ATTN_WIDTH = 768
N_HEADS = 6
HEAD_DIM = 64
V_DIM = 128
IN_WIDTH = POOL_WIDTH + 3 * ATTN_WIDTH
N_EXPERTS = 16
CAPACITY_FACTOR = 2
GRID_W = 64
ROPE_BASE = 10000.0
EPS = 1e-6
LAMBDA_INIT = 0.8 - 0.6 * math.exp(-0.3 * 0)
LOG2E = math.log2(math.e)
V_EXT = V_DIM + 2 * SUBLANES
ROW_TILES = D_MODEL // LANES
HALO = 2 * SUBLANES
SEG_BLOCK = MXU_TILE
ADA_BLOCK = 1536
PRE_ROW_BLOCK = 128
SELECT_GROUP = 8
SCORES_AHEAD = 3
MOE_SLOTS = 3
SCATTER_GROUP = 8


def _cparams(sem, vmem=None):
    return pltpu.CompilerParams(dimension_semantics=sem, vmem_limit_bytes=vmem)


def _ada_kernel(c_ref, w_ref, b_ref, o_ref):
    c = c_ref[...]
    s = c * jax.nn.sigmoid(c)
    o_ref[...] = jnp.dot(s.astype(BF16), w_ref[...].astype(BF16),
                         preferred_element_type=F32) + b_ref[...]


def _ada_call(cvec, w_ada, b_ada):
    rows, d = cvec.shape
    n = w_ada.shape[1]
    bn = ADA_BLOCK
    return pl.pallas_call(
        _ada_kernel,
        grid=(n // bn,),
        in_specs=[pl.BlockSpec((rows, d), lambda j: (0, 0)),
                  pl.BlockSpec((d, bn), lambda j: (0, j)),
                  pl.BlockSpec((1, bn), lambda j: (0, j))],
        out_specs=pl.BlockSpec((rows, bn), lambda j: (0, j)),
        out_shape=jax.ShapeDtypeStruct((rows, n), F32),
        compiler_params=_cparams(("arbitrary",)),
        name="ada",
    )(cvec, w_ada, b_ada.reshape(1, n))


def _segment_mean_square(a, seg_ref):
    sq = (a * a).astype(BF16)
    seg = seg_ref[...]
    return jnp.concatenate(
        [jnp.dot(sq[:, j:j + SEG_BLOCK], seg, preferred_element_type=F32)
         for j in range(0, a.shape[1], SEG_BLOCK)], axis=1)


def _rope(a, cos, sin_signed, first_half):
    parts = []
    for h in range(a.shape[1] // LANES):
        blk = a[:, h * LANES:(h + 1) * LANES]
        fwd = pltpu.roll(blk, LANES - HEAD_DIM // 4, 1)
        bwd = pltpu.roll(blk, HEAD_DIM // 4, 1)
        parts.append(blk * cos + jnp.where(first_half, fwd, bwd) * sin_signed)
    return jnp.concatenate(parts, axis=1)


def _side_cast(refs, n_inputs, count, step=None, steps_per_block=1):
    if not count:
        return refs
    srcs, dsts = refs[n_inputs - count:n_inputs], refs[len(refs) - count:]

    def convert():
        for src_ref, dst_ref in zip(srcs, dsts):
            dst_ref[...] = src_ref[...].astype(BF16)

    if steps_per_block == 1:
        convert()
    else:
        pl.when(step % steps_per_block == 0)(convert)
    return refs[:n_inputs - count] + refs[n_inputs:len(refs) - count]


def _with_side_cast(ws, index_map, in_specs, args, out_specs, out_shapes):
    for w in ws:
        _, rows, cols = w.shape
        in_specs.append(pl.BlockSpec((None, rows, cols), index_map))
        args.append(w)
    for w in ws:
        _, rows, cols = w.shape
        out_specs.append(pl.BlockSpec((None, rows, cols), index_map))
        out_shapes.append(jax.ShapeDtypeStruct(w.shape, BF16))


def _pre_kernel(*refs, rope, emit_f32_kv, side_cast, row_blocks):
    refs = _side_cast(refs, 7 + (2 if rope else 0) + side_cast, side_cast)
    x_ref, mod_ref, g1_ref, win_ref, seg_ref, qg_ref, kg_ref = refs[:7]
    pos = 7
    if rope:
        cos_ref, sin_ref = refs[pos:pos + 2]
        pos += 2
    p_ref, q_ref, k_ref, v_ref = refs[pos:pos + 4]
    pos += 4
    if emit_f32_kv:
        kf_ref, vf_ref = refs[pos:pos + 2]

    mod = mod_ref[0]
    shift1 = mod[:, :D_MODEL]
    scale1 = mod[:, D_MODEL:2 * D_MODEL]
    tm = x_ref.shape[0]
    rb = tm // row_blocks

    def project(j):
        x = x_ref[j * rb:(j + 1) * rb, :]
        ms = jnp.mean(x * x, axis=1, keepdims=True)
        h = x * lax.rsqrt(ms + EPS) * g1_ref[...] * (1.0 + scale1) + shift1
        return jnp.dot(h.astype(BF16), win_ref[...], preferred_element_type=F32)

    def head_stats(z):
        qz = z[:, POOL_WIDTH:POOL_WIDTH + ATTN_WIDTH]
        kz = z[:, POOL_WIDTH + ATTN_WIDTH:POOL_WIDTH + 2 * ATTN_WIDTH]
        return _segment_mean_square(qz, seg_ref), _segment_mean_square(kz, seg_ref)

    def finish(j, z, stats):
        rows = slice(j * rb, (j + 1) * rb)
        p_ref[rows, :] = z[:, :POOL_WIDTH]
        qz = z[:, POOL_WIDTH:POOL_WIDTH + ATTN_WIDTH]
        kz = z[:, POOL_WIDTH + ATTN_WIDTH:POOL_WIDTH + 2 * ATTN_WIDTH]
        vz = z[:, POOL_WIDTH + 2 * ATTN_WIDTH:]
        qn = qz * lax.rsqrt(stats[0] + EPS) * qg_ref[...]
        kn = kz * lax.rsqrt(stats[1] + EPS) * kg_ref[...]
        if rope:
            cos = cos_ref[rows, :]
            sin_signed = sin_ref[rows, :]
            lane = lax.broadcasted_iota(jnp.int32, cos.shape, 1)
            first_half = (lane % (HEAD_DIM // 2)) < (HEAD_DIM // 4)
            qn = _rope(qn, cos, sin_signed, first_half)
            kn = _rope(kn, cos, sin_signed, first_half)
        q_ref[:, rows] = (qn * (LOG2E / math.sqrt(HEAD_DIM))).T.astype(BF16)
        k_ref[rows, :] = kn.astype(BF16)
        vt = vz.T
        ones = jnp.ones((V_EXT - V_DIM, rb), BF16)
        for h in range(N_HEADS):
            v_ref[h * V_EXT:h * V_EXT + V_DIM, rows] = vt[h * V_DIM:(h + 1) * V_DIM, :].astype(BF16)
            v_ref[h * V_EXT + V_DIM:(h + 1) * V_EXT, rows] = ones
        if emit_f32_kv:
            kf_ref[0, :, rows] = kn.T
            for h in range(N_HEADS):
                vf_ref[0, h, rows, :] = vz[:, h * V_DIM:(h + 1) * V_DIM]

    z_prev = project(0)
    for j in range(row_blocks):
        stats = head_stats(z_prev)
        z_next = project(j + 1) if j + 1 < row_blocks else None
        finish(j, z_prev, stats)
        z_prev = z_next


def _pre_call(x2d, mod3, g1, win_bf, seg, qg, kg, rope_tabs, *, seq, tm, mod_base, mod_stride,
              emit_f32_kv, cast_w=(), row_blocks=1):
    n = x2d.shape[0]
    rope = rope_tabs is not None
    tiles_per_seq = seq // tm

    def mod_map(i):
        return (mod_base + mod_stride * (i // tiles_per_seq), 0, 0)

    in_specs = [
        pl.BlockSpec((tm, D_MODEL), lambda i: (i, 0)),
        pl.BlockSpec((1, 1, 2 * D_MODEL), mod_map),
        pl.BlockSpec((1, D_MODEL), lambda i: (0, 0)),
        pl.BlockSpec((D_MODEL, IN_WIDTH), lambda i: (0, 0)),
        pl.BlockSpec((SEG_BLOCK, SEG_BLOCK), lambda i: (0, 0)),
        pl.BlockSpec((1, ATTN_WIDTH), lambda i: (0, 0)),
        pl.BlockSpec((1, ATTN_WIDTH), lambda i: (0, 0)),
    ]
    args = [x2d, mod3, g1, win_bf, seg, qg, kg]
    if rope:
        in_specs += [pl.BlockSpec((tm, LANES), lambda i: (i % tiles_per_seq, 0))] * 2
        args += list(rope_tabs)
    out_shapes = [jax.ShapeDtypeStruct((n, POOL_WIDTH), F32)]
    out_specs = [pl.BlockSpec((tm, POOL_WIDTH), lambda i: (i, 0))]
    out_shapes += [jax.ShapeDtypeStruct((ATTN_WIDTH, n), BF16),
                   jax.ShapeDtypeStruct((n, ATTN_WIDTH), BF16),
                   jax.ShapeDtypeStruct((N_HEADS * V_EXT, n), BF16)]
    out_specs += [pl.BlockSpec((ATTN_WIDTH, tm), lambda i: (0, i)),
                  pl.BlockSpec((tm, ATTN_WIDTH), lambda i: (i, 0)),
                  pl.BlockSpec((N_HEADS * V_EXT, tm), lambda i: (0, i))]
    if emit_f32_kv:
        def seq_map(i):
            return (i // tiles_per_seq, 0, i % tiles_per_seq)

        out_shapes += [jax.ShapeDtypeStruct((n // seq, ATTN_WIDTH, seq), F32),
                       jax.ShapeDtypeStruct((n // seq, N_HEADS, seq, V_DIM), F32)]
        out_specs += [pl.BlockSpec((1, ATTN_WIDTH, tm), seq_map),
                      pl.BlockSpec((1, N_HEADS, tm, V_DIM),
                                   lambda i: (i // tiles_per_seq, 0, i % tiles_per_seq, 0))]
    assert all(w.shape[0] == n // tm for w in cast_w), "one expert matrix per grid step"
    _with_side_cast(cast_w, lambda i: (i, 0, 0), in_specs, args, out_specs, out_shapes)
    return pl.pallas_call(
        functools.partial(_pre_kernel, rope=rope, emit_f32_kv=emit_f32_kv,
                          side_cast=len(cast_w), row_blocks=row_blocks),
        grid=(n // tm,),
        in_specs=in_specs,
        out_specs=out_specs,
        out_shape=out_shapes,
        compiler_params=_cparams(("arbitrary",), VMEM_LIMIT),
        name="pre_rope" if rope else "pre",
    )(*args)


def _attn_kernel(*refs, has_cache, heads, tq, tw, tk, seq, cache_len, side_cast,
                 cast_steps_per_block):
    step = ((pl.program_id(0) * pl.num_programs(1) + pl.program_id(1)) * pl.num_programs(2)
            + pl.program_id(2))
    refs = _side_cast(refs, (7 if has_cache else 5) + side_cast, side_cast, step,
                      cast_steps_per_block)
    if has_cache:
        lam_ref, g_ref, q_ref, kc_ref, vc_ref, k_ref, v_ref, o_ref = refs
    else:
        lam_ref, g_ref, q_ref, k_ref, v_ref, o_ref = refs

    lv = lam_ref[...]
    lam = (jnp.exp(jnp.sum(lv[0:1] * lv[1:2], axis=1, keepdims=True))
           - jnp.exp(jnp.sum(lv[2:3] * lv[3:4], axis=1, keepdims=True)) + LAMBDA_INIT)

    row = lax.broadcasted_iota(jnp.int32, (LANES, tq), 0)
    zero = jnp.zeros((LANES, tq), BF16)

    def sub_queries(h):
        qt = q_ref[h * LANES:(h + 1) * LANES, :]
        return (jnp.where(row < HEAD_DIM, qt, zero), jnp.where(row >= HEAD_DIM, qt, zero))

    def scores(kb, q_one):
        return jnp.dot(kb, q_one, preferred_element_type=F32)

    def softmax_step(s, m):
        m_new = jnp.maximum(m, jnp.max(s, axis=0, keepdims=True))
        return m_new, jnp.exp2(m - m_new), jnp.exp2(s - m_new).astype(BF16)

    def accumulate(vb, p, alpha, acc):
        return alpha * acc + jnp.dot(vb, p, preferred_element_type=F32)

    chunks = []
    if has_cache:
        chunks += [(kc_ref, vc_ref, j) for j in range(cache_len // tk)]
    chunks += [(k_ref, v_ref, j) for j in range(seq // tk)]

    def keys(c, h):
        kr, _, j = chunks[c]
        return kr[j * tk:(j + 1) * tk, h * LANES:(h + 1) * LANES]

    def values(c, h):
        _, vr, j = chunks[c]
        return vr[h * V_EXT:(h + 1) * V_EXT, j * tk:(j + 1) * tk]

    chains = []
    for h in range(heads):
        q_sub = sub_queries(h)
        chains += [(h, q_sub[sub][:, w * tw:(w + 1) * tw])
                   for w in range(tq // tw) for sub in range(2)]
    per_head = len(chains) // heads
    items = [(c, ch) for h in range(heads) for c in range(len(chunks))
             for ch in range(h * per_head, (h + 1) * per_head)]
    m = [jnp.full((1, tw), -jnp.inf, F32)] * len(chains)
    acc = [jnp.zeros((V_EXT, tw), F32)] * len(chains)
    queue = [scores(keys(c, chains[ch][0]), chains[ch][1]) for c, ch in items[:SCORES_AHEAD]]
    for i, (c, ch) in enumerate(items):
        s_cur = queue.pop(0)
        if i + SCORES_AHEAD < len(items):
            nc, nch = items[i + SCORES_AHEAD]
            queue.append(scores(keys(nc, chains[nch][0]), chains[nch][1]))
        m[ch], alpha, p = softmax_step(s_cur, m[ch])
        acc[ch] = accumulate(values(c, chains[ch][0]), p, alpha, acc[ch])
    for h in range(heads):
        outs = []
        for w in range(tq // tw):
            a1, a2 = acc[h * per_head + 2 * w], acc[h * per_head + 2 * w + 1]
            outs.append(a1[:V_DIM] / a1[V_DIM:V_DIM + 1]
                        - lam * (a2[:V_DIM] / a2[V_DIM:V_DIM + 1]))
        o = outs[0] if len(outs) == 1 else jnp.concatenate(outs, axis=1)
        y = (o * lax.rsqrt(jnp.mean(o * o, axis=0, keepdims=True) + EPS) * g_ref[...]
             * (1.0 - LAMBDA_INIT))
        o_ref[:, h * LANES:(h + 1) * LANES] = y.T.astype(BF16)


def _attn_call(lam4, subln_col, qt, k, vt, cache, *, batch, seq, heads, tq, tw, tk, cast_w=()):
    has_cache = cache is not None
    nq = seq // tq
    in_specs = [
        pl.BlockSpec((4, HEAD_DIM), lambda b, h, i: (0, 0)),
        pl.BlockSpec((V_DIM, 1), lambda b, h, i: (0, 0)),
        pl.BlockSpec((heads * LANES, tq), lambda b, h, i: (h, b * nq + i)),
    ]
    args = [lam4, subln_col, qt]
    cache_len = 0
    if has_cache:
        kc, vct = cache
        cache_len = kc.shape[0] // batch
        in_specs += [pl.BlockSpec((cache_len, heads * LANES), lambda b, h, i: (b, h)),
                     pl.BlockSpec((heads * V_EXT, cache_len), lambda b, h, i: (h, b))]
        args += [kc, vct]
    in_specs += [pl.BlockSpec((seq, heads * LANES), lambda b, h, i: (b, h)),
                 pl.BlockSpec((heads * V_EXT, seq), lambda b, h, i: (h, b))]
    args += [k, vt]
    out_specs = [pl.BlockSpec((tq, heads * LANES), lambda b, h, i: (b * nq + i, h))]
    out_shapes = [jax.ShapeDtypeStruct((batch * seq, ATTN_WIDTH), BF16)]
    grid = (batch, N_HEADS // heads, nq)
    steps_per_block = 1
    if cast_w:
        steps_per_block, rem = divmod(grid[0] * grid[1] * grid[2], cast_w[0].shape[0])
        assert rem == 0 and steps_per_block > 0, "whole number of grid steps per expert matrix"
        assert all(w.shape == cast_w[0].shape for w in cast_w)
    _with_side_cast(
        cast_w, lambda b, h, i: (((b * grid[1] + h) * grid[2] + i) // steps_per_block, 0, 0),
        in_specs, args, out_specs, out_shapes)
    return pl.pallas_call(
        functools.partial(_attn_kernel, has_cache=has_cache, heads=heads, tq=tq, tw=tw, tk=tk,
                          seq=seq, cache_len=cache_len, side_cast=len(cast_w),
                          cast_steps_per_block=steps_per_block),
        grid=grid,
        in_specs=in_specs,
        out_specs=out_specs,
        out_shape=out_shapes,
        compiler_params=_cparams(("arbitrary", "arbitrary", "arbitrary"), VMEM_LIMIT),
        name="attn_cache" if has_cache else "attn",
    )(*args)


def _post_kernel(x_ref, a_ref, p_ref, pp_ref, pn_ref, inv_ref, mod_ref, wout_ref, wbd_ref, ps_ref,
                 g2_ref, wr_ref, x1_ref, h2_ref, aff_ref, affc_ref, *, tm, seq):
    i = pl.program_id(0)

    def centred_mean_minus_self(rows, prev, nxt, inv_count):
        r = rows.shape[0]
        ext = jnp.concatenate([prev, rows, nxt], axis=0)
        n_ext = r + 2 * HALO
        s2 = ext + pltpu.roll(ext, 1, 0)
        s4 = pltpu.roll(s2, 1, 0) + pltpu.roll(s2, n_ext - 1, 0)
        s4_hi = s4[:, LANES:]
        s8 = pltpu.roll(s4_hi, 2, 0) + pltpu.roll(s4_hi, n_ext - 2, 0)
        s16 = pltpu.roll(s8, 4, 0) + pltpu.roll(s8, n_ext - 4, 0)
        first = lax.broadcasted_iota(jnp.int32, (r, LANES), 1) < POOL_WIDTH // len(POOL_WINDOWS)
        inner = slice(HALO, HALO + r)
        win = jnp.concatenate([jnp.where(first, s2[inner, :LANES], s4[inner, :LANES]),
                               jnp.where(first, s8[inner], s16[inner])], axis=1)
        return win * inv_count - rows

    if tm <= seq:
        tiles_per_seq = seq // tm
        ti = i % tiles_per_seq
        pooled = centred_mean_minus_self(
            p_ref[...], jnp.where(ti == 0, 0.0, pp_ref[...]),
            jnp.where(ti == tiles_per_seq - 1, 0.0, pn_ref[...]), inv_ref[...])
    else:
        halo = jnp.zeros((HALO, POOL_WIDTH), F32)
        pooled = jnp.concatenate(
            [centred_mean_minus_self(p_ref[j * seq:(j + 1) * seq, :], halo, halo, inv_ref[...])
             for j in range(tm // seq)], axis=0)
    pool = jnp.dot(pooled.astype(BF16), wbd_ref[...], preferred_element_type=F32) * ps_ref[...]

    cat = jnp.concatenate([pool.astype(BF16), a_ref[...]], axis=1)
    mix = jnp.dot(cat, wout_ref[...], preferred_element_type=F32)
    mod = mod_ref[0]
    gate1 = mod[:, 0:D_MODEL]
    shift2 = mod[:, D_MODEL:2 * D_MODEL]
    scale2 = mod[:, 2 * D_MODEL:3 * D_MODEL]
    x1 = x_ref[...] + gate1 * mix
    x1_ref[...] = x1
    ms = jnp.mean(x1 * x1, axis=1, keepdims=True)
    h2 = x1 * lax.rsqrt(ms + EPS) * g2_ref[...] * (1.0 + scale2) + shift2
    for s in range(ROW_TILES):
        h2_ref[pl.ds(s, tm, stride=ROW_TILES), :] = h2[:, s * LANES:(s + 1) * LANES]

    def pieces(v):
        hi = v.astype(BF16)
        return hi, (v - hi.astype(F32)).astype(BF16)

    def contract(a, b):
        return lax.dot_general(a, b, (((1,), (1,)), ((), ())), preferred_element_type=F32)

    w_hi, w_lo = pieces(wr_ref[...])
    h_hi, h_lo = pieces(h2)
    logits = contract(w_hi, h_hi) + (contract(w_hi, h_lo) + contract(w_lo, h_hi))
    e = jnp.exp(logits - jnp.max(logits, axis=0, keepdims=True))
    aff = e / jnp.sum(e, axis=0, keepdims=True)
    aff_ref[...] = aff
    for c in range(tm // LANES):
        affc_ref[c * N_EXPERTS:(c + 1) * N_EXPERTS, :] = aff[:, c * LANES:(c + 1) * LANES]


def _inverse_window_counts(seq):
    t = np.arange(seq)[:, None]
    left = np.repeat(np.array(POOL_WINDOWS) // 2, POOL_WIDTH // len(POOL_WINDOWS))[None, :]
    lo = np.maximum(t - left, 0)
    hi = np.minimum(t + left - 1, seq - 1) + 1
    return jnp.asarray(1.0 / (hi - lo), F32)


def _post_call(x2d, attn, p, mod3, wout_bf, wbd_bf, pool_scale, g2, wr_t, *, seq, tm, mod_base,
               mod_stride):
    n = x2d.shape[0]
    assert tm <= seq or mod_stride == 0, "a tile spanning sequences needs one modulation row"
    halo_per_tile = tm // HALO
    n_halo = n // HALO

    def mod_map(i):
        return (mod_base + mod_stride * (i * tm // seq), 0, 0)

    in_specs = [
            pl.BlockSpec((tm, D_MODEL), lambda i: (i, 0)),
            pl.BlockSpec((tm, ATTN_WIDTH), lambda i: (i, 0)),
            pl.BlockSpec((tm, POOL_WIDTH), lambda i: (i, 0)),
            pl.BlockSpec((HALO, POOL_WIDTH), lambda i: (jnp.maximum(i * halo_per_tile - 1, 0), 0)),
            pl.BlockSpec((HALO, POOL_WIDTH),
                         lambda i: (jnp.minimum((i + 1) * halo_per_tile, n_halo - 1), 0)),
            (pl.BlockSpec((tm, POOL_WIDTH), lambda i: (i % (seq // tm), 0)) if tm <= seq
             else pl.BlockSpec((seq, POOL_WIDTH), lambda i: (0, 0))),
            pl.BlockSpec((1, 1, 3 * D_MODEL), lambda i: mod_map(i)[:2] + (0,)),
            pl.BlockSpec((D_MODEL, D_MODEL), lambda i: (0, 0)),
            pl.BlockSpec((POOL_WIDTH, POOL_WIDTH), lambda i: (0, 0)),
            pl.BlockSpec((1, POOL_WIDTH), lambda i: (0, 0)),
            pl.BlockSpec((1, D_MODEL), lambda i: (0, 0)),
            pl.BlockSpec((N_EXPERTS, D_MODEL), lambda i: (0, 0)),
    ]
    out_specs = [
            pl.BlockSpec((tm, D_MODEL), lambda i: (i, 0)),
            pl.BlockSpec((tm * ROW_TILES, LANES), lambda i: (i, 0)),
            pl.BlockSpec((N_EXPERTS, tm), lambda i: (0, i)),
            pl.BlockSpec((tm // LANES * N_EXPERTS, LANES), lambda i: (i, 0)),
    ]
    out_shapes = [
            jax.ShapeDtypeStruct((n, D_MODEL), F32),
            jax.ShapeDtypeStruct((n * ROW_TILES, LANES), F32),
            jax.ShapeDtypeStruct((N_EXPERTS, n), F32),
            jax.ShapeDtypeStruct((n // LANES * N_EXPERTS, LANES), F32),
    ]
    args = [x2d, attn, p, p, p, _inverse_window_counts(seq), mod3, wout_bf, wbd_bf, pool_scale, g2,
            wr_t]
    return pl.pallas_call(
        functools.partial(_post_kernel, tm=tm, seq=seq),
        grid=(n // tm,),
        in_specs=in_specs,
        out_specs=out_specs,
        out_shape=out_shapes,
        compiler_params=_cparams(("arbitrary",), VMEM_LIMIT),
        name="post",
    )(*args)


def _select_kernel(a_ref, ac_ref, idx_ref, gate_ref, *, n, cap):
    nc = n // LANES
    a = a_ref[...]
    thr = jnp.zeros((N_EXPERTS, 1), jnp.int32)
    for bit in range(30, -1, -1):
        cand = thr | (1 << bit)
        cnt = jnp.sum(jnp.where(a >= pltpu.bitcast(cand, F32), 1.0, 0.0), axis=1, keepdims=True)
        thr = jnp.where(cnt >= cap, cand, thr)
    thr_all = pltpu.bitcast(thr, F32)
    need_all = cap - jnp.sum(jnp.where(a > thr_all, 1.0, 0.0), axis=1, keepdims=True)

    r = lax.broadcasted_iota(jnp.int32, (LANES, LANES), 0)
    c = lax.broadcasted_iota(jnp.int32, (LANES, LANES), 1)
    upper = jnp.where(r <= c, 1.0, 0.0).astype(BF16)
    lower = jnp.where(c < r, 1.0, 0.0).astype(BF16)
    row_valid = r < nc
    chunk_col = lax.broadcasted_iota(jnp.int32, (LANES, 1), 0).astype(F32)
    slot = lax.broadcasted_iota(jnp.int32, (1, cap), 1).astype(F32)

    def lane_counts(mask):
        local = jnp.dot(mask.astype(BF16), upper, preferred_element_type=F32)
        total = jnp.broadcast_to(local[:, LANES - 1:LANES], (LANES, LANES))
        before = jnp.dot(lower, total.astype(BF16), preferred_element_type=F32)
        return local, total, before

    def ties(e):
        av = ac_ref[pl.ds(e, nc, stride=N_EXPERTS), :]
        if nc < LANES:
            av = jnp.concatenate([av, jnp.zeros((LANES - nc, LANES), F32)], axis=0)
        thr_e = thr_all[e:e + 1, :]
        above = jnp.where(row_valid & (av > thr_e), 1.0, 0.0)
        tied = jnp.where(row_valid & (av == thr_e), 1.0, 0.0)
        t_local, _, t_before = lane_counts(tied)
        return {"av": av, "above": above, "tied": tied, "tie_rank": t_local + t_before}

    def selection(e, st):
        sel = st["above"] + st["tied"] * jnp.where(st["tie_rank"] <= need_all[e:e + 1, :], 1.0, 0.0)
        s_local, s_total, s_before = lane_counts(sel)
        rank = jnp.where(sel > 0.0, s_local, 0.0)
        start = s_before[:, 0:1]
        return {"av": st["av"], "rank": rank, "start": start, "stop": start + s_total[:, 0:1]}

    def pick(e, st):
        start = st["start"]
        onehot = jnp.where((slot >= start) & (slot < st["stop"]), 1.0, 0.0)
        chunk_of_slot = jnp.sum(onehot * chunk_col, axis=0, keepdims=True)
        start_of_slot = jnp.sum(onehot * start, axis=0, keepdims=True)
        at = st["av"].T
        hi = at.astype(BF16)
        rest = at - hi.astype(F32)
        mid = rest.astype(BF16)
        lo = (rest - mid.astype(F32)).astype(BF16)
        lhs = jnp.concatenate([st["rank"].T.astype(BF16), hi, mid, lo], axis=0)
        picked = jnp.dot(lhs, onehot.astype(BF16), preferred_element_type=F32)
        return {"picked": picked, "chunk_of_slot": chunk_of_slot, "start_of_slot": start_of_slot}

    def emit(e, st):
        picked = st["picked"]
        rank_p = picked[0:LANES]
        aff_p = picked[LANES:2 * LANES] + picked[2 * LANES:3 * LANES] + picked[3 * LANES:]
        hit = rank_p == (slot - st["start_of_slot"] + 1.0)
        lane_of_slot = jnp.sum(jnp.where(hit, chunk_col, 0.0), axis=0, keepdims=True)
        idx_ref[e:e + 1, :] = (st["chunk_of_slot"] * LANES + lane_of_slot).astype(jnp.int32)
        gate_ref[e:e + 1, :] = jnp.sum(jnp.where(hit, aff_p, 0.0), axis=0, keepdims=True)

    for e0 in range(0, N_EXPERTS, SELECT_GROUP):
        group = range(e0, e0 + SELECT_GROUP)
        states = {e: ties(e) for e in group}
        states = {e: selection(e, states[e]) for e in group}
        states = {e: pick(e, states[e]) for e in group}
        for e in group:
            emit(e, states[e])


def _select_call(aff_t, aff_c, *, cap):
    n = aff_t.shape[1]
    assert n % LANES == 0 and n // LANES <= LANES and cap % LANES == 0
    return pl.pallas_call(
        functools.partial(_select_kernel, n=n, cap=cap),
        grid=(1,),
        in_specs=[pl.BlockSpec((N_EXPERTS, n), lambda i: (0, 0)),
                  pl.BlockSpec(aff_c.shape, lambda i: (0, 0))],
        out_specs=[pl.BlockSpec((N_EXPERTS, cap), lambda i: (0, 0))] * 2,
        out_shape=[jax.ShapeDtypeStruct((N_EXPERTS, cap), jnp.int32),
                   jax.ShapeDtypeStruct((N_EXPERTS, cap), F32)],
        compiler_params=_cparams(("arbitrary",), VMEM_LIMIT),
        name="select",
    )(aff_t, aff_c)


def _moe_kernel(idx_ref, gate_ref, wg_ref, wu_ref, wd_ref, x_hbm, out_hbm,
                gbuf, ybuf, acc_ref, gsem, osem, *, tm, n_tiles_total):
    e = pl.program_id(0)
    t = pl.program_id(1)
    nt = pl.num_programs(1)
    step = e * nt + t
    group = SCATTER_GROUP
    last = n_tiles_total - 1
    rows = tm * ROW_TILES

    def gather_start(tile_step, k, dst_slot):
        tok = idx_ref[tile_step * tm + k]
        pltpu.make_async_copy(
            x_hbm.at[pl.ds(pl.multiple_of(tok * ROW_TILES, ROW_TILES), ROW_TILES), :],
            gbuf.at[dst_slot, pl.ds(k * ROW_TILES, ROW_TILES), :], gsem.at[dst_slot]).start()

    def gather_wait(dst_slot):
        pltpu.make_async_copy(x_hbm.at[pl.ds(0, rows), :], gbuf.at[dst_slot],
                              gsem.at[dst_slot]).wait()

    def scatter_add(tile_step, src_slot, k0):
        pending = []
        for r in range(group):
            k = k0 + r
            tok = idx_ref[tile_step * tm + k]
            off = pl.multiple_of(tok * ROW_TILES, ROW_TILES)
            src = pl.multiple_of(k * ROW_TILES, ROW_TILES)
            pending.append((off, acc_ref[pl.ds(off, ROW_TILES), :]
                            + ybuf[src_slot, pl.ds(src, ROW_TILES), :]))
        for off, val in pending:
            acc_ref[pl.ds(off, ROW_TILES), :] = val

    @pl.when(step == 0)
    def _():
        acc_ref[...] = jnp.zeros_like(acc_ref)
        ybuf[...] = jnp.zeros_like(ybuf)

        def body(k, _):
            gather_start(0, k, 0)
            gather_start(jnp.minimum(1, last), k, 1)
            return 0
        lax.fori_loop(0, tm, body, 0)

    def tile_body(slot):
        behind = (slot + MOE_SLOTS - 1) % MOE_SLOTS
        ahead = jnp.minimum(step + 2, last)
        prev = jnp.maximum(step - 1, 0)
        gather_wait(slot)
        xe = jnp.concatenate(
            [gbuf[slot, pl.ds(s, tm, stride=ROW_TILES), :] for s in range(ROW_TILES)],
            axis=1).astype(BF16)
        g = jnp.dot(xe, wg_ref[...], preferred_element_type=F32)
        for k0 in range(0, tm // 2, group):
            scatter_add(prev, behind, k0)
        u = jnp.dot(xe, wu_ref[...], preferred_element_type=F32)
        for k0 in range(tm // 2, tm, group):
            scatter_add(prev, behind, k0)
        hid = (g * jax.nn.sigmoid(g) * u).astype(BF16)
        y = jnp.dot(hid, wd_ref[...], preferred_element_type=F32)
        for k in range(tm):
            gather_start(ahead, k, behind)
        gate = jnp.broadcast_to(gate_ref[...], (LANES, tm)).T
        for s in range(ROW_TILES):
            ybuf[slot, pl.ds(s, tm, stride=ROW_TILES), :] = y[:, s * LANES:(s + 1) * LANES] * gate

        @pl.when(step == last)
        def _():
            gather_wait((slot + 1) % MOE_SLOTS)
            gather_wait(behind)

            def body(kk, _):
                scatter_add(step, slot, kk * group)
                return 0
            lax.fori_loop(0, tm // group, body, 0)
            cp = pltpu.make_async_copy(acc_ref, out_hbm, osem)
            cp.start()
            cp.wait()

    for ring in range(MOE_SLOTS):
        pl.when(step % MOE_SLOTS == ring)(functools.partial(tile_body, ring))


def _moe_call(idx_flat, gates3, wg_bf, wu_bf, wd_bf, h2, *, cap, tm):
    n_rows = h2.shape[0]
    nt = cap // tm
    grid_spec = pltpu.PrefetchScalarGridSpec(
        num_scalar_prefetch=1,
        grid=(N_EXPERTS, nt),
        in_specs=[
            pl.BlockSpec((None, 1, tm), lambda e, t, idx: (e * nt + t, 0, 0)),
            pl.BlockSpec((None, D_MODEL, D_MODEL), lambda e, t, idx: (e, 0, 0)),
            pl.BlockSpec((None, D_MODEL, D_MODEL), lambda e, t, idx: (e, 0, 0)),
            pl.BlockSpec((None, D_MODEL, D_MODEL), lambda e, t, idx: (e, 0, 0)),
            pl.BlockSpec(memory_space=pl.ANY),
        ],
        out_specs=pl.BlockSpec(memory_space=pl.ANY),
        scratch_shapes=[
            pltpu.VMEM((MOE_SLOTS, tm * ROW_TILES, LANES), F32),
            pltpu.VMEM((MOE_SLOTS, tm * ROW_TILES, LANES), F32),
            pltpu.VMEM((n_rows, LANES), F32),
            pltpu.SemaphoreType.DMA((MOE_SLOTS,)),
            pltpu.SemaphoreType.DMA(()),
        ],
    )
    return pl.pallas_call(
        functools.partial(_moe_kernel, tm=tm, n_tiles_total=N_EXPERTS * nt),
        grid_spec=grid_spec,
        out_shape=jax.ShapeDtypeStruct((n_rows, LANES), F32),
        compiler_params=_cparams(("arbitrary", "arbitrary"), VMEM_LIMIT),
        name="moe",
    )(idx_flat, gates3, wg_bf, wu_bf, wd_bf, h2)


def _final_kernel(x1_ref, moe_ref, mod_ref, o_ref, *, tm):
    moe = jnp.concatenate(
        [moe_ref[pl.ds(s, tm, stride=ROW_TILES), :] for s in range(ROW_TILES)], axis=1)
    o_ref[...] = x1_ref[...] + mod_ref[0] * moe


def _final_call(x1, moe_tiles, mod3, *, seq, tm, mod_base, mod_stride):
    n = x1.shape[0]
    assert tm <= seq or mod_stride == 0, "a tile spanning sequences needs one modulation row"
    gate2_block = 5

    def mod_map(i):
        return (mod_base + mod_stride * (i * tm // seq), 0, gate2_block)

    return pl.pallas_call(
        functools.partial(_final_kernel, tm=tm),
        grid=(n // tm,),
        in_specs=[pl.BlockSpec((tm, D_MODEL), lambda i: (i, 0)),
                  pl.BlockSpec((tm * ROW_TILES, LANES), lambda i: (i, 0)),
                  pl.BlockSpec((1, 1, D_MODEL), mod_map)],
        out_specs=pl.BlockSpec((tm, D_MODEL), lambda i: (i, 0)),
        out_shape=jax.ShapeDtypeStruct((n, D_MODEL), F32),
        compiler_params=_cparams(("arbitrary",)),
        name="final",
    )(x1, moe_tiles, mod3)


def _rope_tables(seq):
    t = np.arange(seq)
    row, col = t // GRID_W, t % GRID_W
    half = HEAD_DIM // 2
    freqs = 1.0 / (ROPE_BASE ** (np.arange(0, half, 2) / half))
    ang_r = row[:, None] * freqs[None, :]
    ang_c = col[:, None] * freqs[None, :]
    ang = np.concatenate([ang_r, ang_r, ang_c, ang_c], axis=-1)
    cos = np.tile(np.cos(ang), (1, LANES // HEAD_DIM))
    sin = np.tile(np.sin(ang), (1, LANES // HEAD_DIM))
    sign = np.where((np.arange(LANES) % (HEAD_DIM // 2)) < (HEAD_DIM // 4), -1.0, 1.0)
    return jnp.asarray(cos, F32), jnp.asarray(sin * sign[None, :], F32)


def _segment_matrix():
    seg = np.arange(SEG_BLOCK) // HEAD_DIM
    return jnp.asarray((seg[:, None] == seg[None, :]) / HEAD_DIM, BF16)


def _token_mixing(x, mod3, w, cache, *, mod_base, mod_stride, tm, tm_post, heads, tq, tk,
                  cast_pre=(), cast_attn=()):
    batch, seq, _ = x.shape
    n = batch * seq
    x2d = x.reshape(n, D_MODEL)
    rope_tabs = _rope_tables(seq) if cache is not None else None
    casts = {}
    pre = list(_pre_call(x2d, mod3[:, :, :2 * D_MODEL], w["g1"], w["win"], w["seg"], w["qg"],
                         w["kg"], rope_tabs, seq=seq, tm=tm, mod_base=mod_base,
                         mod_stride=mod_stride, emit_f32_kv=cache is None, cast_w=cast_pre,
                         row_blocks=tm // PRE_ROW_BLOCK))
    casts["pre"] = [pre.pop() for _ in cast_pre][::-1]
    p, q, k, v = pre[:4]
    attn = list(_attn_call(w["lam4"], w["subln"], q, k, v, cache, batch=batch, seq=seq,
                           heads=heads, tq=tq, tw=256, tk=tk, cast_w=cast_attn))
    casts["attn"] = [attn.pop() for _ in cast_attn][::-1]
    x1, h2, aff_t, aff_c = _post_call(
        x2d, attn[0], p, mod3[:, :, 2 * D_MODEL:5 * D_MODEL], w["wout"], w["wbd"],
        w["pool_scale"], w["g2"], w["wr_t"], seq=seq, tm=tm_post, mod_base=mod_base,
        mod_stride=mod_stride)
    cap = CAPACITY_FACTOR * n // N_EXPERTS
    idx, gates = _select_call(aff_t, aff_c, cap=cap)
    return {"x1": x1, "h2": h2, "idx": idx, "gates": gates, "cap": cap, "kv": pre[4:],
            "casts": casts, "shape": (batch, seq), "tm_post": tm_post,
            "mod": (mod_base, mod_stride)}


def _channel_mixing(mixed, mod3, wg, wu, wd, *, moe_tm):
    batch, seq = mixed["shape"]
    cap = mixed["cap"]
    mod_base, mod_stride = mixed["mod"]
    moe_tiles = _moe_call(mixed["idx"].reshape(N_EXPERTS * cap),
                          mixed["gates"].reshape(-1, 1, moe_tm), wg, wu, wd, mixed["h2"],
                          cap=cap, tm=moe_tm)
    y = _final_call(mixed["x1"], moe_tiles, mod3, seq=seq, tm=mixed["tm_post"],
                    mod_base=mod_base, mod_stride=mod_stride)
    return y.reshape(batch, seq, D_MODEL)


def kernel(x_prompt, x_sample, cache_k, cache_v, c, c_ctx, norm1_g, norm2_g, w_ada, b_ada, w_in,
           q_norm_g, k_norm_g, lambda_q1, lambda_k1, lambda_q2, lambda_k2, subln_g, w_pool,
           pool_scale, w_out, w_router, w_gate, w_up, w_down):
    assert w_ada.shape[0] == 1, "single-layer stack"
    batch, seq, _ = x_prompt.shape
    dec_batch, dec_seq, _ = x_sample.shape

    pad = SUBLANES - 1 - dec_batch
    cvec = jnp.concatenate([c_ctx[None, :], c, jnp.zeros((pad, D_MODEL), F32)], axis=0)
    mod = _ada_call(cvec, w_ada[0], b_ada[0])
    mod3 = mod.reshape(SUBLANES, 1, 6 * D_MODEL)

    n_groups = w_pool.shape[1]
    grp = POOL_WIDTH // n_groups
    eye = jnp.eye(n_groups, dtype=F32)
    wbd = (w_pool[0][:, :, None, :] * eye[:, None, :, None]).reshape(POOL_WIDTH, POOL_WIDTH)

    w = {
        "g1": norm1_g[0].reshape(1, D_MODEL),
        "g2": norm2_g[0].reshape(1, D_MODEL),
        "win": w_in[0].astype(BF16),
        "seg": _segment_matrix(),
        "qg": jnp.tile(q_norm_g[0], ATTN_WIDTH // HEAD_DIM).reshape(1, ATTN_WIDTH),
        "kg": jnp.tile(k_norm_g[0], ATTN_WIDTH // HEAD_DIM).reshape(1, ATTN_WIDTH),
        "lam4": jnp.stack([lambda_q1[0], lambda_k1[0], lambda_q2[0], lambda_k2[0]], axis=0),
        "subln": subln_g[0].reshape(V_DIM, 1),
        "wbd": wbd.astype(BF16),
        "pool_scale": pool_scale[0].reshape(1, POOL_WIDTH),
        "wout": w_out[0].astype(BF16),
        "wr_t": w_router[0].T,
    }

    ctx = _token_mixing(x_prompt, mod3, w, None, mod_base=0, mod_stride=0, tm=256, tm_post=512,
                        heads=N_HEADS, tq=256, tk=256)
    past = cache_k.shape[2]
    cv = cache_v[:, 0].reshape(dec_batch * past, N_HEADS, V_DIM).transpose(1, 2, 0).astype(BF16)
    cv = jnp.concatenate([cv, jnp.ones((N_HEADS, V_EXT - V_DIM, dec_batch * past), BF16)], axis=1)
    cache = (cache_k[:, 0].reshape(dec_batch * past, ATTN_WIDTH).astype(BF16),
             cv.reshape(N_HEADS * V_EXT, dec_batch * past))
    lat = _token_mixing(x_sample, mod3, w, cache, mod_base=1, mod_stride=1, tm=512, tm_post=512,
                        heads=1, tq=1024, tk=512, cast_pre=(w_gate[0],),
                        cast_attn=(w_up[0], w_down[0]))
    (wg,), (wu, wd) = lat["casts"]["pre"], lat["casts"]["attn"]
    yp = _channel_mixing(ctx, mod3, wg, wu, wd, moe_tm=512)
    ys = _channel_mixing(lat, mod3, wg, wu, wd, moe_tm=256)
    k_ctx, v_ctx = ctx["kv"]
    ctx_k =(k_ctx.reshape(batch, N_HEADS, 2, HEAD_DIM, seq).transpose(0, 4, 1, 2, 3)
             .reshape(batch, 1, seq, N_HEADS, 2, HEAD_DIM))
    ctx_v = v_ctx.transpose(0, 2, 1, 3).reshape(batch, 1, seq, N_HEADS, V_DIM)
    return yp, ys, ctx_k, ctx_v
```
